```python
import jax, jax.numpy as jnp
from jax import lax
import numpy as np

D_MODEL = 1024
BATCH = 8
SEQ = 2048
DEPTH = 2
DEC_BATCH = 128
DEC_SEQ = 8
PAST_LEN = 16384
PAGE_SIZE = 128

N_BRANCH = 4
BRANCH_W = D_MODEL // 4
POOL_WINDOWS = (2, 4, 8, 16)
POOL_GROUPS = len(POOL_WINDOWS)
POOL_GW = BRANCH_W // POOL_GROUPS
POOL_HIST = max(POOL_WINDOWS) - 1
CONF_WIDTH = 31
CONF_HIST = CONF_WIDTH - 1
SCONV_WIDTH = 3
SCONV_HIST = SCONV_WIDTH - 1
CHUNK = 128
SGU_HEADS = 4
SGU_HW = BRANCH_W // SGU_HEADS
N_GROUPS = 4
EXPERTS_PER_GROUP = 8
N_EXPERTS = N_GROUPS * EXPERTS_PER_GROUP
TOP_K = 2
D_EXPERT = D_MODEL // 4
RMS_EPS = 1e-6
LN_EPS = 1e-5
COLS = (BRANCH_W, 2 * BRANCH_W, 3 * BRANCH_W, 2 * BRANCH_W, N_BRANCH * D_MODEL)
IN_COLS = sum(COLS)
SPLITS = [sum(COLS[:i + 1]) for i in range(len(COLS) - 1)]

kernel_name = 'hybrid_pool_conv_sgu_hmoe_step'


def rmsnorm(x, g):
    xf = x.astype(jnp.float32)
    y = xf * lax.rsqrt(jnp.mean(xf * xf, axis=-1, keepdims=True) + RMS_EPS)
    return (y * g.astype(jnp.float32)).astype(x.dtype)


def layernorm(x, g, b):
    xf = x.astype(jnp.float32)
    mu = jnp.mean(xf, axis=-1, keepdims=True)
    xc = xf - mu
    y = xc * lax.rsqrt(jnp.mean(xc * xc, axis=-1, keepdims=True) + LN_EPS)
    return (y * g.astype(jnp.float32) + b.astype(jnp.float32)).astype(x.dtype)


def causal_dwconv(ext, w):
    c = ext.shape[-1]
    return lax.conv_general_dilated(ext, w.astype(ext.dtype)[:, None, :], window_strides=(1,), padding='VALID', dimension_numbers=('NWC', 'WIO', 'NWC'), feature_group_count=c)


def pool_mixer(a, prev, start_pos, pool_w, pool_scale):
    bsz, t, c = a.shape
    ext = jnp.concatenate([prev.astype(a.dtype), a], axis=1)
    cs = jnp.concatenate([jnp.zeros((bsz, 1, c), jnp.float32), jnp.cumsum(ext.astype(jnp.float32), axis=1)], axis=1)
    end = cs[:, POOL_HIST + 1:]
    pos = start_pos + jnp.arange(t)
    means = []
    for gi, w in enumerate(POOL_WINDOWS):
        sl = slice(gi * POOL_GW, (gi + 1) * POOL_GW)
        begin = cs[:, POOL_HIST + 1 - w:POOL_HIST + 1 - w + t, sl]
        cnt = jnp.minimum(pos + 1, w).astype(jnp.float32)[None, :, None]
        means.append((end[..., sl] - begin) / cnt)
    pooled = (jnp.concatenate(means, axis=-1) - a.astype(jnp.float32)).astype(a.dtype)
    pooled = pooled.reshape(bsz, t, POOL_GROUPS, POOL_GW)
    mixed = jnp.einsum('btgc,gcd->btgd', pooled, pool_w).reshape(bsz, t, c)
    return mixed * pool_scale, ext[:, -POOL_HIST:]


def chunk_spatial_mix(v, ws, b):
    bsz, t, c = v.shape
    L = min(t, CHUNK)
    n = -(-t // L)
    vp = jnp.pad(v, ((0, 0), (0, n * L - t), (0, 0))).reshape(bsz, n, L, SGU_HEADS, SGU_HW)
    mask = jnp.tril(jnp.ones((L, L), dtype=bool))
    w = jnp.where(mask[None], ws[:, :L, :L], 0)
    s = jnp.einsum('hts,bnshc->bnthc', w, vp) + jnp.transpose(b[:, :L])[None, None, :, :, None]
    return s.reshape(bsz, n * L, c)[:, :t]


def token_mixer(xn, pool_prev, conv_prev, sconv_prev, start_pos, w_in, pool_w, pool_scale, conf_dw, conf_dw_b, conf_ln_g, conf_ln_b, sconv_w, sgu_ln_g, sgu_ln_b, sgu_ws, sgu_b, w_branch, w_out):
    proj = jnp.einsum('btd,df->btf', xn, w_in)
    a_pool, a_conf, a_sc, a_sgu, a_gate = jnp.split(proj, SPLITS, axis=-1)
    br_a, pool_new = pool_mixer(a_pool, pool_prev, start_pos, pool_w, pool_scale)
    g_lin, g_gate = jnp.split(a_conf, 2, axis=-1)
    glu = g_lin * jax.nn.sigmoid(g_gate)
    ext_b = jnp.concatenate([conv_prev.astype(glu.dtype), glu], axis=1)
    cb = causal_dwconv(ext_b, conf_dw) + conf_dw_b
    br_b = jax.nn.silu(layernorm(cb, conf_ln_g, conf_ln_b))
    b_gate, c_gate, h_in = jnp.split(a_sc, 3, axis=-1)
    z = c_gate * h_in
    ext_c = jnp.concatenate([sconv_prev.astype(z.dtype), z], axis=1)
    br_c = b_gate * causal_dwconv(ext_c, sconv_w)
    u, v = jnp.split(jax.nn.gelu(a_sgu), 2, axis=-1)
    v = layernorm(v, sgu_ln_g, sgu_ln_b)
    br_d = u * chunk_spatial_mix(v, sgu_ws, sgu_b)
    t = v.shape[1]
    v_new = v[:, ((t - 1) // CHUNK) * CHUNK:]
    merged = None
    for i, br in enumerate((br_a, br_b, br_c, br_d)):
        gate = jax.nn.sigmoid(a_gate[..., i * D_MODEL:(i + 1) * D_MODEL])
        term = gate * jnp.einsum('btc,cd->btd', br, w_branch[i])
        merged = term if merged is None else merged + term
    out = jnp.einsum('btd,de->bte', merged, w_out)
    return out, pool_new, ext_b[:, -CONF_HIST:], ext_c[:, -SCONV_HIST:], v_new


def hier_moe(xn, router_g, router_g_b, router_e, router_e_b, w_gate, w_up, w_down):
    bsz, t, d = xn.shape
    x2 = xn.reshape(-1, d)
    xf = x2.astype(jnp.float32)
    lg = xf @ router_g.astype(jnp.float32) + router_g_b.astype(jnp.float32)
    pg = jax.nn.softmax(lg, axis=-1)
    gsel = jnp.argmax(lg, axis=-1)
    pg_sel = jnp.take_along_axis(pg, gsel[:, None], axis=1)[:, 0]
    le = (xf @ router_e.astype(jnp.float32) + router_e_b.astype(jnp.float32)).reshape(-1, N_GROUPS, EXPERTS_PER_GROUP)
    le_g = jnp.take_along_axis(le, gsel[:, None, None], axis=1)[:, 0]
    tv, ti = lax.top_k(le_g, TOP_K)
    tw = jax.nn.softmax(tv, axis=-1) * pg_sel[:, None]
    eidx = gsel[:, None] * EXPERTS_PER_GROUP + ti
    cw = jnp.sum(jax.nn.one_hot(eidx, N_EXPERTS, dtype=jnp.float32) * tw[..., None], axis=1)
    cw = cw.astype(xn.dtype).reshape(-1, N_GROUPS, EXPERTS_PER_GROUP)
    out = jnp.zeros_like(x2)
    for g in range(N_GROUPS):
        sl = slice(g * EXPERTS_PER_GROUP, (g + 1) * EXPERTS_PER_GROUP)
        ha = jnp.einsum('nd,edf->nef', x2, w_gate[sl])
        hb = jnp.einsum('nd,edf->nef', x2, w_up[sl])
        hh = jax.nn.silu(ha) * hb * cw[:, g, :, None]
        out = out + jnp.einsum('nef,efd->nd', hh, w_down[sl])
    return out.reshape(bsz, t, d)


def trunk(x, pool_prev, conv_prev, sconv_prev, start_pos, p):
    h = x
    pools, convs, sconvs, vs = [], [], [], []
    for l in range(DEPTH):
        xn = rmsnorm(h, p['norm1_g'][l])
        mix, pn, cn, sn, vn = token_mixer(xn, pool_prev[l], conv_prev[l], sconv_prev[l], start_pos, p['w_in'][l], p['pool_w'][l], p['pool_scale'][l], p['conf_dw'][l], p['conf_dw_b'][l], p['conf_ln_g'][l], p['conf_ln_b'][l], p['sconv_w'][l], p['sgu_ln_g'][l], p['sgu_ln_b'][l], p['sgu_ws'][l], p['sgu_b'][l], p['w_branch'][l], p['w_out'][l])
        h = h + mix
        h = h + hier_moe(rmsnorm(h, p['norm2_g'][l]), p['router_g'][l], p['router_g_b'][l], p['router_e'][l], p['router_e_b'][l], p['w_gate'][l], p['w_up'][l], p['w_down'][l])
        pools.append(pn); convs.append(cn); sconvs.append(sn); vs.append(vn)
    y = rmsnorm(h, p['final_g'])
    return y, jnp.stack(pools), jnp.stack(convs), jnp.stack(sconvs), jnp.stack(vs)


def setup_inputs(seed: int = 0) -> dict:
    key = jax.random.key(seed)
    ks = jax.random.split(key, 32)
    f32 = jnp.float32

    def nrm(k, shape, s):
        return jax.random.normal(k, shape, f32) * s

    return {
        'x_prompt': nrm(ks[0], (BATCH, SEQ, D_MODEL), 1.0),
        'x_sample': nrm(ks[1], (DEC_BATCH, DEC_SEQ, D_MODEL), 1.0),
        'state_pool': nrm(ks[2], (DEPTH, DEC_BATCH, POOL_HIST, BRANCH_W), 1.0),
        'state_conv': nrm(ks[3], (DEPTH, DEC_BATCH, CONF_HIST, BRANCH_W), 0.5),
        'state_sconv': nrm(ks[4], (DEPTH, DEC_BATCH, SCONV_HIST, BRANCH_W), 0.5),
        'norm1_g': 1.0 + nrm(ks[5], (DEPTH, D_MODEL), 0.02),
        'w_in': nrm(ks[6], (DEPTH, D_MODEL, IN_COLS), D_MODEL ** -0.5),
        'pool_w': nrm(ks[7], (DEPTH, POOL_GROUPS, POOL_GW, POOL_GW), POOL_GW ** -0.5),
        'pool_scale': 1.0 + nrm(ks[8], (DEPTH, BRANCH_W), 0.02),
        'conf_dw': nrm(ks[9], (DEPTH, CONF_WIDTH, BRANCH_W), CONF_WIDTH ** -0.5),
        'conf_dw_b': nrm(ks[10], (DEPTH, BRANCH_W), 0.02),
        'conf_ln_g': 1.0 + nrm(ks[11], (DEPTH, BRANCH_W), 0.02),
        'conf_ln_b': nrm(ks[12], (DEPTH, BRANCH_W), 0.02),
        'sconv_w': nrm(ks[13], (DEPTH, SCONV_WIDTH, BRANCH_W), SCONV_WIDTH ** -0.5),
        'sgu_ln_g': 1.0 + nrm(ks[14], (DEPTH, BRANCH_W), 0.02),
        'sgu_ln_b': nrm(ks[15], (DEPTH, BRANCH_W), 0.02),
        'sgu_ws': nrm(ks[16], (DEPTH, SGU_HEADS, CHUNK, CHUNK), CHUNK ** -0.5),
        'sgu_b': 1.0 + nrm(ks[17], (DEPTH, SGU_HEADS, CHUNK), 0.02),
        'w_branch': nrm(ks[18], (DEPTH, N_BRANCH, BRANCH_W, D_MODEL), BRANCH_W ** -0.5),
        'w_out': nrm(ks[19], (DEPTH, D_MODEL, D_MODEL), D_MODEL ** -0.5),
        'norm2_g': 1.0 + nrm(ks[20], (DEPTH, D_MODEL), 0.02),
        'router_g': nrm(ks[21], (DEPTH, D_MODEL, N_GROUPS), D_MODEL ** -0.5),
        'router_g_b': nrm(ks[22], (DEPTH, N_GROUPS), 0.01),
        'router_e': nrm(ks[23], (DEPTH, D_MODEL, N_EXPERTS), D_MODEL ** -0.5),
        'router_e_b': nrm(ks[24], (DEPTH, N_EXPERTS), 0.01),
        'w_gate': nrm(ks[25], (DEPTH, N_EXPERTS, D_MODEL, D_EXPERT), D_MODEL ** -0.5),
        'w_up': nrm(ks[26], (DEPTH, N_EXPERTS, D_MODEL, D_EXPERT), D_MODEL ** -0.5),
        'w_down': nrm(ks[27], (DEPTH, N_EXPERTS, D_EXPERT, D_MODEL), D_EXPERT ** -0.5),
        'final_g': 1.0 + nrm(ks[28], (D_MODEL,), 0.02),
    }


def reference(x_prompt, x_sample, state_pool, state_conv, state_sconv, norm1_g, w_in, pool_w, pool_scale, conf_dw, conf_dw_b, conf_ln_g, conf_ln_b, sconv_w, sgu_ln_g, sgu_ln_b, sgu_ws, sgu_b, w_branch, w_out, norm2_g, router_g, router_g_b, router_e, router_e_b, w_gate, w_up, w_down, final_g):
    p = dict(norm1_g=norm1_g, w_in=w_in, pool_w=pool_w, pool_scale=pool_scale, conf_dw=conf_dw, conf_dw_b=conf_dw_b, conf_ln_g=conf_ln_g, conf_ln_b=conf_ln_b, sconv_w=sconv_w, sgu_ln_g=sgu_ln_g, sgu_ln_b=sgu_ln_b, sgu_ws=sgu_ws, sgu_b=sgu_b, w_branch=w_branch, w_out=w_out, norm2_g=norm2_g, router_g=router_g, router_g_b=router_g_b, router_e=router_e, router_e_b=router_e_b, w_gate=w_gate, w_up=w_up, w_down=w_down, final_g=final_g)
    bp = x_prompt.shape[0]
    zp = jnp.zeros((DEPTH, bp, POOL_HIST, BRANCH_W), x_prompt.dtype)
    zc = jnp.zeros((DEPTH, bp, CONF_HIST, BRANCH_W), x_prompt.dtype)
    zs = jnp.zeros((DEPTH, bp, SCONV_HIST, BRANCH_W), x_prompt.dtype)
    y_prompt, pool_p, conv_p, sconv_p, sguv_p = trunk(x_prompt, zp, zc, zs, 0, p)
    y_sample, pool_s, conv_s, sconv_s, sguv_s = trunk(x_sample, state_pool, state_conv, state_sconv, PAST_LEN, p)
    return (y_prompt, y_sample, pool_p, conv_p, sconv_p, sguv_p, pool_s, conv_s, sconv_s, sguv_s)
```

```python
import functools
from typing import NamedTuple

import jax
import jax.numpy as jnp
from jax import lax
from jax.experimental import pallas as pl
from jax.experimental.pallas import tpu as pltpu

F32 = jnp.float32
BF16 = jnp.bfloat16

D_MODEL = 1024
BRANCH_W = 256
N_BRANCH = 4
POOL_WINDOWS = (2, 4, 8, 16)
POOL_GW = 64
POOL_HIST = 15
CONF_WIDTH = 31
CONF_HIST = 30
SCONV_WIDTH = 3
SCONV_HIST = 2
CHUNK = 128
SGU_HEADS = 4
SGU_HW = 64
N_GROUPS = 4
EXPERTS_PER_GROUP = 8
N_EXPERTS = 32
D_EXPERT = 256
RMS_EPS = 1e-6
LN_EPS = 1e-5
IN_COLS = 6144
GATE_COL0 = 2048

LANES = 128
SUBLANES = 8
POOL_PAD = 16
CONF_PAD = 32
SCONV_PAD = 8
ROW_BLOCK = 32
ROUTE_GROUP_LANE0 = 32
NEG_BIG = -3.0e38

MIX_TILE = 256
RANK_TILE = 512
DISP_TILE = 256
EXP_TILE = 256
COMB_TILE = 256
VMEM_LIMIT = 56 * 1024 * 1024


class MixCfg(NamedTuple):
    nb: int
    t: int
    has_state: bool
    start_pos: int
    row_off: int
    v_rows: int


def _rms(x, g):
    return x * lax.rsqrt(jnp.mean(x * x, axis=-1, keepdims=True) + RMS_EPS) * g


def _ln(x, g, b):
    mu = jnp.mean(x, axis=-1, keepdims=True)
    xc = x - mu
    return xc * lax.rsqrt(jnp.mean(xc * xc, axis=-1, keepdims=True) + LN_EPS) * g + b


def _sigmoid(x):
    return 1.0 / (1.0 + jnp.exp(-x))


def _gelu_tanh(x):
    return 0.5 * x * (1.0 + jnp.tanh(0.7978845608028654 * (x + 0.044715 * (x * x * x))))


def _row_blocks(nb, t):
    if t >= ROW_BLOCK:
        return [(slice(b, b + 1), t0, ROW_BLOCK) for b in range(nb) for t0 in range(0, t, ROW_BLOCK)]
    bb = ROW_BLOCK // t
    return [(slice(b0, b0 + bb), 0, t) for b0 in range(0, nb, bb)]


def _mixer_kernel(cfg, *refs):
    nb, t, tm = cfg.nb, cfg.t, cfg.nb * cfg.t
    refs = list(refs)
    x_ref = refs.pop(0)
    if cfg.has_state:
        pool_st, conv_st, sconv_st = refs.pop(0), refs.pop(0), refs.pop(0)
    (n1g, w_in, pool_w, pool_sc, cdw, cdb, clg, clb, scw, slg, slb, wcat, sbias,
     w_br, w_out, n2g, w_rt, b_rt) = refs[:18]
    refs = refs[18:]
    if cfg.has_state:
        refs = refs[2:]
    h1_o, route_o, pool_o, conv_o, sconv_o, v_o = refs[:6]
    pool_ext, conv_ext, sc_ext, buf_a, buf_b, buf_c = refs[6:]

    if cfg.has_state:
        seq_pos0 = cfg.start_pos
        pool_ext[:, POOL_PAD - POOL_HIST:POOL_PAD, :] = pool_st[...]
        conv_ext[:, CONF_PAD - CONF_HIST:CONF_PAD, :] = conv_st[...]
        sc_ext[:, SCONV_PAD - SCONV_HIST:SCONV_PAD, :] = sconv_st[...]
    else:
        c = pl.program_id(1)
        seq_pos0 = cfg.start_pos + c * t

        @pl.when(c == 0)
        def _():
            pool_ext[:, 0:POOL_PAD, :] = jnp.zeros((nb, POOL_PAD, BRANCH_W), F32)
            conv_ext[:, 0:CONF_PAD, :] = jnp.zeros((nb, CONF_PAD, BRANCH_W), F32)
            sc_ext[:, 0:SCONV_PAD, :] = jnp.zeros((nb, SCONV_PAD, BRANCH_W), F32)

    x = x_ref[...]
    xb = _rms(x, n1g[...]).astype(BF16)

    def proj(lo, hi):
        return jnp.dot(xb, w_in[:, lo:hi], preferred_element_type=F32)

    a_pool = proj(0, 256)
    pool_ext[:, POOL_PAD:, :] = a_pool.reshape(nb, t, BRANCH_W)
    glu = proj(256, 512) * _sigmoid(proj(512, 768))
    conv_ext[:, CONF_PAD:, :] = glu.reshape(nb, t, BRANCH_W)
    z = proj(1024, 1280) * proj(1280, 1536)
    sc_ext[:, SCONV_PAD:, :] = z.reshape(nb, t, BRANCH_W)

    for bs, t0, tb in _row_blocks(nb, t):
        bb = bs.stop - bs.start
        shp = (bb, tb, BRANCH_W)
        lane = lax.broadcasted_iota(jnp.int32, shp, 2)
        pos = seq_pos0 + t0 + lax.broadcasted_iota(jnp.int32, shp, 1)
        cur = pool_ext[bs, POOL_PAD + t0:POOL_PAD + t0 + tb, :]
        acc = cur
        sums = []
        for j in range(1, POOL_WINDOWS[-1]):
            acc = acc + pool_ext[bs, POOL_PAD + t0 - j:POOL_PAD + t0 - j + tb, :]
            if j + 1 in POOL_WINDOWS:
                sums.append(acc)
        s2, s4, s8, s16 = sums
        win_sum = jnp.where(lane < 64, s2, jnp.where(lane < 128, s4, jnp.where(lane < 192, s8, s16)))
        win = jnp.where(lane < 64, 2, jnp.where(lane < 128, 4, jnp.where(lane < 192, 8, 16)))
        cnt = jnp.minimum(pos + 1, win).astype(F32)
        buf_a[bs, t0:t0 + tb, :] = win_sum / cnt - cur
        acc = jnp.broadcast_to(cdb[...].reshape(1, 1, BRANCH_W), shp)
        base = CONF_PAD - CONF_HIST + t0
        for k in range(CONF_WIDTH):
            acc = acc + conv_ext[bs, base + k:base + k + tb, :] * cdw[k:k + 1, :].reshape(1, 1, BRANCH_W)
        buf_b[bs, t0:t0 + tb, :] = acc
        base = SCONV_PAD - SCONV_HIST + t0
        acc = sc_ext[bs, base:base + tb, :] * scw[0:1, :].reshape(1, 1, BRANCH_W)
        for k in range(1, SCONV_WIDTH):
            acc = acc + sc_ext[bs, base + k:base + k + tb, :] * scw[k:k + 1, :].reshape(1, 1, BRANCH_W)
        buf_c[bs, t0:t0 + tb, :] = acc

    def write_state():
        pool_o[...] = pool_ext[:, POOL_PAD + t - POOL_HIST:POOL_PAD + t, :]
        conv_o[...] = conv_ext[:, CONF_PAD + t - CONF_HIST:CONF_PAD + t, :]
        sconv_o[...] = sc_ext[:, SCONV_PAD + t - SCONV_HIST:SCONV_PAD + t, :]

    if cfg.has_state:
        write_state()
    else:
        pl.when(c == pl.num_programs(1) - 1)(write_state)
        pool_ext[:, 0:POOL_PAD, :] = pool_ext[:, t:t + POOL_PAD, :]
        conv_ext[:, 0:CONF_PAD, :] = conv_ext[:, t:t + CONF_PAD, :]
        sc_ext[:, 0:SCONV_PAD, :] = sc_ext[:, t:t + SCONV_PAD, :]

    pooled = buf_a[...].reshape(tm, BRANCH_W).astype(BF16)
    br_a = jnp.dot(pooled, pool_w[...], preferred_element_type=F32) * pool_sc[...]

    cb = _ln(buf_b[...].reshape(tm, BRANCH_W), clg[...], clb[...])
    br_b = cb * _sigmoid(cb)

    br_c = proj(768, 1024) * buf_c[...].reshape(tm, BRANCH_W)

    gu = _gelu_tanh(proj(1536, 1792))
    v = _ln(_gelu_tanh(proj(1792, 2048)), slg[...], slb[...])
    if cfg.has_state:
        v_o[...] = v.reshape(nb, t, BRANCH_W)
    else:
        @pl.when(c == pl.num_programs(1) - 1)
        def _():
            v_o[...] = v[tm - cfg.v_rows:, :].reshape(1, cfg.v_rows, BRANCH_W)
    head = lax.broadcasted_iota(jnp.int32, (CHUNK, BRANCH_W), 1) // SGU_HW
    mixed = []
    for j in range(tm // CHUNK):
        vj = v[j * CHUNK:(j + 1) * CHUNK, :]
        stacked = jnp.concatenate([jnp.where(head == h, vj, 0.0) for h in range(SGU_HEADS)], axis=0)
        mixed.append(jnp.dot(wcat[...], stacked.astype(BF16), preferred_element_type=F32) + sbias[...])
    br_d = gu * jnp.concatenate(mixed, axis=0)

    merged = None
    for i, br in enumerate((br_a, br_b, br_c, br_d)):
        gate = _sigmoid(proj(GATE_COL0 + i * D_MODEL, GATE_COL0 + (i + 1) * D_MODEL))
        term = gate * jnp.dot(br.astype(BF16), w_br[i], preferred_element_type=F32)
        merged = term if merged is None else merged + term
    h1 = x + jnp.dot(merged.astype(BF16), w_out[...], preferred_element_type=F32)
    h1_o[...] = h1

    xn2 = _rms(h1, n2g[...])
    logits = jnp.dot(xn2, w_rt[...], preferred_element_type=F32, precision=lax.Precision.HIGHEST) + b_rt[...]
    lane = lax.broadcasted_iota(jnp.int32, (tm, LANES), 1)
    is_g = (lane >= ROUTE_GROUP_LANE0) & (lane < ROUTE_GROUP_LANE0 + N_GROUPS)
    glog = jnp.where(is_g, logits, NEG_BIG)
    gmax = jnp.max(glog, axis=-1, keepdims=True)
    lane_f = lane.astype(F32)
    no_lane = float(4 * LANES)
    gsel = jnp.min(jnp.where(glog == gmax, lane_f, no_lane), axis=-1, keepdims=True) - ROUTE_GROUP_LANE0
    pg = 1.0 / jnp.sum(jnp.where(is_g, jnp.exp(glog - gmax), 0.0), axis=-1, keepdims=True)
    in_grp = (lane < N_EXPERTS) & ((lane // EXPERTS_PER_GROUP) == gsel.astype(jnp.int32))
    el = jnp.where(in_grp, logits, NEG_BIG)
    m1 = jnp.max(el, axis=-1, keepdims=True)
    i1 = jnp.min(jnp.where(in_grp & (el == m1), lane_f, no_lane), axis=-1, keepdims=True)
    rest = in_grp & (lane_f != i1)
    el2 = jnp.where(rest, logits, NEG_BIG)
    m2 = jnp.max(el2, axis=-1, keepdims=True)
    i2 = jnp.min(jnp.where(rest & (el2 == m2), lane_f, no_lane), axis=-1, keepdims=True)
    e21 = jnp.exp(m2 - m1)
    w1 = pg / (1.0 + e21)
    w2 = pg * e21 / (1.0 + e21)
    route_o[...] = jnp.where(lane == 0, i1, jnp.where(lane == 1, i2,
                             jnp.where(lane == 2, w1, jnp.where(lane == 3, w2, 0.0))))


def _const_spec(shape):
    nd = len(shape)
    return pl.BlockSpec(shape, lambda *_: (0,) * nd, pipeline_mode=pl.Buffered(1))


def _mixer_call(cfg, n_seq, n_rows_total, x2d, x_tile_off, states, weights, aliased):
    nb, t = cfg.nb, cfg.t
    tm = nb * t
    if cfg.has_state:
        grid = (n_seq // nb,)
        tok = lambda i: (x_tile_off + i, 0)
        out_tok = lambda i: (cfg.row_off + i, 0)
        seq3 = lambda i: (i, 0, 0)
    else:
        grid = (n_seq, 2048 // t)
        nt = grid[1]
        tok = lambda b, c: (x_tile_off + b * nt + c, 0)
        out_tok = lambda b, c: (cfg.row_off + b * nt + c, 0)
        seq3 = lambda b, c: (b, 0, 0)

    in_specs = [pl.BlockSpec((tm, D_MODEL), tok)]
    args = [x2d]
    if cfg.has_state:
        for s in states:
            in_specs.append(pl.BlockSpec((nb,) + s.shape[1:], seq3))
            args.append(s)
    for w in weights:
        in_specs.append(_const_spec(w.shape))
        args.append(w)
    io_alias = {}
    if cfg.has_state:
        for k, a in enumerate(aliased):
            in_specs.append(pl.BlockSpec(memory_space=pl.ANY))
            io_alias[len(args)] = k
            args.append(a)

    out_shape = [
        jax.ShapeDtypeStruct((n_rows_total, D_MODEL), F32),
        jax.ShapeDtypeStruct((n_rows_total, LANES), F32),
        jax.ShapeDtypeStruct((n_seq, POOL_HIST, BRANCH_W), F32),
        jax.ShapeDtypeStruct((n_seq, CONF_HIST, BRANCH_W), F32),
        jax.ShapeDtypeStruct((n_seq, SCONV_HIST, BRANCH_W), F32),
        jax.ShapeDtypeStruct((n_seq, cfg.v_rows, BRANCH_W), F32),
    ]
    out_specs = [
        pl.BlockSpec((tm, D_MODEL), out_tok),
        pl.BlockSpec((tm, LANES), out_tok),
        pl.BlockSpec((nb, POOL_HIST, BRANCH_W), seq3),
        pl.BlockSpec((nb, CONF_HIST, BRANCH_W), seq3),
        pl.BlockSpec((nb, SCONV_HIST, BRANCH_W), seq3),
        pl.BlockSpec((nb, cfg.v_rows, BRANCH_W), seq3),
    ]
    scratch = [
        pltpu.VMEM((nb, POOL_PAD + t, BRANCH_W), F32),
        pltpu.VMEM((nb, CONF_PAD + t, BRANCH_W), F32),
        pltpu.VMEM((nb, SCONV_PAD + t, BRANCH_W), F32),
        pltpu.VMEM((nb, t, BRANCH_W), F32),
        pltpu.VMEM((nb, t, BRANCH_W), F32),
        pltpu.VMEM((nb, t, BRANCH_W), F32),
    ]
    return pl.pallas_call(
        functools.partial(_mixer_kernel, cfg),
        grid=grid,
        in_specs=in_specs,
        out_specs=out_specs,
        out_shape=out_shape,
        scratch_shapes=scratch,
        input_output_aliases=io_alias,
        compiler_params=pltpu.CompilerParams(
            dimension_semantics=("arbitrary",) * len(grid), vmem_limit_bytes=VMEM_LIMIT),
        name="mixer_sample" if cfg.has_state else "mixer_prompt",
    )(*args)


def _rank_kernel(route_ref, rank_o, cnt_o, run):
    i = pl.program_id(0)

    @pl.when(i == 0)
    def _():
        run[...] = jnp.zeros_like(run)

    r = route_ref[...]
    tr = r.shape[0]
    lane = lax.broadcasted_iota(jnp.int32, (tr, LANES), 1).astype(F32)
    e1 = r[:, 0:1]
    e2 = r[:, 1:2]
    hit1 = lane == e1
    hit2 = lane == e2
    onehot = jnp.where(hit1 | hit2, 1.0, 0.0)
    row = lax.broadcasted_iota(jnp.int32, (tr, tr), 0)
    col = lax.broadcasted_iota(jnp.int32, (tr, tr), 1)
    strict_lower = jnp.where(col < row, 1.0, 0.0).astype(BF16)
    before = jnp.dot(strict_lower, onehot.astype(BF16), preferred_element_type=F32) + run[0:1, :]
    rank1 = jnp.sum(jnp.where(hit1, before, 0.0), axis=-1, keepdims=True)
    rank2 = jnp.sum(jnp.where(hit2, before, 0.0), axis=-1, keepdims=True)
    lane_i = lax.broadcasted_iota(jnp.int32, (tr, LANES), 1)
    rank_o[...] = jnp.where(lane_i == 0, rank1, jnp.where(lane_i == 1, rank2, 0.0))
    total = run[0:1, :] + jnp.sum(onehot, axis=0, keepdims=True)
    run[...] = jnp.broadcast_to(total, run.shape)
    cnt_o[...] = jnp.broadcast_to(total, cnt_o.shape)


def _rank_call(route):
    n = route.shape[0]
    return pl.pallas_call(
        _rank_kernel,
        grid=(n // RANK_TILE,),
        in_specs=[pl.BlockSpec((RANK_TILE, LANES), lambda i: (i, 0))],
        out_specs=[pl.BlockSpec((RANK_TILE, LANES), lambda i: (i, 0)),
                   pl.BlockSpec((SUBLANES, LANES), lambda i: (0, 0))],
        out_shape=[jax.ShapeDtypeStruct((n, LANES), F32),
                   jax.ShapeDtypeStruct((SUBLANES, LANES), F32)],
        scratch_shapes=[pltpu.VMEM((SUBLANES, LANES), F32)],
        compiler_params=pltpu.CompilerParams(dimension_semantics=("arbitrary",)),
        name="moe_rank",
    )(route)


def _row_copy(src, src_row, dst, dst_row, sem):
    return pltpu.make_async_copy(src.at[pl.ds(src_row, 1), :], dst.at[pl.ds(dst_row, 1), :], sem)


def _dispatch_kernel(d0_ref, d1_ref, h1_ref, n2g, xs_in, xs_o, buf, sem):
    del xs_in
    i = pl.program_id(0)
    tmd = h1_ref.shape[0]
    buf[...] = _rms(h1_ref[...], n2g[...])
    base = i * tmd

    def issue(r, carry):
        _row_copy(buf, r, xs_o, d0_ref[base + r], sem).start()
        _row_copy(buf, r, xs_o, d1_ref[base + r], sem).start()
        return carry

    lax.fori_loop(0, tmd, issue, 0, unroll=8)

    def drain(r, carry):
        _row_copy(buf, 0, xs_o, 0, sem).wait()
        _row_copy(buf, 0, xs_o, 0, sem).wait()
        return carry

    lax.fori_loop(0, tmd, drain, 0, unroll=8)


def _dispatch_call(h1, n2g, dest0, dest1, xs_init):
    n = h1.shape[0]
    grid_spec = pltpu.PrefetchScalarGridSpec(
        num_scalar_prefetch=2,
        grid=(n // DISP_TILE,),
        in_specs=[pl.BlockSpec((DISP_TILE, D_MODEL), lambda i, *_: (i, 0)),
                  pl.BlockSpec((1, D_MODEL), lambda i, *_: (0, 0)),
                  pl.BlockSpec(memory_space=pl.ANY)],
        out_specs=pl.BlockSpec(memory_space=pl.ANY),
        scratch_shapes=[pltpu.VMEM((DISP_TILE, D_MODEL), F32), pltpu.SemaphoreType.DMA(())],
    )
    return pl.pallas_call(
        _dispatch_kernel,
        grid_spec=grid_spec,
        out_shape=jax.ShapeDtypeStruct(xs_init.shape, F32),
        input_output_aliases={4: 0},
        compiler_params=pltpu.CompilerParams(dimension_semantics=("arbitrary",), has_side_effects=True),
        name="moe_dispatch",
    )(dest0, dest1, h1, n2g, xs_init)


def _expert_kernel(te_ref, nv_ref, xs_ref, wg, wu, wd, y_o):
    i = pl.program_id(0)

    @pl.when(i < nv_ref[0])
    def _():
        xb = xs_ref[...].astype(BF16)
        a = jnp.dot(xb, wg[0], preferred_element_type=F32)
        b = jnp.dot(xb, wu[0], preferred_element_type=F32)
        hh = (a * _sigmoid(a) * b).astype(BF16)
        y_o[...] = jnp.dot(hh, wd[0], preferred_element_type=F32)

    @pl.when(i >= nv_ref[0])
    def _():
        y_o[...] = jnp.zeros_like(y_o)


def _expert_call(xs, tile_expert, n_valid, w_gate, w_up, w_down):
    n_tiles = xs.shape[0] // EXP_TILE
    grid_spec = pltpu.PrefetchScalarGridSpec(
        num_scalar_prefetch=2,
        grid=(n_tiles,),
        in_specs=[pl.BlockSpec((EXP_TILE, D_MODEL), lambda i, te, nv: (jnp.minimum(i, nv[0] - 1), 0)),
                  pl.BlockSpec((1, D_MODEL, D_EXPERT), lambda i, te, nv: (te[i], 0, 0)),
                  pl.BlockSpec((1, D_MODEL, D_EXPERT), lambda i, te, nv: (te[i], 0, 0)),
                  pl.BlockSpec((1, D_EXPERT, D_MODEL), lambda i, te, nv: (te[i], 0, 0))],
        out_specs=pl.BlockSpec((EXP_TILE, D_MODEL), lambda i, te, nv: (i, 0)),
    )
    return pl.pallas_call(
        _expert_kernel,
        grid_spec=grid_spec,
        out_shape=jax.ShapeDtypeStruct(xs.shape, F32),
        compiler_params=pltpu.CompilerParams(dimension_semantics=("arbitrary",)),
        name="moe_experts",
    )(tile_expert, n_valid, xs, w_gate, w_up, w_down)


def _combine_kernel(final_norm, d0_ref, d1_ref, h1_ref, route_ref, fg, y_hbm, out_o, ybuf, sem):
    i = pl.program_id(0)
    tmc = h1_ref.shape[0]
    base = i * tmc

    def issue(r, carry):
        _row_copy(y_hbm, d0_ref[base + r], ybuf.at[0], r, sem).start()
        _row_copy(y_hbm, d1_ref[base + r], ybuf.at[1], r, sem).start()
        return carry

    lax.fori_loop(0, tmc, issue, 0, unroll=8)

    def drain(r, carry):
        _row_copy(y_hbm, 0, ybuf.at[0], 0, sem).wait()
        _row_copy(y_hbm, 0, ybuf.at[1], 0, sem).wait()
        return carry

    lax.fori_loop(0, tmc, drain, 0, unroll=8)

    r = route_ref[...]
    h2 = h1_ref[...] + r[:, 2:3] * ybuf[0] + r[:, 3:4] * ybuf[1]
    out_o[...] = _rms(h2, fg[...]) if final_norm else h2


def _combine_call(h1, route, y, dest0, dest1, final_g, final_norm):
    n = h1.shape[0]
    grid_spec = pltpu.PrefetchScalarGridSpec(
        num_scalar_prefetch=2,
        grid=(n // COMB_TILE,),
        in_specs=[pl.BlockSpec((COMB_TILE, D_MODEL), lambda i, *_: (i, 0)),
                  pl.BlockSpec((COMB_TILE, LANES), lambda i, *_: (i, 0)),
                  pl.BlockSpec((1, D_MODEL), lambda i, *_: (0, 0)),
                  pl.BlockSpec(memory_space=pl.ANY)],
        out_specs=pl.BlockSpec((COMB_TILE, D_MODEL), lambda i, *_: (i, 0)),
        scratch_shapes=[pltpu.VMEM((2, COMB_TILE, D_MODEL), F32), pltpu.SemaphoreType.DMA(())],
    )
    return pl.pallas_call(
        functools.partial(_combine_kernel, final_norm),
        grid_spec=grid_spec,
        out_shape=jax.ShapeDtypeStruct(h1.shape, F32),
        compiler_params=pltpu.CompilerParams(dimension_semantics=("arbitrary",)),
        name="moe_combine",
    )(dest0, dest1, h1, route, final_g, y)


def _moe(h1, route, n2g, w_gate, w_up, w_down, final_g, final_norm):
    n = h1.shape[0]
    n_tiles = (2 * n) // EXP_TILE + N_EXPERTS
    rank, counts = _rank_call(route)
    cnt = counts[0, :N_EXPERTS].astype(jnp.int32)
    seg_tiles = (cnt + EXP_TILE - 1) // EXP_TILE
    tile_end = jnp.cumsum(seg_tiles)
    offs = (tile_end - seg_tiles) * EXP_TILE
    e1 = route[:, 0].astype(jnp.int32)
    e2 = route[:, 1].astype(jnp.int32)
    dest0 = offs[e1] + rank[:, 0].astype(jnp.int32)
    dest1 = offs[e2] + rank[:, 1].astype(jnp.int32)
    n_valid = tile_end[-1:].astype(jnp.int32)
    tile_ids = jnp.arange(n_tiles, dtype=jnp.int32)
    tile_expert = jnp.minimum(
        jnp.sum((tile_end[None, :] <= tile_ids[:, None]).astype(jnp.int32), axis=1), N_EXPERTS - 1)

    xs = _dispatch_call(h1, n2g, dest0, dest1, jnp.zeros((n_tiles * EXP_TILE, D_MODEL), F32))
    y = _expert_call(xs, tile_expert, n_valid, w_gate, w_up, w_down)
    return _combine_call(h1, route, y, dest0, dest1, final_g, final_norm)


def _block_diag(pw):
    out = jnp.zeros((BRANCH_W, BRANCH_W), pw.dtype)
    for g in range(len(POOL_WINDOWS)):
        out = out.at[g * POOL_GW:(g + 1) * POOL_GW, g * POOL_GW:(g + 1) * POOL_GW].set(pw[g])
    return out


def kernel(x_prompt, x_sample, state_pool, state_conv, state_sconv, norm1_g, w_in, pool_w, pool_scale, conf_dw, conf_dw_b, conf_ln_g, conf_ln_b, sconv_w, sgu_ln_g, sgu_ln_b, sgu_ws, sgu_b, w_branch, w_out, norm2_g, router_g, router_g_b, router_e, router_e_b, w_gate, w_up, w_down, final_g):
    depth = w_in.shape[0]
    bp, seq, _ = x_prompt.shape
    bs, dseq, _ = x_sample.shape
    n_p, n_s = bp * seq, bs * dseq
    n = n_p + n_s
    past_len = 16384
    assert seq % MIX_TILE == 0 and n_s % MIX_TILE == 0 and MIX_TILE % CHUNK == 0 and dseq == SUBLANES

    cfg_p = MixCfg(nb=1, t=MIX_TILE, has_state=False, start_pos=0, row_off=0, v_rows=CHUNK)
    cfg_s = MixCfg(nb=MIX_TILE // dseq, t=dseq, has_state=True, start_pos=past_len,
                   row_off=n_p // MIX_TILE, v_rows=dseq)

    row = lambda a: a.reshape(1, -1)
    tril = jnp.tril(jnp.ones((CHUNK, CHUNK), F32))
    eye_blk = jnp.kron(jnp.eye(CHUNK // dseq, dtype=F32), jnp.ones((dseq, dseq), F32))
    final_row = row(final_g)

    h_p, h_s = x_prompt.reshape(n_p, D_MODEL), x_sample.reshape(n_s, D_MODEL)
    off_p, off_s = 0, 0
    states_out = []
    for l in range(depth):
        ws_p = sgu_ws[l] * tril[None]
        ws_small = jnp.tile(sgu_ws[l][:, :dseq, :dseq], (1, CHUNK // dseq, CHUNK // dseq))
        ws_s = ws_small * (tril * eye_blk)[None]
        cat = lambda w: jnp.concatenate([w[h] for h in range(SGU_HEADS)], axis=1).astype(BF16)
        bias_p = jnp.repeat(sgu_b[l].T, SGU_HW, axis=1)
        bias_s = jnp.tile(jnp.repeat(sgu_b[l][:, :dseq].T, SGU_HW, axis=1), (CHUNK // dseq, 1))
        w_rt = jnp.zeros((D_MODEL, LANES), F32)
        w_rt = w_rt.at[:, :N_EXPERTS].set(router_e[l]).at[:, ROUTE_GROUP_LANE0:ROUTE_GROUP_LANE0 + N_GROUPS].set(router_g[l])
        b_rt = jnp.zeros((1, LANES), F32)
        b_rt = b_rt.at[0, :N_EXPERTS].set(router_e_b[l]).at[0, ROUTE_GROUP_LANE0:ROUTE_GROUP_LANE0 + N_GROUPS].set(router_g_b[l])

        def weights(wcat, sbias):
            return [row(norm1_g[l]), w_in[l].astype(BF16), _block_diag(pool_w[l]).astype(BF16), row(pool_scale[l]),
                    conf_dw[l], row(conf_dw_b[l]), row(conf_ln_g[l]), row(conf_ln_b[l]), sconv_w[l],
                    row(sgu_ln_g[l]), row(sgu_ln_b[l]), wcat, sbias,
                    w_branch[l].astype(BF16), w_out[l].astype(BF16), row(norm2_g[l]), w_rt, b_rt]

        h1, route, pool_p, conv_p, sconv_p, v_p = _mixer_call(
            cfg_p, bp, n, h_p, off_p, None, weights(cat(ws_p), bias_p), None)
        h1, route, pool_s, conv_s, sconv_s, v_s = _mixer_call(
            cfg_s, bs, n, h_s, off_s, (state_pool[l], state_conv[l], state_sconv[l]),
            weights(cat(ws_s), bias_s), (h1, route))
        states_out.append((pool_p, conv_p, sconv_p, v_p, pool_s, conv_s, sconv_s, v_s))

        h2 = _moe(h1, route, row(norm2_g[l]), w_gate[l].astype(BF16), w_up[l].astype(BF16),
                  w_down[l].astype(BF16), final_row, l == depth - 1)
        h_p = h_s = h2
        off_p, off_s = 0, n_p // MIX_TILE

    y_prompt = h2[:n_p].reshape(bp, seq, D_MODEL)
    y_sample = h2[n_p:].reshape(bs, dseq, D_MODEL)
    st = [jnp.stack([s[k] for s in states_out]) for k in range(8)]
    return (y_prompt, y_sample, st[0], st[1], st[2], st[3], st[4], st[5], st[6], st[7])
```

```python
import functools
from typing import NamedTuple

import jax
import jax.numpy as jnp
from jax import lax
from jax.experimental import pallas as pl
from jax.experimental.pallas import tpu as pltpu

F32 = jnp.float32
BF16 = jnp.bfloat16

D_MODEL = 1024
BRANCH_W = 256
N_BRANCH = 4
POOL_WINDOWS = (2, 4, 8, 16)
POOL_GW = 64
POOL_HIST = 15
CONF_WIDTH = 31
CONF_HIST = 30
SCONV_WIDTH = 3
SCONV_HIST = 2
CHUNK = 128
SGU_HEADS = 4
SGU_HW = 64
N_GROUPS = 4
EXPERTS_PER_GROUP = 8
N_EXPERTS = 32
D_EXPERT = 256
RMS_EPS = 1e-6
LN_EPS = 1e-5
IN_COLS = 6144
GATE_COL0 = 2048

LANES = 128
SUBLANES = 8
POOL_PAD = 16
CONF_PAD = 32
SCONV_PAD = 8
ROW_BLOCK = 32
ROUTE_GROUP_LANE0 = 32
NEG_BIG = -3.0e38

MIX_TILE = 256
SORT_TILE = 512
SEG_PAD = 16
PERM_BLOCK = 256
N_PERM_BLOCKS = -(-(2 * SORT_TILE + N_EXPERTS * (SEG_PAD - 1)) // PERM_BLOCK)
LOCAL_ROWS = N_PERM_BLOCKS * PERM_BLOCK
EXP_TILE = 256
CHUNKS_PER_STEP = EXP_TILE // SEG_PAD
VMEM_LIMIT = 56 * 1024 * 1024


class MixCfg(NamedTuple):
    nb: int
    t: int
    has_state: bool
    start_pos: int
    row_off: int
    v_rows: int


def _rms(x, g):
    return x * lax.rsqrt(jnp.mean(x * x, axis=-1, keepdims=True) + RMS_EPS) * g


def _ln(x, g, b):
    mu = jnp.mean(x, axis=-1, keepdims=True)
    xc = x - mu
    return xc * lax.rsqrt(jnp.mean(xc * xc, axis=-1, keepdims=True) + LN_EPS) * g + b


def _sigmoid(x):
    return 1.0 / (1.0 + jnp.exp(-x))


def _gelu_tanh(x):
    return 0.5 * x * (1.0 + jnp.tanh(0.7978845608028654 * (x + 0.044715 * (x * x * x))))


def _row_blocks(nb, t):
    if t >= ROW_BLOCK:
        return [(slice(b, b + 1), t0, ROW_BLOCK) for b in range(nb) for t0 in range(0, t, ROW_BLOCK)]
    bb = ROW_BLOCK // t
    return [(slice(b0, b0 + bb), 0, t) for b0 in range(0, nb, bb)]


def _mixer_kernel(cfg, *refs):
    nb, t, tm = cfg.nb, cfg.t, cfg.nb * cfg.t
    refs = list(refs)
    x_ref = refs.pop(0)
    if cfg.has_state:
        pool_st, conv_st, sconv_st = refs.pop(0), refs.pop(0), refs.pop(0)
    (n1g, w_in, pool_w, pool_sc, cdw, cdb, clg, clb, scw, slg, slb, wcat, sbias,
     w_br, w_out, n2g, w_rt, b_rt) = refs[:18]
    refs = refs[18:]
    if cfg.has_state:
        refs = refs[2:]
    h1_o, route_o, pool_o, conv_o, sconv_o, v_o = refs[:6]
    pool_ext, conv_ext, sc_ext, buf_a, buf_b, buf_c = refs[6:]

    if cfg.has_state:
        seq_pos0 = cfg.start_pos
        pool_ext[:, POOL_PAD - POOL_HIST:POOL_PAD, :] = pool_st[...]
        conv_ext[:, CONF_PAD - CONF_HIST:CONF_PAD, :] = conv_st[...]
        sc_ext[:, SCONV_PAD - SCONV_HIST:SCONV_PAD, :] = sconv_st[...]
    else:
        c = pl.program_id(1)
        seq_pos0 = cfg.start_pos + c * t

        @pl.when(c == 0)
        def _():
            pool_ext[:, 0:POOL_PAD, :] = jnp.zeros((nb, POOL_PAD, BRANCH_W), F32)
            conv_ext[:, 0:CONF_PAD, :] = jnp.zeros((nb, CONF_PAD, BRANCH_W), F32)
            sc_ext[:, 0:SCONV_PAD, :] = jnp.zeros((nb, SCONV_PAD, BRANCH_W), F32)

    x = x_ref[...]
    xb = _rms(x, n1g[...]).astype(BF16)

    def proj(lo, hi):
        return jnp.dot(xb, w_in[:, lo:hi], preferred_element_type=F32)

    a_pool = proj(0, 256)
    pool_ext[:, POOL_PAD:, :] = a_pool.reshape(nb, t, BRANCH_W)
    glu = proj(256, 512) * _sigmoid(proj(512, 768))
    conv_ext[:, CONF_PAD:, :] = glu.reshape(nb, t, BRANCH_W)
    z = proj(1024, 1280) * proj(1280, 1536)
    sc_ext[:, SCONV_PAD:, :] = z.reshape(nb, t, BRANCH_W)

    for bs, t0, tb in _row_blocks(nb, t):
        bb = bs.stop - bs.start
        shp = (bb, tb, BRANCH_W)
        lane = lax.broadcasted_iota(jnp.int32, shp, 2)
        pos = seq_pos0 + t0 + lax.broadcasted_iota(jnp.int32, shp, 1)
        cur = pool_ext[bs, POOL_PAD + t0:POOL_PAD + t0 + tb, :]
        acc = cur
        sums = []
        for j in range(1, POOL_WINDOWS[-1]):
            acc = acc + pool_ext[bs, POOL_PAD + t0 - j:POOL_PAD + t0 - j + tb, :]
            if j + 1 in POOL_WINDOWS:
                sums.append(acc)
        s2, s4, s8, s16 = sums
        win_sum = jnp.where(lane < 64, s2, jnp.where(lane < 128, s4, jnp.where(lane < 192, s8, s16)))
        win = jnp.where(lane < 64, 2, jnp.where(lane < 128, 4, jnp.where(lane < 192, 8, 16)))
        cnt = jnp.minimum(pos + 1, win).astype(F32)
        buf_a[bs, t0:t0 + tb, :] = win_sum / cnt - cur
        acc = jnp.broadcast_to(cdb[...].reshape(1, 1, BRANCH_W), shp)
        base = CONF_PAD - CONF_HIST + t0
        for k in range(CONF_WIDTH):
            acc = acc + conv_ext[bs, base + k:base + k + tb, :] * cdw[k:k + 1, :].reshape(1, 1, BRANCH_W)
        buf_b[bs, t0:t0 + tb, :] = acc
        base = SCONV_PAD - SCONV_HIST + t0
        acc = sc_ext[bs, base:base + tb, :] * scw[0:1, :].reshape(1, 1, BRANCH_W)
        for k in range(1, SCONV_WIDTH):
            acc = acc + sc_ext[bs, base + k:base + k + tb, :] * scw[k:k + 1, :].reshape(1, 1, BRANCH_W)
        buf_c[bs, t0:t0 + tb, :] = acc

    def write_state():
        pool_o[...] = pool_ext[:, POOL_PAD + t - POOL_HIST:POOL_PAD + t, :]
        conv_o[...] = conv_ext[:, CONF_PAD + t - CONF_HIST:CONF_PAD + t, :]
        sconv_o[...] = sc_ext[:, SCONV_PAD + t - SCONV_HIST:SCONV_PAD + t, :]

    if cfg.has_state:
        write_state()
    else:
        pl.when(c == pl.num_programs(1) - 1)(write_state)
        pool_ext[:, 0:POOL_PAD, :] = pool_ext[:, t:t + POOL_PAD, :]
        conv_ext[:, 0:CONF_PAD, :] = conv_ext[:, t:t + CONF_PAD, :]
        sc_ext[:, 0:SCONV_PAD, :] = sc_ext[:, t:t + SCONV_PAD, :]

    pooled = buf_a[...].reshape(tm, BRANCH_W).astype(BF16)
    br_a = jnp.dot(pooled, pool_w[...], preferred_element_type=F32) * pool_sc[...]

    cb = _ln(buf_b[...].reshape(tm, BRANCH_W), clg[...], clb[...])
    br_b = cb * _sigmoid(cb)

    br_c = proj(768, 1024) * buf_c[...].reshape(tm, BRANCH_W)

    gu = _gelu_tanh(proj(1536, 1792))
    v = _ln(_gelu_tanh(proj(1792, 2048)), slg[...], slb[...])
    if cfg.has_state:
        v_o[...] = v.reshape(nb, t, BRANCH_W)
    else:
        @pl.when(c == pl.num_programs(1) - 1)
        def _():
            v_o[...] = v[tm - cfg.v_rows:, :].reshape(1, cfg.v_rows, BRANCH_W)
    head = lax.broadcasted_iota(jnp.int32, (CHUNK, BRANCH_W), 1) // SGU_HW
    mixed = []
    for j in range(tm // CHUNK):
        vj = v[j * CHUNK:(j + 1) * CHUNK, :]
        stacked = jnp.concatenate([jnp.where(head == h, vj, 0.0) for h in range(SGU_HEADS)], axis=0)
        mixed.append(jnp.dot(wcat[...], stacked.astype(BF16), preferred_element_type=F32) + sbias[...])
    br_d = gu * jnp.concatenate(mixed, axis=0)

    merged = None
    for i, br in enumerate((br_a, br_b, br_c, br_d)):
        gate = _sigmoid(proj(GATE_COL0 + i * D_MODEL, GATE_COL0 + (i + 1) * D_MODEL))
        term = gate * jnp.dot(br.astype(BF16), w_br[i], preferred_element_type=F32)
        merged = term if merged is None else merged + term
    h1 = x + jnp.dot(merged.astype(BF16), w_out[...], preferred_element_type=F32)
    h1_o[...] = h1

    xn2 = _rms(h1, n2g[...])
    logits = jnp.dot(xn2, w_rt[...], preferred_element_type=F32, precision=lax.Precision.HIGHEST) + b_rt[...]
    lane = lax.broadcasted_iota(jnp.int32, (tm, LANES), 1)
    is_g = (lane >= ROUTE_GROUP_LANE0) & (lane < ROUTE_GROUP_LANE0 + N_GROUPS)
    glog = jnp.where(is_g, logits, NEG_BIG)
    gmax = jnp.max(glog, axis=-1, keepdims=True)
    lane_f = lane.astype(F32)
    no_lane = float(4 * LANES)
    gsel = jnp.min(jnp.where(glog == gmax, lane_f, no_lane), axis=-1, keepdims=True) - ROUTE_GROUP_LANE0
    pg = 1.0 / jnp.sum(jnp.where(is_g, jnp.exp(glog - gmax), 0.0), axis=-1, keepdims=True)
    in_grp = (lane < N_EXPERTS) & ((lane // EXPERTS_PER_GROUP) == gsel.astype(jnp.int32))
    el = jnp.where(in_grp, logits, NEG_BIG)
    m1 = jnp.max(el, axis=-1, keepdims=True)
    i1 = jnp.min(jnp.where(in_grp & (el == m1), lane_f, no_lane), axis=-1, keepdims=True)
    rest = in_grp & (lane_f != i1)
    el2 = jnp.where(rest, logits, NEG_BIG)
    m2 = jnp.max(el2, axis=-1, keepdims=True)
    i2 = jnp.min(jnp.where(rest & (el2 == m2), lane_f, no_lane), axis=-1, keepdims=True)
    e21 = jnp.exp(m2 - m1)
    w1 = pg / (1.0 + e21)
    w2 = pg * e21 / (1.0 + e21)
    route_o[...] = jnp.where(lane == 0, i1, jnp.where(lane == 1, i2,
                             jnp.where(lane == 2, w1, jnp.where(lane == 3, w2, 0.0))))


def _const_spec(shape):
    nd = len(shape)
    return pl.BlockSpec(shape, lambda *_: (0,) * nd, pipeline_mode=pl.Buffered(1))


def _mixer_call(cfg, n_seq, n_rows_total, x2d, x_tile_off, states, weights, aliased):
    nb, t = cfg.nb, cfg.t
    tm = nb * t
    if cfg.has_state:
        grid = (n_seq // nb,)
        tok = lambda i: (x_tile_off + i, 0)
        out_tok = lambda i: (cfg.row_off + i, 0)
        seq3 = lambda i: (i, 0, 0)
    else:
        grid = (n_seq, 2048 // t)
        nt = grid[1]
        tok = lambda b, c: (x_tile_off + b * nt + c, 0)
        out_tok = lambda b, c: (cfg.row_off + b * nt + c, 0)
        seq3 = lambda b, c: (b, 0, 0)

    in_specs = [pl.BlockSpec((tm, D_MODEL), tok)]
    args = [x2d]
    if cfg.has_state:
        for s in states:
            in_specs.append(pl.BlockSpec((nb,) + s.shape[1:], seq3))
            args.append(s)
    for w in weights:
        in_specs.append(_const_spec(w.shape))
        args.append(w)
    io_alias = {}
    if cfg.has_state:
        for k, a in enumerate(aliased):
            in_specs.append(pl.BlockSpec(memory_space=pl.ANY))
            io_alias[len(args)] = k
            args.append(a)

    out_shape = [
        jax.ShapeDtypeStruct((n_rows_total, D_MODEL), F32),
        jax.ShapeDtypeStruct((n_rows_total, LANES), F32),
        jax.ShapeDtypeStruct((n_seq, POOL_HIST, BRANCH_W), F32),
        jax.ShapeDtypeStruct((n_seq, CONF_HIST, BRANCH_W), F32),
        jax.ShapeDtypeStruct((n_seq, SCONV_HIST, BRANCH_W), F32),
        jax.ShapeDtypeStruct((n_seq, cfg.v_rows, BRANCH_W), F32),
    ]
    out_specs = [
        pl.BlockSpec((tm, D_MODEL), out_tok),
        pl.BlockSpec((tm, LANES), out_tok),
        pl.BlockSpec((nb, POOL_HIST, BRANCH_W), seq3),
        pl.BlockSpec((nb, CONF_HIST, BRANCH_W), seq3),
        pl.BlockSpec((nb, SCONV_HIST, BRANCH_W), seq3),
        pl.BlockSpec((nb, cfg.v_rows, BRANCH_W), seq3),
    ]
    scratch = [
        pltpu.VMEM((nb, POOL_PAD + t, BRANCH_W), F32),
        pltpu.VMEM((nb, CONF_PAD + t, BRANCH_W), F32),
        pltpu.VMEM((nb, SCONV_PAD + t, BRANCH_W), F32),
        pltpu.VMEM((nb, t, BRANCH_W), F32),
        pltpu.VMEM((nb, t, BRANCH_W), F32),
        pltpu.VMEM((nb, t, BRANCH_W), F32),
    ]
    return pl.pallas_call(
        functools.partial(_mixer_kernel, cfg),
        grid=grid,
        in_specs=in_specs,
        out_specs=out_specs,
        out_shape=out_shape,
        scratch_shapes=scratch,
        input_output_aliases=io_alias,
        compiler_params=pltpu.CompilerParams(
            dimension_semantics=("arbitrary",) * len(grid), vmem_limit_bytes=VMEM_LIMIT),
        name="mixer_sample" if cfg.has_state else "mixer_prompt",
    )(*args)


def _sort_kernel(h1_ref, route_ref, n2g, xl_o, pos_o, nch_o):
    t = SORT_TILE
    xn = _rms(h1_ref[...], n2g[...]).astype(BF16)
    r = route_ref[...]
    lane_i = lax.broadcasted_iota(jnp.int32, (t, LANES), 1)
    lane = lane_i.astype(F32)
    hit1 = lane == r[:, 0:1]
    hit2 = lane == r[:, 1:2]
    onehot = jnp.where(hit1 | hit2, 1.0, 0.0)
    row = lax.broadcasted_iota(jnp.int32, (t, t), 0)
    col = lax.broadcasted_iota(jnp.int32, (t, t), 1)
    strict_lower = jnp.where(col < row, 1.0, 0.0).astype(BF16)
    before = jnp.dot(strict_lower, onehot.astype(BF16), preferred_element_type=F32)
    cnt = jnp.sum(onehot, axis=0, keepdims=True)
    nch = jnp.floor((cnt + (SEG_PAD - 1)) * (1.0 / SEG_PAD))
    er = lax.broadcasted_iota(jnp.int32, (LANES, LANES), 0)
    ec = lax.broadcasted_iota(jnp.int32, (LANES, LANES), 1)
    strict_upper = jnp.where(er < ec, 1.0, 0.0).astype(BF16)
    nch8 = jnp.broadcast_to(nch, (SUBLANES, LANES)).astype(BF16)
    seg0 = jnp.dot(nch8, strict_upper, preferred_element_type=F32)[0:1, :] * SEG_PAD
    where = seg0 + before
    pos0 = jnp.sum(jnp.where(hit1, where, 0.0), axis=-1, keepdims=True)
    pos1 = jnp.sum(jnp.where(hit2, where, 0.0), axis=-1, keepdims=True)
    pos = jnp.where(lane_i == 0, pos0, jnp.where(lane_i == 1, pos1, -1.0))
    pos_o[...] = pos
    nch_o[...] = jnp.broadcast_to(nch, (1, SUBLANES, LANES))
    pos_t = jnp.transpose(pos)
    p0 = pos_t[0:1, :]
    p1 = pos_t[1:2, :]
    for blk in range(N_PERM_BLOCKS):
        dst = (lax.broadcasted_iota(jnp.int32, (PERM_BLOCK, t), 0) + blk * PERM_BLOCK).astype(F32)
        perm = jnp.where((dst == p0) | (dst == p1), 1.0, 0.0).astype(BF16)
        xl_o[0, blk * PERM_BLOCK:(blk + 1) * PERM_BLOCK, :] = jnp.dot(
            perm, xn, preferred_element_type=F32).astype(BF16)


def _sort_call(h1, route, n2g):
    n = h1.shape[0]
    nt = n // SORT_TILE
    return pl.pallas_call(
        _sort_kernel,
        grid=(nt,),
        in_specs=[pl.BlockSpec((SORT_TILE, D_MODEL), lambda i: (i, 0)),
                  pl.BlockSpec((SORT_TILE, LANES), lambda i: (i, 0)),
                  pl.BlockSpec((1, D_MODEL), lambda i: (0, 0))],
        out_specs=[pl.BlockSpec((1, LOCAL_ROWS, D_MODEL), lambda i: (i, 0, 0)),
                   pl.BlockSpec((SORT_TILE, LANES), lambda i: (i, 0)),
                   pl.BlockSpec((1, SUBLANES, LANES), lambda i: (i, 0, 0))],
        out_shape=[jax.ShapeDtypeStruct((nt, LOCAL_ROWS, D_MODEL), BF16),
                   jax.ShapeDtypeStruct((n, LANES), F32),
                   jax.ShapeDtypeStruct((nt, SUBLANES, LANES), F32)],
        compiler_params=pltpu.CompilerParams(dimension_semantics=("arbitrary",), vmem_limit_bytes=VMEM_LIMIT),
        name="moe_sort",
    )(h1, route, n2g)


def _chunk_copy(src, src_row, dst, dst_row, sem):
    return pltpu.make_async_copy(src.at[pl.ds(src_row, SEG_PAD), :], dst.at[pl.ds(dst_row, SEG_PAD), :], sem)


def _expert_kernel(se_ref, nv_ref, row_ref, ok_ref, xl_in, wg, wu, wd, xl_io, xbuf, ybuf, sem_in, sem_out):
    del se_ref, xl_in
    s = pl.program_id(0)

    @pl.when(s < nv_ref[0])
    def _():
        base = s * CHUNKS_PER_STEP
        rows = [pl.multiple_of(row_ref[base + j], SEG_PAD) for j in range(CHUNKS_PER_STEP)]
        for j in range(CHUNKS_PER_STEP):
            _chunk_copy(xl_io, rows[j], xbuf, j * SEG_PAD, sem_in).start()
        for j in range(CHUNKS_PER_STEP):
            _chunk_copy(xl_io, rows[j], xbuf, j * SEG_PAD, sem_in).wait()
        xb = xbuf[...]
        a = jnp.dot(xb, wg[0], preferred_element_type=F32)
        b = jnp.dot(xb, wu[0], preferred_element_type=F32)
        hh = (a * _sigmoid(a) * b).astype(BF16)
        ybuf[...] = jnp.dot(hh, wd[0], preferred_element_type=F32).astype(BF16)
        for j in range(CHUNKS_PER_STEP):
            @pl.when(ok_ref[base + j] == 1)
            def _():
                _chunk_copy(ybuf, j * SEG_PAD, xl_io, rows[j], sem_out).start()
        for j in range(CHUNKS_PER_STEP):
            @pl.when(ok_ref[base + j] == 1)
            def _():
                _chunk_copy(ybuf, j * SEG_PAD, xl_io, rows[j], sem_out).wait()


def _expert_call(xl, step_expert, n_valid, chunk_row, chunk_ok, w_gate, w_up, w_down):
    n_steps = step_expert.shape[0]
    wmap = lambda s, se, *_: (se[s], 0, 0)
    grid_spec = pltpu.PrefetchScalarGridSpec(
        num_scalar_prefetch=4,
        grid=(n_steps,),
        in_specs=[pl.BlockSpec(memory_space=pl.ANY),
                  pl.BlockSpec((1, D_MODEL, D_EXPERT), wmap),
                  pl.BlockSpec((1, D_MODEL, D_EXPERT), wmap),
                  pl.BlockSpec((1, D_EXPERT, D_MODEL), wmap)],
        out_specs=pl.BlockSpec(memory_space=pl.ANY),
        scratch_shapes=[pltpu.VMEM((EXP_TILE, D_MODEL), BF16), pltpu.VMEM((EXP_TILE, D_MODEL), BF16),
                        pltpu.SemaphoreType.DMA(()), pltpu.SemaphoreType.DMA(())],
    )
    return pl.pallas_call(
        _expert_kernel,
        grid_spec=grid_spec,
        out_shape=jax.ShapeDtypeStruct(xl.shape, BF16),
        input_output_aliases={4: 0},
        compiler_params=pltpu.CompilerParams(dimension_semantics=("arbitrary",), has_side_effects=True),
        name="moe_experts",
    )(step_expert, n_valid, chunk_row, chunk_ok, xl, w_gate, w_up, w_down)


def _combine_kernel(final_norm, n_first, h1_ref, route_ref, pos_ref, fg, yl_ref, *outs):
    t = SORT_TILE
    r = route_ref[...]
    p = pos_ref[...]
    w1, w2 = r[:, 2:3], r[:, 3:4]
    pos0, pos1 = p[:, 0:1], p[:, 1:2]
    acc = h1_ref[...]
    for blk in range(N_PERM_BLOCKS):
        src = (lax.broadcasted_iota(jnp.int32, (t, PERM_BLOCK), 1) + blk * PERM_BLOCK).astype(F32)
        pw = jnp.where(src == pos0, w1, 0.0) + jnp.where(src == pos1, w2, 0.0)
        acc = acc + jnp.dot(pw.astype(BF16), yl_ref[0, blk * PERM_BLOCK:(blk + 1) * PERM_BLOCK, :],
                            preferred_element_type=F32)
    if not final_norm:
        outs[0][...] = acc
        return
    y = _rms(acc, fg[...])
    i = pl.program_id(0)

    @pl.when(i < n_first)
    def _():
        outs[0][...] = y

    @pl.when(i >= n_first)
    def _():
        outs[1][...] = y


def _combine_call(h1, route, pos, yl, final_g, final_norm, n_first_rows):
    n = h1.shape[0]
    nt = n // SORT_TILE
    n_first = n_first_rows // SORT_TILE
    tok = lambda i: (i, 0)
    if final_norm:
        out_shape = [jax.ShapeDtypeStruct((n_first_rows, D_MODEL), F32),
                     jax.ShapeDtypeStruct((n - n_first_rows, D_MODEL), F32)]
        out_specs = [pl.BlockSpec((SORT_TILE, D_MODEL), lambda i: (jnp.minimum(i, n_first - 1), 0)),
                     pl.BlockSpec((SORT_TILE, D_MODEL), lambda i: (jnp.maximum(i - n_first, 0), 0))]
    else:
        out_shape = [jax.ShapeDtypeStruct((n, D_MODEL), F32)]
        out_specs = [pl.BlockSpec((SORT_TILE, D_MODEL), tok)]
    return pl.pallas_call(
        functools.partial(_combine_kernel, final_norm, n_first),
        grid=(nt,),
        in_specs=[pl.BlockSpec((SORT_TILE, D_MODEL), tok),
                  pl.BlockSpec((SORT_TILE, LANES), tok),
                  pl.BlockSpec((SORT_TILE, LANES), tok),
                  pl.BlockSpec((1, D_MODEL), lambda i: (0, 0)),
                  pl.BlockSpec((1, LOCAL_ROWS, D_MODEL), lambda i: (i, 0, 0))],
        out_specs=out_specs,
        out_shape=out_shape,
        compiler_params=pltpu.CompilerParams(dimension_semantics=("arbitrary",), vmem_limit_bytes=VMEM_LIMIT),
        name="moe_combine",
    )(h1, route, pos, final_g, yl)


def _expert_tables(nch, n_steps):
    nt = nch.shape[0]
    cps = CHUNKS_PER_STEP
    seg_row0 = (jnp.cumsum(nch, axis=1) - nch) * SEG_PAD
    first = jnp.cumsum(nch, axis=0) - nch
    tot = jnp.sum(nch, axis=0)
    steps = (tot + cps - 1) // cps
    step_end = jnp.cumsum(steps)
    n_valid = step_end[-1:]
    s_ids = jnp.arange(n_steps, dtype=jnp.int32)
    step_expert = jnp.minimum(jnp.sum((step_end[None, :] <= s_ids[:, None]).astype(jnp.int32), axis=1),
                              N_EXPERTS - 1)
    sel = (step_expert[:, None] == jnp.arange(N_EXPERTS, dtype=jnp.int32)[None, :]).astype(jnp.int32)
    step0 = sel @ (step_end - steps)
    tot_s = sel @ tot
    first_s = sel @ first.T
    row0_s = sel @ seg_row0.T
    k = (s_ids - step0)[:, None] * cps + jnp.arange(cps, dtype=jnp.int32)[None, :]
    ok = (k < tot_s[:, None]) & (s_ids < n_valid[0])[:, None]
    tile = jnp.sum((first_s[:, None, :] <= k[:, :, None]).astype(jnp.int32), axis=2) - 1
    tsel = (tile[:, :, None] == jnp.arange(nt, dtype=jnp.int32)[None, None, :]).astype(jnp.int32)
    first_k = jnp.sum(tsel * first_s[:, None, :], axis=2)
    row0_k = jnp.sum(tsel * row0_s[:, None, :], axis=2)
    row = tile * LOCAL_ROWS + row0_k + (k - first_k) * SEG_PAD
    row = jnp.where(ok, row, row[:, 0:1])
    row = jnp.where((s_ids < n_valid[0])[:, None], row, 0)
    return step_expert, n_valid.astype(jnp.int32), row.reshape(-1).astype(jnp.int32), ok.reshape(-1).astype(jnp.int32)


def _moe(h1, route, n2g, w_gate, w_up, w_down, final_g, final_norm, n_first_rows):
    n = h1.shape[0]
    nt = n // SORT_TILE
    xl, pos, nch = _sort_call(h1, route, n2g)
    nch = nch[:, 0, :N_EXPERTS].astype(jnp.int32)
    max_chunks = (2 * n) // SEG_PAD + nt * N_EXPERTS
    n_steps = max_chunks // CHUNKS_PER_STEP + N_EXPERTS
    step_expert, n_valid, chunk_row, chunk_ok = _expert_tables(nch, n_steps)
    yl = _expert_call(xl.reshape(nt * LOCAL_ROWS, D_MODEL), step_expert, n_valid, chunk_row, chunk_ok,
                      w_gate, w_up, w_down)
    return _combine_call(h1, route, pos, yl.reshape(nt, LOCAL_ROWS, D_MODEL), final_g, final_norm, n_first_rows)


def _block_diag(pw):
    out = jnp.zeros((BRANCH_W, BRANCH_W), pw.dtype)
    for g in range(len(POOL_WINDOWS)):
        out = out.at[g * POOL_GW:(g + 1) * POOL_GW, g * POOL_GW:(g + 1) * POOL_GW].set(pw[g])
    return out


def kernel(x_prompt, x_sample, state_pool, state_conv, state_sconv, norm1_g, w_in, pool_w, pool_scale, conf_dw, conf_dw_b, conf_ln_g, conf_ln_b, sconv_w, sgu_ln_g, sgu_ln_b, sgu_ws, sgu_b, w_branch, w_out, norm2_g, router_g, router_g_b, router_e, router_e_b, w_gate, w_up, w_down, final_g):
    depth = w_in.shape[0]
    bp, seq, _ = x_prompt.shape
    bs, dseq, _ = x_sample.shape
    n_p, n_s = bp * seq, bs * dseq
    n = n_p + n_s
    past_len = 16384
    assert seq % MIX_TILE == 0 and n_s % MIX_TILE == 0 and MIX_TILE % CHUNK == 0 and dseq == SUBLANES

    cfg_p = MixCfg(nb=1, t=MIX_TILE, has_state=False, start_pos=0, row_off=0, v_rows=CHUNK)
    cfg_s = MixCfg(nb=MIX_TILE // dseq, t=dseq, has_state=True, start_pos=past_len,
                   row_off=n_p // MIX_TILE, v_rows=dseq)

    row = lambda a: a.reshape(1, -1)
    tril = jnp.tril(jnp.ones((CHUNK, CHUNK), F32))
    eye_blk = jnp.kron(jnp.eye(CHUNK // dseq, dtype=F32), jnp.ones((dseq, dseq), F32))
    final_row = row(final_g)

    h_p, h_s = x_prompt.reshape(n_p, D_MODEL), x_sample.reshape(n_s, D_MODEL)
    off_p, off_s = 0, 0
    states_out = []
    for l in range(depth):
        ws_p = sgu_ws[l] * tril[None]
        ws_small = jnp.tile(sgu_ws[l][:, :dseq, :dseq], (1, CHUNK // dseq, CHUNK // dseq))
        ws_s = ws_small * (tril * eye_blk)[None]
        cat = lambda w: jnp.concatenate([w[h] for h in range(SGU_HEADS)], axis=1).astype(BF16)
        bias_p = jnp.repeat(sgu_b[l].T, SGU_HW, axis=1)
        bias_s = jnp.tile(jnp.repeat(sgu_b[l][:, :dseq].T, SGU_HW, axis=1), (CHUNK // dseq, 1))
        w_rt = jnp.zeros((D_MODEL, LANES), F32)
        w_rt = w_rt.at[:, :N_EXPERTS].set(router_e[l]).at[:, ROUTE_GROUP_LANE0:ROUTE_GROUP_LANE0 + N_GROUPS].set(router_g[l])
        b_rt = jnp.zeros((1, LANES), F32)
        b_rt = b_rt.at[0, :N_EXPERTS].set(router_e_b[l]).at[0, ROUTE_GROUP_LANE0:ROUTE_GROUP_LANE0 + N_GROUPS].set(router_g_b[l])

        def weights(wcat, sbias):
            return [row(norm1_g[l]), w_in[l].astype(BF16), _block_diag(pool_w[l]).astype(BF16), row(pool_scale[l]),
                    conf_dw[l], row(conf_dw_b[l]), row(conf_ln_g[l]), row(conf_ln_b[l]), sconv_w[l],
                    row(sgu_ln_g[l]), row(sgu_ln_b[l]), wcat, sbias,
                    w_branch[l].astype(BF16), w_out[l].astype(BF16), row(norm2_g[l]), w_rt, b_rt]

        h1, route, pool_p, conv_p, sconv_p, v_p = _mixer_call(
            cfg_p, bp, n, h_p, off_p, None, weights(cat(ws_p), bias_p), None)
        h1, route, pool_s, conv_s, sconv_s, v_s = _mixer_call(
            cfg_s, bs, n, h_s, off_s, (state_pool[l], state_conv[l], state_sconv[l]),
            weights(cat(ws_s), bias_s), (h1, route))
        states_out.append((pool_p, conv_p, sconv_p, v_p, pool_s, conv_s, sconv_s, v_s))

        outs = _moe(h1, route, row(norm2_g[l]), w_gate[l].astype(BF16), w_up[l].astype(BF16),
                    w_down[l].astype(BF16), final_row, l == depth - 1, n_p)
        h_p = h_s = outs[0]
        off_p, off_s = 0, n_p // MIX_TILE

    y_prompt = outs[0].reshape(bp, seq, D_MODEL)
    y_sample = outs[1].reshape(bs, dseq, D_MODEL)
    st = [jnp.stack([s[k] for s in states_out]) for k in range(8)]
    return (y_prompt, y_sample, st[0], st[1], st[2], st[3], st[4], st[5], st[6], st[7])
```

```python
import functools
from typing import NamedTuple

import jax
import jax.numpy as jnp
from jax import lax
from jax.experimental import pallas as pl
from jax.experimental.pallas import tpu as pltpu

F32 = jnp.float32
BF16 = jnp.bfloat16

D_MODEL = 1024
BRANCH_W = 256
N_BRANCH = 4
POOL_WINDOWS = (2, 4, 8, 16)
POOL_GW = 64
POOL_HIST = 15
CONF_WIDTH = 31
CONF_HIST = 30
SCONV_WIDTH = 3
SCONV_HIST = 2
CHUNK = 128
SGU_HEADS = 4
SGU_HW = 64
N_GROUPS = 4
EXPERTS_PER_GROUP = 8
N_EXPERTS = 32
D_EXPERT = 256
RMS_EPS = 1e-6
LN_EPS = 1e-5
IN_COLS = 6144
GATE_COL0 = 2048

LANES = 128
SUBLANES = 8
POOL_PAD = 16
CONF_PAD = 32
SCONV_PAD = 8
ROW_BLOCK = 32
ROUTE_GROUP_LANE0 = 32
NEG_BIG = -3.0e38

MIX_TILE = 256
SORT_TILE = 512
SEG_PAD = 16
PERM_BLOCK = 256
N_PERM_BLOCKS = -(-(2 * SORT_TILE + N_EXPERTS * (SEG_PAD - 1)) // PERM_BLOCK)
LOCAL_ROWS = N_PERM_BLOCKS * PERM_BLOCK
EXP_TILE = 256
CHUNKS_PER_STEP = EXP_TILE // SEG_PAD
VMEM_LIMIT = 56 * 1024 * 1024


class MixCfg(NamedTuple):
    nb: int
    t: int
    has_state: bool
    start_pos: int
    row_off: int
    v_rows: int


def _rms(x, g):
    return x * lax.rsqrt(jnp.mean(x * x, axis=-1, keepdims=True) + RMS_EPS) * g


def _ln(x, g, b):
    mu = jnp.mean(x, axis=-1, keepdims=True)
    xc = x - mu
    return xc * lax.rsqrt(jnp.mean(xc * xc, axis=-1, keepdims=True) + LN_EPS) * g + b


def _sigmoid(x):
    return 1.0 / (1.0 + jnp.exp(-x))


def _gelu_tanh(x):
    return 0.5 * x * (1.0 + jnp.tanh(0.7978845608028654 * (x + 0.044715 * (x * x * x))))


def _row_blocks(nb, t):
    if t >= ROW_BLOCK:
        return [(slice(b, b + 1), t0, ROW_BLOCK) for b in range(nb) for t0 in range(0, t, ROW_BLOCK)]
    bb = ROW_BLOCK // t
    return [(slice(b0, b0 + bb), 0, t) for b0 in range(0, nb, bb)]


def _mixer_kernel(cfg, *refs):
    nb, t, tm = cfg.nb, cfg.t, cfg.nb * cfg.t
    refs = list(refs)
    x_ref = refs.pop(0)
    if cfg.has_state:
        pool_st, conv_st, sconv_st = refs.pop(0), refs.pop(0), refs.pop(0)
    (n1g, w_in, pool_w, pool_sc, cdw, cdb, clg, clb, scw, slg, slb, wcat, sbias,
     w_br, w_out, n2g, w_rt, b_rt) = refs[:18]
    refs = refs[18:]
    if cfg.has_state:
        refs = refs[2:]
    h1_o, route_o, pool_o, conv_o, sconv_o, v_o = refs[:6]
    pool_ext, conv_ext, sc_ext, buf_a, buf_b, buf_c = refs[6:]

    if cfg.has_state:
        seq_pos0 = cfg.start_pos
        pool_ext[:, POOL_PAD - POOL_HIST:POOL_PAD, :] = pool_st[...]
        conv_ext[:, CONF_PAD - CONF_HIST:CONF_PAD, :] = conv_st[...]
        sc_ext[:, SCONV_PAD - SCONV_HIST:SCONV_PAD, :] = sconv_st[...]
    else:
        c = pl.program_id(1)
        seq_pos0 = cfg.start_pos + c * t

        @pl.when(c == 0)
        def _():
            pool_ext[:, 0:POOL_PAD, :] = jnp.zeros((nb, POOL_PAD, BRANCH_W), F32)
            conv_ext[:, 0:CONF_PAD, :] = jnp.zeros((nb, CONF_PAD, BRANCH_W), F32)
            sc_ext[:, 0:SCONV_PAD, :] = jnp.zeros((nb, SCONV_PAD, BRANCH_W), F32)

    x = x_ref[...]
    xb = _rms(x, n1g[...]).astype(BF16)

    def proj(lo, hi):
        return jnp.dot(xb, w_in[:, lo:hi], preferred_element_type=F32)

    a_pool = proj(0, 256)
    pool_ext[:, POOL_PAD:, :] = a_pool.reshape(nb, t, BRANCH_W)
    glu = proj(256, 512) * _sigmoid(proj(512, 768))
    conv_ext[:, CONF_PAD:, :] = glu.reshape(nb, t, BRANCH_W)
    z = proj(1024, 1280) * proj(1280, 1536)
    sc_ext[:, SCONV_PAD:, :] = z.reshape(nb, t, BRANCH_W)

    for bs, t0, tb in _row_blocks(nb, t):
        bb = bs.stop - bs.start
        shp = (bb, tb, BRANCH_W)
        lane = lax.broadcasted_iota(jnp.int32, shp, 2)
        pos = seq_pos0 + t0 + lax.broadcasted_iota(jnp.int32, shp, 1)
        cur = pool_ext[bs, POOL_PAD + t0:POOL_PAD + t0 + tb, :]
        acc = cur
        sums = []
        for j in range(1, POOL_WINDOWS[-1]):
            acc = acc + pool_ext[bs, POOL_PAD + t0 - j:POOL_PAD + t0 - j + tb, :]
            if j + 1 in POOL_WINDOWS:
                sums.append(acc)
        s2, s4, s8, s16 = sums
        win_sum = jnp.where(lane < 64, s2, jnp.where(lane < 128, s4, jnp.where(lane < 192, s8, s16)))
        win = jnp.where(lane < 64, 2, jnp.where(lane < 128, 4, jnp.where(lane < 192, 8, 16)))
        cnt = jnp.minimum(pos + 1, win).astype(F32)
        buf_a[bs, t0:t0 + tb, :] = win_sum / cnt - cur
        acc = jnp.broadcast_to(cdb[...].reshape(1, 1, BRANCH_W), shp)
        base = CONF_PAD - CONF_HIST + t0
        for k in range(CONF_WIDTH):
            acc = acc + conv_ext[bs, base + k:base + k + tb, :] * cdw[k:k + 1, :].reshape(1, 1, BRANCH_W)
        buf_b[bs, t0:t0 + tb, :] = acc
        base = SCONV_PAD - SCONV_HIST + t0
        acc = sc_ext[bs, base:base + tb, :] * scw[0:1, :].reshape(1, 1, BRANCH_W)
        for k in range(1, SCONV_WIDTH):
            acc = acc + sc_ext[bs, base + k:base + k + tb, :] * scw[k:k + 1, :].reshape(1, 1, BRANCH_W)
        buf_c[bs, t0:t0 + tb, :] = acc

    def write_state():
        pool_o[...] = pool_ext[:, POOL_PAD + t - POOL_HIST:POOL_PAD + t, :]
        conv_o[...] = conv_ext[:, CONF_PAD + t - CONF_HIST:CONF_PAD + t, :]
        sconv_o[...] = sc_ext[:, SCONV_PAD + t - SCONV_HIST:SCONV_PAD + t, :]

    if cfg.has_state:
        write_state()
    else:
        pl.when(c == pl.num_programs(1) - 1)(write_state)
        pool_ext[:, 0:POOL_PAD, :] = pool_ext[:, t:t + POOL_PAD, :]
        conv_ext[:, 0:CONF_PAD, :] = conv_ext[:, t:t + CONF_PAD, :]
        sc_ext[:, 0:SCONV_PAD, :] = sc_ext[:, t:t + SCONV_PAD, :]

    pooled = buf_a[...].reshape(tm, BRANCH_W).astype(BF16)
    br_a = jnp.dot(pooled, pool_w[...], preferred_element_type=F32) * pool_sc[...]

    cb = _ln(buf_b[...].reshape(tm, BRANCH_W), clg[...], clb[...])
    br_b = cb * _sigmoid(cb)

    br_c = proj(768, 1024) * buf_c[...].reshape(tm, BRANCH_W)

    gu = _gelu_tanh(proj(1536, 1792))
    v = _ln(_gelu_tanh(proj(1792, 2048)), slg[...], slb[...])
    if cfg.has_state:
        v_o[...] = v.reshape(nb, t, BRANCH_W)
    else:
        @pl.when(c == pl.num_programs(1) - 1)
        def _():
            v_o[...] = v[tm - cfg.v_rows:, :].reshape(1, cfg.v_rows, BRANCH_W)
    head = lax.broadcasted_iota(jnp.int32, (CHUNK, BRANCH_W), 1) // SGU_HW
    mixed = []
    for j in range(tm // CHUNK):
        vj = v[j * CHUNK:(j + 1) * CHUNK, :]
        stacked = jnp.concatenate([jnp.where(head == h, vj, 0.0) for h in range(SGU_HEADS)], axis=0)
        mixed.append(jnp.dot(wcat[...], stacked.astype(BF16), preferred_element_type=F32) + sbias[...])
    br_d = gu * jnp.concatenate(mixed, axis=0)

    merged = None
    for i, br in enumerate((br_a, br_b, br_c, br_d)):
        gate = _sigmoid(proj(GATE_COL0 + i * D_MODEL, GATE_COL0 + (i + 1) * D_MODEL))
        term = gate * jnp.dot(br.astype(BF16), w_br[i], preferred_element_type=F32)
        merged = term if merged is None else merged + term
    h1 = x + jnp.dot(merged.astype(BF16), w_out[...], preferred_element_type=F32)
    h1_o[...] = h1

    xn2 = _rms(h1, n2g[...])
    logits = jnp.dot(xn2, w_rt[...], preferred_element_type=F32, precision=lax.Precision.HIGHEST) + b_rt[...]
    lane = lax.broadcasted_iota(jnp.int32, (tm, LANES), 1)
    is_g = (lane >= ROUTE_GROUP_LANE0) & (lane < ROUTE_GROUP_LANE0 + N_GROUPS)
    glog = jnp.where(is_g, logits, NEG_BIG)
    gmax = jnp.max(glog, axis=-1, keepdims=True)
    lane_f = lane.astype(F32)
    no_lane = float(4 * LANES)
    gsel = jnp.min(jnp.where(glog == gmax, lane_f, no_lane), axis=-1, keepdims=True) - ROUTE_GROUP_LANE0
    pg = 1.0 / jnp.sum(jnp.where(is_g, jnp.exp(glog - gmax), 0.0), axis=-1, keepdims=True)
    in_grp = (lane < N_EXPERTS) & ((lane // EXPERTS_PER_GROUP) == gsel.astype(jnp.int32))
    el = jnp.where(in_grp, logits, NEG_BIG)
    m1 = jnp.max(el, axis=-1, keepdims=True)
    i1 = jnp.min(jnp.where(in_grp & (el == m1), lane_f, no_lane), axis=-1, keepdims=True)
    rest = in_grp & (lane_f != i1)
    el2 = jnp.where(rest, logits, NEG_BIG)
    m2 = jnp.max(el2, axis=-1, keepdims=True)
    i2 = jnp.min(jnp.where(rest & (el2 == m2), lane_f, no_lane), axis=-1, keepdims=True)
    e21 = jnp.exp(m2 - m1)
    w1 = pg / (1.0 + e21)
    w2 = pg * e21 / (1.0 + e21)
    route_o[...] = jnp.where(lane == 0, i1, jnp.where(lane == 1, i2,
                             jnp.where(lane == 2, w1, jnp.where(lane == 3, w2, 0.0))))


def _const_spec(shape):
    nd = len(shape)
    return pl.BlockSpec(shape, lambda *_: (0,) * nd, pipeline_mode=pl.Buffered(1))


def _mixer_call(cfg, n_seq, n_rows_total, x2d, x_tile_off, states, weights, aliased):
    nb, t = cfg.nb, cfg.t
    tm = nb * t
    if cfg.has_state:
        grid = (n_seq // nb,)
        tok = lambda i: (x_tile_off + i, 0)
        out_tok = lambda i: (cfg.row_off + i, 0)
        seq3 = lambda i: (i, 0, 0)
    else:
        grid = (n_seq, 2048 // t)
        nt = grid[1]
        tok = lambda b, c: (x_tile_off + b * nt + c, 0)
        out_tok = lambda b, c: (cfg.row_off + b * nt + c, 0)
        seq3 = lambda b, c: (b, 0, 0)

    in_specs = [pl.BlockSpec((tm, D_MODEL), tok)]
    args = [x2d]
    if cfg.has_state:
        for s in states:
            in_specs.append(pl.BlockSpec((nb,) + s.shape[1:], seq3))
            args.append(s)
    for w in weights:
        in_specs.append(_const_spec(w.shape))
        args.append(w)
    io_alias = {}
    if cfg.has_state:
        for k, a in enumerate(aliased):
            in_specs.append(pl.BlockSpec(memory_space=pl.ANY))
            io_alias[len(args)] = k
            args.append(a)

    out_shape = [
        jax.ShapeDtypeStruct((n_rows_total, D_MODEL), F32),
        jax.ShapeDtypeStruct((n_rows_total, LANES), F32),
        jax.ShapeDtypeStruct((n_seq, POOL_HIST, BRANCH_W), F32),
        jax.ShapeDtypeStruct((n_seq, CONF_HIST, BRANCH_W), F32),
        jax.ShapeDtypeStruct((n_seq, SCONV_HIST, BRANCH_W), F32),
        jax.ShapeDtypeStruct((n_seq, cfg.v_rows, BRANCH_W), F32),
    ]
    out_specs = [
        pl.BlockSpec((tm, D_MODEL), out_tok),
        pl.BlockSpec((tm, LANES), out_tok),
        pl.BlockSpec((nb, POOL_HIST, BRANCH_W), seq3),
        pl.BlockSpec((nb, CONF_HIST, BRANCH_W), seq3),
        pl.BlockSpec((nb, SCONV_HIST, BRANCH_W), seq3),
        pl.BlockSpec((nb, cfg.v_rows, BRANCH_W), seq3),
    ]
    scratch = [
        pltpu.VMEM((nb, POOL_PAD + t, BRANCH_W), F32),
        pltpu.VMEM((nb, CONF_PAD + t, BRANCH_W), F32),
        pltpu.VMEM((nb, SCONV_PAD + t, BRANCH_W), F32),
        pltpu.VMEM((nb, t, BRANCH_W), F32),
        pltpu.VMEM((nb, t, BRANCH_W), F32),
        pltpu.VMEM((nb, t, BRANCH_W), F32),
    ]
    return pl.pallas_call(
        functools.partial(_mixer_kernel, cfg),
        grid=grid,
        in_specs=in_specs,
        out_specs=out_specs,
        out_shape=out_shape,
        scratch_shapes=scratch,
        input_output_aliases=io_alias,
        compiler_params=pltpu.CompilerParams(
            dimension_semantics=("arbitrary",) * len(grid), vmem_limit_bytes=VMEM_LIMIT),
        name="mixer_sample" if cfg.has_state else "mixer_prompt",
    )(*args)


def _sort_kernel(h1_ref, route_ref, n2g, xl_o, pos_o, nch_o):
    t = SORT_TILE
    xn = _rms(h1_ref[...], n2g[...]).astype(BF16)
    r = route_ref[...]
    lane_i = lax.broadcasted_iota(jnp.int32, (t, LANES), 1)
    lane = lane_i.astype(F32)
    hit1 = lane == r[:, 0:1]
    hit2 = lane == r[:, 1:2]
    onehot = jnp.where(hit1 | hit2, 1.0, 0.0)
    row = lax.broadcasted_iota(jnp.int32, (t, t), 0)
    col = lax.broadcasted_iota(jnp.int32, (t, t), 1)
    strict_lower = jnp.where(col < row, 1.0, 0.0).astype(BF16)
    before = jnp.dot(strict_lower, onehot.astype(BF16), preferred_element_type=F32)
    cnt = jnp.sum(onehot, axis=0, keepdims=True)
    nch = jnp.floor((cnt + (SEG_PAD - 1)) * (1.0 / SEG_PAD))
    er = lax.broadcasted_iota(jnp.int32, (LANES, LANES), 0)
    ec = lax.broadcasted_iota(jnp.int32, (LANES, LANES), 1)
    strict_upper = jnp.where(er < ec, 1.0, 0.0).astype(BF16)
    nch8 = jnp.broadcast_to(nch, (SUBLANES, LANES)).astype(BF16)
    seg0 = jnp.dot(nch8, strict_upper, preferred_element_type=F32)[0:1, :] * SEG_PAD
    where = seg0 + before
    pos0 = jnp.sum(jnp.where(hit1, where, 0.0), axis=-1, keepdims=True)
    pos1 = jnp.sum(jnp.where(hit2, where, 0.0), axis=-1, keepdims=True)
    pos = jnp.where(lane_i == 0, pos0, jnp.where(lane_i == 1, pos1, -1.0))
    pos_o[...] = pos
    nch_o[...] = jnp.broadcast_to(nch, (1, SUBLANES, LANES))
    pos_t = jnp.transpose(pos)
    p0 = pos_t[0:1, :]
    p1 = pos_t[1:2, :]
    for blk in range(N_PERM_BLOCKS):
        dst = (lax.broadcasted_iota(jnp.int32, (PERM_BLOCK, t), 0) + blk * PERM_BLOCK).astype(F32)
        perm = jnp.where((dst == p0) | (dst == p1), 1.0, 0.0).astype(BF16)
        xl_o[0, blk * PERM_BLOCK:(blk + 1) * PERM_BLOCK, :] = jnp.dot(
            perm, xn, preferred_element_type=F32).astype(BF16)


def _sort_call(h1, route, n2g):
    n = h1.shape[0]
    nt = n // SORT_TILE
    return pl.pallas_call(
        _sort_kernel,
        grid=(nt,),
        in_specs=[pl.BlockSpec((SORT_TILE, D_MODEL), lambda i: (i, 0)),
                  pl.BlockSpec((SORT_TILE, LANES), lambda i: (i, 0)),
                  pl.BlockSpec((1, D_MODEL), lambda i: (0, 0))],
        out_specs=[pl.BlockSpec((1, LOCAL_ROWS, D_MODEL), lambda i: (i, 0, 0)),
                   pl.BlockSpec((SORT_TILE, LANES), lambda i: (i, 0)),
                   pl.BlockSpec((1, SUBLANES, LANES), lambda i: (i, 0, 0))],
        out_shape=[jax.ShapeDtypeStruct((nt, LOCAL_ROWS, D_MODEL), BF16),
                   jax.ShapeDtypeStruct((n, LANES), F32),
                   jax.ShapeDtypeStruct((nt, SUBLANES, LANES), F32)],
        compiler_params=pltpu.CompilerParams(dimension_semantics=("arbitrary",), vmem_limit_bytes=VMEM_LIMIT),
        name="moe_sort",
    )(h1, route, n2g)


def _chunk_copy(src, src_row, dst, dst_row, sem):
    return pltpu.make_async_copy(src.at[pl.ds(src_row, SEG_PAD), :], dst.at[pl.ds(dst_row, SEG_PAD), :], sem)


def _expert_kernel(se_ref, nv_ref, row_ref, ok_ref, xl_in, wg, wu, wd, xl_io, xbuf, ybuf, sem_in, sem_out):
    del se_ref, xl_in
    s = pl.program_id(0)
    nv = nv_ref[0]

    def row(step, j):
        return pl.multiple_of(row_ref[step * CHUNKS_PER_STEP + j], SEG_PAD)

    def gather(step, slot):
        return [_chunk_copy(xl_io, row(step, j), xbuf.at[slot], j * SEG_PAD, sem_in.at[slot])
                for j in range(CHUNKS_PER_STEP)]

    def for_real_chunks(step, slot, fn):
        for j in range(CHUNKS_PER_STEP):
            @pl.when(ok_ref[step * CHUNKS_PER_STEP + j] == 1)
            def _():
                fn(_chunk_copy(ybuf.at[slot], j * SEG_PAD, xl_io, row(step, j), sem_out.at[slot]))

    slot = s % 2

    @pl.when(s == 0)
    def _():
        for c in gather(0, 0):
            c.start()

    @pl.when(s + 1 < nv)
    def _():
        for c in gather(s + 1, 1 - slot):
            c.start()

    @pl.when(s < nv)
    def _():
        for c in gather(s, slot):
            c.wait()

        @pl.when(s >= 2)
        def _():
            for_real_chunks(s - 2, slot, lambda c: c.wait())

        xb = xbuf[slot]
        a = jnp.dot(xb, wg[0], preferred_element_type=F32)
        b = jnp.dot(xb, wu[0], preferred_element_type=F32)
        hh = (a * _sigmoid(a) * b).astype(BF16)
        ybuf[slot] = jnp.dot(hh, wd[0], preferred_element_type=F32).astype(BF16)
        for_real_chunks(s, slot, lambda c: c.start())

        @pl.when(s == nv - 1)
        def _():
            @pl.when(s >= 1)
            def _():
                for_real_chunks(s - 1, 1 - slot, lambda c: c.wait())
            for_real_chunks(s, slot, lambda c: c.wait())


def _expert_call(xl, step_expert, n_valid, chunk_row, chunk_ok, w_gate, w_up, w_down):
    n_steps = step_expert.shape[0]
    wmap = lambda s, se, *_: (se[s], 0, 0)
    grid_spec = pltpu.PrefetchScalarGridSpec(
        num_scalar_prefetch=4,
        grid=(n_steps,),
        in_specs=[pl.BlockSpec(memory_space=pl.ANY),
                  pl.BlockSpec((1, D_MODEL, D_EXPERT), wmap),
                  pl.BlockSpec((1, D_MODEL, D_EXPERT), wmap),
                  pl.BlockSpec((1, D_EXPERT, D_MODEL), wmap)],
        out_specs=pl.BlockSpec(memory_space=pl.ANY),
        scratch_shapes=[pltpu.VMEM((2, EXP_TILE, D_MODEL), BF16), pltpu.VMEM((2, EXP_TILE, D_MODEL), BF16),
                        pltpu.SemaphoreType.DMA((2,)), pltpu.SemaphoreType.DMA((2,))],
    )
    return pl.pallas_call(
        _expert_kernel,
        grid_spec=grid_spec,
        out_shape=jax.ShapeDtypeStruct(xl.shape, BF16),
        input_output_aliases={4: 0},
        compiler_params=pltpu.CompilerParams(dimension_semantics=("arbitrary",), has_side_effects=True),
        name="moe_experts",
    )(step_expert, n_valid, chunk_row, chunk_ok, xl, w_gate, w_up, w_down)


def _combine_kernel(final_norm, n_first, h1_ref, route_ref, pos_ref, fg, yl_ref, *outs):
    t = SORT_TILE
    r = route_ref[...]
    p = pos_ref[...]
    w1, w2 = r[:, 2:3], r[:, 3:4]
    pos0, pos1 = p[:, 0:1], p[:, 1:2]
    acc = h1_ref[...]
    for blk in range(N_PERM_BLOCKS):
        src = (lax.broadcasted_iota(jnp.int32, (t, PERM_BLOCK), 1) + blk * PERM_BLOCK).astype(F32)
        pw = jnp.where(src == pos0, w1, 0.0) + jnp.where(src == pos1, w2, 0.0)
        acc = acc + jnp.dot(pw.astype(BF16), yl_ref[0, blk * PERM_BLOCK:(blk + 1) * PERM_BLOCK, :],
                            preferred_element_type=F32)
    if not final_norm:
        outs[0][...] = acc
        return
    y = _rms(acc, fg[...])
    i = pl.program_id(0)

    @pl.when(i < n_first)
    def _():
        outs[0][...] = y

    @pl.when(i >= n_first)
    def _():
        outs[1][...] = y


def _combine_call(h1, route, pos, yl, final_g, final_norm, n_first_rows):
    n = h1.shape[0]
    nt = n // SORT_TILE
    n_first = n_first_rows // SORT_TILE
    tok = lambda i: (i, 0)
    if final_norm:
        out_shape = [jax.ShapeDtypeStruct((n_first_rows, D_MODEL), F32),
                     jax.ShapeDtypeStruct((n - n_first_rows, D_MODEL), F32)]
        out_specs = [pl.BlockSpec((SORT_TILE, D_MODEL), lambda i: (jnp.minimum(i, n_first - 1), 0)),
                     pl.BlockSpec((SORT_TILE, D_MODEL), lambda i: (jnp.maximum(i - n_first, 0), 0))]
    else:
        out_shape = [jax.ShapeDtypeStruct((n, D_MODEL), F32)]
        out_specs = [pl.BlockSpec((SORT_TILE, D_MODEL), tok)]
    return pl.pallas_call(
        functools.partial(_combine_kernel, final_norm, n_first),
        grid=(nt,),
        in_specs=[pl.BlockSpec((SORT_TILE, D_MODEL), tok),
                  pl.BlockSpec((SORT_TILE, LANES), tok),
                  pl.BlockSpec((SORT_TILE, LANES), tok),
                  pl.BlockSpec((1, D_MODEL), lambda i: (0, 0)),
                  pl.BlockSpec((1, LOCAL_ROWS, D_MODEL), lambda i: (i, 0, 0))],
        out_specs=out_specs,
        out_shape=out_shape,
        compiler_params=pltpu.CompilerParams(dimension_semantics=("arbitrary",), vmem_limit_bytes=VMEM_LIMIT),
        name="moe_combine",
    )(h1, route, pos, final_g, yl)


def _expert_tables(nch, n_steps):
    nt = nch.shape[0]
    cps = CHUNKS_PER_STEP
    seg_row0 = (jnp.cumsum(nch, axis=1) - nch) * SEG_PAD
    first = jnp.cumsum(nch, axis=0) - nch
    tot = jnp.sum(nch, axis=0)
    steps = (tot + cps - 1) // cps
    step_end = jnp.cumsum(steps)
    n_valid = step_end[-1:]
    s_ids = jnp.arange(n_steps, dtype=jnp.int32)
    step_expert = jnp.minimum(jnp.sum((step_end[None, :] <= s_ids[:, None]).astype(jnp.int32), axis=1),
                              N_EXPERTS - 1)
    sel = (step_expert[:, None] == jnp.arange(N_EXPERTS, dtype=jnp.int32)[None, :]).astype(jnp.int32)
    step0 = sel @ (step_end - steps)
    tot_s = sel @ tot
    first_s = sel @ first.T
    row0_s = sel @ seg_row0.T
    k = (s_ids - step0)[:, None] * cps + jnp.arange(cps, dtype=jnp.int32)[None, :]
    ok = (k < tot_s[:, None]) & (s_ids < n_valid[0])[:, None]
    tile = jnp.sum((first_s[:, None, :] <= k[:, :, None]).astype(jnp.int32), axis=2) - 1
    tsel = (tile[:, :, None] == jnp.arange(nt, dtype=jnp.int32)[None, None, :]).astype(jnp.int32)
    first_k = jnp.sum(tsel * first_s[:, None, :], axis=2)
    row0_k = jnp.sum(tsel * row0_s[:, None, :], axis=2)
    row = tile * LOCAL_ROWS + row0_k + (k - first_k) * SEG_PAD
    row = jnp.where(ok, row, row[:, 0:1])
    row = jnp.where((s_ids < n_valid[0])[:, None], row, 0)
    return step_expert, n_valid.astype(jnp.int32), row.reshape(-1).astype(jnp.int32), ok.reshape(-1).astype(jnp.int32)


def _moe(h1, route, n2g, w_gate, w_up, w_down, final_g, final_norm, n_first_rows):
    n = h1.shape[0]
    nt = n // SORT_TILE
    xl, pos, nch = _sort_call(h1, route, n2g)
    nch = nch[:, 0, :N_EXPERTS].astype(jnp.int32)
    max_chunks = (2 * n) // SEG_PAD + nt * N_EXPERTS
    n_steps = max_chunks // CHUNKS_PER_STEP + N_EXPERTS
    step_expert, n_valid, chunk_row, chunk_ok = _expert_tables(nch, n_steps)
    yl = _expert_call(xl.reshape(nt * LOCAL_ROWS, D_MODEL), step_expert, n_valid, chunk_row, chunk_ok,
                      w_gate, w_up, w_down)
    return _combine_call(h1, route, pos, yl.reshape(nt, LOCAL_ROWS, D_MODEL), final_g, final_norm, n_first_rows)


def _block_diag(pw):
    out = jnp.zeros((BRANCH_W, BRANCH_W), pw.dtype)
    for g in range(len(POOL_WINDOWS)):
        out = out.at[g * POOL_GW:(g + 1) * POOL_GW, g * POOL_GW:(g + 1) * POOL_GW].set(pw[g])
    return out


def kernel(x_prompt, x_sample, state_pool, state_conv, state_sconv, norm1_g, w_in, pool_w, pool_scale, conf_dw, conf_dw_b, conf_ln_g, conf_ln_b, sconv_w, sgu_ln_g, sgu_ln_b, sgu_ws, sgu_b, w_branch, w_out, norm2_g, router_g, router_g_b, router_e, router_e_b, w_gate, w_up, w_down, final_g):
    depth = w_in.shape[0]
    bp, seq, _ = x_prompt.shape
    bs, dseq, _ = x_sample.shape
    n_p, n_s = bp * seq, bs * dseq
    n = n_p + n_s
    past_len = 16384
    assert seq % MIX_TILE == 0 and n_s % MIX_TILE == 0 and MIX_TILE % CHUNK == 0 and dseq == SUBLANES

    cfg_p = MixCfg(nb=1, t=MIX_TILE, has_state=False, start_pos=0, row_off=0, v_rows=CHUNK)
    cfg_s = MixCfg(nb=MIX_TILE // dseq, t=dseq, has_state=True, start_pos=past_len,
                   row_off=n_p // MIX_TILE, v_rows=dseq)

    row = lambda a: a.reshape(1, -1)
    tril = jnp.tril(jnp.ones((CHUNK, CHUNK), F32))
    eye_blk = jnp.kron(jnp.eye(CHUNK // dseq, dtype=F32), jnp.ones((dseq, dseq), F32))
    final_row = row(final_g)

    h_p, h_s = x_prompt.reshape(n_p, D_MODEL), x_sample.reshape(n_s, D_MODEL)
    off_p, off_s = 0, 0
    states_out = []
    for l in range(depth):
        ws_p = sgu_ws[l] * tril[None]
        ws_small = jnp.tile(sgu_ws[l][:, :dseq, :dseq], (1, CHUNK // dseq, CHUNK // dseq))
        ws_s = ws_small * (tril * eye_blk)[None]
        cat = lambda w: jnp.concatenate([w[h] for h in range(SGU_HEADS)], axis=1).astype(BF16)
        bias_p = jnp.repeat(sgu_b[l].T, SGU_HW, axis=1)
        bias_s = jnp.tile(jnp.repeat(sgu_b[l][:, :dseq].T, SGU_HW, axis=1), (CHUNK // dseq, 1))
        w_rt = jnp.zeros((D_MODEL, LANES), F32)
        w_rt = w_rt.at[:, :N_EXPERTS].set(router_e[l]).at[:, ROUTE_GROUP_LANE0:ROUTE_GROUP_LANE0 + N_GROUPS].set(router_g[l])
        b_rt = jnp.zeros((1, LANES), F32)
        b_rt = b_rt.at[0, :N_EXPERTS].set(router_e_b[l]).at[0, ROUTE_GROUP_LANE0:ROUTE_GROUP_LANE0 + N_GROUPS].set(router_g_b[l])

        def weights(wcat, sbias):
            return [row(norm1_g[l]), w_in[l].astype(BF16), _block_diag(pool_w[l]).astype(BF16), row(pool_scale[l]),
                    conf_dw[l], row(conf_dw_b[l]), row(conf_ln_g[l]), row(conf_ln_b[l]), sconv_w[l],
                    row(sgu_ln_g[l]), row(sgu_ln_b[l]), wcat, sbias,
                    w_branch[l].astype(BF16), w_out[l].astype(BF16), row(norm2_g[l]), w_rt, b_rt]

        h1, route, pool_p, conv_p, sconv_p, v_p = _mixer_call(
            cfg_p, bp, n, h_p, off_p, None, weights(cat(ws_p), bias_p), None)
        h1, route, pool_s, conv_s, sconv_s, v_s = _mixer_call(
            cfg_s, bs, n, h_s, off_s, (state_pool[l], state_conv[l], state_sconv[l]),
            weights(cat(ws_s), bias_s), (h1, route))
        states_out.append((pool_p, conv_p, sconv_p, v_p, pool_s, conv_s, sconv_s, v_s))

        outs = _moe(h1, route, row(norm2_g[l]), w_gate[l].astype(BF16), w_up[l].astype(BF16),
                    w_down[l].astype(BF16), final_row, l == depth - 1, n_p)
        h_p = h_s = outs[0]
        off_p, off_s = 0, n_p // MIX_TILE

    y_prompt = outs[0].reshape(bp, seq, D_MODEL)
    y_sample = outs[1].reshape(bs, dseq, D_MODEL)
    st = [jnp.stack([s[k] for s in states_out]) for k in range(8)]
    return (y_prompt, y_sample, st[0], st[1], st[2], st[3], st[4], st[5], st[6], st[7])
```

```python
import functools
from typing import NamedTuple

import jax
import jax.numpy as jnp
from jax import lax
from jax.experimental import pallas as pl
from jax.experimental.pallas import tpu as pltpu

F32 = jnp.float32
BF16 = jnp.bfloat16

D_MODEL = 1024
BRANCH_W = 256
N_BRANCH = 4
POOL_WINDOWS = (2, 4, 8, 16)
POOL_GW = 64
POOL_HIST = 15
CONF_WIDTH = 31
CONF_HIST = 30
SCONV_WIDTH = 3
SCONV_HIST = 2
CHUNK = 128
SGU_HEADS = 4
SGU_HW = 64
N_GROUPS = 4
EXPERTS_PER_GROUP = 8
N_EXPERTS = 32
D_EXPERT = 256
RMS_EPS = 1e-6
LN_EPS = 1e-5
IN_COLS = 6144
GATE_COL0 = 2048

COL_TILE = 256
LANES = 128
SUBLANES = 8
POOL_PAD = 32
CONF_PAD = 32
SCONV_PAD = 8
ROW_BLOCK = 32
ROUTE_GROUP_LANE0 = 32
NEG_BIG = -3.0e38

MIX_TILE = 256
SORT_TILE = 512
SEG_PAD = 16
PERM_BLOCK = 256
N_PERM_BLOCKS = -(-(2 * SORT_TILE + N_EXPERTS * (SEG_PAD - 1)) // PERM_BLOCK)
LOCAL_ROWS = N_PERM_BLOCKS * PERM_BLOCK
EXP_TILE = 256
CHUNKS_PER_STEP = EXP_TILE // SEG_PAD
VMEM_LIMIT = 56 * 1024 * 1024


class MixCfg(NamedTuple):
    nb: int
    t: int
    has_state: bool
    start_pos: int
    row_off: int
    v_rows: int


def _rms(x, g):
    return x * lax.rsqrt(jnp.mean(x * x, axis=-1, keepdims=True) + RMS_EPS) * g


def _ln(x, g, b):
    mu = jnp.mean(x, axis=-1, keepdims=True)
    xc = x - mu
    return xc * lax.rsqrt(jnp.mean(xc * xc, axis=-1, keepdims=True) + LN_EPS) * g + b


def _sigmoid(x):
    return 0.5 * jnp.tanh(0.5 * x) + 0.5


def _gelu_tanh(x):
    return 0.5 * x * (1.0 + jnp.tanh(0.7978845608028654 * (x + 0.044715 * (x * x * x))))


def _row_blocks(nb, t):
    if t >= ROW_BLOCK:
        return [(slice(b, b + 1), t0, ROW_BLOCK) for b in range(nb) for t0 in range(0, t, ROW_BLOCK)]
    bb = ROW_BLOCK // t
    return [(slice(b0, b0 + bb), 0, t) for b0 in range(0, nb, bb)]


def _mixer_kernel(cfg, *refs):
    nb, t, tm = cfg.nb, cfg.t, cfg.nb * cfg.t
    refs = list(refs)
    x_ref = refs.pop(0)
    if cfg.has_state:
        pool_st, conv_st, sconv_st = refs.pop(0), refs.pop(0), refs.pop(0)
    (n1g, w_in, pool_w, pool_sc, cdw, cdb, clg, clb, scw, slg, slb, wcat, sbias,
     w_br, w_out, n2g, w_rt, b_rt) = refs[:18]
    refs = refs[18:]
    if cfg.has_state:
        refs = refs[2:]
    h1_o, route_o, pool_o, conv_o, sconv_o, v_o = refs[:6]
    pool_ext, sum_a, sum_b, conv_ext, conv_sh, sc_ext, buf_a, buf_b, buf_c, gate_buf, xb_buf = refs[6:]

    if cfg.has_state:
        seq_pos0 = cfg.start_pos
        pool_ext[:, 0:POOL_PAD - SUBLANES, :] = jnp.zeros((nb, POOL_PAD - SUBLANES, BRANCH_W), F32)
        pool_ext[:, POOL_PAD - POOL_HIST:POOL_PAD, :] = pool_st[...]
        conv_ext[:, CONF_PAD - CONF_HIST:CONF_PAD, :] = conv_st[...]
        sc_ext[:, SCONV_PAD - SCONV_HIST:SCONV_PAD, :] = sconv_st[...]
    else:
        c = pl.program_id(1)
        seq_pos0 = cfg.start_pos + c * t

        @pl.when(c == 0)
        def _():
            pool_ext[:, 0:POOL_PAD, :] = jnp.zeros((nb, POOL_PAD, BRANCH_W), F32)
            conv_ext[:, 0:CONF_PAD, :] = jnp.zeros((nb, CONF_PAD, BRANCH_W), F32)
            sc_ext[:, 0:SCONV_PAD, :] = jnp.zeros((nb, SCONV_PAD, BRANCH_W), F32)

    x = x_ref[...]
    xb_buf[...] = _rms(x, n1g[...]).astype(BF16)

    def proj(lo, hi):
        assert lo % COL_TILE == 0 and hi == lo + COL_TILE
        return jnp.dot(xb_buf[...], w_in[lo // COL_TILE], preferred_element_type=F32)

    a_pool = proj(0, 256)
    pool_ext[:, POOL_PAD:, :] = a_pool.reshape(nb, t, BRANCH_W)
    glu = proj(256, 512) * _sigmoid(proj(512, 768))
    conv_ext[:, CONF_PAD:, :] = glu.reshape(nb, t, BRANCH_W)
    z = proj(1024, 1280) * proj(1280, 1536)
    sc_ext[:, SCONV_PAD:, :] = z.reshape(nb, t, BRANCH_W)

    pl_len = POOL_PAD + t
    sum_b[:, 8:pl_len, :] = pool_ext[:, 8:pl_len, :] + pool_ext[:, 7:pl_len - 1, :]
    sum_a[:, 16:pl_len, :] = sum_b[:, 16:pl_len, :] + sum_b[:, 14:pl_len - 2, :]
    sum_b[:, 24:pl_len, :] = sum_a[:, 24:pl_len, :] + sum_a[:, 20:pl_len - 4, :]
    for sh in range(SUBLANES):
        n_rows = t + SUBLANES * ((CONF_WIDTH - 1 - sh) // SUBLANES)
        first = CONF_PAD - CONF_HIST + sh
        conv_sh[sh, :, 0:n_rows, :] = conv_ext[:, first:first + n_rows, :]

    blocks = _row_blocks(nb, t)
    gate_cols = (N_BRANCH * D_MODEL) // len(blocks)
    for bi, (bs, t0, tb) in enumerate(blocks):
        bb = bs.stop - bs.start
        shp = (bb, tb, BRANCH_W)
        lane = lax.broadcasted_iota(jnp.int32, shp, 2)
        pos = seq_pos0 + t0 + lax.broadcasted_iota(jnp.int32, shp, 1)
        r0 = POOL_PAD + t0
        cur = pool_ext[bs, r0:r0 + tb, :]
        s2 = cur + pool_ext[bs, r0 - 1:r0 - 1 + tb, :]
        s4 = sum_a[bs, r0:r0 + tb, :]
        s8 = sum_b[bs, r0:r0 + tb, :]
        s16 = s8 + sum_b[bs, r0 - SUBLANES:r0 - SUBLANES + tb, :]
        win_sum = jnp.where(lane < 64, s2, jnp.where(lane < 128, s4, jnp.where(lane < 192, s8, s16)))
        win = jnp.where(lane < 64, 2, jnp.where(lane < 128, 4, jnp.where(lane < 192, 8, 16)))
        cnt = jnp.minimum(pos + 1, win).astype(F32)
        buf_a[bs, t0:t0 + tb, :] = win_sum / cnt - cur
        tiles = (ROW_BLOCK // SUBLANES, SUBLANES, BRANCH_W)
        acc = jnp.broadcast_to(cdb[...][None], tiles)
        for k in range(CONF_WIDTH):
            q, sh = divmod(k, SUBLANES)
            r = t0 + SUBLANES * q
            acc = acc + conv_sh[sh, bs, r:r + tb, :].reshape(tiles) * cdw[k][None]
        buf_b[bs, t0:t0 + tb, :] = acc.reshape(shp)
        base = SCONV_PAD - SCONV_HIST + t0
        acc = sc_ext[bs, base:base + tb, :].reshape(tiles) * scw[0][None]
        for k in range(1, SCONV_WIDTH):
            acc = acc + sc_ext[bs, base + k:base + k + tb, :].reshape(tiles) * scw[k][None]
        buf_c[bs, t0:t0 + tb, :] = acc.reshape(shp)
        for g0 in range(bi * gate_cols, (bi + 1) * gate_cols, COL_TILE):
            gate_buf[:, g0:g0 + COL_TILE] = proj(GATE_COL0 + g0, GATE_COL0 + g0 + COL_TILE)

    def write_state():
        pool_o[...] = pool_ext[:, POOL_PAD + t - POOL_HIST:POOL_PAD + t, :]
        conv_o[...] = conv_ext[:, CONF_PAD + t - CONF_HIST:CONF_PAD + t, :]
        sconv_o[...] = sc_ext[:, SCONV_PAD + t - SCONV_HIST:SCONV_PAD + t, :]

    if cfg.has_state:
        write_state()
    else:
        pl.when(c == pl.num_programs(1) - 1)(write_state)
        pool_ext[:, 0:POOL_PAD, :] = pool_ext[:, t:t + POOL_PAD, :]
        conv_ext[:, 0:CONF_PAD, :] = conv_ext[:, t:t + CONF_PAD, :]
        sc_ext[:, 0:SCONV_PAD, :] = sc_ext[:, t:t + SCONV_PAD, :]

    pooled = buf_a[...].reshape(tm, BRANCH_W).astype(BF16)
    br_a = jnp.dot(pooled, pool_w[...], preferred_element_type=F32) * pool_sc[...]

    cb = _ln(buf_b[...].reshape(tm, BRANCH_W), clg[...], clb[...])
    br_b = cb * _sigmoid(cb)

    br_c = proj(768, 1024) * buf_c[...].reshape(tm, BRANCH_W)

    gu = _gelu_tanh(proj(1536, 1792))
    v = _ln(_gelu_tanh(proj(1792, 2048)), slg[...], slb[...])
    if cfg.has_state:
        v_o[...] = v.reshape(nb, t, BRANCH_W)
    else:
        @pl.when(c == pl.num_programs(1) - 1)
        def _():
            v_o[...] = v[tm - cfg.v_rows:, :].reshape(1, cfg.v_rows, BRANCH_W)
    head = lax.broadcasted_iota(jnp.int32, (CHUNK, BRANCH_W), 1) // SGU_HW
    mixed = []
    for j in range(tm // CHUNK):
        vj = v[j * CHUNK:(j + 1) * CHUNK, :]
        stacked = jnp.concatenate([jnp.where(head == h, vj, 0.0) for h in range(SGU_HEADS)], axis=0)
        mixed.append(jnp.dot(wcat[...], stacked.astype(BF16), preferred_element_type=F32) + sbias[...])
    br_d = gu * jnp.concatenate(mixed, axis=0)

    n_ct = D_MODEL // COL_TILE
    brs = [br.astype(BF16) for br in (br_a, br_b, br_c, br_d)]
    for c in range(n_ct):
        part = None
        for i in range(N_BRANCH):
            g0 = i * D_MODEL + c * COL_TILE
            term = _sigmoid(gate_buf[:, g0:g0 + COL_TILE]) * jnp.dot(
                brs[i], w_br[i * n_ct + c], preferred_element_type=F32)
            part = term if part is None else part + term
        xb_buf[:, c * COL_TILE:(c + 1) * COL_TILE] = part.astype(BF16)
    for c in range(n_ct):
        cs = slice(c * COL_TILE, (c + 1) * COL_TILE)
        h1_o[:, cs] = x_ref[:, cs] + jnp.dot(xb_buf[...], w_out[c], preferred_element_type=F32)

    xn2 = _rms(h1_o[...], n2g[...])
    x_hi = xn2.astype(BF16)
    x_lo = (xn2 - x_hi.astype(F32)).astype(BF16)
    logits = (jnp.dot(x_hi, w_rt[0], preferred_element_type=F32)
              + (jnp.dot(x_lo, w_rt[0], preferred_element_type=F32)
                 + jnp.dot(x_hi, w_rt[1], preferred_element_type=F32))) + b_rt[...]
    lane = lax.broadcasted_iota(jnp.int32, (tm, LANES), 1)
    is_g = (lane >= ROUTE_GROUP_LANE0) & (lane < ROUTE_GROUP_LANE0 + N_GROUPS)
    glog = jnp.where(is_g, logits, NEG_BIG)
    gmax = jnp.max(glog, axis=-1, keepdims=True)
    lane_f = lane.astype(F32)
    no_lane = float(4 * LANES)
    gsel = jnp.min(jnp.where(glog == gmax, lane_f, no_lane), axis=-1, keepdims=True) - ROUTE_GROUP_LANE0
    pg = 1.0 / jnp.sum(jnp.where(is_g, jnp.exp(glog - gmax), 0.0), axis=-1, keepdims=True)
    in_grp = (lane < N_EXPERTS) & ((lane // EXPERTS_PER_GROUP) == gsel.astype(jnp.int32))
    el = jnp.where(in_grp, logits, NEG_BIG)
    m1 = jnp.max(el, axis=-1, keepdims=True)
    i1 = jnp.min(jnp.where(in_grp & (el == m1), lane_f, no_lane), axis=-1, keepdims=True)
    rest = in_grp & (lane_f != i1)
    el2 = jnp.where(rest, logits, NEG_BIG)
    m2 = jnp.max(el2, axis=-1, keepdims=True)
    i2 = jnp.min(jnp.where(rest & (el2 == m2), lane_f, no_lane), axis=-1, keepdims=True)
    e21 = jnp.exp(m2 - m1)
    w1 = pg / (1.0 + e21)
    w2 = pg * e21 / (1.0 + e21)
    route_o[...] = jnp.where(lane == 0, i1, jnp.where(lane == 1, i2,
                             jnp.where(lane == 2, w1, jnp.where(lane == 3, w2, 0.0))))


def _const_spec(shape):
    nd = len(shape)
    return pl.BlockSpec(shape, lambda *_: (0,) * nd, pipeline_mode=pl.Buffered(1))


def _mixer_call(cfg, n_seq, n_rows_total, x2d, x_tile_off, states, weights, aliased):
    nb, t = cfg.nb, cfg.t
    tm = nb * t
    if cfg.has_state:
        grid = (n_seq // nb,)
        tok = lambda i: (x_tile_off + i, 0)
        out_tok = lambda i: (cfg.row_off + i, 0)
        seq3 = lambda i: (i, 0, 0)
    else:
        grid = (n_seq, 2048 // t)
        nt = grid[1]
        tok = lambda b, c: (x_tile_off + b * nt + c, 0)
        out_tok = lambda b, c: (cfg.row_off + b * nt + c, 0)
        seq3 = lambda b, c: (b, 0, 0)

    in_specs = [pl.BlockSpec((tm, D_MODEL), tok)]
    args = [x2d]
    if cfg.has_state:
        for s in states:
            in_specs.append(pl.BlockSpec((nb,) + s.shape[1:], seq3))
            args.append(s)
    for w in weights:
        in_specs.append(_const_spec(w.shape))
        args.append(w)
    io_alias = {}
    if cfg.has_state:
        for k, a in enumerate(aliased):
            in_specs.append(pl.BlockSpec(memory_space=pl.ANY))
            io_alias[len(args)] = k
            args.append(a)

    out_shape = [
        jax.ShapeDtypeStruct((n_rows_total, D_MODEL), F32),
        jax.ShapeDtypeStruct((n_rows_total, LANES), F32),
        jax.ShapeDtypeStruct((n_seq, POOL_HIST, BRANCH_W), F32),
        jax.ShapeDtypeStruct((n_seq, CONF_HIST, BRANCH_W), F32),
        jax.ShapeDtypeStruct((n_seq, SCONV_HIST, BRANCH_W), F32),
        jax.ShapeDtypeStruct((n_seq, cfg.v_rows, BRANCH_W), F32),
    ]
    out_specs = [
        pl.BlockSpec((tm, D_MODEL), out_tok),
        pl.BlockSpec((tm, LANES), out_tok),
        pl.BlockSpec((nb, POOL_HIST, BRANCH_W), seq3),
        pl.BlockSpec((nb, CONF_HIST, BRANCH_W), seq3),
        pl.BlockSpec((nb, SCONV_HIST, BRANCH_W), seq3),
        pl.BlockSpec((nb, cfg.v_rows, BRANCH_W), seq3),
    ]
    scratch = [
        pltpu.VMEM((nb, POOL_PAD + t, BRANCH_W), F32),
        pltpu.VMEM((nb, POOL_PAD + t, BRANCH_W), F32),
        pltpu.VMEM((nb, POOL_PAD + t, BRANCH_W), F32),
        pltpu.VMEM((nb, CONF_PAD + t, BRANCH_W), F32),
        pltpu.VMEM((SUBLANES, nb, t + CONF_PAD - SUBLANES, BRANCH_W), F32),
        pltpu.VMEM((nb, SCONV_PAD + t, BRANCH_W), F32),
        pltpu.VMEM((nb, t, BRANCH_W), F32),
        pltpu.VMEM((nb, t, BRANCH_W), F32),
        pltpu.VMEM((nb, t, BRANCH_W), F32),
        pltpu.VMEM((tm, N_BRANCH * D_MODEL), F32),
        pltpu.VMEM((tm, D_MODEL), BF16),
    ]
    return pl.pallas_call(
        functools.partial(_mixer_kernel, cfg),
        grid=grid,
        in_specs=in_specs,
        out_specs=out_specs,
        out_shape=out_shape,
        scratch_shapes=scratch,
        input_output_aliases=io_alias,
        compiler_params=pltpu.CompilerParams(
            dimension_semantics=("arbitrary",) * len(grid), vmem_limit_bytes=VMEM_LIMIT),
        name="mixer_sample" if cfg.has_state else "mixer_prompt",
    )(*args)


def _sort_kernel(h1_ref, route_ref, n2g, xl_o, pos_o, nch_o):
    t = SORT_TILE
    xn = _rms(h1_ref[...], n2g[...]).astype(BF16)
    r = route_ref[...]
    lane_i = lax.broadcasted_iota(jnp.int32, (t, LANES), 1)
    lane = lane_i.astype(F32)
    hit1 = lane == r[:, 0:1]
    hit2 = lane == r[:, 1:2]
    onehot = jnp.where(hit1 | hit2, 1.0, 0.0)
    row = lax.broadcasted_iota(jnp.int32, (t, t), 0)
    col = lax.broadcasted_iota(jnp.int32, (t, t), 1)
    strict_lower = jnp.where(col < row, 1.0, 0.0).astype(BF16)
    before = jnp.dot(strict_lower, onehot.astype(BF16), preferred_element_type=F32)
    cnt = jnp.sum(onehot, axis=0, keepdims=True)
    nch = jnp.floor((cnt + (SEG_PAD - 1)) * (1.0 / SEG_PAD))
    er = lax.broadcasted_iota(jnp.int32, (LANES, LANES), 0)
    ec = lax.broadcasted_iota(jnp.int32, (LANES, LANES), 1)
    strict_upper = jnp.where(er < ec, 1.0, 0.0).astype(BF16)
    nch8 = jnp.broadcast_to(nch, (SUBLANES, LANES)).astype(BF16)
    seg0 = jnp.dot(nch8, strict_upper, preferred_element_type=F32)[0:1, :] * SEG_PAD
    where = seg0 + before
    pos0 = jnp.sum(jnp.where(hit1, where, 0.0), axis=-1, keepdims=True)
    pos1 = jnp.sum(jnp.where(hit2, where, 0.0), axis=-1, keepdims=True)
    pos = jnp.where(lane_i == 0, pos0, jnp.where(lane_i == 1, pos1, -1.0))
    pos_o[...] = pos
    nch_o[...] = jnp.broadcast_to(nch, (1, SUBLANES, LANES))
    pos_t = jnp.transpose(pos)
    p0 = pos_t[0:1, :]
    p1 = pos_t[1:2, :]
    for blk in range(N_PERM_BLOCKS):
        dst = (lax.broadcasted_iota(jnp.int32, (PERM_BLOCK, t), 0) + blk * PERM_BLOCK).astype(F32)
        perm = jnp.where((dst == p0) | (dst == p1), 1.0, 0.0).astype(BF16)
        xl_o[0, blk * PERM_BLOCK:(blk + 1) * PERM_BLOCK, :] = jnp.dot(
            perm, xn, preferred_element_type=F32).astype(BF16)


def _sort_call(h1, route, n2g):
    n = h1.shape[0]
    nt = n // SORT_TILE
    return pl.pallas_call(
        _sort_kernel,
        grid=(nt,),
        in_specs=[pl.BlockSpec((SORT_TILE, D_MODEL), lambda i: (i, 0)),
                  pl.BlockSpec((SORT_TILE, LANES), lambda i: (i, 0)),
                  pl.BlockSpec((1, D_MODEL), lambda i: (0, 0))],
        out_specs=[pl.BlockSpec((1, LOCAL_ROWS, D_MODEL), lambda i: (i, 0, 0)),
                   pl.BlockSpec((SORT_TILE, LANES), lambda i: (i, 0)),
                   pl.BlockSpec((1, SUBLANES, LANES), lambda i: (i, 0, 0))],
        out_shape=[jax.ShapeDtypeStruct((nt, LOCAL_ROWS, D_MODEL), BF16),
                   jax.ShapeDtypeStruct((n, LANES), F32),
                   jax.ShapeDtypeStruct((nt, SUBLANES, LANES), F32)],
        compiler_params=pltpu.CompilerParams(dimension_semantics=("arbitrary",), vmem_limit_bytes=VMEM_LIMIT),
        name="moe_sort",
    )(h1, route, n2g)


def _chunk_copy(src, src_row, dst, dst_row, sem):
    return pltpu.make_async_copy(src.at[pl.ds(src_row, SEG_PAD), :], dst.at[pl.ds(dst_row, SEG_PAD), :], sem)


def _expert_kernel(se_ref, nv_ref, row_ref, ok_ref, xl_in, wg, wu, wd, xl_io, xbuf, ybuf, sem_in, sem_out):
    del se_ref, xl_in
    s = pl.program_id(0)
    nv = nv_ref[0]

    def row(step, j):
        return pl.multiple_of(row_ref[step * CHUNKS_PER_STEP + j], SEG_PAD)

    def gather(step, slot):
        return [_chunk_copy(xl_io, row(step, j), xbuf.at[slot], j * SEG_PAD, sem_in.at[slot])
                for j in range(CHUNKS_PER_STEP)]

    def for_real_chunks(step, slot, fn):
        for j in range(CHUNKS_PER_STEP):
            @pl.when(ok_ref[step * CHUNKS_PER_STEP + j] == 1)
            def _():
                fn(_chunk_copy(ybuf.at[slot], j * SEG_PAD, xl_io, row(step, j), sem_out.at[slot]))

    slot = s % 2

    @pl.when(s == 0)
    def _():
        for c in gather(0, 0):
            c.start()

    @pl.when(s + 1 < nv)
    def _():
        for c in gather(s + 1, 1 - slot):
            c.start()

    @pl.when(s < nv)
    def _():
        for c in gather(s, slot):
            c.wait()

        @pl.when(s >= 2)
        def _():
            for_real_chunks(s - 2, slot, lambda c: c.wait())

        xb = xbuf[slot]
        a = jnp.dot(xb, wg[0], preferred_element_type=F32)
        b = jnp.dot(xb, wu[0], preferred_element_type=F32)
        hh = (a * _sigmoid(a) * b).astype(BF16)
        ybuf[slot] = jnp.dot(hh, wd[0], preferred_element_type=F32).astype(BF16)
        for_real_chunks(s, slot, lambda c: c.start())

        @pl.when(s == nv - 1)
        def _():
            @pl.when(s >= 1)
            def _():
                for_real_chunks(s - 1, 1 - slot, lambda c: c.wait())
            for_real_chunks(s, slot, lambda c: c.wait())


def _expert_call(xl, step_expert, n_valid, chunk_row, chunk_ok, w_gate, w_up, w_down):
    n_steps = step_expert.shape[0]
    wmap = lambda s, se, *_: (se[s], 0, 0)
    grid_spec = pltpu.PrefetchScalarGridSpec(
        num_scalar_prefetch=4,
        grid=(n_steps,),
        in_specs=[pl.BlockSpec(memory_space=pl.ANY),
                  pl.BlockSpec((1, D_MODEL, D_EXPERT), wmap),
                  pl.BlockSpec((1, D_MODEL, D_EXPERT), wmap),
                  pl.BlockSpec((1, D_EXPERT, D_MODEL), wmap)],
        out_specs=pl.BlockSpec(memory_space=pl.ANY),
        scratch_shapes=[pltpu.VMEM((2, EXP_TILE, D_MODEL), BF16), pltpu.VMEM((2, EXP_TILE, D_MODEL), BF16),
                        pltpu.SemaphoreType.DMA((2,)), pltpu.SemaphoreType.DMA((2,))],
    )
    return pl.pallas_call(
        _expert_kernel,
        grid_spec=grid_spec,
        out_shape=jax.ShapeDtypeStruct(xl.shape, BF16),
        input_output_aliases={4: 0},
        compiler_params=pltpu.CompilerParams(dimension_semantics=("arbitrary",), has_side_effects=True),
        name="moe_experts",
    )(step_expert, n_valid, chunk_row, chunk_ok, xl, w_gate, w_up, w_down)


def _combine_kernel(final_norm, n_first, h1_ref, route_ref, pos_ref, fg, yl_ref, *outs):
    t = SORT_TILE
    r = route_ref[...]
    p = pos_ref[...]
    w1, w2 = r[:, 2:3], r[:, 3:4]
    pos0, pos1 = p[:, 0:1], p[:, 1:2]
    acc = h1_ref[...]
    for blk in range(N_PERM_BLOCKS):
        src = (lax.broadcasted_iota(jnp.int32, (t, PERM_BLOCK), 1) + blk * PERM_BLOCK).astype(F32)
        pw = jnp.where(src == pos0, w1, 0.0) + jnp.where(src == pos1, w2, 0.0)
        acc = acc + jnp.dot(pw.astype(BF16), yl_ref[0, blk * PERM_BLOCK:(blk + 1) * PERM_BLOCK, :],
                            preferred_element_type=F32)
    if not final_norm:
        outs[0][...] = acc
        return
    y = _rms(acc, fg[...])
    i = pl.program_id(0)

    @pl.when(i < n_first)
    def _():
        outs[0][...] = y

    @pl.when(i >= n_first)
    def _():
        outs[1][...] = y


def _combine_call(h1, route, pos, yl, final_g, final_norm, n_first_rows):
    n = h1.shape[0]
    nt = n // SORT_TILE
    n_first = n_first_rows // SORT_TILE
    tok = lambda i: (i, 0)
    if final_norm:
        out_shape = [jax.ShapeDtypeStruct((n_first_rows, D_MODEL), F32),
                     jax.ShapeDtypeStruct((n - n_first_rows, D_MODEL), F32)]
        out_specs = [pl.BlockSpec((SORT_TILE, D_MODEL), lambda i: (jnp.minimum(i, n_first - 1), 0)),
                     pl.BlockSpec((SORT_TILE, D_MODEL), lambda i: (jnp.maximum(i - n_first, 0), 0))]
    else:
        out_shape = [jax.ShapeDtypeStruct((n, D_MODEL), F32)]
        out_specs = [pl.BlockSpec((SORT_TILE, D_MODEL), tok)]
    return pl.pallas_call(
        functools.partial(_combine_kernel, final_norm, n_first),
        grid=(nt,),
        in_specs=[pl.BlockSpec((SORT_TILE, D_MODEL), tok),
                  pl.BlockSpec((SORT_TILE, LANES), tok),
                  pl.BlockSpec((SORT_TILE, LANES), tok),
                  pl.BlockSpec((1, D_MODEL), lambda i: (0, 0)),
                  pl.BlockSpec((1, LOCAL_ROWS, D_MODEL), lambda i: (i, 0, 0))],
        out_specs=out_specs,
        out_shape=out_shape,
        compiler_params=pltpu.CompilerParams(dimension_semantics=("arbitrary",), vmem_limit_bytes=VMEM_LIMIT),
        name="moe_combine",
    )(h1, route, pos, final_g, yl)


def _expert_tables(nch, n_steps):
    nt = nch.shape[0]
    cps = CHUNKS_PER_STEP
    seg_row0 = (jnp.cumsum(nch, axis=1) - nch) * SEG_PAD
    first = jnp.cumsum(nch, axis=0) - nch
    tot = jnp.sum(nch, axis=0)
    steps = (tot + cps - 1) // cps
    step_end = jnp.cumsum(steps)
    n_valid = step_end[-1:]
    s_ids = jnp.arange(n_steps, dtype=jnp.int32)
    step_expert = jnp.minimum(jnp.sum((step_end[None, :] <= s_ids[:, None]).astype(jnp.int32), axis=1),
                              N_EXPERTS - 1)
    sel = (step_expert[:, None] == jnp.arange(N_EXPERTS, dtype=jnp.int32)[None, :]).astype(jnp.int32)
    step0 = sel @ (step_end - steps)
    tot_s = sel @ tot
    first_s = sel @ first.T
    row0_s = sel @ seg_row0.T
    k = (s_ids - step0)[:, None] * cps + jnp.arange(cps, dtype=jnp.int32)[None, :]
    ok = (k < tot_s[:, None]) & (s_ids < n_valid[0])[:, None]
    tile = jnp.sum((first_s[:, None, :] <= k[:, :, None]).astype(jnp.int32), axis=2) - 1
    tsel = (tile[:, :, None] == jnp.arange(nt, dtype=jnp.int32)[None, None, :]).astype(jnp.int32)
    first_k = jnp.sum(tsel * first_s[:, None, :], axis=2)
    row0_k = jnp.sum(tsel * row0_s[:, None, :], axis=2)
    row = tile * LOCAL_ROWS + row0_k + (k - first_k) * SEG_PAD
    row = jnp.where(ok, row, row[:, 0:1])
    row = jnp.where((s_ids < n_valid[0])[:, None], row, 0)
    return step_expert, n_valid.astype(jnp.int32), row.reshape(-1).astype(jnp.int32), ok.reshape(-1).astype(jnp.int32)


def _moe(h1, route, n2g, w_gate, w_up, w_down, final_g, final_norm, n_first_rows):
    n = h1.shape[0]
    nt = n // SORT_TILE
    xl, pos, nch = _sort_call(h1, route, n2g)
    nch = nch[:, 0, :N_EXPERTS].astype(jnp.int32)
    max_chunks = (2 * n) // SEG_PAD + nt * N_EXPERTS
    n_steps = max_chunks // CHUNKS_PER_STEP + N_EXPERTS
    step_expert, n_valid, chunk_row, chunk_ok = _expert_tables(nch, n_steps)
    yl = _expert_call(xl.reshape(nt * LOCAL_ROWS, D_MODEL), step_expert, n_valid, chunk_row, chunk_ok,
                      w_gate, w_up, w_down)
    return _combine_call(h1, route, pos, yl.reshape(nt, LOCAL_ROWS, D_MODEL), final_g, final_norm, n_first_rows)


def _col_tiles(w):
    *lead, k, n = w.shape
    wt = w.astype(BF16).reshape(*lead, k, n // COL_TILE, COL_TILE)
    return jnp.swapaxes(wt, -3, -2)


def _block_diag(pw):
    out = jnp.zeros((BRANCH_W, BRANCH_W), pw.dtype)
    for g in range(len(POOL_WINDOWS)):
        out = out.at[g * POOL_GW:(g + 1) * POOL_GW, g * POOL_GW:(g + 1) * POOL_GW].set(pw[g])
    return out


def kernel(x_prompt, x_sample, state_pool, state_conv, state_sconv, norm1_g, w_in, pool_w, pool_scale, conf_dw, conf_dw_b, conf_ln_g, conf_ln_b, sconv_w, sgu_ln_g, sgu_ln_b, sgu_ws, sgu_b, w_branch, w_out, norm2_g, router_g, router_g_b, router_e, router_e_b, w_gate, w_up, w_down, final_g):
    depth = w_in.shape[0]
    bp, seq, _ = x_prompt.shape
    bs, dseq, _ = x_sample.shape
    n_p, n_s = bp * seq, bs * dseq
    n = n_p + n_s
    past_len = 16384
    assert seq % MIX_TILE == 0 and n_s % MIX_TILE == 0 and MIX_TILE % CHUNK == 0 and dseq == SUBLANES

    cfg_p = MixCfg(nb=1, t=MIX_TILE, has_state=False, start_pos=0, row_off=0, v_rows=CHUNK)
    cfg_s = MixCfg(nb=MIX_TILE // dseq, t=dseq, has_state=True, start_pos=past_len,
                   row_off=n_p // MIX_TILE, v_rows=dseq)

    row = lambda a: a.reshape(1, -1)
    rep8 = lambda a: jnp.broadcast_to(a[..., None, :], a.shape[:-1] + (SUBLANES, a.shape[-1]))
    tril = jnp.tril(jnp.ones((CHUNK, CHUNK), F32))
    eye_blk = jnp.kron(jnp.eye(CHUNK // dseq, dtype=F32), jnp.ones((dseq, dseq), F32))
    final_row = row(final_g)

    h_p, h_s = x_prompt.reshape(n_p, D_MODEL), x_sample.reshape(n_s, D_MODEL)
    off_p, off_s = 0, 0
    states_out = []
    for l in range(depth):
        ws_p = sgu_ws[l] * tril[None]
        ws_small = jnp.tile(sgu_ws[l][:, :dseq, :dseq], (1, CHUNK // dseq, CHUNK // dseq))
        ws_s = ws_small * (tril * eye_blk)[None]
        cat = lambda w: jnp.concatenate([w[h] for h in range(SGU_HEADS)], axis=1).astype(BF16)
        bias_p = jnp.repeat(sgu_b[l].T, SGU_HW, axis=1)
        bias_s = jnp.tile(jnp.repeat(sgu_b[l][:, :dseq].T, SGU_HW, axis=1), (CHUNK // dseq, 1))
        lane_pad = LANES - N_EXPERTS - N_GROUPS
        w_rt32 = jnp.pad(jnp.concatenate([router_e[l], router_g[l]], axis=1), ((0, 0), (0, lane_pad)))
        w_rt_hi = w_rt32.astype(BF16)
        w_rt = jnp.stack([w_rt_hi, (w_rt32 - w_rt_hi.astype(F32)).astype(BF16)])
        b_rt = jnp.pad(jnp.concatenate([router_e_b[l], router_g_b[l]]), (0, lane_pad)).reshape(1, LANES)

        def weights(wcat, sbias):
            return [row(norm1_g[l]), _col_tiles(w_in[l]), _block_diag(pool_w[l]).astype(BF16), row(pool_scale[l]),
                    rep8(conf_dw[l]), rep8(conf_dw_b[l]), row(conf_ln_g[l]), row(conf_ln_b[l]), rep8(sconv_w[l]),
                    row(sgu_ln_g[l]), row(sgu_ln_b[l]), wcat, sbias,
                    _col_tiles(w_branch[l]).reshape(-1, BRANCH_W, COL_TILE), _col_tiles(w_out[l]),
                    row(norm2_g[l]), w_rt, b_rt]

        h1, route, pool_p, conv_p, sconv_p, v_p = _mixer_call(
            cfg_p, bp, n, h_p, off_p, None, weights(cat(ws_p), bias_p), None)
        h1, route, pool_s, conv_s, sconv_s, v_s = _mixer_call(
            cfg_s, bs, n, h_s, off_s, (state_pool[l], state_conv[l], state_sconv[l]),
            weights(cat(ws_s), bias_s), (h1, route))
        states_out.append((pool_p, conv_p, sconv_p, v_p, pool_s, conv_s, sconv_s, v_s))

        outs = _moe(h1, route, row(norm2_g[l]), w_gate[l].astype(BF16), w_up[l].astype(BF16),
                    w_down[l].astype(BF16), final_row, l == depth - 1, n_p)
        h_p = h_s = outs[0]
        off_p, off_s = 0, n_p // MIX_TILE

    y_prompt = outs[0].reshape(bp, seq, D_MODEL)
    y_sample = outs[1].reshape(bs, dseq, D_MODEL)
    st = [jnp.stack([s[k] for s in states_out]) for k in range(8)]
    return (y_prompt, y_sample, st[0], st[1], st[2], st[3], st[4], st[5], st[6], st[7])
```

```python
import functools
from typing import NamedTuple

import jax
import jax.numpy as jnp
from jax import lax
from jax.experimental import pallas as pl
from jax.experimental.pallas import tpu as pltpu

F32 = jnp.float32
BF16 = jnp.bfloat16

D_MODEL = 1024
BRANCH_W = 256
N_BRANCH = 4
POOL_WINDOWS = (2, 4, 8, 16)
POOL_GW = 64
POOL_HIST = 15
CONF_WIDTH = 31
CONF_HIST = 30
SCONV_WIDTH = 3
SCONV_HIST = 2
CHUNK = 128
SGU_HEADS = 4
SGU_HW = 64
N_GROUPS = 4
EXPERTS_PER_GROUP = 8
N_EXPERTS = 32
D_EXPERT = 256
RMS_EPS = 1e-6
LN_EPS = 1e-5
IN_COLS = 6144
GATE_COL0 = 2048

COL_TILE = 256
LANES = 128
SUBLANES = 8
POOL_PAD = 32
CONF_PAD = 32
SCONV_PAD = 8
ROW_BLOCK = 32
ROUTE_GROUP_LANE0 = 32
NEG_BIG = -3.0e38

MIX_TILE = 512
MIX_TILE_S = 256
SORT_TILE = 512
SEG_PAD = 16
PERM_BLOCK = 256
N_PERM_BLOCKS = -(-(2 * SORT_TILE + N_EXPERTS * (SEG_PAD - 1)) // PERM_BLOCK)
LOCAL_ROWS = N_PERM_BLOCKS * PERM_BLOCK
EXP_TILE = 256
CHUNKS_PER_STEP = EXP_TILE // SEG_PAD
VMEM_LIMIT = 56 * 1024 * 1024


class MixCfg(NamedTuple):
    nb: int
    t: int
    has_state: bool
    start_pos: int
    row_off: int
    v_rows: int


def _rms(x, g):
    return x * lax.rsqrt(jnp.mean(x * x, axis=-1, keepdims=True) + RMS_EPS) * g


def _ln(x, g, b):
    mu = jnp.mean(x, axis=-1, keepdims=True)
    xc = x - mu
    return xc * lax.rsqrt(jnp.mean(xc * xc, axis=-1, keepdims=True) + LN_EPS) * g + b


def _sigmoid(x):
    return 0.5 * jnp.tanh(0.5 * x) + 0.5


def _zero_like_bits(x):
    bits = lax.bitcast_convert_type(x, jnp.uint32)
    return ((bits >> 16) >> 16).astype(jnp.int32).astype(F32)


def _gelu_tanh(x):
    return 0.5 * x * (1.0 + jnp.tanh(0.7978845608028654 * (x + 0.044715 * (x * x * x))))


def _row_blocks(nb, t):
    if t >= ROW_BLOCK:
        return [(slice(b, b + 1), t0, ROW_BLOCK) for b in range(nb) for t0 in range(0, t, ROW_BLOCK)]
    bb = ROW_BLOCK // t
    return [(slice(b0, b0 + bb), 0, t) for b0 in range(0, nb, bb)]


def _mixer_kernel(cfg, *refs):
    nb, t, tm = cfg.nb, cfg.t, cfg.nb * cfg.t
    refs = list(refs)
    x_ref = refs.pop(0)
    if cfg.has_state:
        pool_st, conv_st, sconv_st = refs.pop(0), refs.pop(0), refs.pop(0)
    (n1g, w_in, pool_w, pool_sc, cdw, cdb, clg, clb, scw, slg, slb, wcat, sbias,
     w_br, w_out, n2g, w_rt, b_rt) = refs[:18]
    refs = refs[18:]
    if cfg.has_state:
        refs = refs[2:]
    h1_o, route_o, pool_o, conv_o, sconv_o, v_o = refs[:6]
    pool_ext, sum_a, sum_b, conv_ext, conv_sh, sc_ext, buf_a, buf_b, buf_c, gate_buf, xb_buf = refs[6:]

    if cfg.has_state:
        seq_pos0 = cfg.start_pos
        pool_ext[:, 0:POOL_PAD - SUBLANES, :] = jnp.zeros((nb, POOL_PAD - SUBLANES, BRANCH_W), F32)
        pool_ext[:, POOL_PAD - POOL_HIST:POOL_PAD, :] = pool_st[...]
        conv_ext[:, CONF_PAD - CONF_HIST:CONF_PAD, :] = conv_st[...]
        sc_ext[:, SCONV_PAD - SCONV_HIST:SCONV_PAD, :] = sconv_st[...]
    else:
        c = pl.program_id(1)
        seq_pos0 = cfg.start_pos + c * t

        @pl.when(c == 0)
        def _():
            pool_ext[:, 0:POOL_PAD, :] = jnp.zeros((nb, POOL_PAD, BRANCH_W), F32)
            conv_ext[:, 0:CONF_PAD, :] = jnp.zeros((nb, CONF_PAD, BRANCH_W), F32)
            sc_ext[:, 0:SCONV_PAD, :] = jnp.zeros((nb, SCONV_PAD, BRANCH_W), F32)

    x = x_ref[...]
    xb_buf[...] = _rms(x, n1g[...]).astype(BF16)

    def proj(lo, hi):
        assert lo % COL_TILE == 0 and hi == lo + COL_TILE
        return jnp.dot(xb_buf[...], w_in[lo // COL_TILE], preferred_element_type=F32)

    a_pool = proj(0, 256)
    pool_ext[:, POOL_PAD:, :] = a_pool.reshape(nb, t, BRANCH_W)
    glu = proj(256, 512) * _sigmoid(proj(512, 768))
    conv_ext[:, CONF_PAD:, :] = glu.reshape(nb, t, BRANCH_W)
    z = proj(1024, 1280) * proj(1280, 1536)
    sc_ext[:, SCONV_PAD:, :] = z.reshape(nb, t, BRANCH_W)

    pl_len = POOL_PAD + t
    sum_b[:, 8:pl_len, :] = pool_ext[:, 8:pl_len, :] + pool_ext[:, 7:pl_len - 1, :]
    sum_a[:, 16:pl_len, :] = sum_b[:, 16:pl_len, :] + sum_b[:, 14:pl_len - 2, :]
    sum_b[:, 24:pl_len, :] = sum_a[:, 24:pl_len, :] + sum_a[:, 20:pl_len - 4, :]
    for sh in range(SUBLANES):
        n_rows = t + SUBLANES * ((CONF_WIDTH - 1 - sh) // SUBLANES)
        first = CONF_PAD - CONF_HIST + sh
        conv_sh[sh, :, 0:n_rows, :] = conv_ext[:, first:first + n_rows, :]

    blocks = _row_blocks(nb, t)
    gate_cols = (N_BRANCH * D_MODEL) // len(blocks)
    bias_tile = cdb[...]
    for bi, (bs, t0, tb) in enumerate(blocks):
        bb = bs.stop - bs.start
        shp = (bb, tb, BRANCH_W)
        lane = lax.broadcasted_iota(jnp.int32, shp, 2)
        pos = seq_pos0 + t0 + lax.broadcasted_iota(jnp.int32, shp, 1)
        r0 = POOL_PAD + t0
        cur = pool_ext[bs, r0:r0 + tb, :]
        s2 = cur + pool_ext[bs, r0 - 1:r0 - 1 + tb, :]
        s4 = sum_a[bs, r0:r0 + tb, :]
        s8 = sum_b[bs, r0:r0 + tb, :]
        s16 = s8 + sum_b[bs, r0 - SUBLANES:r0 - SUBLANES + tb, :]
        win_sum = jnp.where(lane < 64, s2, jnp.where(lane < 128, s4, jnp.where(lane < 192, s8, s16)))
        win = jnp.where(lane < 64, 2, jnp.where(lane < 128, 4, jnp.where(lane < 192, 8, 16)))
        cnt = jnp.minimum(pos + 1, win).astype(F32)
        buf_a[bs, t0:t0 + tb, :] = win_sum / cnt - cur
        tiles = (ROW_BLOCK // SUBLANES, SUBLANES, BRANCH_W)
        acc = jnp.broadcast_to(bias_tile[None], tiles)
        for k in range(CONF_WIDTH):
            q, sh = divmod(k, SUBLANES)
            r = t0 + SUBLANES * q
            acc = acc + conv_sh[sh, bs, r:r + tb, :].reshape(tiles) * cdw[k][None]
        buf_b[bs, t0:t0 + tb, :] = acc.reshape(shp)
        base = SCONV_PAD - SCONV_HIST + t0
        acc = sc_ext[bs, base:base + tb, :].reshape(tiles) * scw[0][None]
        for k in range(1, SCONV_WIDTH):
            acc = acc + sc_ext[bs, base + k:base + k + tb, :].reshape(tiles) * scw[k][None]
        buf_c[bs, t0:t0 + tb, :] = acc.reshape(shp)
        for g0 in range(bi * gate_cols, (bi + 1) * gate_cols, COL_TILE):
            g = jnp.tanh(proj(GATE_COL0 + g0, GATE_COL0 + g0 + COL_TILE)) + 1.0
            gate_buf[:, g0:g0 + COL_TILE] = g
        bias_tile = cdb[...] + _zero_like_bits(g[0:SUBLANES, :])

    pool_o[...] = pool_ext[:, POOL_PAD + t - POOL_HIST:POOL_PAD + t, :]
    conv_o[...] = conv_ext[:, CONF_PAD + t - CONF_HIST:CONF_PAD + t, :]
    sconv_o[...] = sc_ext[:, SCONV_PAD + t - SCONV_HIST:SCONV_PAD + t, :]
    if not cfg.has_state:
        pool_ext[:, 0:POOL_PAD, :] = pool_ext[:, t:t + POOL_PAD, :]
        conv_ext[:, 0:CONF_PAD, :] = conv_ext[:, t:t + CONF_PAD, :]
        sc_ext[:, 0:SCONV_PAD, :] = sc_ext[:, t:t + SCONV_PAD, :]

    pooled = buf_a[...].reshape(tm, BRANCH_W).astype(BF16)
    br_a = jnp.dot(pooled, pool_w[...], preferred_element_type=F32) * pool_sc[...]

    cb = _ln(buf_b[...].reshape(tm, BRANCH_W), clg[...], clb[...])
    br_b = cb * _sigmoid(cb)

    br_c = proj(768, 1024) * buf_c[...].reshape(tm, BRANCH_W)

    gu = _gelu_tanh(proj(1536, 1792))
    v = _ln(_gelu_tanh(proj(1792, 2048)), slg[...], slb[...])
    if cfg.has_state:
        v_o[...] = v.reshape(nb, t, BRANCH_W)
    else:
        v_o[...] = v[tm - cfg.v_rows:, :].reshape(1, cfg.v_rows, BRANCH_W)
    head = lax.broadcasted_iota(jnp.int32, (CHUNK, BRANCH_W), 1) // SGU_HW
    mixed = []
    for j in range(tm // CHUNK):
        vj = v[j * CHUNK:(j + 1) * CHUNK, :]
        stacked = jnp.concatenate([jnp.where(head == h, vj, 0.0) for h in range(SGU_HEADS)], axis=0)
        mixed.append(jnp.dot(wcat[...], stacked.astype(BF16), preferred_element_type=F32) + sbias[...])
    br_d = gu * jnp.concatenate(mixed, axis=0)

    n_ct = D_MODEL // COL_TILE
    brs = [br.astype(BF16) for br in (br_a, br_b, br_c, br_d)]
    for c in range(n_ct):
        part = None
        for i in range(N_BRANCH):
            g0 = i * D_MODEL + c * COL_TILE
            term = gate_buf[:, g0:g0 + COL_TILE] * jnp.dot(
                brs[i], w_br[i * n_ct + c], preferred_element_type=F32)
            part = term if part is None else part + term
        xb_buf[:, c * COL_TILE:(c + 1) * COL_TILE] = part.astype(BF16)
    for c in range(n_ct):
        cs = slice(c * COL_TILE, (c + 1) * COL_TILE)
        h1_o[:, cs] = x_ref[:, cs] + jnp.dot(xb_buf[...], w_out[c], preferred_element_type=F32)

    xn2 = _rms(h1_o[...], n2g[...])
    x_hi = xn2.astype(BF16)
    x_lo = (xn2 - x_hi.astype(F32)).astype(BF16)
    logits = (jnp.dot(x_hi, w_rt[0], preferred_element_type=F32)
              + (jnp.dot(x_lo, w_rt[0], preferred_element_type=F32)
                 + jnp.dot(x_hi, w_rt[1], preferred_element_type=F32))) + b_rt[...]
    lane = lax.broadcasted_iota(jnp.int32, (tm, LANES), 1)
    is_g = (lane >= ROUTE_GROUP_LANE0) & (lane < ROUTE_GROUP_LANE0 + N_GROUPS)
    glog = jnp.where(is_g, logits, NEG_BIG)
    gmax = jnp.max(glog, axis=-1, keepdims=True)
    lane_f = lane.astype(F32)
    no_lane = float(4 * LANES)
    gsel = jnp.min(jnp.where(glog == gmax, lane_f, no_lane), axis=-1, keepdims=True) - ROUTE_GROUP_LANE0
    pg = 1.0 / jnp.sum(jnp.where(is_g, jnp.exp(glog - gmax), 0.0), axis=-1, keepdims=True)
    in_grp = (lane < N_EXPERTS) & ((lane // EXPERTS_PER_GROUP) == gsel.astype(jnp.int32))
    el = jnp.where(in_grp, logits, NEG_BIG)
    m1 = jnp.max(el, axis=-1, keepdims=True)
    i1 = jnp.min(jnp.where(in_grp & (el == m1), lane_f, no_lane), axis=-1, keepdims=True)
    rest = in_grp & (lane_f != i1)
    el2 = jnp.where(rest, logits, NEG_BIG)
    m2 = jnp.max(el2, axis=-1, keepdims=True)
    i2 = jnp.min(jnp.where(rest & (el2 == m2), lane_f, no_lane), axis=-1, keepdims=True)
    e21 = jnp.exp(m2 - m1)
    w1 = pg / (1.0 + e21)
    w2 = pg * e21 / (1.0 + e21)
    route_o[...] = jnp.where(lane == 0, i1, jnp.where(lane == 1, i2,
                             jnp.where(lane == 2, w1, jnp.where(lane == 3, w2, 0.0))))


def _const_spec(shape):
    nd = len(shape)
    return pl.BlockSpec(shape, lambda *_: (0,) * nd, pipeline_mode=pl.Buffered(1))


def _mixer_call(cfg, n_seq, n_rows_total, x2d, x_tile_off, states, weights, aliased):
    nb, t = cfg.nb, cfg.t
    tm = nb * t
    if cfg.has_state:
        grid = (n_seq // nb,)
        tok = lambda i: (x_tile_off + i, 0)
        out_tok = lambda i: (cfg.row_off + i, 0)
        seq3 = lambda i: (i, 0, 0)
    else:
        grid = (n_seq, 2048 // t)
        nt = grid[1]
        tok = lambda b, c: (x_tile_off + b * nt + c, 0)
        out_tok = lambda b, c: (cfg.row_off + b * nt + c, 0)
        seq3 = lambda b, c: (b, 0, 0)

    in_specs = [pl.BlockSpec((tm, D_MODEL), tok)]
    args = [x2d]
    if cfg.has_state:
        for s in states:
            in_specs.append(pl.BlockSpec((nb,) + s.shape[1:], seq3))
            args.append(s)
    for w in weights:
        in_specs.append(_const_spec(w.shape))
        args.append(w)
    io_alias = {}
    if cfg.has_state:
        for k, a in enumerate(aliased):
            in_specs.append(pl.BlockSpec(memory_space=pl.ANY))
            io_alias[len(args)] = k
            args.append(a)

    out_shape = [
        jax.ShapeDtypeStruct((n_rows_total, D_MODEL), F32),
        jax.ShapeDtypeStruct((n_rows_total, LANES), F32),
        jax.ShapeDtypeStruct((n_seq, POOL_HIST, BRANCH_W), F32),
        jax.ShapeDtypeStruct((n_seq, CONF_HIST, BRANCH_W), F32),
        jax.ShapeDtypeStruct((n_seq, SCONV_HIST, BRANCH_W), F32),
        jax.ShapeDtypeStruct((n_seq, cfg.v_rows, BRANCH_W), F32),
    ]
    out_specs = [
        pl.BlockSpec((tm, D_MODEL), out_tok),
        pl.BlockSpec((tm, LANES), out_tok),
        pl.BlockSpec((nb, POOL_HIST, BRANCH_W), seq3),
        pl.BlockSpec((nb, CONF_HIST, BRANCH_W), seq3),
        pl.BlockSpec((nb, SCONV_HIST, BRANCH_W), seq3),
        pl.BlockSpec((nb, cfg.v_rows, BRANCH_W), seq3),
    ]
    scratch = [
        pltpu.VMEM((nb, POOL_PAD + t, BRANCH_W), F32),
        pltpu.VMEM((nb, POOL_PAD + t, BRANCH_W), F32),
        pltpu.VMEM((nb, POOL_PAD + t, BRANCH_W), F32),
        pltpu.VMEM((nb, CONF_PAD + t, BRANCH_W), F32),
        pltpu.VMEM((SUBLANES, nb, t + CONF_PAD - SUBLANES, BRANCH_W), F32),
        pltpu.VMEM((nb, SCONV_PAD + t, BRANCH_W), F32),
        pltpu.VMEM((nb, t, BRANCH_W), F32),
        pltpu.VMEM((nb, t, BRANCH_W), F32),
        pltpu.VMEM((nb, t, BRANCH_W), F32),
        pltpu.VMEM((tm, N_BRANCH * D_MODEL), F32),
        pltpu.VMEM((tm, D_MODEL), BF16),
    ]
    return pl.pallas_call(
        functools.partial(_mixer_kernel, cfg),
        grid=grid,
        in_specs=in_specs,
        out_specs=out_specs,
        out_shape=out_shape,
        scratch_shapes=scratch,
        input_output_aliases=io_alias,
        compiler_params=pltpu.CompilerParams(
            dimension_semantics=("arbitrary",) * len(grid), vmem_limit_bytes=VMEM_LIMIT),
        name="mixer_sample" if cfg.has_state else "mixer_prompt",
    )(*args)


def _sort_kernel(h1_ref, route_ref, n2g, xl_o, pos_o, nch_o):
    t = SORT_TILE
    xn = _rms(h1_ref[...], n2g[...]).astype(BF16)
    r = route_ref[...]
    lane_i = lax.broadcasted_iota(jnp.int32, (t, LANES), 1)
    lane = lane_i.astype(F32)
    hit1 = lane == r[:, 0:1]
    hit2 = lane == r[:, 1:2]
    onehot = jnp.where(hit1 | hit2, 1.0, 0.0)
    row = lax.broadcasted_iota(jnp.int32, (t, t), 0)
    col = lax.broadcasted_iota(jnp.int32, (t, t), 1)
    strict_lower = jnp.where(col < row, 1.0, 0.0).astype(BF16)
    before = jnp.dot(strict_lower, onehot.astype(BF16), preferred_element_type=F32)
    cnt = jnp.sum(onehot, axis=0, keepdims=True)
    nch = jnp.floor((cnt + (SEG_PAD - 1)) * (1.0 / SEG_PAD))
    er = lax.broadcasted_iota(jnp.int32, (LANES, LANES), 0)
    ec = lax.broadcasted_iota(jnp.int32, (LANES, LANES), 1)
    strict_upper = jnp.where(er < ec, 1.0, 0.0).astype(BF16)
    nch8 = jnp.broadcast_to(nch, (SUBLANES, LANES)).astype(BF16)
    seg0 = jnp.dot(nch8, strict_upper, preferred_element_type=F32)[0:1, :] * SEG_PAD
    where = seg0 + before
    pos0 = jnp.sum(jnp.where(hit1, where, 0.0), axis=-1, keepdims=True)
    pos1 = jnp.sum(jnp.where(hit2, where, 0.0), axis=-1, keepdims=True)
    pos = jnp.where(lane_i == 0, pos0, jnp.where(lane_i == 1, pos1, -1.0))
    pos_o[...] = pos
    nch_o[...] = jnp.broadcast_to(nch, (1, SUBLANES, LANES))
    pos_t = jnp.transpose(pos)
    p0 = pos_t[0:1, :]
    p1 = pos_t[1:2, :]
    for blk in range(N_PERM_BLOCKS):
        dst = (lax.broadcasted_iota(jnp.int32, (PERM_BLOCK, t), 0) + blk * PERM_BLOCK).astype(F32)
        perm = jnp.where((dst == p0) | (dst == p1), 1.0, 0.0).astype(BF16)
        xl_o[0, blk * PERM_BLOCK:(blk + 1) * PERM_BLOCK, :] = jnp.dot(
            perm, xn, preferred_element_type=F32).astype(BF16)


def _sort_call(h1, route, n2g):
    n = h1.shape[0]
    nt = n // SORT_TILE
    return pl.pallas_call(
        _sort_kernel,
        grid=(nt,),
        in_specs=[pl.BlockSpec((SORT_TILE, D_MODEL), lambda i: (i, 0)),
                  pl.BlockSpec((SORT_TILE, LANES), lambda i: (i, 0)),
                  pl.BlockSpec((1, D_MODEL), lambda i: (0, 0))],
        out_specs=[pl.BlockSpec((1, LOCAL_ROWS, D_MODEL), lambda i: (i, 0, 0)),
                   pl.BlockSpec((SORT_TILE, LANES), lambda i: (i, 0)),
                   pl.BlockSpec((1, SUBLANES, LANES), lambda i: (i, 0, 0))],
        out_shape=[jax.ShapeDtypeStruct((nt, LOCAL_ROWS, D_MODEL), BF16),
                   jax.ShapeDtypeStruct((n, LANES), F32),
                   jax.ShapeDtypeStruct((nt, SUBLANES, LANES), F32)],
        compiler_params=pltpu.CompilerParams(dimension_semantics=("arbitrary",), vmem_limit_bytes=VMEM_LIMIT),
        name="moe_sort",
    )(h1, route, n2g)


def _chunk_copy(src, src_row, dst, dst_row, sem):
    return pltpu.make_async_copy(src.at[pl.ds(src_row, SEG_PAD), :], dst.at[pl.ds(dst_row, SEG_PAD), :], sem)


def _expert_kernel(se_ref, nv_ref, row_ref, ok_ref, xl_in, wg, wu, wd, xl_io, xbuf, ybuf, sem_in, sem_out):
    del se_ref, xl_in
    s = pl.program_id(0)
    nv = nv_ref[0]

    def row(step, j):
        return pl.multiple_of(row_ref[step * CHUNKS_PER_STEP + j], SEG_PAD)

    def gather(step, slot):
        return [_chunk_copy(xl_io, row(step, j), xbuf.at[slot], j * SEG_PAD, sem_in.at[slot])
                for j in range(CHUNKS_PER_STEP)]

    def for_real_chunks(step, slot, fn):
        for j in range(CHUNKS_PER_STEP):
            @pl.when(ok_ref[step * CHUNKS_PER_STEP + j] == 1)
            def _():
                fn(_chunk_copy(ybuf.at[slot], j * SEG_PAD, xl_io, row(step, j), sem_out.at[slot]))

    slot = s % 2

    @pl.when(s == 0)
    def _():
        for c in gather(0, 0):
            c.start()

    @pl.when(s + 1 < nv)
    def _():
        for c in gather(s + 1, 1 - slot):
            c.start()

    @pl.when(s < nv)
    def _():
        for c in gather(s, slot):
            c.wait()

        @pl.when(s >= 2)
        def _():
            for_real_chunks(s - 2, slot, lambda c: c.wait())

        xb = xbuf[slot]
        a = jnp.dot(xb, wg[0], preferred_element_type=F32)
        b = jnp.dot(xb, wu[0], preferred_element_type=F32)
        hh = (a * _sigmoid(a) * b).astype(BF16)
        ybuf[slot] = jnp.dot(hh, wd[0], preferred_element_type=F32).astype(BF16)
        for_real_chunks(s, slot, lambda c: c.start())

        @pl.when(s == nv - 1)
        def _():
            @pl.when(s >= 1)
            def _():
                for_real_chunks(s - 1, 1 - slot, lambda c: c.wait())
            for_real_chunks(s, slot, lambda c: c.wait())


def _expert_call(xl, step_expert, n_valid, chunk_row, chunk_ok, w_gate, w_up, w_down):
    n_steps = step_expert.shape[0]
    wmap = lambda s, se, *_: (se[s], 0, 0)
    grid_spec = pltpu.PrefetchScalarGridSpec(
        num_scalar_prefetch=4,
        grid=(n_steps,),
        in_specs=[pl.BlockSpec(memory_space=pl.ANY),
                  pl.BlockSpec((1, D_MODEL, D_EXPERT), wmap),
                  pl.BlockSpec((1, D_MODEL, D_EXPERT), wmap),
                  pl.BlockSpec((1, D_EXPERT, D_MODEL), wmap)],
        out_specs=pl.BlockSpec(memory_space=pl.ANY),
        scratch_shapes=[pltpu.VMEM((2, EXP_TILE, D_MODEL), BF16), pltpu.VMEM((2, EXP_TILE, D_MODEL), BF16),
                        pltpu.SemaphoreType.DMA((2,)), pltpu.SemaphoreType.DMA((2,))],
    )
    return pl.pallas_call(
        _expert_kernel,
        grid_spec=grid_spec,
        out_shape=jax.ShapeDtypeStruct(xl.shape, BF16),
        input_output_aliases={4: 0},
        compiler_params=pltpu.CompilerParams(dimension_semantics=("arbitrary",), has_side_effects=True),
        name="moe_experts",
    )(step_expert, n_valid, chunk_row, chunk_ok, xl, w_gate, w_up, w_down)


def _combine_kernel(final_norm, n_first, h1_ref, route_ref, pos_ref, fg, yl_ref, *outs):
    t = SORT_TILE
    r = route_ref[...]
    p = pos_ref[...]
    w1, w2 = r[:, 2:3], r[:, 3:4]
    pos0, pos1 = p[:, 0:1], p[:, 1:2]
    acc = h1_ref[...]
    for blk in range(N_PERM_BLOCKS):
        src = (lax.broadcasted_iota(jnp.int32, (t, PERM_BLOCK), 1) + blk * PERM_BLOCK).astype(F32)
        pw = jnp.where(src == pos0, w1, 0.0) + jnp.where(src == pos1, w2, 0.0)
        acc = acc + jnp.dot(pw.astype(BF16), yl_ref[0, blk * PERM_BLOCK:(blk + 1) * PERM_BLOCK, :],
                            preferred_element_type=F32)
    if not final_norm:
        outs[0][...] = acc
        return
    y = _rms(acc, fg[...])
    i = pl.program_id(0)

    @pl.when(i < n_first)
    def _():
        outs[0][...] = y

    @pl.when(i >= n_first)
    def _():
        outs[1][...] = y


def _combine_call(h1, route, pos, yl, final_g, final_norm, n_first_rows):
    n = h1.shape[0]
    nt = n // SORT_TILE
    n_first = n_first_rows // SORT_TILE
    tok = lambda i: (i, 0)
    if final_norm:
        out_shape = [jax.ShapeDtypeStruct((n_first_rows, D_MODEL), F32),
                     jax.ShapeDtypeStruct((n - n_first_rows, D_MODEL), F32)]
        out_specs = [pl.BlockSpec((SORT_TILE, D_MODEL), lambda i: (jnp.minimum(i, n_first - 1), 0)),
                     pl.BlockSpec((SORT_TILE, D_MODEL), lambda i: (jnp.maximum(i - n_first, 0), 0))]
    else:
        out_shape = [jax.ShapeDtypeStruct((n, D_MODEL), F32)]
        out_specs = [pl.BlockSpec((SORT_TILE, D_MODEL), tok)]
    return pl.pallas_call(
        functools.partial(_combine_kernel, final_norm, n_first),
        grid=(nt,),
        in_specs=[pl.BlockSpec((SORT_TILE, D_MODEL), tok),
                  pl.BlockSpec((SORT_TILE, LANES), tok),
                  pl.BlockSpec((SORT_TILE, LANES), tok),
                  pl.BlockSpec((1, D_MODEL), lambda i: (0, 0)),
                  pl.BlockSpec((1, LOCAL_ROWS, D_MODEL), lambda i: (i, 0, 0))],
        out_specs=out_specs,
        out_shape=out_shape,
        compiler_params=pltpu.CompilerParams(dimension_semantics=("arbitrary",), vmem_limit_bytes=VMEM_LIMIT),
        name="moe_combine",
    )(h1, route, pos, final_g, yl)


def _expert_tables(nch, n_steps):
    nt = nch.shape[0]
    cps = CHUNKS_PER_STEP
    seg_row0 = (jnp.cumsum(nch, axis=1) - nch) * SEG_PAD
    first = jnp.cumsum(nch, axis=0) - nch
    tot = jnp.sum(nch, axis=0)
    steps = (tot + cps - 1) // cps
    step_end = jnp.cumsum(steps)
    n_valid = step_end[-1:]
    s_ids = jnp.arange(n_steps, dtype=jnp.int32)
    step_expert = jnp.minimum(jnp.sum((step_end[None, :] <= s_ids[:, None]).astype(jnp.int32), axis=1),
                              N_EXPERTS - 1)
    sel = (step_expert[:, None] == jnp.arange(N_EXPERTS, dtype=jnp.int32)[None, :]).astype(jnp.int32)
    step0 = sel @ (step_end - steps)
    tot_s = sel @ tot
    first_s = sel @ first.T
    row0_s = sel @ seg_row0.T
    k = (s_ids - step0)[:, None] * cps + jnp.arange(cps, dtype=jnp.int32)[None, :]
    ok = (k < tot_s[:, None]) & (s_ids < n_valid[0])[:, None]
    tile = jnp.sum((first_s[:, None, :] <= k[:, :, None]).astype(jnp.int32), axis=2) - 1
    tsel = (tile[:, :, None] == jnp.arange(nt, dtype=jnp.int32)[None, None, :]).astype(jnp.int32)
    first_k = jnp.sum(tsel * first_s[:, None, :], axis=2)
    row0_k = jnp.sum(tsel * row0_s[:, None, :], axis=2)
    row = tile * LOCAL_ROWS + row0_k + (k - first_k) * SEG_PAD
    row = jnp.where(ok, row, row[:, 0:1])
    row = jnp.where((s_ids < n_valid[0])[:, None], row, 0)
    return step_expert, n_valid.astype(jnp.int32), row.reshape(-1).astype(jnp.int32), ok.reshape(-1).astype(jnp.int32)


def _moe(h1, route, n2g, w_gate, w_up, w_down, final_g, final_norm, n_first_rows):
    n = h1.shape[0]
    nt = n // SORT_TILE
    xl, pos, nch = _sort_call(h1, route, n2g)
    nch = nch[:, 0, :N_EXPERTS].astype(jnp.int32)
    max_chunks = (2 * n) // SEG_PAD + nt * N_EXPERTS
    n_steps = max_chunks // CHUNKS_PER_STEP + N_EXPERTS
    step_expert, n_valid, chunk_row, chunk_ok = _expert_tables(nch, n_steps)
    yl = _expert_call(xl.reshape(nt * LOCAL_ROWS, D_MODEL), step_expert, n_valid, chunk_row, chunk_ok,
                      w_gate, w_up, w_down)
    return _combine_call(h1, route, pos, yl.reshape(nt, LOCAL_ROWS, D_MODEL), final_g, final_norm, n_first_rows)


def _col_tiles(w):
    *lead, k, n = w.shape
    wt = w.astype(BF16).reshape(*lead, k, n // COL_TILE, COL_TILE)
    return jnp.swapaxes(wt, -3, -2)


def _block_diag(pw):
    out = jnp.zeros((BRANCH_W, BRANCH_W), pw.dtype)
    for g in range(len(POOL_WINDOWS)):
        out = out.at[g * POOL_GW:(g + 1) * POOL_GW, g * POOL_GW:(g + 1) * POOL_GW].set(pw[g])
    return out


def kernel(x_prompt, x_sample, state_pool, state_conv, state_sconv, norm1_g, w_in, pool_w, pool_scale, conf_dw, conf_dw_b, conf_ln_g, conf_ln_b, sconv_w, sgu_ln_g, sgu_ln_b, sgu_ws, sgu_b, w_branch, w_out, norm2_g, router_g, router_g_b, router_e, router_e_b, w_gate, w_up, w_down, final_g):
    depth = w_in.shape[0]
    bp, seq, _ = x_prompt.shape
    bs, dseq, _ = x_sample.shape
    n_p, n_s = bp * seq, bs * dseq
    n = n_p + n_s
    past_len = 16384
    assert seq % MIX_TILE == 0 and n_s % MIX_TILE_S == 0 and n_p % MIX_TILE_S == 0 and dseq == SUBLANES
    assert MIX_TILE % CHUNK == 0 and MIX_TILE_S % CHUNK == 0

    cfg_p = MixCfg(nb=1, t=MIX_TILE, has_state=False, start_pos=0, row_off=0, v_rows=CHUNK)
    cfg_s = MixCfg(nb=MIX_TILE_S // dseq, t=dseq, has_state=True, start_pos=past_len,
                   row_off=n_p // MIX_TILE_S, v_rows=dseq)

    row = lambda a: a.reshape(1, -1)
    rep8 = lambda a: jnp.broadcast_to(a[..., None, :], a.shape[:-1] + (SUBLANES, a.shape[-1]))
    tril = jnp.tril(jnp.ones((CHUNK, CHUNK), F32))
    eye_blk = jnp.kron(jnp.eye(CHUNK // dseq, dtype=F32), jnp.ones((dseq, dseq), F32))
    final_row = row(final_g)
    gate_half = jnp.where(jnp.arange(IN_COLS) < GATE_COL0, 1.0, 0.5).astype(F32)[None, :]

    h_p, h_s = x_prompt.reshape(n_p, D_MODEL), x_sample.reshape(n_s, D_MODEL)
    off_p, off_s = 0, 0
    states_out = []
    for l in range(depth):
        ws_p = sgu_ws[l] * tril[None]
        ws_small = jnp.tile(sgu_ws[l][:, :dseq, :dseq], (1, CHUNK // dseq, CHUNK // dseq))
        ws_s = ws_small * (tril * eye_blk)[None]
        cat = lambda w: jnp.concatenate([w[h] for h in range(SGU_HEADS)], axis=1).astype(BF16)
        bias_p = jnp.repeat(sgu_b[l].T, SGU_HW, axis=1)
        bias_s = jnp.tile(jnp.repeat(sgu_b[l][:, :dseq].T, SGU_HW, axis=1), (CHUNK // dseq, 1))
        lane_pad = LANES - N_EXPERTS - N_GROUPS
        w_rt32 = jnp.pad(jnp.concatenate([router_e[l], router_g[l]], axis=1), ((0, 0), (0, lane_pad)))
        w_rt_hi = w_rt32.astype(BF16)
        w_rt = jnp.stack([w_rt_hi, (w_rt32 - w_rt_hi.astype(F32)).astype(BF16)])
        b_rt = jnp.pad(jnp.concatenate([router_e_b[l], router_g_b[l]]), (0, lane_pad)).reshape(1, LANES)

        def weights(wcat, sbias):
            return [row(norm1_g[l]), _col_tiles(w_in[l] * gate_half), _block_diag(pool_w[l]).astype(BF16), row(pool_scale[l]),
                    rep8(conf_dw[l]), rep8(conf_dw_b[l]), row(conf_ln_g[l]), row(conf_ln_b[l]), rep8(sconv_w[l]),
                    row(sgu_ln_g[l]), row(sgu_ln_b[l]), wcat, sbias,
                    _col_tiles(w_branch[l] * 0.5).reshape(-1, BRANCH_W, COL_TILE), _col_tiles(w_out[l]),
                    row(norm2_g[l]), w_rt, b_rt]

        h1, route, pool_p, conv_p, sconv_p, v_p = _mixer_call(
            cfg_p, bp, n, h_p, off_p, None, weights(cat(ws_p), bias_p), None)
        h1, route, pool_s, conv_s, sconv_s, v_s = _mixer_call(
            cfg_s, bs, n, h_s, off_s, (state_pool[l], state_conv[l], state_sconv[l]),
            weights(cat(ws_s), bias_s), (h1, route))
        states_out.append((pool_p, conv_p, sconv_p, v_p, pool_s, conv_s, sconv_s, v_s))

        outs = _moe(h1, route, row(norm2_g[l]), w_gate[l].astype(BF16), w_up[l].astype(BF16),
                    w_down[l].astype(BF16), final_row, l == depth - 1, n_p)
        h_p = h_s = outs[0]
        off_p, off_s = 0, n_p // MIX_TILE_S

    y_prompt = outs[0].reshape(bp, seq, D_MODEL)
    y_sample = outs[1].reshape(bs, dseq, D_MODEL)
    st = [jnp.stack([s[k] for s in states_out]) for k in range(8)]
    return (y_prompt, y_sample, st[0], st[1], st[2], st[3], st[4], st[5], st[6], st[7])
```

```python
import functools
from typing import NamedTuple

import jax
import jax.numpy as jnp
from jax import lax
from jax.experimental import pallas as pl
from jax.experimental.pallas import tpu as pltpu

F32 = jnp.float32
BF16 = jnp.bfloat16

D_MODEL = 1024
BRANCH_W = 256
N_BRANCH = 4
POOL_WINDOWS = (2, 4, 8, 16)
POOL_GW = 64
POOL_HIST = 15
CONF_WIDTH = 31
CONF_HIST = 30
SCONV_WIDTH = 3
SCONV_HIST = 2
CHUNK = 128
SGU_HEADS = 4
SGU_HW = 64
N_GROUPS = 4
EXPERTS_PER_GROUP = 8
N_EXPERTS = 32
D_EXPERT = 256
RMS_EPS = 1e-6
LN_EPS = 1e-5
IN_COLS = 6144
GATE_COL0 = 2048

COL_TILE = 256
LANES = 128
SUBLANES = 8
POOL_PAD = 32
CONF_PAD = 32
SCONV_PAD = 8
ROW_BLOCK = 32
ROUTE_GROUP_LANE0 = 32
NEG_BIG = -3.0e38

MIX_TILE = 512
MIX_TILE_S = 256
SORT_TILE = 512
SEG_PAD = 16
PERM_BLOCK = 256
N_PERM_BLOCKS = -(-(2 * SORT_TILE + N_EXPERTS * (SEG_PAD - 1)) // PERM_BLOCK)
LOCAL_ROWS = N_PERM_BLOCKS * PERM_BLOCK
EXP_TILE = 256
CHUNKS_PER_STEP = EXP_TILE // SEG_PAD
VMEM_LIMIT = 56 * 1024 * 1024


class MixCfg(NamedTuple):
    nb: int
    t: int
    has_state: bool
    start_pos: int
    row_off: int
    v_rows: int


def _rms(x, g):
    return x * lax.rsqrt(jnp.mean(x * x, axis=-1, keepdims=True) + RMS_EPS) * g


def _ln(x, g, b):
    mu = jnp.mean(x, axis=-1, keepdims=True)
    xc = x - mu
    return xc * lax.rsqrt(jnp.mean(xc * xc, axis=-1, keepdims=True) + LN_EPS) * g + b


def _sigmoid(x):
    return 0.5 * jnp.tanh(0.5 * x) + 0.5


def _zero_like_bits(x):
    bits = lax.bitcast_convert_type(x, jnp.uint32)
    return ((bits >> 16) >> 16).astype(jnp.int32).astype(F32)


def _gelu_tanh(x):
    return 0.5 * x * (1.0 + jnp.tanh(0.7978845608028654 * (x + 0.044715 * (x * x * x))))


def _row_blocks(nb, t):
    if t >= ROW_BLOCK:
        return [(slice(b, b + 1), t0, ROW_BLOCK) for b in range(nb) for t0 in range(0, t, ROW_BLOCK)]
    bb = ROW_BLOCK // t
    return [(slice(b0, b0 + bb), 0, t) for b0 in range(0, nb, bb)]


def _mixer_kernel(cfg, *refs):
    nb, t, tm = cfg.nb, cfg.t, cfg.nb * cfg.t
    refs = list(refs)
    x_ref = refs.pop(0)
    if cfg.has_state:
        pool_st, conv_st, sconv_st = refs.pop(0), refs.pop(0), refs.pop(0)
    (n1g, w_in, pool_w, pool_sc, cdw, cdb, clg, clb, scw, slg, slb, wcat, sbias,
     w_br, w_out, n2g, w_rt, b_rt) = refs[:18]
    refs = refs[18:]
    if cfg.has_state:
        refs = refs[2:]
    h1_o, route_o, pool_o, conv_o, sconv_o, v_o = refs[:6]
    pool_ext, sum_a, sum_b, conv_ext, conv_sh, sc_ext, buf_a, buf_b, buf_c, gate_buf, xb_buf = refs[6:]

    if cfg.has_state:
        seq_pos0 = cfg.start_pos
        pool_ext[:, 0:POOL_PAD - SUBLANES, :] = jnp.zeros((nb, POOL_PAD - SUBLANES, BRANCH_W), F32)
        pool_ext[:, POOL_PAD - POOL_HIST:POOL_PAD, :] = pool_st[...]
        conv_ext[:, CONF_PAD - CONF_HIST:CONF_PAD, :] = conv_st[...]
        sc_ext[:, SCONV_PAD - SCONV_HIST:SCONV_PAD, :] = sconv_st[...]
    else:
        c = pl.program_id(1)
        seq_pos0 = cfg.start_pos + c * t

        @pl.when(c == 0)
        def _():
            pool_ext[:, 0:POOL_PAD, :] = jnp.zeros((nb, POOL_PAD, BRANCH_W), F32)
            conv_ext[:, 0:CONF_PAD, :] = jnp.zeros((nb, CONF_PAD, BRANCH_W), F32)
            sc_ext[:, 0:SCONV_PAD, :] = jnp.zeros((nb, SCONV_PAD, BRANCH_W), F32)

    x = x_ref[...]
    xb_buf[...] = _rms(x, n1g[...]).astype(BF16)

    def proj(lo, hi):
        assert lo % COL_TILE == 0 and hi == lo + COL_TILE
        return jnp.dot(xb_buf[...], w_in[lo // COL_TILE], preferred_element_type=F32)

    a_pool = proj(0, 256)
    pool_ext[:, POOL_PAD:, :] = a_pool.reshape(nb, t, BRANCH_W)
    glu = proj(256, 512) * _sigmoid(proj(512, 768))
    conv_ext[:, CONF_PAD:, :] = glu.reshape(nb, t, BRANCH_W)
    z = proj(1024, 1280) * proj(1280, 1536)
    sc_ext[:, SCONV_PAD:, :] = z.reshape(nb, t, BRANCH_W)

    pl_len = POOL_PAD + t
    sum_b[:, 8:pl_len, :] = pool_ext[:, 8:pl_len, :] + pool_ext[:, 7:pl_len - 1, :]
    sum_a[:, 16:pl_len, :] = sum_b[:, 16:pl_len, :] + sum_b[:, 14:pl_len - 2, :]
    sum_b[:, 24:pl_len, :] = sum_a[:, 24:pl_len, :] + sum_a[:, 20:pl_len - 4, :]
    for sh in range(SUBLANES):
        n_rows = t + SUBLANES * ((CONF_WIDTH - 1 - sh) // SUBLANES)
        first = CONF_PAD - CONF_HIST + sh
        conv_sh[sh, :, 0:n_rows, :] = conv_ext[:, first:first + n_rows, :]

    blocks = _row_blocks(nb, t)
    gate_cols = (N_BRANCH * D_MODEL) // len(blocks)
    bias_tile = cdb[...]
    for bi, (bs, t0, tb) in enumerate(blocks):
        bb = bs.stop - bs.start
        shp = (bb, tb, BRANCH_W)
        lane = lax.broadcasted_iota(jnp.int32, shp, 2)
        pos = seq_pos0 + t0 + lax.broadcasted_iota(jnp.int32, shp, 1)
        r0 = POOL_PAD + t0
        cur = pool_ext[bs, r0:r0 + tb, :]
        s2 = cur + pool_ext[bs, r0 - 1:r0 - 1 + tb, :]
        s4 = sum_a[bs, r0:r0 + tb, :]
        s8 = sum_b[bs, r0:r0 + tb, :]
        s16 = s8 + sum_b[bs, r0 - SUBLANES:r0 - SUBLANES + tb, :]
        win_sum = jnp.where(lane < 64, s2, jnp.where(lane < 128, s4, jnp.where(lane < 192, s8, s16)))
        win = jnp.where(lane < 64, 2, jnp.where(lane < 128, 4, jnp.where(lane < 192, 8, 16)))
        cnt = jnp.minimum(pos + 1, win).astype(F32)
        buf_a[bs, t0:t0 + tb, :] = win_sum / cnt - cur
        tiles = (ROW_BLOCK // SUBLANES, SUBLANES, BRANCH_W)
        acc = jnp.broadcast_to(bias_tile[None], tiles)
        for k in range(CONF_WIDTH):
            q, sh = divmod(k, SUBLANES)
            r = t0 + SUBLANES * q
            acc = acc + conv_sh[sh, bs, r:r + tb, :].reshape(tiles) * cdw[k][None]
        buf_b[bs, t0:t0 + tb, :] = acc.reshape(shp)
        base = SCONV_PAD - SCONV_HIST + t0
        acc = sc_ext[bs, base:base + tb, :].reshape(tiles) * scw[0][None]
        for k in range(1, SCONV_WIDTH):
            acc = acc + sc_ext[bs, base + k:base + k + tb, :].reshape(tiles) * scw[k][None]
        buf_c[bs, t0:t0 + tb, :] = acc.reshape(shp)
        for g0 in range(bi * gate_cols, (bi + 1) * gate_cols, COL_TILE):
            g = jnp.tanh(proj(GATE_COL0 + g0, GATE_COL0 + g0 + COL_TILE)) + 1.0
            gate_buf[:, g0:g0 + COL_TILE] = g
        bias_tile = cdb[...] + _zero_like_bits(g[0:SUBLANES, :])

    pool_o[...] = pool_ext[:, POOL_PAD + t - POOL_HIST:POOL_PAD + t, :]
    conv_o[...] = conv_ext[:, CONF_PAD + t - CONF_HIST:CONF_PAD + t, :]
    sconv_o[...] = sc_ext[:, SCONV_PAD + t - SCONV_HIST:SCONV_PAD + t, :]
    if not cfg.has_state:
        pool_ext[:, 0:POOL_PAD, :] = pool_ext[:, t:t + POOL_PAD, :]
        conv_ext[:, 0:CONF_PAD, :] = conv_ext[:, t:t + CONF_PAD, :]
        sc_ext[:, 0:SCONV_PAD, :] = sc_ext[:, t:t + SCONV_PAD, :]

    pooled = buf_a[...].reshape(tm, BRANCH_W).astype(BF16)
    br_a = jnp.dot(pooled, pool_w[...], preferred_element_type=F32) * pool_sc[...]

    cb = _ln(buf_b[...].reshape(tm, BRANCH_W), clg[...], clb[...])
    br_b = cb * _sigmoid(cb)

    br_c = proj(768, 1024) * buf_c[...].reshape(tm, BRANCH_W)

    gu = _gelu_tanh(proj(1536, 1792))
    v = _ln(_gelu_tanh(proj(1792, 2048)), slg[...], slb[...])
    if cfg.has_state:
        v_o[...] = v.reshape(nb, t, BRANCH_W)
    else:
        v_o[...] = v[tm - cfg.v_rows:, :].reshape(1, cfg.v_rows, BRANCH_W)
    head = lax.broadcasted_iota(jnp.int32, (CHUNK, BRANCH_W), 1) // SGU_HW
    mixed = []
    for j in range(tm // CHUNK):
        vj = v[j * CHUNK:(j + 1) * CHUNK, :]
        stacked = jnp.concatenate([jnp.where(head == h, vj, 0.0) for h in range(SGU_HEADS)], axis=0)
        mixed.append(jnp.dot(wcat[...], stacked.astype(BF16), preferred_element_type=F32) + sbias[...])
    br_d = gu * jnp.concatenate(mixed, axis=0)

    n_ct = D_MODEL // COL_TILE
    brs = [br.astype(BF16) for br in (br_a, br_b, br_c, br_d)]
    for c in range(n_ct):
        part = None
        for i in range(N_BRANCH):
            g0 = i * D_MODEL + c * COL_TILE
            term = gate_buf[:, g0:g0 + COL_TILE] * jnp.dot(
                brs[i], w_br[i * n_ct + c], preferred_element_type=F32)
            part = term if part is None else part + term
        xb_buf[:, c * COL_TILE:(c + 1) * COL_TILE] = part.astype(BF16)
    for c in range(n_ct):
        cs = slice(c * COL_TILE, (c + 1) * COL_TILE)
        h1_o[:, cs] = x_ref[:, cs] + jnp.dot(xb_buf[...], w_out[c], preferred_element_type=F32)

    xn2 = _rms(h1_o[...], n2g[...])
    x_hi = xn2.astype(BF16)
    x_lo = (xn2 - x_hi.astype(F32)).astype(BF16)
    logits = (jnp.dot(x_hi, w_rt[0], preferred_element_type=F32)
              + (jnp.dot(x_lo, w_rt[0], preferred_element_type=F32)
                 + jnp.dot(x_hi, w_rt[1], preferred_element_type=F32))) + b_rt[...]
    lane = lax.broadcasted_iota(jnp.int32, (tm, LANES), 1)
    is_g = (lane >= ROUTE_GROUP_LANE0) & (lane < ROUTE_GROUP_LANE0 + N_GROUPS)
    glog = jnp.where(is_g, logits, NEG_BIG)
    gmax = jnp.max(glog, axis=-1, keepdims=True)
    lane_f = lane.astype(F32)
    no_lane = float(4 * LANES)
    gsel = jnp.min(jnp.where(glog == gmax, lane_f, no_lane), axis=-1, keepdims=True) - ROUTE_GROUP_LANE0
    pg = 1.0 / jnp.sum(jnp.where(is_g, jnp.exp(glog - gmax), 0.0), axis=-1, keepdims=True)
    in_grp = (lane < N_EXPERTS) & ((lane // EXPERTS_PER_GROUP) == gsel.astype(jnp.int32))
    el = jnp.where(in_grp, logits, NEG_BIG)
    m1 = jnp.max(el, axis=-1, keepdims=True)
    i1 = jnp.min(jnp.where(in_grp & (el == m1), lane_f, no_lane), axis=-1, keepdims=True)
    rest = in_grp & (lane_f != i1)
    el2 = jnp.where(rest, logits, NEG_BIG)
    m2 = jnp.max(el2, axis=-1, keepdims=True)
    i2 = jnp.min(jnp.where(rest & (el2 == m2), lane_f, no_lane), axis=-1, keepdims=True)
    e21 = jnp.exp(m2 - m1)
    w1 = pg / (1.0 + e21)
    w2 = pg * e21 / (1.0 + e21)
    route_o[...] = jnp.where(lane == 0, i1, jnp.where(lane == 1, i2,
                             jnp.where(lane == 2, w1, jnp.where(lane == 3, w2, 0.0))))


def _const_spec(shape):
    nd = len(shape)
    return pl.BlockSpec(shape, lambda *_: (0,) * nd, pipeline_mode=pl.Buffered(1))


def _mixer_call(cfg, n_seq, n_rows_total, x2d, x_tile_off, states, weights, aliased):
    nb, t = cfg.nb, cfg.t
    tm = nb * t
    if cfg.has_state:
        grid = (n_seq // nb,)
        tok = lambda i: (x_tile_off + i, 0)
        out_tok = lambda i: (cfg.row_off + i, 0)
        seq3 = lambda i: (i, 0, 0)
    else:
        grid = (n_seq, 2048 // t)
        nt = grid[1]
        tok = lambda b, c: (x_tile_off + b * nt + c, 0)
        out_tok = lambda b, c: (cfg.row_off + b * nt + c, 0)
        seq3 = lambda b, c: (b, 0, 0)

    in_specs = [pl.BlockSpec((tm, D_MODEL), tok)]
    args = [x2d]
    if cfg.has_state:
        for s in states:
            in_specs.append(pl.BlockSpec((nb,) + s.shape[1:], seq3))
            args.append(s)
    for w in weights:
        in_specs.append(_const_spec(w.shape))
        args.append(w)
    io_alias = {}
    if cfg.has_state:
        for k, a in enumerate(aliased):
            in_specs.append(pl.BlockSpec(memory_space=pl.ANY))
            io_alias[len(args)] = k
            args.append(a)

    out_shape = [
        jax.ShapeDtypeStruct((n_rows_total, D_MODEL), F32),
        jax.ShapeDtypeStruct((n_rows_total, LANES), F32),
        jax.ShapeDtypeStruct((n_seq, POOL_HIST, BRANCH_W), F32),
        jax.ShapeDtypeStruct((n_seq, CONF_HIST, BRANCH_W), F32),
        jax.ShapeDtypeStruct((n_seq, SCONV_HIST, BRANCH_W), F32),
        jax.ShapeDtypeStruct((n_seq, cfg.v_rows, BRANCH_W), F32),
    ]
    out_specs = [
        pl.BlockSpec((tm, D_MODEL), out_tok),
        pl.BlockSpec((tm, LANES), out_tok),
        pl.BlockSpec((nb, POOL_HIST, BRANCH_W), seq3),
        pl.BlockSpec((nb, CONF_HIST, BRANCH_W), seq3),
        pl.BlockSpec((nb, SCONV_HIST, BRANCH_W), seq3),
        pl.BlockSpec((nb, cfg.v_rows, BRANCH_W), seq3),
    ]
    scratch = [
        pltpu.VMEM((nb, POOL_PAD + t, BRANCH_W), F32),
        pltpu.VMEM((nb, POOL_PAD + t, BRANCH_W), F32),
        pltpu.VMEM((nb, POOL_PAD + t, BRANCH_W), F32),
        pltpu.VMEM((nb, CONF_PAD + t, BRANCH_W), F32),
        pltpu.VMEM((SUBLANES, nb, t + CONF_PAD - SUBLANES, BRANCH_W), F32),
        pltpu.VMEM((nb, SCONV_PAD + t, BRANCH_W), F32),
        pltpu.VMEM((nb, t, BRANCH_W), F32),
        pltpu.VMEM((nb, t, BRANCH_W), F32),
        pltpu.VMEM((nb, t, BRANCH_W), F32),
        pltpu.VMEM((tm, N_BRANCH * D_MODEL), F32),
        pltpu.VMEM((tm, D_MODEL), BF16),
    ]
    return pl.pallas_call(
        functools.partial(_mixer_kernel, cfg),
        grid=grid,
        in_specs=in_specs,
        out_specs=out_specs,
        out_shape=out_shape,
        scratch_shapes=scratch,
        input_output_aliases=io_alias,
        compiler_params=pltpu.CompilerParams(
            dimension_semantics=("arbitrary",) * len(grid), vmem_limit_bytes=VMEM_LIMIT),
        name="mixer_sample" if cfg.has_state else "mixer_prompt",
    )(*args)


def _sort_kernel(h1_ref, route_ref, n2g, xl_o, pos_o, nch_o):
    t = SORT_TILE
    xn = _rms(h1_ref[...], n2g[...]).astype(BF16)
    r = route_ref[...]
    lane_i = lax.broadcasted_iota(jnp.int32, (t, LANES), 1)
    lane = lane_i.astype(F32)
    hit1 = lane == r[:, 0:1]
    hit2 = lane == r[:, 1:2]
    onehot = jnp.where(hit1 | hit2, 1.0, 0.0)
    row = lax.broadcasted_iota(jnp.int32, (t, t), 0)
    col = lax.broadcasted_iota(jnp.int32, (t, t), 1)
    strict_lower = jnp.where(col < row, 1.0, 0.0).astype(BF16)
    before = jnp.dot(strict_lower, onehot.astype(BF16), preferred_element_type=F32)
    cnt = jnp.sum(onehot, axis=0, keepdims=True)
    nch = jnp.floor((cnt + (SEG_PAD - 1)) * (1.0 / SEG_PAD))
    er = lax.broadcasted_iota(jnp.int32, (LANES, LANES), 0)
    ec = lax.broadcasted_iota(jnp.int32, (LANES, LANES), 1)
    strict_upper = jnp.where(er < ec, 1.0, 0.0).astype(BF16)
    nch8 = jnp.broadcast_to(nch, (SUBLANES, LANES)).astype(BF16)
    seg0 = jnp.dot(nch8, strict_upper, preferred_element_type=F32)[0:1, :] * SEG_PAD
    where = seg0 + before
    pos0 = jnp.sum(jnp.where(hit1, where, 0.0), axis=-1, keepdims=True)
    pos1 = jnp.sum(jnp.where(hit2, where, 0.0), axis=-1, keepdims=True)
    pos = jnp.where(lane_i == 0, pos0, jnp.where(lane_i == 1, pos1, -1.0))
    pos_o[...] = pos
    nch_o[...] = jnp.broadcast_to(nch, (1, SUBLANES, LANES))
    pos_t = jnp.transpose(pos)
    p0 = pos_t[0:1, :]
    p1 = pos_t[1:2, :]
    for blk in range(N_PERM_BLOCKS):
        dst = (lax.broadcasted_iota(jnp.int32, (PERM_BLOCK, t), 0) + blk * PERM_BLOCK).astype(F32)
        perm = jnp.where((dst == p0) | (dst == p1), 1.0, 0.0).astype(BF16)
        xl_o[0, blk * PERM_BLOCK:(blk + 1) * PERM_BLOCK, :] = jnp.dot(
            perm, xn, preferred_element_type=F32).astype(BF16)


def _sort_call(h1, route, n2g):
    n = h1.shape[0]
    nt = n // SORT_TILE
    return pl.pallas_call(
        _sort_kernel,
        grid=(nt,),
        in_specs=[pl.BlockSpec((SORT_TILE, D_MODEL), lambda i: (i, 0)),
                  pl.BlockSpec((SORT_TILE, LANES), lambda i: (i, 0)),
                  pl.BlockSpec((1, D_MODEL), lambda i: (0, 0))],
        out_specs=[pl.BlockSpec((1, LOCAL_ROWS, D_MODEL), lambda i: (i, 0, 0)),
                   pl.BlockSpec((SORT_TILE, LANES), lambda i: (i, 0)),
                   pl.BlockSpec((1, SUBLANES, LANES), lambda i: (i, 0, 0))],
        out_shape=[jax.ShapeDtypeStruct((nt + 1, LOCAL_ROWS, D_MODEL), BF16),
                   jax.ShapeDtypeStruct((n, LANES), F32),
                   jax.ShapeDtypeStruct((nt, SUBLANES, LANES), F32)],
        compiler_params=pltpu.CompilerParams(dimension_semantics=("arbitrary",), vmem_limit_bytes=VMEM_LIMIT),
        name="moe_sort",
    )(h1, route, n2g)


def _chunk_copy(src, src_row, dst, dst_row, sem):
    return pltpu.make_async_copy(src.at[pl.ds(src_row, SEG_PAD), :], dst.at[pl.ds(dst_row, SEG_PAD), :], sem)


def _expert_kernel(trash_row0, se_ref, nv_ref, src_ref, dst_ref, xl_in, wg, wu, wd, xl_io,
                   xbuf0, xbuf1, ybuf0, ybuf1, sem_in, sem_out):
    del se_ref, xl_in
    xbuf, ybuf = (xbuf0, xbuf1), (ybuf0, ybuf1)
    s = pl.program_id(0)
    nv = nv_ref[0]

    def gather(step, sl):
        return [_chunk_copy(xl_io, pl.multiple_of(src_ref[step * CHUNKS_PER_STEP + j], SEG_PAD),
                            xbuf[sl], j * SEG_PAD, sem_in.at[sl]) for j in range(CHUNKS_PER_STEP)]

    def write_back(step, sl):
        return [_chunk_copy(ybuf[sl], j * SEG_PAD, xl_io,
                            pl.multiple_of(dst_ref[step * CHUNKS_PER_STEP + j], SEG_PAD), sem_out.at[sl])
                for j in range(CHUNKS_PER_STEP)]

    def spare_write(sl):
        return [_chunk_copy(ybuf[sl], j * SEG_PAD, xl_io, trash_row0 + sl * EXP_TILE + j * SEG_PAD,
                            sem_out.at[sl]) for j in range(CHUNKS_PER_STEP)]

    @pl.when(s == 0)
    def _():
        for c in gather(0, 0):
            c.start()
        for sl in range(2):
            ybuf[sl][...] = jnp.zeros_like(ybuf[sl])
            for c in spare_write(sl):
                c.start()

    def live_step(sl):
        for c in gather(s, sl):
            c.wait()
        for c in write_back(s, sl):
            c.wait()
        for c in gather(s + 1, 1 - sl):
            c.start()
        xb = xbuf[sl][...]
        a = jnp.dot(xb, wg[0, 0].astype(BF16), preferred_element_type=F32)
        b = jnp.dot(xb, wu[0, 0].astype(BF16), preferred_element_type=F32)
        hh = (a * _sigmoid(a) * b).astype(BF16)
        ybuf[sl][...] = jnp.dot(hh, wd[0, 0].astype(BF16), preferred_element_type=F32).astype(BF16)
        for c in write_back(s, sl):
            c.start()

    def drain_step(sl):
        for c in gather(s, sl):
            c.wait()
        for both in range(2):
            for c in spare_write(both):
                c.wait()

    for sl in range(2):
        pl.when((s < nv) & (s % 2 == sl))(functools.partial(live_step, sl))
        pl.when((s == nv) & (s % 2 == sl))(functools.partial(drain_step, sl))


def _expert_call(xl, trash_row0, step_expert, n_valid, chunk_src, chunk_dst, layer, w_gate, w_up, w_down):
    n_steps = step_expert.shape[0]
    wmap = lambda s, se, *_: (layer, se[s], 0, 0)
    grid_spec = pltpu.PrefetchScalarGridSpec(
        num_scalar_prefetch=4,
        grid=(n_steps,),
        in_specs=[pl.BlockSpec(memory_space=pl.ANY),
                  pl.BlockSpec((1, 1, D_MODEL, D_EXPERT), wmap),
                  pl.BlockSpec((1, 1, D_MODEL, D_EXPERT), wmap),
                  pl.BlockSpec((1, 1, D_EXPERT, D_MODEL), wmap)],
        out_specs=pl.BlockSpec(memory_space=pl.ANY),
        scratch_shapes=[pltpu.VMEM((EXP_TILE, D_MODEL), BF16) for _ in range(4)] + [
                        pltpu.SemaphoreType.DMA((2,)), pltpu.SemaphoreType.DMA((2,))],
    )
    return pl.pallas_call(
        functools.partial(_expert_kernel, trash_row0),
        grid_spec=grid_spec,
        out_shape=jax.ShapeDtypeStruct(xl.shape, BF16),
        input_output_aliases={4: 0},
        compiler_params=pltpu.CompilerParams(dimension_semantics=("arbitrary",), has_side_effects=True),
        name="moe_experts",
    )(step_expert, n_valid, chunk_src, chunk_dst, xl, w_gate, w_up, w_down)


def _combine_kernel(final_norm, n_first, h1_ref, route_ref, pos_ref, fg, yl_ref, *outs):
    t = SORT_TILE
    r = route_ref[...]
    p = pos_ref[...]
    w1, w2 = r[:, 2:3], r[:, 3:4]
    pos0, pos1 = p[:, 0:1], p[:, 1:2]
    acc = h1_ref[...]
    for blk in range(N_PERM_BLOCKS):
        src = (lax.broadcasted_iota(jnp.int32, (t, PERM_BLOCK), 1) + blk * PERM_BLOCK).astype(F32)
        pw = jnp.where(src == pos0, w1, 0.0) + jnp.where(src == pos1, w2, 0.0)
        acc = acc + jnp.dot(pw.astype(BF16), yl_ref[0, blk * PERM_BLOCK:(blk + 1) * PERM_BLOCK, :],
                            preferred_element_type=F32)
    if not final_norm:
        outs[0][...] = acc
        return
    y = _rms(acc, fg[...])
    i = pl.program_id(0)

    @pl.when(i < n_first)
    def _():
        outs[0][...] = y

    @pl.when(i >= n_first)
    def _():
        outs[1][...] = y


def _combine_call(h1, route, pos, yl, final_g, final_norm, n_first_rows):
    n = h1.shape[0]
    nt = n // SORT_TILE
    n_first = n_first_rows // SORT_TILE
    tok = lambda i: (i, 0)
    if final_norm:
        out_shape = [jax.ShapeDtypeStruct((n_first_rows, D_MODEL), F32),
                     jax.ShapeDtypeStruct((n - n_first_rows, D_MODEL), F32)]
        out_specs = [pl.BlockSpec((SORT_TILE, D_MODEL), lambda i: (jnp.minimum(i, n_first - 1), 0)),
                     pl.BlockSpec((SORT_TILE, D_MODEL), lambda i: (jnp.maximum(i - n_first, 0), 0))]
    else:
        out_shape = [jax.ShapeDtypeStruct((n, D_MODEL), F32)]
        out_specs = [pl.BlockSpec((SORT_TILE, D_MODEL), tok)]
    return pl.pallas_call(
        functools.partial(_combine_kernel, final_norm, n_first),
        grid=(nt,),
        in_specs=[pl.BlockSpec((SORT_TILE, D_MODEL), tok),
                  pl.BlockSpec((SORT_TILE, LANES), tok),
                  pl.BlockSpec((SORT_TILE, LANES), tok),
                  pl.BlockSpec((1, D_MODEL), lambda i: (0, 0)),
                  pl.BlockSpec((1, LOCAL_ROWS, D_MODEL), lambda i: (i, 0, 0))],
        out_specs=out_specs,
        out_shape=out_shape,
        compiler_params=pltpu.CompilerParams(dimension_semantics=("arbitrary",), vmem_limit_bytes=VMEM_LIMIT),
        name="moe_combine",
    )(h1, route, pos, final_g, yl)


def _expert_tables(nch, n_steps, trash_row0):
    nt = nch.shape[0]
    cps = CHUNKS_PER_STEP
    seg_row0 = (jnp.cumsum(nch, axis=1) - nch) * SEG_PAD
    first = jnp.cumsum(nch, axis=0) - nch
    tot = jnp.sum(nch, axis=0)
    steps = (tot + cps - 1) // cps
    step_end = jnp.cumsum(steps)
    n_valid = step_end[-1:]
    s_ids = jnp.arange(n_steps, dtype=jnp.int32)
    step_expert = jnp.minimum(jnp.sum((step_end[None, :] <= s_ids[:, None]).astype(jnp.int32), axis=1),
                              N_EXPERTS - 1)
    sel = (step_expert[:, None] == jnp.arange(N_EXPERTS, dtype=jnp.int32)[None, :]).astype(jnp.int32)
    step0 = sel @ (step_end - steps)
    tot_s = sel @ tot
    first_s = sel @ first.T
    row0_s = sel @ seg_row0.T
    k = (s_ids - step0)[:, None] * cps + jnp.arange(cps, dtype=jnp.int32)[None, :]
    ok = (k < tot_s[:, None]) & (s_ids < n_valid[0])[:, None]
    tile = jnp.sum((first_s[:, None, :] <= k[:, :, None]).astype(jnp.int32), axis=2) - 1
    tsel = (tile[:, :, None] == jnp.arange(nt, dtype=jnp.int32)[None, None, :]).astype(jnp.int32)
    first_k = jnp.sum(tsel * first_s[:, None, :], axis=2)
    row0_k = jnp.sum(tsel * row0_s[:, None, :], axis=2)
    row = tile * LOCAL_ROWS + row0_k + (k - first_k) * SEG_PAD
    live = (s_ids < n_valid[0])[:, None]
    src = jnp.where(live, jnp.where(ok, row, row[:, 0:1]), 0)
    spare = trash_row0 + (s_ids % 2)[:, None] * EXP_TILE + jnp.arange(cps, dtype=jnp.int32)[None, :] * SEG_PAD
    dst = jnp.where(ok, row, spare)
    return (step_expert, n_valid.astype(jnp.int32), src.reshape(-1).astype(jnp.int32),
            dst.reshape(-1).astype(jnp.int32))


def _moe(h1, route, n2g, layer, w_gate, w_up, w_down, final_g, final_norm, n_first_rows):
    n = h1.shape[0]
    nt = n // SORT_TILE
    xl, pos, nch = _sort_call(h1, route, n2g)
    nch = nch[:, 0, :N_EXPERTS].astype(jnp.int32)
    max_chunks = (2 * n) // SEG_PAD + nt * N_EXPERTS
    n_steps = max_chunks // CHUNKS_PER_STEP + N_EXPERTS + 1
    trash_row0 = nt * LOCAL_ROWS
    step_expert, n_valid, chunk_src, chunk_dst = _expert_tables(nch, n_steps, trash_row0)
    yl = _expert_call(xl.reshape((nt + 1) * LOCAL_ROWS, D_MODEL), trash_row0, step_expert, n_valid,
                      chunk_src, chunk_dst, layer, w_gate, w_up, w_down)
    return _combine_call(h1, route, pos, yl.reshape(nt + 1, LOCAL_ROWS, D_MODEL), final_g, final_norm, n_first_rows)


def _col_tiles(w):
    *lead, k, n = w.shape
    wt = w.astype(BF16).reshape(*lead, k, n // COL_TILE, COL_TILE)
    return jnp.swapaxes(wt, -3, -2)


def _block_diag(pw):
    out = jnp.zeros((BRANCH_W, BRANCH_W), pw.dtype)
    for g in range(len(POOL_WINDOWS)):
        out = out.at[g * POOL_GW:(g + 1) * POOL_GW, g * POOL_GW:(g + 1) * POOL_GW].set(pw[g])
    return out


def kernel(x_prompt, x_sample, state_pool, state_conv, state_sconv, norm1_g, w_in, pool_w, pool_scale, conf_dw, conf_dw_b, conf_ln_g, conf_ln_b, sconv_w, sgu_ln_g, sgu_ln_b, sgu_ws, sgu_b, w_branch, w_out, norm2_g, router_g, router_g_b, router_e, router_e_b, w_gate, w_up, w_down, final_g):
    depth = w_in.shape[0]
    bp, seq, _ = x_prompt.shape
    bs, dseq, _ = x_sample.shape
    n_p, n_s = bp * seq, bs * dseq
    n = n_p + n_s
    past_len = 16384
    assert seq % MIX_TILE == 0 and n_s % MIX_TILE_S == 0 and n_p % MIX_TILE_S == 0 and dseq == SUBLANES
    assert MIX_TILE % CHUNK == 0 and MIX_TILE_S % CHUNK == 0

    cfg_p = MixCfg(nb=1, t=MIX_TILE, has_state=False, start_pos=0, row_off=0, v_rows=CHUNK)
    cfg_s = MixCfg(nb=MIX_TILE_S // dseq, t=dseq, has_state=True, start_pos=past_len,
                   row_off=n_p // MIX_TILE_S, v_rows=dseq)

    row = lambda a: a.reshape(1, -1)
    rep8 = lambda a: jnp.broadcast_to(a[..., None, :], a.shape[:-1] + (SUBLANES, a.shape[-1]))
    tril = jnp.tril(jnp.ones((CHUNK, CHUNK), F32))
    eye_blk = jnp.kron(jnp.eye(CHUNK // dseq, dtype=F32), jnp.ones((dseq, dseq), F32))
    final_row = row(final_g)
    gate_half = jnp.where(jnp.arange(IN_COLS) < GATE_COL0, 1.0, 0.5).astype(F32)[None, :]

    h_p, h_s = x_prompt.reshape(n_p, D_MODEL), x_sample.reshape(n_s, D_MODEL)
    off_p, off_s = 0, 0
    states_out = []
    for l in range(depth):
        ws_p = sgu_ws[l] * tril[None]
        ws_small = jnp.tile(sgu_ws[l][:, :dseq, :dseq], (1, CHUNK // dseq, CHUNK // dseq))
        ws_s = ws_small * (tril * eye_blk)[None]
        cat = lambda w: jnp.concatenate([w[h] for h in range(SGU_HEADS)], axis=1).astype(BF16)
        bias_p = jnp.repeat(sgu_b[l].T, SGU_HW, axis=1)
        bias_s = jnp.tile(jnp.repeat(sgu_b[l][:, :dseq].T, SGU_HW, axis=1), (CHUNK // dseq, 1))
        lane_pad = LANES - N_EXPERTS - N_GROUPS
        w_rt32 = jnp.pad(jnp.concatenate([router_e[l], router_g[l]], axis=1), ((0, 0), (0, lane_pad)))
        w_rt_hi = w_rt32.astype(BF16)
        w_rt = jnp.stack([w_rt_hi, (w_rt32 - w_rt_hi.astype(F32)).astype(BF16)])
        b_rt = jnp.pad(jnp.concatenate([router_e_b[l], router_g_b[l]]), (0, lane_pad)).reshape(1, LANES)

        def weights(wcat, sbias):
            return [row(norm1_g[l]), _col_tiles(w_in[l] * gate_half), _block_diag(pool_w[l]).astype(BF16), row(pool_scale[l]),
                    rep8(conf_dw[l]), rep8(conf_dw_b[l]), row(conf_ln_g[l]), row(conf_ln_b[l]), rep8(sconv_w[l]),
                    row(sgu_ln_g[l]), row(sgu_ln_b[l]), wcat, sbias,
                    _col_tiles(w_branch[l] * 0.5).reshape(-1, BRANCH_W, COL_TILE), _col_tiles(w_out[l]),
                    row(norm2_g[l]), w_rt, b_rt]

        h1, route, pool_p, conv_p, sconv_p, v_p = _mixer_call(
            cfg_p, bp, n, h_p, off_p, None, weights(cat(ws_p), bias_p), None)
        h1, route, pool_s, conv_s, sconv_s, v_s = _mixer_call(
            cfg_s, bs, n, h_s, off_s, (state_pool[l], state_conv[l], state_sconv[l]),
            weights(cat(ws_s), bias_s), (h1, route))
        states_out.append((pool_p, conv_p, sconv_p, v_p, pool_s, conv_s, sconv_s, v_s))

        outs = _moe(h1, route, row(norm2_g[l]), l, w_gate, w_up, w_down, final_row, l == depth - 1, n_p)
        h_p = h_s = outs[0]
        off_p, off_s = 0, n_p // MIX_TILE_S

    y_prompt = outs[0].reshape(bp, seq, D_MODEL)
    y_sample = outs[1].reshape(bs, dseq, D_MODEL)
    st = [jnp.stack([s[k] for s in states_out]) for k in range(8)]
    return (y_prompt, y_sample, st[0], st[1], st[2], st[3], st[4], st[5], st[6], st[7])
```

```python
import functools
from typing import NamedTuple

import jax
import jax.numpy as jnp
from jax import lax
from jax.experimental import pallas as pl
from jax.experimental.pallas import tpu as pltpu

F32 = jnp.float32
BF16 = jnp.bfloat16

D_MODEL = 1024
BRANCH_W = 256
N_BRANCH = 4
POOL_WINDOWS = (2, 4, 8, 16)
POOL_GW = 64
POOL_HIST = 15
CONF_WIDTH = 31
CONF_HIST = 30
SCONV_WIDTH = 3
SCONV_HIST = 2
CHUNK = 128
SGU_HEADS = 4
SGU_HW = 64
N_GROUPS = 4
EXPERTS_PER_GROUP = 8
N_EXPERTS = 32
D_EXPERT = 256
RMS_EPS = 1e-6
LN_EPS = 1e-5
IN_COLS = 6144
GATE_COL0 = 2048

COL_TILE = 256
LANES = 128
SUBLANES = 8
POOL_PAD = 32
CONF_PAD = 32
SCONV_PAD = 8
ROW_BLOCK = 32
ROUTE_GROUP_LANE0 = 32
NEG_BIG = -3.0e38

MIX_TILE = 512
MIX_TILE_S = 256
SORT_TILE = 512
SEG_PAD = 16
PERM_BLOCK = 256
N_PERM_BLOCKS = -(-(2 * SORT_TILE + N_EXPERTS * (SEG_PAD - 1)) // PERM_BLOCK)
LOCAL_ROWS = N_PERM_BLOCKS * PERM_BLOCK
EXP_TILE = 512
CHUNKS_PER_STEP = EXP_TILE // SEG_PAD
VMEM_LIMIT = 56 * 1024 * 1024


class MixCfg(NamedTuple):
    layer: int
    nb: int
    t: int
    has_state: bool
    start_pos: int
    row_off: int
    v_rows: int


def _rms(x, g):
    return x * lax.rsqrt(jnp.mean(x * x, axis=-1, keepdims=True) + RMS_EPS) * g


def _ln(x, g, b):
    mu = jnp.mean(x, axis=-1, keepdims=True)
    xc = x - mu
    return xc * lax.rsqrt(jnp.mean(xc * xc, axis=-1, keepdims=True) + LN_EPS) * g + b


def _sigmoid(x):
    return 0.5 * jnp.tanh(0.5 * x) + 0.5


def _zero_like_bits(x):
    bits = lax.bitcast_convert_type(x, jnp.uint32)
    return ((bits >> 16) >> 16).astype(jnp.int32).astype(F32)


def _gelu_tanh(x):
    return 0.5 * x * (1.0 + jnp.tanh(0.7978845608028654 * (x + 0.044715 * (x * x * x))))


def _row_blocks(nb, t):
    if t >= ROW_BLOCK:
        return [(slice(b, b + 1), t0, ROW_BLOCK) for b in range(nb) for t0 in range(0, t, ROW_BLOCK)]
    bb = ROW_BLOCK // t
    return [(slice(b0, b0 + bb), 0, t) for b0 in range(0, nb, bb)]


def _load_big_weights(layer, w_in_hbm, w_br_hbm, w_out_hbm, w_in, w_br, w_out, stage, stage_b, sem):
    n_in, n_out = IN_COLS // COL_TILE, D_MODEL // COL_TILE
    jobs = [("in", j) for j in range(n_in)] + [("out", j) for j in range(n_out)] + [("br", i) for i in range(N_BRANCH)]

    def copy(job, slot):
        kind, j = job
        if kind == "br":
            return pltpu.make_async_copy(w_br_hbm.at[layer, j], stage_b.at[slot], sem.at[slot])
        src = w_in_hbm if kind == "in" else w_out_hbm
        return pltpu.make_async_copy(src.at[layer, :, pl.ds(j * COL_TILE, COL_TILE)], stage.at[slot], sem.at[slot])

    copy(jobs[0], 0).start()
    for n, job in enumerate(jobs):
        slot = n % 2
        if n + 1 < len(jobs):
            copy(jobs[n + 1], 1 - slot).start()
        copy(job, slot).wait()
        kind, j = job
        if kind == "in":
            tile = stage[slot]
            w_in[j] = (tile * 0.5 if j * COL_TILE >= GATE_COL0 else tile).astype(BF16)
        elif kind == "out":
            w_out[j] = stage[slot].astype(BF16)
        else:
            for c in range(n_out):
                w_br[j * n_out + c] = (stage_b[slot, :, c * COL_TILE:(c + 1) * COL_TILE] * 0.5).astype(BF16)


def _mixer_kernel(cfg, *refs):
    nb, t, tm = cfg.nb, cfg.t, cfg.nb * cfg.t
    refs = list(refs)
    x_ref = refs.pop(0)
    if cfg.has_state:
        pool_st, conv_st, sconv_st = refs.pop(0), refs.pop(0), refs.pop(0)
    (n1g, pool_w, pool_sc, cdw, cdb, clg, clb, scw, slg, slb, wcat, sbias, n2g, w_rt, b_rt,
     w_in_hbm, w_br_hbm, w_out_hbm) = refs[:18]
    refs = refs[18:]
    if cfg.has_state:
        refs = refs[2:]
    h1_o, route_o, pool_o, conv_o, sconv_o, v_o = refs[:6]
    (pool_ext, sum_a, sum_b, conv_ext, conv_sh, sc_ext, buf_a, buf_b, buf_c, gate_buf, xb_buf,
     w_in, w_br, w_out, stage, stage_b, w_sem) = refs[6:]

    first_step = pl.program_id(0) == 0
    if not cfg.has_state:
        first_step = first_step & (pl.program_id(1) == 0)
    pl.when(first_step)(functools.partial(
        _load_big_weights, cfg.layer, w_in_hbm, w_br_hbm, w_out_hbm, w_in, w_br, w_out, stage, stage_b, w_sem))

    if cfg.has_state:
        seq_pos0 = cfg.start_pos
        pool_ext[:, 0:POOL_PAD - SUBLANES, :] = jnp.zeros((nb, POOL_PAD - SUBLANES, BRANCH_W), F32)
        pool_ext[:, POOL_PAD - POOL_HIST:POOL_PAD, :] = pool_st[...]
        conv_ext[:, CONF_PAD - CONF_HIST:CONF_PAD, :] = conv_st[...]
        sc_ext[:, SCONV_PAD - SCONV_HIST:SCONV_PAD, :] = sconv_st[...]
    else:
        c = pl.program_id(1)
        seq_pos0 = cfg.start_pos + c * t

        @pl.when(c == 0)
        def _():
            pool_ext[:, 0:POOL_PAD, :] = jnp.zeros((nb, POOL_PAD, BRANCH_W), F32)
            conv_ext[:, 0:CONF_PAD, :] = jnp.zeros((nb, CONF_PAD, BRANCH_W), F32)
            sc_ext[:, 0:SCONV_PAD, :] = jnp.zeros((nb, SCONV_PAD, BRANCH_W), F32)

    x = x_ref[...]
    xb_buf[...] = _rms(x, n1g[...]).astype(BF16)

    def proj(lo, hi):
        assert lo % COL_TILE == 0 and hi == lo + COL_TILE
        return jnp.dot(xb_buf[...], w_in[lo // COL_TILE], preferred_element_type=F32)

    a_pool = proj(0, 256)
    pool_ext[:, POOL_PAD:, :] = a_pool.reshape(nb, t, BRANCH_W)
    glu = proj(256, 512) * _sigmoid(proj(512, 768))
    conv_ext[:, CONF_PAD:, :] = glu.reshape(nb, t, BRANCH_W)
    z = proj(1024, 1280) * proj(1280, 1536)
    sc_ext[:, SCONV_PAD:, :] = z.reshape(nb, t, BRANCH_W)

    pl_len = POOL_PAD + t
    sum_b[:, 8:pl_len, :] = pool_ext[:, 8:pl_len, :] + pool_ext[:, 7:pl_len - 1, :]
    sum_a[:, 16:pl_len, :] = sum_b[:, 16:pl_len, :] + sum_b[:, 14:pl_len - 2, :]
    sum_b[:, 24:pl_len, :] = sum_a[:, 24:pl_len, :] + sum_a[:, 20:pl_len - 4, :]
    for sh in range(SUBLANES):
        n_rows = t + SUBLANES * ((CONF_WIDTH - 1 - sh) // SUBLANES)
        first = CONF_PAD - CONF_HIST + sh
        conv_sh[sh, :, 0:n_rows, :] = conv_ext[:, first:first + n_rows, :]

    blocks = _row_blocks(nb, t)
    gate_cols = (N_BRANCH * D_MODEL) // len(blocks)
    bias_tile = cdb[...]
    for bi, (bs, t0, tb) in enumerate(blocks):
        bb = bs.stop - bs.start
        shp = (bb, tb, BRANCH_W)
        lane = lax.broadcasted_iota(jnp.int32, shp, 2)
        pos = seq_pos0 + t0 + lax.broadcasted_iota(jnp.int32, shp, 1)
        r0 = POOL_PAD + t0
        cur = pool_ext[bs, r0:r0 + tb, :]
        s2 = cur + pool_ext[bs, r0 - 1:r0 - 1 + tb, :]
        s4 = sum_a[bs, r0:r0 + tb, :]
        s8 = sum_b[bs, r0:r0 + tb, :]
        s16 = s8 + sum_b[bs, r0 - SUBLANES:r0 - SUBLANES + tb, :]
        win_sum = jnp.where(lane < 64, s2, jnp.where(lane < 128, s4, jnp.where(lane < 192, s8, s16)))
        win = jnp.where(lane < 64, 2, jnp.where(lane < 128, 4, jnp.where(lane < 192, 8, 16)))
        cnt = jnp.minimum(pos + 1, win).astype(F32)
        buf_a[bs, t0:t0 + tb, :] = win_sum / cnt - cur
        tiles = (ROW_BLOCK // SUBLANES, SUBLANES, BRANCH_W)
        acc = jnp.broadcast_to(bias_tile[None], tiles)
        for k in range(CONF_WIDTH):
            q, sh = divmod(k, SUBLANES)
            r = t0 + SUBLANES * q
            acc = acc + conv_sh[sh, bs, r:r + tb, :].reshape(tiles) * cdw[k][None]
        buf_b[bs, t0:t0 + tb, :] = acc.reshape(shp)
        base = SCONV_PAD - SCONV_HIST + t0
        acc = sc_ext[bs, base:base + tb, :].reshape(tiles) * scw[0][None]
        for k in range(1, SCONV_WIDTH):
            acc = acc + sc_ext[bs, base + k:base + k + tb, :].reshape(tiles) * scw[k][None]
        buf_c[bs, t0:t0 + tb, :] = acc.reshape(shp)
        for g0 in range(bi * gate_cols, (bi + 1) * gate_cols, COL_TILE):
            g = jnp.tanh(proj(GATE_COL0 + g0, GATE_COL0 + g0 + COL_TILE)) + 1.0
            gate_buf[:, g0:g0 + COL_TILE] = g
        bias_tile = cdb[...] + _zero_like_bits(g[0:SUBLANES, :])

    pool_o[...] = pool_ext[:, POOL_PAD + t - POOL_HIST:POOL_PAD + t, :]
    conv_o[...] = conv_ext[:, CONF_PAD + t - CONF_HIST:CONF_PAD + t, :]
    sconv_o[...] = sc_ext[:, SCONV_PAD + t - SCONV_HIST:SCONV_PAD + t, :]
    if not cfg.has_state:
        pool_ext[:, 0:POOL_PAD, :] = pool_ext[:, t:t + POOL_PAD, :]
        conv_ext[:, 0:CONF_PAD, :] = conv_ext[:, t:t + CONF_PAD, :]
        sc_ext[:, 0:SCONV_PAD, :] = sc_ext[:, t:t + SCONV_PAD, :]

    pooled = buf_a[...].reshape(tm, BRANCH_W).astype(BF16)
    br_a = jnp.dot(pooled, pool_w[...], preferred_element_type=F32) * pool_sc[...]

    cb = _ln(buf_b[...].reshape(tm, BRANCH_W), clg[...], clb[...])
    br_b = cb * _sigmoid(cb)

    br_c = proj(768, 1024) * buf_c[...].reshape(tm, BRANCH_W)

    gu = _gelu_tanh(proj(1536, 1792))
    v = _ln(_gelu_tanh(proj(1792, 2048)), slg[...], slb[...])
    if cfg.has_state:
        v_o[...] = v.reshape(nb, t, BRANCH_W)
    else:
        v_o[...] = v[tm - cfg.v_rows:, :].reshape(1, cfg.v_rows, BRANCH_W)
    head = lax.broadcasted_iota(jnp.int32, (CHUNK, BRANCH_W), 1) // SGU_HW
    mixed = []
    for j in range(tm // CHUNK):
        vj = v[j * CHUNK:(j + 1) * CHUNK, :]
        stacked = jnp.concatenate([jnp.where(head == h, vj, 0.0) for h in range(SGU_HEADS)], axis=0)
        mixed.append(jnp.dot(wcat[...], stacked.astype(BF16), preferred_element_type=F32) + sbias[...])
    br_d = gu * jnp.concatenate(mixed, axis=0)

    n_ct = D_MODEL // COL_TILE
    brs = [br.astype(BF16) for br in (br_a, br_b, br_c, br_d)]
    for c in range(n_ct):
        part = None
        for i in range(N_BRANCH):
            g0 = i * D_MODEL + c * COL_TILE
            term = gate_buf[:, g0:g0 + COL_TILE] * jnp.dot(
                brs[i], w_br[i * n_ct + c], preferred_element_type=F32)
            part = term if part is None else part + term
        xb_buf[:, c * COL_TILE:(c + 1) * COL_TILE] = part.astype(BF16)
    for c in range(n_ct):
        cs = slice(c * COL_TILE, (c + 1) * COL_TILE)
        h1_o[:, cs] = x_ref[:, cs] + jnp.dot(xb_buf[...], w_out[c], preferred_element_type=F32)

    xn2 = _rms(h1_o[...], n2g[...])
    x_hi = xn2.astype(BF16)
    x_lo = (xn2 - x_hi.astype(F32)).astype(BF16)
    logits = (jnp.dot(x_hi, w_rt[0], preferred_element_type=F32)
              + (jnp.dot(x_lo, w_rt[0], preferred_element_type=F32)
                 + jnp.dot(x_hi, w_rt[1], preferred_element_type=F32))) + b_rt[...]
    lane = lax.broadcasted_iota(jnp.int32, (tm, LANES), 1)
    is_g = (lane >= ROUTE_GROUP_LANE0) & (lane < ROUTE_GROUP_LANE0 + N_GROUPS)
    glog = jnp.where(is_g, logits, NEG_BIG)
    gmax = jnp.max(glog, axis=-1, keepdims=True)
    lane_f = lane.astype(F32)
    no_lane = float(4 * LANES)
    gsel = jnp.min(jnp.where(glog == gmax, lane_f, no_lane), axis=-1, keepdims=True) - ROUTE_GROUP_LANE0
    pg = 1.0 / jnp.sum(jnp.where(is_g, jnp.exp(glog - gmax), 0.0), axis=-1, keepdims=True)
    in_grp = (lane < N_EXPERTS) & ((lane // EXPERTS_PER_GROUP) == gsel.astype(jnp.int32))
    el = jnp.where(in_grp, logits, NEG_BIG)
    m1 = jnp.max(el, axis=-1, keepdims=True)
    i1 = jnp.min(jnp.where(in_grp & (el == m1), lane_f, no_lane), axis=-1, keepdims=True)
    rest = in_grp & (lane_f != i1)
    el2 = jnp.where(rest, logits, NEG_BIG)
    m2 = jnp.max(el2, axis=-1, keepdims=True)
    i2 = jnp.min(jnp.where(rest & (el2 == m2), lane_f, no_lane), axis=-1, keepdims=True)
    e21 = jnp.exp(m2 - m1)
    w1 = pg / (1.0 + e21)
    w2 = pg * e21 / (1.0 + e21)
    route_o[...] = jnp.where(lane == 0, i1, jnp.where(lane == 1, i2,
                             jnp.where(lane == 2, w1, jnp.where(lane == 3, w2, 0.0))))


def _const_spec(shape):
    nd = len(shape)
    return pl.BlockSpec(shape, lambda *_: (0,) * nd, pipeline_mode=pl.Buffered(1))


def _mixer_call(cfg, n_seq, n_rows_total, x2d, x_tile_off, states, weights, big_weights, aliased):
    nb, t = cfg.nb, cfg.t
    tm = nb * t
    if cfg.has_state:
        grid = (n_seq // nb,)
        tok = lambda i: (x_tile_off + i, 0)
        out_tok = lambda i: (cfg.row_off + i, 0)
        seq3 = lambda i: (i, 0, 0)
    else:
        grid = (n_seq, 2048 // t)
        nt = grid[1]
        tok = lambda b, c: (x_tile_off + b * nt + c, 0)
        out_tok = lambda b, c: (cfg.row_off + b * nt + c, 0)
        seq3 = lambda b, c: (b, 0, 0)

    in_specs = [pl.BlockSpec((tm, D_MODEL), tok)]
    args = [x2d]
    if cfg.has_state:
        for s in states:
            in_specs.append(pl.BlockSpec((nb,) + s.shape[1:], seq3))
            args.append(s)
    for w in weights:
        in_specs.append(_const_spec(w.shape))
        args.append(w)
    for w in big_weights:
        in_specs.append(pl.BlockSpec(memory_space=pl.ANY))
        args.append(w)
    io_alias = {}
    if cfg.has_state:
        for k, a in enumerate(aliased):
            in_specs.append(pl.BlockSpec(memory_space=pl.ANY))
            io_alias[len(args)] = k
            args.append(a)

    out_shape = [
        jax.ShapeDtypeStruct((n_rows_total, D_MODEL), F32),
        jax.ShapeDtypeStruct((n_rows_total, LANES), F32),
        jax.ShapeDtypeStruct((n_seq, POOL_HIST, BRANCH_W), F32),
        jax.ShapeDtypeStruct((n_seq, CONF_HIST, BRANCH_W), F32),
        jax.ShapeDtypeStruct((n_seq, SCONV_HIST, BRANCH_W), F32),
        jax.ShapeDtypeStruct((n_seq, cfg.v_rows, BRANCH_W), F32),
    ]
    out_specs = [
        pl.BlockSpec((tm, D_MODEL), out_tok),
        pl.BlockSpec((tm, LANES), out_tok),
        pl.BlockSpec((nb, POOL_HIST, BRANCH_W), seq3),
        pl.BlockSpec((nb, CONF_HIST, BRANCH_W), seq3),
        pl.BlockSpec((nb, SCONV_HIST, BRANCH_W), seq3),
        pl.BlockSpec((nb, cfg.v_rows, BRANCH_W), seq3),
    ]
    scratch = [
        pltpu.VMEM((nb, POOL_PAD + t, BRANCH_W), F32),
        pltpu.VMEM((nb, POOL_PAD + t, BRANCH_W), F32),
        pltpu.VMEM((nb, POOL_PAD + t, BRANCH_W), F32),
        pltpu.VMEM((nb, CONF_PAD + t, BRANCH_W), F32),
        pltpu.VMEM((SUBLANES, nb, t + CONF_PAD - SUBLANES, BRANCH_W), F32),
        pltpu.VMEM((nb, SCONV_PAD + t, BRANCH_W), F32),
        pltpu.VMEM((nb, t, BRANCH_W), F32),
        pltpu.VMEM((nb, t, BRANCH_W), F32),
        pltpu.VMEM((nb, t, BRANCH_W), F32),
        pltpu.VMEM((tm, N_BRANCH * D_MODEL), F32),
        pltpu.VMEM((tm, D_MODEL), BF16),
        pltpu.VMEM((IN_COLS // COL_TILE, D_MODEL, COL_TILE), BF16),
        pltpu.VMEM((N_BRANCH * D_MODEL // COL_TILE, BRANCH_W, COL_TILE), BF16),
        pltpu.VMEM((D_MODEL // COL_TILE, D_MODEL, COL_TILE), BF16),
        pltpu.VMEM((2, D_MODEL, COL_TILE), F32),
        pltpu.VMEM((2, BRANCH_W, D_MODEL), F32),
        pltpu.SemaphoreType.DMA((2,)),
    ]
    return pl.pallas_call(
        functools.partial(_mixer_kernel, cfg),
        grid=grid,
        in_specs=in_specs,
        out_specs=out_specs,
        out_shape=out_shape,
        scratch_shapes=scratch,
        input_output_aliases=io_alias,
        compiler_params=pltpu.CompilerParams(
            dimension_semantics=("arbitrary",) * len(grid), vmem_limit_bytes=VMEM_LIMIT),
        name="mixer_sample" if cfg.has_state else "mixer_prompt",
    )(*args)


def _sort_kernel(h1_ref, route_ref, n2g, xl_o, pos_o, nch_o):
    t = SORT_TILE
    xn = _rms(h1_ref[...], n2g[...]).astype(BF16)
    r = route_ref[...]
    lane_i = lax.broadcasted_iota(jnp.int32, (t, LANES), 1)
    lane = lane_i.astype(F32)
    hit1 = lane == r[:, 0:1]
    hit2 = lane == r[:, 1:2]
    onehot = jnp.where(hit1 | hit2, 1.0, 0.0)
    row = lax.broadcasted_iota(jnp.int32, (t, t), 0)
    col = lax.broadcasted_iota(jnp.int32, (t, t), 1)
    strict_lower = jnp.where(col < row, 1.0, 0.0).astype(BF16)
    before = jnp.dot(strict_lower, onehot.astype(BF16), preferred_element_type=F32)
    cnt = jnp.sum(onehot, axis=0, keepdims=True)
    nch = jnp.floor((cnt + (SEG_PAD - 1)) * (1.0 / SEG_PAD))
    er = lax.broadcasted_iota(jnp.int32, (LANES, LANES), 0)
    ec = lax.broadcasted_iota(jnp.int32, (LANES, LANES), 1)
    strict_upper = jnp.where(er < ec, 1.0, 0.0).astype(BF16)
    nch8 = jnp.broadcast_to(nch, (SUBLANES, LANES)).astype(BF16)
    seg0 = jnp.dot(nch8, strict_upper, preferred_element_type=F32)[0:1, :] * SEG_PAD
    where = seg0 + before
    pos0 = jnp.sum(jnp.where(hit1, where, 0.0), axis=-1, keepdims=True)
    pos1 = jnp.sum(jnp.where(hit2, where, 0.0), axis=-1, keepdims=True)
    pos = jnp.where(lane_i == 0, pos0, jnp.where(lane_i == 1, pos1, -1.0))
    pos_o[...] = pos
    nch_o[...] = jnp.broadcast_to(nch, (1, SUBLANES, LANES))
    pos_t = jnp.transpose(pos)
    p0 = pos_t[0:1, :]
    p1 = pos_t[1:2, :]
    for blk in range(N_PERM_BLOCKS):
        dst = (lax.broadcasted_iota(jnp.int32, (PERM_BLOCK, t), 0) + blk * PERM_BLOCK).astype(F32)
        perm = jnp.where((dst == p0) | (dst == p1), 1.0, 0.0).astype(BF16)
        xl_o[0, blk * PERM_BLOCK:(blk + 1) * PERM_BLOCK, :] = jnp.dot(
            perm, xn, preferred_element_type=F32).astype(BF16)


def _sort_call(h1, route, n2g):
    n = h1.shape[0]
    nt = n // SORT_TILE
    return pl.pallas_call(
        _sort_kernel,
        grid=(nt,),
        in_specs=[pl.BlockSpec((SORT_TILE, D_MODEL), lambda i: (i, 0)),
                  pl.BlockSpec((SORT_TILE, LANES), lambda i: (i, 0)),
                  pl.BlockSpec((1, D_MODEL), lambda i: (0, 0))],
        out_specs=[pl.BlockSpec((1, LOCAL_ROWS, D_MODEL), lambda i: (i, 0, 0)),
                   pl.BlockSpec((SORT_TILE, LANES), lambda i: (i, 0)),
                   pl.BlockSpec((1, SUBLANES, LANES), lambda i: (i, 0, 0))],
        out_shape=[jax.ShapeDtypeStruct((nt + 1, LOCAL_ROWS, D_MODEL), BF16),
                   jax.ShapeDtypeStruct((n, LANES), F32),
                   jax.ShapeDtypeStruct((nt, SUBLANES, LANES), F32)],
        compiler_params=pltpu.CompilerParams(dimension_semantics=("arbitrary",), vmem_limit_bytes=VMEM_LIMIT),
        name="moe_sort",
    )(h1, route, n2g)


def _chunk_copy(src, src_row, dst, dst_row, sem):
    return pltpu.make_async_copy(src.at[pl.ds(src_row, SEG_PAD), :], dst.at[pl.ds(dst_row, SEG_PAD), :], sem)


def _expert_kernel(trash_row0, se_ref, nv_ref, src_ref, dst_ref, xl_in, wg, wu, wd, xl_io,
                   xbuf0, xbuf1, ybuf0, ybuf1, sem_in, sem_out):
    del se_ref, xl_in
    xbuf, ybuf = (xbuf0, xbuf1), (ybuf0, ybuf1)
    s = pl.program_id(0)
    nv = nv_ref[0]

    def gather(step, sl):
        return [_chunk_copy(xl_io, pl.multiple_of(src_ref[step * CHUNKS_PER_STEP + j], SEG_PAD),
                            xbuf[sl], j * SEG_PAD, sem_in.at[sl]) for j in range(CHUNKS_PER_STEP)]

    def write_back(step, sl):
        return [_chunk_copy(ybuf[sl], j * SEG_PAD, xl_io,
                            pl.multiple_of(dst_ref[step * CHUNKS_PER_STEP + j], SEG_PAD), sem_out.at[sl])
                for j in range(CHUNKS_PER_STEP)]

    def spare_write(sl):
        return [_chunk_copy(ybuf[sl], j * SEG_PAD, xl_io, trash_row0 + sl * EXP_TILE + j * SEG_PAD,
                            sem_out.at[sl]) for j in range(CHUNKS_PER_STEP)]

    @pl.when(s == 0)
    def _():
        for c in gather(0, 0):
            c.start()
        for sl in range(2):
            ybuf[sl][...] = jnp.zeros_like(ybuf[sl])
            for c in spare_write(sl):
                c.start()

    def live_step(sl):
        for c in gather(s, sl):
            c.wait()
        for c in write_back(s, sl):
            c.wait()
        for c in gather(s + 1, 1 - sl):
            c.start()
        xb = xbuf[sl][...]
        a = jnp.dot(xb, wg[0, 0].astype(BF16), preferred_element_type=F32)
        b = jnp.dot(xb, wu[0, 0].astype(BF16), preferred_element_type=F32)
        hh = (a * _sigmoid(a) * b).astype(BF16)
        ybuf[sl][...] = jnp.dot(hh, wd[0, 0].astype(BF16), preferred_element_type=F32).astype(BF16)
        for c in write_back(s, sl):
            c.start()

    def drain_step(sl):
        for c in gather(s, sl):
            c.wait()
        for both in range(2):
            for c in spare_write(both):
                c.wait()

    for sl in range(2):
        pl.when((s < nv) & (s % 2 == sl))(functools.partial(live_step, sl))
        pl.when((s == nv) & (s % 2 == sl))(functools.partial(drain_step, sl))


def _expert_call(xl, trash_row0, step_expert, n_valid, chunk_src, chunk_dst, layer, w_gate, w_up, w_down):
    n_steps = step_expert.shape[0]
    wmap = lambda s, se, *_: (layer, se[s], 0, 0)
    grid_spec = pltpu.PrefetchScalarGridSpec(
        num_scalar_prefetch=4,
        grid=(n_steps,),
        in_specs=[pl.BlockSpec(memory_space=pl.ANY),
                  pl.BlockSpec((1, 1, D_MODEL, D_EXPERT), wmap),
                  pl.BlockSpec((1, 1, D_MODEL, D_EXPERT), wmap),
                  pl.BlockSpec((1, 1, D_EXPERT, D_MODEL), wmap)],
        out_specs=pl.BlockSpec(memory_space=pl.ANY),
        scratch_shapes=[pltpu.VMEM((EXP_TILE, D_MODEL), BF16) for _ in range(4)] + [
                        pltpu.SemaphoreType.DMA((2,)), pltpu.SemaphoreType.DMA((2,))],
    )
    return pl.pallas_call(
        functools.partial(_expert_kernel, trash_row0),
        grid_spec=grid_spec,
        out_shape=jax.ShapeDtypeStruct(xl.shape, BF16),
        input_output_aliases={4: 0},
        compiler_params=pltpu.CompilerParams(dimension_semantics=("arbitrary",), has_side_effects=True),
        name="moe_experts",
    )(step_expert, n_valid, chunk_src, chunk_dst, xl, w_gate, w_up, w_down)


def _combine_kernel(final_norm, n_first, h1_ref, route_ref, pos_ref, fg, yl_ref, *outs):
    t = SORT_TILE
    r = route_ref[...]
    p = pos_ref[...]
    w1, w2 = r[:, 2:3], r[:, 3:4]
    pos0, pos1 = p[:, 0:1], p[:, 1:2]
    acc = h1_ref[...]
    for blk in range(N_PERM_BLOCKS):
        src = (lax.broadcasted_iota(jnp.int32, (t, PERM_BLOCK), 1) + blk * PERM_BLOCK).astype(F32)
        pw = jnp.where(src == pos0, w1, 0.0) + jnp.where(src == pos1, w2, 0.0)
        acc = acc + jnp.dot(pw.astype(BF16), yl_ref[0, blk * PERM_BLOCK:(blk + 1) * PERM_BLOCK, :],
                            preferred_element_type=F32)
    if not final_norm:
        outs[0][...] = acc
        return
    y = _rms(acc, fg[...])
    i = pl.program_id(0)

    @pl.when(i < n_first)
    def _():
        outs[0][...] = y

    @pl.when(i >= n_first)
    def _():
        outs[1][...] = y


def _combine_call(h1, route, pos, yl, final_g, final_norm, n_first_rows):
    n = h1.shape[0]
    nt = n // SORT_TILE
    n_first = n_first_rows // SORT_TILE
    tok = lambda i: (i, 0)
    if final_norm:
        out_shape = [jax.ShapeDtypeStruct((n_first_rows, D_MODEL), F32),
                     jax.ShapeDtypeStruct((n - n_first_rows, D_MODEL), F32)]
        out_specs = [pl.BlockSpec((SORT_TILE, D_MODEL), lambda i: (jnp.minimum(i, n_first - 1), 0)),
                     pl.BlockSpec((SORT_TILE, D_MODEL), lambda i: (jnp.maximum(i - n_first, 0), 0))]
    else:
        out_shape = [jax.ShapeDtypeStruct((n, D_MODEL), F32)]
        out_specs = [pl.BlockSpec((SORT_TILE, D_MODEL), tok)]
    return pl.pallas_call(
        functools.partial(_combine_kernel, final_norm, n_first),
        grid=(nt,),
        in_specs=[pl.BlockSpec((SORT_TILE, D_MODEL), tok),
                  pl.BlockSpec((SORT_TILE, LANES), tok),
                  pl.BlockSpec((SORT_TILE, LANES), tok),
                  pl.BlockSpec((1, D_MODEL), lambda i: (0, 0)),
                  pl.BlockSpec((1, LOCAL_ROWS, D_MODEL), lambda i: (i, 0, 0))],
        out_specs=out_specs,
        out_shape=out_shape,
        compiler_params=pltpu.CompilerParams(dimension_semantics=("arbitrary",), vmem_limit_bytes=VMEM_LIMIT),
        name="moe_combine",
    )(h1, route, pos, final_g, yl)


def _expert_tables(nch, n_steps, trash_row0):
    nt = nch.shape[0]
    cps = CHUNKS_PER_STEP
    seg_row0 = (jnp.cumsum(nch, axis=1) - nch) * SEG_PAD
    first = jnp.cumsum(nch, axis=0) - nch
    tot = jnp.sum(nch, axis=0)
    steps = (tot + cps - 1) // cps
    step_end = jnp.cumsum(steps)
    n_valid = step_end[-1:]
    s_ids = jnp.arange(n_steps, dtype=jnp.int32)
    step_expert = jnp.minimum(jnp.sum((step_end[None, :] <= s_ids[:, None]).astype(jnp.int32), axis=1),
                              N_EXPERTS - 1)
    sel = (step_expert[:, None] == jnp.arange(N_EXPERTS, dtype=jnp.int32)[None, :]).astype(jnp.int32)
    step0 = sel @ (step_end - steps)
    tot_s = sel @ tot
    first_s = sel @ first.T
    row0_s = sel @ seg_row0.T
    k = (s_ids - step0)[:, None] * cps + jnp.arange(cps, dtype=jnp.int32)[None, :]
    ok = (k < tot_s[:, None]) & (s_ids < n_valid[0])[:, None]
    tile = jnp.sum((first_s[:, None, :] <= k[:, :, None]).astype(jnp.int32), axis=2) - 1
    tsel = (tile[:, :, None] == jnp.arange(nt, dtype=jnp.int32)[None, None, :]).astype(jnp.int32)
    first_k = jnp.sum(tsel * first_s[:, None, :], axis=2)
    row0_k = jnp.sum(tsel * row0_s[:, None, :], axis=2)
    row = tile * LOCAL_ROWS + row0_k + (k - first_k) * SEG_PAD
    live = (s_ids < n_valid[0])[:, None]
    src = jnp.where(live, jnp.where(ok, row, row[:, 0:1]), 0)
    spare = trash_row0 + (s_ids % 2)[:, None] * EXP_TILE + jnp.arange(cps, dtype=jnp.int32)[None, :] * SEG_PAD
    dst = jnp.where(ok, row, spare)
    return (step_expert, n_valid.astype(jnp.int32), src.reshape(-1).astype(jnp.int32),
            dst.reshape(-1).astype(jnp.int32))


def _moe(h1, route, n2g, layer, w_gate, w_up, w_down, final_g, final_norm, n_first_rows):
    n = h1.shape[0]
    nt = n // SORT_TILE
    xl, pos, nch = _sort_call(h1, route, n2g)
    nch = nch[:, 0, :N_EXPERTS].astype(jnp.int32)
    max_chunks = (2 * n) // SEG_PAD + nt * N_EXPERTS
    n_steps = max_chunks // CHUNKS_PER_STEP + N_EXPERTS + 1
    trash_row0 = nt * LOCAL_ROWS
    step_expert, n_valid, chunk_src, chunk_dst = _expert_tables(nch, n_steps, trash_row0)
    yl = _expert_call(xl.reshape((nt + 1) * LOCAL_ROWS, D_MODEL), trash_row0, step_expert, n_valid,
                      chunk_src, chunk_dst, layer, w_gate, w_up, w_down)
    return _combine_call(h1, route, pos, yl.reshape(nt + 1, LOCAL_ROWS, D_MODEL), final_g, final_norm, n_first_rows)


def _block_diag(pw):
    out = jnp.zeros((BRANCH_W, BRANCH_W), pw.dtype)
    for g in range(len(POOL_WINDOWS)):
        out = out.at[g * POOL_GW:(g + 1) * POOL_GW, g * POOL_GW:(g + 1) * POOL_GW].set(pw[g])
    return out


def kernel(x_prompt, x_sample, state_pool, state_conv, state_sconv, norm1_g, w_in, pool_w, pool_scale, conf_dw, conf_dw_b, conf_ln_g, conf_ln_b, sconv_w, sgu_ln_g, sgu_ln_b, sgu_ws, sgu_b, w_branch, w_out, norm2_g, router_g, router_g_b, router_e, router_e_b, w_gate, w_up, w_down, final_g):
    depth = w_in.shape[0]
    bp, seq, _ = x_prompt.shape
    bs, dseq, _ = x_sample.shape
    n_p, n_s = bp * seq, bs * dseq
    n = n_p + n_s
    past_len = 16384
    assert seq % MIX_TILE == 0 and n_s % MIX_TILE_S == 0 and n_p % MIX_TILE_S == 0 and dseq == SUBLANES
    assert MIX_TILE % CHUNK == 0 and MIX_TILE_S % CHUNK == 0

    cfg_p = MixCfg(layer=0, nb=1, t=MIX_TILE, has_state=False, start_pos=0, row_off=0, v_rows=CHUNK)
    cfg_s = MixCfg(layer=0, nb=MIX_TILE_S // dseq, t=dseq, has_state=True, start_pos=past_len,
                   row_off=n_p // MIX_TILE_S, v_rows=dseq)
    big_weights = (w_in, w_branch, w_out)

    row = lambda a: a.reshape(1, -1)
    rep8 = lambda a: jnp.broadcast_to(a[..., None, :], a.shape[:-1] + (SUBLANES, a.shape[-1]))
    tril = jnp.tril(jnp.ones((CHUNK, CHUNK), F32))
    eye_blk = jnp.kron(jnp.eye(CHUNK // dseq, dtype=F32), jnp.ones((dseq, dseq), F32))
    final_row = row(final_g)

    h_p, h_s = x_prompt.reshape(n_p, D_MODEL), x_sample.reshape(n_s, D_MODEL)
    off_p, off_s = 0, 0
    states_out = []
    for l in range(depth):
        ws_p = sgu_ws[l] * tril[None]
        ws_small = jnp.tile(sgu_ws[l][:, :dseq, :dseq], (1, CHUNK // dseq, CHUNK // dseq))
        ws_s = ws_small * (tril * eye_blk)[None]
        cat = lambda w: jnp.concatenate([w[h] for h in range(SGU_HEADS)], axis=1).astype(BF16)
        bias_p = jnp.repeat(sgu_b[l].T, SGU_HW, axis=1)
        bias_s = jnp.tile(jnp.repeat(sgu_b[l][:, :dseq].T, SGU_HW, axis=1), (CHUNK // dseq, 1))
        lane_pad = LANES - N_EXPERTS - N_GROUPS
        w_rt32 = jnp.pad(jnp.concatenate([router_e[l], router_g[l]], axis=1), ((0, 0), (0, lane_pad)))
        w_rt_hi = w_rt32.astype(BF16)
        w_rt = jnp.stack([w_rt_hi, (w_rt32 - w_rt_hi.astype(F32)).astype(BF16)])
        b_rt = jnp.pad(jnp.concatenate([router_e_b[l], router_g_b[l]]), (0, lane_pad)).reshape(1, LANES)

        def weights(wcat, sbias):
            return [row(norm1_g[l]), _block_diag(pool_w[l]).astype(BF16), row(pool_scale[l]),
                    rep8(conf_dw[l]), rep8(conf_dw_b[l]), row(conf_ln_g[l]), row(conf_ln_b[l]), rep8(sconv_w[l]),
                    row(sgu_ln_g[l]), row(sgu_ln_b[l]), wcat, sbias, row(norm2_g[l]), w_rt, b_rt]

        h1, route, pool_p, conv_p, sconv_p, v_p = _mixer_call(
            cfg_p._replace(layer=l), bp, n, h_p, off_p, None, weights(cat(ws_p), bias_p), big_weights, None)
        h1, route, pool_s, conv_s, sconv_s, v_s = _mixer_call(
            cfg_s._replace(layer=l), bs, n, h_s, off_s, (state_pool[l], state_conv[l], state_sconv[l]),
            weights(cat(ws_s), bias_s), big_weights, (h1, route))
        states_out.append((pool_p, conv_p, sconv_p, v_p, pool_s, conv_s, sconv_s, v_s))

        outs = _moe(h1, route, row(norm2_g[l]), l, w_gate, w_up, w_down, final_row, l == depth - 1, n_p)
        h_p = h_s = outs[0]
        off_p, off_s = 0, n_p // MIX_TILE_S

    y_prompt = outs[0].reshape(bp, seq, D_MODEL)
    y_sample = outs[1].reshape(bs, dseq, D_MODEL)
    st = [jnp.stack([s[k] for s in states_out]) for k in range(8)]
    return (y_prompt, y_sample, st[0], st[1], st[2], st[3], st[4], st[5], st[6], st[7])
```

```python
import functools
from typing import NamedTuple

import jax
import jax.numpy as jnp
from jax import lax
from jax.experimental import pallas as pl
from jax.experimental.pallas import tpu as pltpu

F32 = jnp.float32
BF16 = jnp.bfloat16

D_MODEL = 1024
BRANCH_W = 256
N_BRANCH = 4
POOL_WINDOWS = (2, 4, 8, 16)
POOL_GW = 64
POOL_HIST = 15
CONF_WIDTH = 31
CONF_HIST = 30
SCONV_WIDTH = 3
SCONV_HIST = 2
CHUNK = 128
SGU_HEADS = 4
SGU_HW = 64
N_GROUPS = 4
EXPERTS_PER_GROUP = 8
N_EXPERTS = 32
D_EXPERT = 256
RMS_EPS = 1e-6
LN_EPS = 1e-5
IN_COLS = 6144
GATE_COL0 = 2048

COL_TILE = 256
LANES = 128
SUBLANES = 8
POOL_PAD = 32
CONF_PAD = 32
SCONV_PAD = 8
ROW_BLOCK = 32
ROUTE_GROUP_LANE0 = 32
NEG_BIG = -3.0e38

MIX_TILE = 512
MIX_TILE_S = 256
SORT_TILE = 512
SEG_PAD = 16
PERM_BLOCK = 256
N_PERM_BLOCKS = -(-(2 * SORT_TILE + N_EXPERTS * (SEG_PAD - 1)) // PERM_BLOCK)
LOCAL_ROWS = N_PERM_BLOCKS * PERM_BLOCK
EXP_TILE = 512
CHUNKS_PER_STEP = EXP_TILE // SEG_PAD
VMEM_LIMIT = 56 * 1024 * 1024


class MixCfg(NamedTuple):
    layer: int
    nb: int
    t: int
    has_state: bool
    start_pos: int
    row_off: int
    v_rows: int


def _rms(x, g):
    return x * lax.rsqrt(jnp.mean(x * x, axis=-1, keepdims=True) + RMS_EPS) * g


def _ln(x, g, b):
    mu = jnp.mean(x, axis=-1, keepdims=True)
    xc = x - mu
    return xc * lax.rsqrt(jnp.mean(xc * xc, axis=-1, keepdims=True) + LN_EPS) * g + b


def _sigmoid(x):
    return 0.5 * jnp.tanh(0.5 * x) + 0.5


def _zero_like_bits(x):
    bits = lax.bitcast_convert_type(x, jnp.uint32)
    return ((bits >> 16) >> 16).astype(jnp.int32).astype(F32)


def _gelu_tanh(x):
    return 0.5 * x * (1.0 + jnp.tanh(0.7978845608028654 * (x + 0.044715 * (x * x * x))))


def _row_blocks(nb, t):
    if t >= ROW_BLOCK:
        return [(slice(b, b + 1), t0, ROW_BLOCK) for b in range(nb) for t0 in range(0, t, ROW_BLOCK)]
    bb = ROW_BLOCK // t
    return [(slice(b0, b0 + bb), 0, t) for b0 in range(0, nb, bb)]


def _load_big_weights(layer, w_in_hbm, w_br_hbm, w_out_hbm, w_in, w_br, w_out, stage, stage_b, sem):
    n_in, n_out = IN_COLS // COL_TILE, D_MODEL // COL_TILE
    jobs = [("in", j) for j in range(n_in)] + [("out", j) for j in range(n_out)] + [("br", i) for i in range(N_BRANCH)]

    def copy(job, slot):
        kind, j = job
        if kind == "br":
            return pltpu.make_async_copy(w_br_hbm.at[layer, j], stage_b.at[slot], sem.at[slot])
        src = w_in_hbm if kind == "in" else w_out_hbm
        return pltpu.make_async_copy(src.at[layer, :, pl.ds(j * COL_TILE, COL_TILE)], stage.at[slot], sem.at[slot])

    copy(jobs[0], 0).start()
    for n, job in enumerate(jobs):
        slot = n % 2
        if n + 1 < len(jobs):
            copy(jobs[n + 1], 1 - slot).start()
        copy(job, slot).wait()
        kind, j = job
        if kind == "in":
            tile = stage[slot]
            w_in[j] = (tile * 0.5 if j * COL_TILE >= GATE_COL0 else tile).astype(BF16)
        elif kind == "out":
            w_out[j] = stage[slot].astype(BF16)
        else:
            for c in range(n_out):
                w_br[j * n_out + c] = (stage_b[slot, :, c * COL_TILE:(c + 1) * COL_TILE] * 0.5).astype(BF16)


def _mixer_kernel(cfg, *refs):
    nb, t, tm = cfg.nb, cfg.t, cfg.nb * cfg.t
    refs = list(refs)
    x_ref = refs.pop(0)
    if cfg.has_state:
        pool_st, conv_st, sconv_st = refs.pop(0), refs.pop(0), refs.pop(0)
    (n1g, pool_w, pool_sc, cdw, cdb, clg, clb, scw, slg, slb, wcat, sbias, n2g, w_rt, b_rt,
     w_in_hbm, w_br_hbm, w_out_hbm) = refs[:18]
    refs = refs[18:]
    if cfg.has_state:
        refs = refs[2:]
    h1_o, route_o, pool_o, conv_o, sconv_o, v_o = refs[:6]
    (pool_ext, sum_a, sum_b, conv_ext, conv_sh, sc_ext, buf_a, buf_b, buf_c, gate_buf, xb_buf,
     w_in, w_br, w_out, stage, stage_b, w_sem) = refs[6:]

    first_step = pl.program_id(0) == 0
    if not cfg.has_state:
        first_step = first_step & (pl.program_id(1) == 0)
    pl.when(first_step)(functools.partial(
        _load_big_weights, cfg.layer, w_in_hbm, w_br_hbm, w_out_hbm, w_in, w_br, w_out, stage, stage_b, w_sem))

    if cfg.has_state:
        seq_pos0 = cfg.start_pos
        pool_ext[:, 0:POOL_PAD - SUBLANES, :] = jnp.zeros((nb, POOL_PAD - SUBLANES, BRANCH_W), F32)
        pool_ext[:, POOL_PAD - POOL_HIST:POOL_PAD, :] = pool_st[...]
        conv_ext[:, CONF_PAD - CONF_HIST:CONF_PAD, :] = conv_st[...]
        sc_ext[:, SCONV_PAD - SCONV_HIST:SCONV_PAD, :] = sconv_st[...]
    else:
        c = pl.program_id(1)
        seq_pos0 = cfg.start_pos + c * t

        @pl.when(c == 0)
        def _():
            pool_ext[:, 0:POOL_PAD, :] = jnp.zeros((nb, POOL_PAD, BRANCH_W), F32)
            conv_ext[:, 0:CONF_PAD, :] = jnp.zeros((nb, CONF_PAD, BRANCH_W), F32)
            sc_ext[:, 0:SCONV_PAD, :] = jnp.zeros((nb, SCONV_PAD, BRANCH_W), F32)

    x = x_ref[...]
    xb_buf[...] = _rms(x, n1g[...]).astype(BF16)

    def proj(lo, hi):
        assert lo % COL_TILE == 0 and hi == lo + COL_TILE
        return jnp.dot(xb_buf[...], w_in[lo // COL_TILE], preferred_element_type=F32)

    a_pool = proj(0, 256)
    pool_ext[:, POOL_PAD:, :] = a_pool.reshape(nb, t, BRANCH_W)
    glu = proj(256, 512) * _sigmoid(proj(512, 768))
    conv_ext[:, CONF_PAD:, :] = glu.reshape(nb, t, BRANCH_W)
    z = proj(1024, 1280) * proj(1280, 1536)
    sc_ext[:, SCONV_PAD:, :] = z.reshape(nb, t, BRANCH_W)

    pl_len = POOL_PAD + t
    sum_b[:, 8:pl_len, :] = pool_ext[:, 8:pl_len, :] + pool_ext[:, 7:pl_len - 1, :]
    sum_a[:, 16:pl_len, :] = sum_b[:, 16:pl_len, :] + sum_b[:, 14:pl_len - 2, :]
    sum_b[:, 24:pl_len, :] = sum_a[:, 24:pl_len, :] + sum_a[:, 20:pl_len - 4, :]
    for sh in range(SUBLANES):
        n_rows = t + SUBLANES * ((CONF_WIDTH - 1 - sh) // SUBLANES)
        first = CONF_PAD - CONF_HIST + sh
        conv_sh[sh, :, 0:n_rows, :] = conv_ext[:, first:first + n_rows, :]

    blocks = _row_blocks(nb, t)
    gate_cols = (N_BRANCH * D_MODEL) // len(blocks)
    bias_tile = cdb[...]
    for bi, (bs, t0, tb) in enumerate(blocks):
        bb = bs.stop - bs.start
        shp = (bb, tb, BRANCH_W)
        lane = lax.broadcasted_iota(jnp.int32, shp, 2)
        pos = seq_pos0 + t0 + lax.broadcasted_iota(jnp.int32, shp, 1)
        r0 = POOL_PAD + t0
        cur = pool_ext[bs, r0:r0 + tb, :]
        s2 = cur + pool_ext[bs, r0 - 1:r0 - 1 + tb, :]
        s4 = sum_a[bs, r0:r0 + tb, :]
        s8 = sum_b[bs, r0:r0 + tb, :]
        s16 = s8 + sum_b[bs, r0 - SUBLANES:r0 - SUBLANES + tb, :]
        win_sum = jnp.where(lane < 64, s2, jnp.where(lane < 128, s4, jnp.where(lane < 192, s8, s16)))
        win = jnp.where(lane < 64, 2, jnp.where(lane < 128, 4, jnp.where(lane < 192, 8, 16)))
        cnt = jnp.minimum(pos + 1, win).astype(F32)
        buf_a[bs, t0:t0 + tb, :] = win_sum / cnt - cur
        tiles = (ROW_BLOCK // SUBLANES, SUBLANES, BRANCH_W)
        acc = jnp.broadcast_to(bias_tile[None], tiles)
        for k in range(CONF_WIDTH):
            q, sh = divmod(k, SUBLANES)
            r = t0 + SUBLANES * q
            acc = acc + conv_sh[sh, bs, r:r + tb, :].reshape(tiles) * cdw[k][None]
        buf_b[bs, t0:t0 + tb, :] = acc.reshape(shp)
        base = SCONV_PAD - SCONV_HIST + t0
        acc = sc_ext[bs, base:base + tb, :].reshape(tiles) * scw[0][None]
        for k in range(1, SCONV_WIDTH):
            acc = acc + sc_ext[bs, base + k:base + k + tb, :].reshape(tiles) * scw[k][None]
        buf_c[bs, t0:t0 + tb, :] = acc.reshape(shp)
        for g0 in range(bi * gate_cols, (bi + 1) * gate_cols, COL_TILE):
            g = jnp.tanh(proj(GATE_COL0 + g0, GATE_COL0 + g0 + COL_TILE)) + 1.0
            gate_buf[:, g0:g0 + COL_TILE] = g
        bias_tile = cdb[...] + _zero_like_bits(g[0:SUBLANES, :])

    pool_o[...] = pool_ext[:, POOL_PAD + t - POOL_HIST:POOL_PAD + t, :]
    conv_o[...] = conv_ext[:, CONF_PAD + t - CONF_HIST:CONF_PAD + t, :]
    sconv_o[...] = sc_ext[:, SCONV_PAD + t - SCONV_HIST:SCONV_PAD + t, :]
    if not cfg.has_state:
        pool_ext[:, 0:POOL_PAD, :] = pool_ext[:, t:t + POOL_PAD, :]
        conv_ext[:, 0:CONF_PAD, :] = conv_ext[:, t:t + CONF_PAD, :]
        sc_ext[:, 0:SCONV_PAD, :] = sc_ext[:, t:t + SCONV_PAD, :]

    pooled = buf_a[...].reshape(tm, BRANCH_W).astype(BF16)
    br_a = jnp.dot(pooled, pool_w[...], preferred_element_type=F32) * pool_sc[...]

    cb = _ln(buf_b[...].reshape(tm, BRANCH_W), clg[...], clb[...])
    br_b = cb * _sigmoid(cb)

    br_c = proj(768, 1024) * buf_c[...].reshape(tm, BRANCH_W)

    gu = _gelu_tanh(proj(1536, 1792))
    v = _ln(_gelu_tanh(proj(1792, 2048)), slg[...], slb[...])
    if cfg.has_state:
        v_o[...] = v.reshape(nb, t, BRANCH_W)
    else:
        v_o[...] = v[tm - cfg.v_rows:, :].reshape(1, cfg.v_rows, BRANCH_W)
    head = lax.broadcasted_iota(jnp.int32, (CHUNK, BRANCH_W), 1) // SGU_HW
    mixed = []
    for j in range(tm // CHUNK):
        vj = v[j * CHUNK:(j + 1) * CHUNK, :]
        stacked = jnp.concatenate([jnp.where(head == h, vj, 0.0) for h in range(SGU_HEADS)], axis=0)
        mixed.append(jnp.dot(wcat[...], stacked.astype(BF16), preferred_element_type=F32) + sbias[...])
    br_d = gu * jnp.concatenate(mixed, axis=0)

    n_ct = D_MODEL // COL_TILE
    brs = [br.astype(BF16) for br in (br_a, br_b, br_c, br_d)]
    for c in range(n_ct):
        part = None
        for i in range(N_BRANCH):
            g0 = i * D_MODEL + c * COL_TILE
            term = gate_buf[:, g0:g0 + COL_TILE] * jnp.dot(
                brs[i], w_br[i * n_ct + c], preferred_element_type=F32)
            part = term if part is None else part + term
        xb_buf[:, c * COL_TILE:(c + 1) * COL_TILE] = part.astype(BF16)
    for c in range(n_ct):
        cs = slice(c * COL_TILE, (c + 1) * COL_TILE)
        h1_o[:, cs] = x_ref[:, cs] + jnp.dot(xb_buf[...], w_out[c], preferred_element_type=F32)

    xn2 = _rms(h1_o[...], n2g[...])
    x_hi = xn2.astype(BF16)
    x_lo = (xn2 - x_hi.astype(F32)).astype(BF16)
    logits = (jnp.dot(x_hi, w_rt[0], preferred_element_type=F32)
              + (jnp.dot(x_lo, w_rt[0], preferred_element_type=F32)
                 + jnp.dot(x_hi, w_rt[1], preferred_element_type=F32))) + b_rt[...]
    lane = lax.broadcasted_iota(jnp.int32, (tm, LANES), 1)
    is_g = (lane >= ROUTE_GROUP_LANE0) & (lane < ROUTE_GROUP_LANE0 + N_GROUPS)
    glog = jnp.where(is_g, logits, NEG_BIG)
    gmax = jnp.max(glog, axis=-1, keepdims=True)
    lane_f = lane.astype(F32)
    no_lane = float(4 * LANES)
    gsel = jnp.min(jnp.where(glog == gmax, lane_f, no_lane), axis=-1, keepdims=True) - ROUTE_GROUP_LANE0
    pg = 1.0 / jnp.sum(jnp.where(is_g, jnp.exp(glog - gmax), 0.0), axis=-1, keepdims=True)
    in_grp = (lane < N_EXPERTS) & ((lane // EXPERTS_PER_GROUP) == gsel.astype(jnp.int32))
    el = jnp.where(in_grp, logits, NEG_BIG)
    m1 = jnp.max(el, axis=-1, keepdims=True)
    i1 = jnp.min(jnp.where(in_grp & (el == m1), lane_f, no_lane), axis=-1, keepdims=True)
    rest = in_grp & (lane_f != i1)
    el2 = jnp.where(rest, logits, NEG_BIG)
    m2 = jnp.max(el2, axis=-1, keepdims=True)
    i2 = jnp.min(jnp.where(rest & (el2 == m2), lane_f, no_lane), axis=-1, keepdims=True)
    e21 = jnp.exp(m2 - m1)
    w1 = pg / (1.0 + e21)
    w2 = pg * e21 / (1.0 + e21)
    route_o[...] = jnp.where(lane == 0, i1, jnp.where(lane == 1, i2,
                             jnp.where(lane == 2, w1, jnp.where(lane == 3, w2, 0.0))))


def _const_spec(shape):
    nd = len(shape)
    return pl.BlockSpec(shape, lambda *_: (0,) * nd, pipeline_mode=pl.Buffered(1))


def _mixer_call(cfg, n_seq, n_rows_total, x2d, x_tile_off, states, weights, big_weights, aliased):
    nb, t = cfg.nb, cfg.t
    tm = nb * t
    if cfg.has_state:
        grid = (n_seq // nb,)
        tok = lambda i: (x_tile_off + i, 0)
        out_tok = lambda i: (cfg.row_off + i, 0)
        seq3 = lambda i: (i, 0, 0)
    else:
        grid = (n_seq, 2048 // t)
        nt = grid[1]
        tok = lambda b, c: (x_tile_off + b * nt + c, 0)
        out_tok = lambda b, c: (cfg.row_off + b * nt + c, 0)
        seq3 = lambda b, c: (b, 0, 0)

    in_specs = [pl.BlockSpec((tm, D_MODEL), tok)]
    args = [x2d]
    if cfg.has_state:
        for s in states:
            in_specs.append(pl.BlockSpec((nb,) + s.shape[1:], seq3))
            args.append(s)
    for w in weights:
        in_specs.append(_const_spec(w.shape))
        args.append(w)
    for w in big_weights:
        in_specs.append(pl.BlockSpec(memory_space=pl.ANY))
        args.append(w)
    io_alias = {}
    if cfg.has_state:
        for k, a in enumerate(aliased):
            in_specs.append(pl.BlockSpec(memory_space=pl.ANY))
            io_alias[len(args)] = k
            args.append(a)

    out_shape = [
        jax.ShapeDtypeStruct((n_rows_total, D_MODEL), F32),
        jax.ShapeDtypeStruct((n_rows_total, LANES), F32),
        jax.ShapeDtypeStruct((n_seq, POOL_HIST, BRANCH_W), F32),
        jax.ShapeDtypeStruct((n_seq, CONF_HIST, BRANCH_W), F32),
        jax.ShapeDtypeStruct((n_seq, SCONV_HIST, BRANCH_W), F32),
        jax.ShapeDtypeStruct((n_seq, cfg.v_rows, BRANCH_W), F32),
    ]
    out_specs = [
        pl.BlockSpec((tm, D_MODEL), out_tok),
        pl.BlockSpec((tm, LANES), out_tok),
        pl.BlockSpec((nb, POOL_HIST, BRANCH_W), seq3),
        pl.BlockSpec((nb, CONF_HIST, BRANCH_W), seq3),
        pl.BlockSpec((nb, SCONV_HIST, BRANCH_W), seq3),
        pl.BlockSpec((nb, cfg.v_rows, BRANCH_W), seq3),
    ]
    scratch = [
        pltpu.VMEM((nb, POOL_PAD + t, BRANCH_W), F32),
        pltpu.VMEM((nb, POOL_PAD + t, BRANCH_W), F32),
        pltpu.VMEM((nb, POOL_PAD + t, BRANCH_W), F32),
        pltpu.VMEM((nb, CONF_PAD + t, BRANCH_W), F32),
        pltpu.VMEM((SUBLANES, nb, t + CONF_PAD - SUBLANES, BRANCH_W), F32),
        pltpu.VMEM((nb, SCONV_PAD + t, BRANCH_W), F32),
        pltpu.VMEM((nb, t, BRANCH_W), F32),
        pltpu.VMEM((nb, t, BRANCH_W), F32),
        pltpu.VMEM((nb, t, BRANCH_W), F32),
        pltpu.VMEM((tm, N_BRANCH * D_MODEL), F32),
        pltpu.VMEM((tm, D_MODEL), BF16),
        pltpu.VMEM((IN_COLS // COL_TILE, D_MODEL, COL_TILE), BF16),
        pltpu.VMEM((N_BRANCH * D_MODEL // COL_TILE, BRANCH_W, COL_TILE), BF16),
        pltpu.VMEM((D_MODEL // COL_TILE, D_MODEL, COL_TILE), BF16),
        pltpu.VMEM((2, D_MODEL, COL_TILE), F32),
        pltpu.VMEM((2, BRANCH_W, D_MODEL), F32),
        pltpu.SemaphoreType.DMA((2,)),
    ]
    return pl.pallas_call(
        functools.partial(_mixer_kernel, cfg),
        grid=grid,
        in_specs=in_specs,
        out_specs=out_specs,
        out_shape=out_shape,
        scratch_shapes=scratch,
        input_output_aliases=io_alias,
        compiler_params=pltpu.CompilerParams(
            dimension_semantics=("arbitrary",) * len(grid), vmem_limit_bytes=VMEM_LIMIT),
        name="mixer_sample" if cfg.has_state else "mixer_prompt",
    )(*args)


def _sort_kernel(h1_ref, route_ref, n2g, xl_o, pos_o, nch_o):
    t = SORT_TILE
    xn = _rms(h1_ref[...], n2g[...]).astype(BF16)
    r = route_ref[...]
    lane_i = lax.broadcasted_iota(jnp.int32, (t, LANES), 1)
    lane = lane_i.astype(F32)
    hit1 = lane == r[:, 0:1]
    hit2 = lane == r[:, 1:2]
    onehot = jnp.where(hit1 | hit2, 1.0, 0.0)
    row = lax.broadcasted_iota(jnp.int32, (t, t), 0)
    col = lax.broadcasted_iota(jnp.int32, (t, t), 1)
    strict_lower = jnp.where(col < row, 1.0, 0.0).astype(BF16)
    before = jnp.dot(strict_lower, onehot.astype(BF16), preferred_element_type=F32)
    cnt = jnp.sum(onehot, axis=0, keepdims=True)
    nch = jnp.floor((cnt + (SEG_PAD - 1)) * (1.0 / SEG_PAD))
    er = lax.broadcasted_iota(jnp.int32, (LANES, LANES), 0)
    ec = lax.broadcasted_iota(jnp.int32, (LANES, LANES), 1)
    strict_upper = jnp.where(er < ec, 1.0, 0.0).astype(BF16)
    nch8 = jnp.broadcast_to(nch, (SUBLANES, LANES)).astype(BF16)
    seg0 = jnp.dot(nch8, strict_upper, preferred_element_type=F32)[0:1, :] * SEG_PAD
    where = seg0 + before
    pos0 = jnp.sum(jnp.where(hit1, where, 0.0), axis=-1, keepdims=True)
    pos1 = jnp.sum(jnp.where(hit2, where, 0.0), axis=-1, keepdims=True)
    pos = jnp.where(lane_i == 0, pos0, jnp.where(lane_i == 1, pos1, -1.0))
    pos_o[...] = pos
    nch_o[...] = jnp.broadcast_to(nch, (1, SUBLANES, LANES))
    pos_t = jnp.transpose(pos)
    p0 = pos_t[0:1, :]
    p1 = pos_t[1:2, :]
    for blk in range(N_PERM_BLOCKS):
        dst = (lax.broadcasted_iota(jnp.int32, (PERM_BLOCK, t), 0) + blk * PERM_BLOCK).astype(F32)
        perm = jnp.where((dst == p0) | (dst == p1), 1.0, 0.0).astype(BF16)
        xl_o[0, blk * PERM_BLOCK:(blk + 1) * PERM_BLOCK, :] = jnp.dot(
            perm, xn, preferred_element_type=F32).astype(BF16)


def _sort_call(h1, route, n2g):
    n = h1.shape[0]
    nt = n // SORT_TILE
    return pl.pallas_call(
        _sort_kernel,
        grid=(nt,),
        in_specs=[pl.BlockSpec((SORT_TILE, D_MODEL), lambda i: (i, 0)),
                  pl.BlockSpec((SORT_TILE, LANES), lambda i: (i, 0)),
                  pl.BlockSpec((1, D_MODEL), lambda i: (0, 0))],
        out_specs=[pl.BlockSpec((1, LOCAL_ROWS, D_MODEL), lambda i: (i, 0, 0)),
                   pl.BlockSpec((SORT_TILE, LANES), lambda i: (i, 0)),
                   pl.BlockSpec((1, SUBLANES, LANES), lambda i: (i, 0, 0))],
        out_shape=[jax.ShapeDtypeStruct((nt + 1, LOCAL_ROWS, D_MODEL), BF16),
                   jax.ShapeDtypeStruct((n, LANES), F32),
                   jax.ShapeDtypeStruct((nt, SUBLANES, LANES), F32)],
        compiler_params=pltpu.CompilerParams(dimension_semantics=("arbitrary",), vmem_limit_bytes=VMEM_LIMIT),
        name="moe_sort",
    )(h1, route, n2g)


def _chunk_copy(src, src_row, dst, dst_row, sem):
    return pltpu.make_async_copy(src.at[pl.ds(src_row, SEG_PAD), :], dst.at[pl.ds(dst_row, SEG_PAD), :], sem)


def _expert_kernel(trash_row0, se_ref, nv_ref, src_ref, dst_ref, xl_in, wg, wu, wd, xl_io,
                   xbuf0, xbuf1, ybuf0, ybuf1, sem_in, sem_out):
    del se_ref, xl_in
    xbuf, ybuf = (xbuf0, xbuf1), (ybuf0, ybuf1)
    s = pl.program_id(0)
    nv = nv_ref[0]

    def gather(step, sl):
        return [_chunk_copy(xl_io, pl.multiple_of(src_ref[step * CHUNKS_PER_STEP + j], SEG_PAD),
                            xbuf[sl], j * SEG_PAD, sem_in.at[sl]) for j in range(CHUNKS_PER_STEP)]

    def write_back(step, sl):
        return [_chunk_copy(ybuf[sl], j * SEG_PAD, xl_io,
                            pl.multiple_of(dst_ref[step * CHUNKS_PER_STEP + j], SEG_PAD), sem_out.at[sl])
                for j in range(CHUNKS_PER_STEP)]

    def spare_write(sl):
        return [_chunk_copy(ybuf[sl], j * SEG_PAD, xl_io, trash_row0 + sl * EXP_TILE + j * SEG_PAD,
                            sem_out.at[sl]) for j in range(CHUNKS_PER_STEP)]

    @pl.when(s == 0)
    def _():
        for c in gather(0, 0):
            c.start()
        for sl in range(2):
            ybuf[sl][...] = jnp.zeros_like(ybuf[sl])
            for c in spare_write(sl):
                c.start()

    def live_step(sl):
        for c in gather(s + 1, 1 - sl):
            c.start()
        for c in gather(s, sl):
            c.wait()
        xb = xbuf[sl][...]
        a = jnp.dot(xb, wg[0, 0].astype(BF16), preferred_element_type=F32)
        b = jnp.dot(xb, wu[0, 0].astype(BF16), preferred_element_type=F32)
        hh = (a * _sigmoid(a) * b).astype(BF16)
        y = jnp.dot(hh, wd[0, 0].astype(BF16), preferred_element_type=F32).astype(BF16)
        for c in write_back(s, sl):
            c.wait()
        ybuf[sl][...] = y
        for c in write_back(s, sl):
            c.start()

    def drain_step(sl):
        for c in gather(s, sl):
            c.wait()
        for both in range(2):
            for c in spare_write(both):
                c.wait()

    for sl in range(2):
        pl.when((s < nv) & (s % 2 == sl))(functools.partial(live_step, sl))
        pl.when((s == nv) & (s % 2 == sl))(functools.partial(drain_step, sl))


def _expert_call(xl, trash_row0, step_expert, n_valid, chunk_src, chunk_dst, layer, w_gate, w_up, w_down):
    n_steps = step_expert.shape[0]
    wmap = lambda s, se, *_: (layer, se[s], 0, 0)
    grid_spec = pltpu.PrefetchScalarGridSpec(
        num_scalar_prefetch=4,
        grid=(n_steps,),
        in_specs=[pl.BlockSpec(memory_space=pl.ANY),
                  pl.BlockSpec((1, 1, D_MODEL, D_EXPERT), wmap),
                  pl.BlockSpec((1, 1, D_MODEL, D_EXPERT), wmap),
                  pl.BlockSpec((1, 1, D_EXPERT, D_MODEL), wmap)],
        out_specs=pl.BlockSpec(memory_space=pl.ANY),
        scratch_shapes=[pltpu.VMEM((EXP_TILE, D_MODEL), BF16) for _ in range(4)] + [
                        pltpu.SemaphoreType.DMA((2,)), pltpu.SemaphoreType.DMA((2,))],
    )
    return pl.pallas_call(
        functools.partial(_expert_kernel, trash_row0),
        grid_spec=grid_spec,
        out_shape=jax.ShapeDtypeStruct(xl.shape, BF16),
        input_output_aliases={4: 0},
        compiler_params=pltpu.CompilerParams(dimension_semantics=("arbitrary",), has_side_effects=True),
        name="moe_experts",
    )(step_expert, n_valid, chunk_src, chunk_dst, xl, w_gate, w_up, w_down)


def _combine_kernel(final_norm, n_first, h1_ref, route_ref, pos_ref, fg, yl_ref, *outs):
    t = SORT_TILE
    r = route_ref[...]
    p = pos_ref[...]
    w1, w2 = r[:, 2:3], r[:, 3:4]
    pos0, pos1 = p[:, 0:1], p[:, 1:2]
    acc = h1_ref[...]
    for blk in range(N_PERM_BLOCKS):
        src = (lax.broadcasted_iota(jnp.int32, (t, PERM_BLOCK), 1) + blk * PERM_BLOCK).astype(F32)
        pw = jnp.where(src == pos0, w1, 0.0) + jnp.where(src == pos1, w2, 0.0)
        acc = acc + jnp.dot(pw.astype(BF16), yl_ref[0, blk * PERM_BLOCK:(blk + 1) * PERM_BLOCK, :],
                            preferred_element_type=F32)
    if not final_norm:
        outs[0][...] = acc
        return
    y = _rms(acc, fg[...])
    i = pl.program_id(0)

    @pl.when(i < n_first)
    def _():
        outs[0][...] = y

    @pl.when(i >= n_first)
    def _():
        outs[1][...] = y


def _combine_call(h1, route, pos, yl, final_g, final_norm, n_first_rows):
    n = h1.shape[0]
    nt = n // SORT_TILE
    n_first = n_first_rows // SORT_TILE
    tok = lambda i: (i, 0)
    if final_norm:
        out_shape = [jax.ShapeDtypeStruct((n_first_rows, D_MODEL), F32),
                     jax.ShapeDtypeStruct((n - n_first_rows, D_MODEL), F32)]
        out_specs = [pl.BlockSpec((SORT_TILE, D_MODEL), lambda i: (jnp.minimum(i, n_first - 1), 0)),
                     pl.BlockSpec((SORT_TILE, D_MODEL), lambda i: (jnp.maximum(i - n_first, 0), 0))]
    else:
        out_shape = [jax.ShapeDtypeStruct((n, D_MODEL), F32)]
        out_specs = [pl.BlockSpec((SORT_TILE, D_MODEL), tok)]
    return pl.pallas_call(
        functools.partial(_combine_kernel, final_norm, n_first),
        grid=(nt,),
        in_specs=[pl.BlockSpec((SORT_TILE, D_MODEL), tok),
                  pl.BlockSpec((SORT_TILE, LANES), tok),
                  pl.BlockSpec((SORT_TILE, LANES), tok),
                  pl.BlockSpec((1, D_MODEL), lambda i: (0, 0)),
                  pl.BlockSpec((1, LOCAL_ROWS, D_MODEL), lambda i: (i, 0, 0))],
        out_specs=out_specs,
        out_shape=out_shape,
        compiler_params=pltpu.CompilerParams(dimension_semantics=("arbitrary",), vmem_limit_bytes=VMEM_LIMIT),
        name="moe_combine",
    )(h1, route, pos, final_g, yl)


def _expert_tables(nch, n_steps, trash_row0):
    nt = nch.shape[0]
    cps = CHUNKS_PER_STEP
    seg_row0 = (jnp.cumsum(nch, axis=1) - nch) * SEG_PAD
    first = jnp.cumsum(nch, axis=0) - nch
    tot = jnp.sum(nch, axis=0)
    steps = (tot + cps - 1) // cps
    step_end = jnp.cumsum(steps)
    n_valid = step_end[-1:]
    s_ids = jnp.arange(n_steps, dtype=jnp.int32)
    step_expert = jnp.minimum(jnp.sum((step_end[None, :] <= s_ids[:, None]).astype(jnp.int32), axis=1),
                              N_EXPERTS - 1)
    sel = (step_expert[:, None] == jnp.arange(N_EXPERTS, dtype=jnp.int32)[None, :]).astype(jnp.int32)
    step0 = sel @ (step_end - steps)
    tot_s = sel @ tot
    first_s = sel @ first.T
    row0_s = sel @ seg_row0.T
    k = (s_ids - step0)[:, None] * cps + jnp.arange(cps, dtype=jnp.int32)[None, :]
    ok = (k < tot_s[:, None]) & (s_ids < n_valid[0])[:, None]
    tile = jnp.sum((first_s[:, None, :] <= k[:, :, None]).astype(jnp.int32), axis=2) - 1
    tsel = (tile[:, :, None] == jnp.arange(nt, dtype=jnp.int32)[None, None, :]).astype(jnp.int32)
    first_k = jnp.sum(tsel * first_s[:, None, :], axis=2)
    row0_k = jnp.sum(tsel * row0_s[:, None, :], axis=2)
    row = tile * LOCAL_ROWS + row0_k + (k - first_k) * SEG_PAD
    live = (s_ids < n_valid[0])[:, None]
    src = jnp.where(live, jnp.where(ok, row, row[:, 0:1]), 0)
    spare = trash_row0 + (s_ids % 2)[:, None] * EXP_TILE + jnp.arange(cps, dtype=jnp.int32)[None, :] * SEG_PAD
    dst = jnp.where(ok, row, spare)
    return (step_expert, n_valid.astype(jnp.int32), src.reshape(-1).astype(jnp.int32),
            dst.reshape(-1).astype(jnp.int32))


def _moe(h1, route, n2g, layer, w_gate, w_up, w_down, final_g, final_norm, n_first_rows):
    n = h1.shape[0]
    nt = n // SORT_TILE
    xl, pos, nch = _sort_call(h1, route, n2g)
    nch = nch[:, 0, :N_EXPERTS].astype(jnp.int32)
    max_chunks = (2 * n) // SEG_PAD + nt * N_EXPERTS
    n_steps = max_chunks // CHUNKS_PER_STEP + N_EXPERTS + 1
    trash_row0 = nt * LOCAL_ROWS
    step_expert, n_valid, chunk_src, chunk_dst = _expert_tables(nch, n_steps, trash_row0)
    yl = _expert_call(xl.reshape((nt + 1) * LOCAL_ROWS, D_MODEL), trash_row0, step_expert, n_valid,
                      chunk_src, chunk_dst, layer, w_gate, w_up, w_down)
    return _combine_call(h1, route, pos, yl.reshape(nt + 1, LOCAL_ROWS, D_MODEL), final_g, final_norm, n_first_rows)


def _block_diag(pw):
    out = jnp.zeros((BRANCH_W, BRANCH_W), pw.dtype)
    for g in range(len(POOL_WINDOWS)):
        out = out.at[g * POOL_GW:(g + 1) * POOL_GW, g * POOL_GW:(g + 1) * POOL_GW].set(pw[g])
    return out


def kernel(x_prompt, x_sample, state_pool, state_conv, state_sconv, norm1_g, w_in, pool_w, pool_scale, conf_dw, conf_dw_b, conf_ln_g, conf_ln_b, sconv_w, sgu_ln_g, sgu_ln_b, sgu_ws, sgu_b, w_branch, w_out, norm2_g, router_g, router_g_b, router_e, router_e_b, w_gate, w_up, w_down, final_g):
    depth = w_in.shape[0]
    bp, seq, _ = x_prompt.shape
    bs, dseq, _ = x_sample.shape
    n_p, n_s = bp * seq, bs * dseq
    n = n_p + n_s
    past_len = 16384
    assert seq % MIX_TILE == 0 and n_s % MIX_TILE_S == 0 and n_p % MIX_TILE_S == 0 and dseq == SUBLANES
    assert MIX_TILE % CHUNK == 0 and MIX_TILE_S % CHUNK == 0

    cfg_p = MixCfg(layer=0, nb=1, t=MIX_TILE, has_state=False, start_pos=0, row_off=0, v_rows=CHUNK)
    cfg_s = MixCfg(layer=0, nb=MIX_TILE_S // dseq, t=dseq, has_state=True, start_pos=past_len,
                   row_off=n_p // MIX_TILE_S, v_rows=dseq)
    big_weights = (w_in, w_branch, w_out)

    row = lambda a: a.reshape(1, -1)
    rep8 = lambda a: jnp.broadcast_to(a[..., None, :], a.shape[:-1] + (SUBLANES, a.shape[-1]))
    tril = jnp.tril(jnp.ones((CHUNK, CHUNK), F32))
    eye_blk = jnp.kron(jnp.eye(CHUNK // dseq, dtype=F32), jnp.ones((dseq, dseq), F32))
    final_row = row(final_g)

    h_p, h_s = x_prompt.reshape(n_p, D_MODEL), x_sample.reshape(n_s, D_MODEL)
    off_p, off_s = 0, 0
    states_out = []
    for l in range(depth):
        ws_p = sgu_ws[l] * tril[None]
        ws_small = jnp.tile(sgu_ws[l][:, :dseq, :dseq], (1, CHUNK // dseq, CHUNK // dseq))
        ws_s = ws_small * (tril * eye_blk)[None]
        cat = lambda w: jnp.concatenate([w[h] for h in range(SGU_HEADS)], axis=1).astype(BF16)
        bias_p = jnp.repeat(sgu_b[l].T, SGU_HW, axis=1)
        bias_s = jnp.tile(jnp.repeat(sgu_b[l][:, :dseq].T, SGU_HW, axis=1), (CHUNK // dseq, 1))
        lane_pad = LANES - N_EXPERTS - N_GROUPS
        w_rt32 = jnp.pad(jnp.concatenate([router_e[l], router_g[l]], axis=1), ((0, 0), (0, lane_pad)))
        w_rt_hi = w_rt32.astype(BF16)
        w_rt = jnp.stack([w_rt_hi, (w_rt32 - w_rt_hi.astype(F32)).astype(BF16)])
        b_rt = jnp.pad(jnp.concatenate([router_e_b[l], router_g_b[l]]), (0, lane_pad)).reshape(1, LANES)

        def weights(wcat, sbias):
            return [row(norm1_g[l]), _block_diag(pool_w[l]).astype(BF16), row(pool_scale[l]),
                    rep8(conf_dw[l]), rep8(conf_dw_b[l]), row(conf_ln_g[l]), row(conf_ln_b[l]), rep8(sconv_w[l]),
                    row(sgu_ln_g[l]), row(sgu_ln_b[l]), wcat, sbias, row(norm2_g[l]), w_rt, b_rt]

        h1, route, pool_p, conv_p, sconv_p, v_p = _mixer_call(
            cfg_p._replace(layer=l), bp, n, h_p, off_p, None, weights(cat(ws_p), bias_p), big_weights, None)
        h1, route, pool_s, conv_s, sconv_s, v_s = _mixer_call(
            cfg_s._replace(layer=l), bs, n, h_s, off_s, (state_pool[l], state_conv[l], state_sconv[l]),
            weights(cat(ws_s), bias_s), big_weights, (h1, route))
        states_out.append((pool_p, conv_p, sconv_p, v_p, pool_s, conv_s, sconv_s, v_s))

        outs = _moe(h1, route, row(norm2_g[l]), l, w_gate, w_up, w_down, final_row, l == depth - 1, n_p)
        h_p = h_s = outs[0]
        off_p, off_s = 0, n_p // MIX_TILE_S

    y_prompt = outs[0].reshape(bp, seq, D_MODEL)
    y_sample = outs[1].reshape(bs, dseq, D_MODEL)
    st = [jnp.stack([s[k] for s in states_out]) for k in range(8)]
    return (y_prompt, y_sample, st[0], st[1], st[2], st[3], st[4], st[5], st[6], st[7])
```

```python
import functools
from typing import NamedTuple

import jax
import jax.numpy as jnp
from jax import lax
from jax.experimental import pallas as pl
from jax.experimental.pallas import tpu as pltpu

F32 = jnp.float32
BF16 = jnp.bfloat16

D_MODEL = 1024
BRANCH_W = 256
N_BRANCH = 4
POOL_WINDOWS = (2, 4, 8, 16)
POOL_GW = 64
POOL_HIST = 15
CONF_WIDTH = 31
CONF_HIST = 30
SCONV_WIDTH = 3
SCONV_HIST = 2
CHUNK = 128
SGU_HEADS = 4
SGU_HW = 64
N_GROUPS = 4
EXPERTS_PER_GROUP = 8
N_EXPERTS = 32
D_EXPERT = 256
RMS_EPS = 1e-6
LN_EPS = 1e-5
IN_COLS = 6144
GATE_COL0 = 2048

COL_TILE = 256
LANES = 128
SUBLANES = 8
POOL_PAD = 32
CONF_PAD = 32
SCONV_PAD = 8
ROW_BLOCK = 32
GATE_LEAD = 2
ROUTE_GROUP_LANE0 = 32
NEG_BIG = -3.0e38

MIX_TILE = 512
MIX_TILE_S = 256
SORT_TILE = 512
SEG_PAD = 16
PERM_BLOCK = 256
N_PERM_BLOCKS = -(-(2 * SORT_TILE + N_EXPERTS * (SEG_PAD - 1)) // PERM_BLOCK)
LOCAL_ROWS = N_PERM_BLOCKS * PERM_BLOCK
EXP_TILE = 512
CHUNKS_PER_STEP = EXP_TILE // SEG_PAD
EXP_SLOTS = 3
assert EXP_SLOTS * EXP_TILE <= LOCAL_ROWS
VMEM_LIMIT = 56 * 1024 * 1024


class MixCfg(NamedTuple):
    layer: int
    nb: int
    t: int
    has_state: bool
    start_pos: int
    row_off: int
    v_rows: int


def _rms(x, g):
    return x * lax.rsqrt(jnp.mean(x * x, axis=-1, keepdims=True) + RMS_EPS) * g


def _ln(x, g, b):
    mu = jnp.mean(x, axis=-1, keepdims=True)
    xc = x - mu
    return xc * lax.rsqrt(jnp.mean(xc * xc, axis=-1, keepdims=True) + LN_EPS) * g + b


def _sigmoid(x):
    return 0.5 * jnp.tanh(0.5 * x) + 0.5


def _zero_like_bits(x):
    bits = lax.bitcast_convert_type(x, jnp.uint32)
    return ((bits >> 16) >> 16).astype(jnp.int32).astype(F32)


def _gelu_tanh(x):
    return 0.5 * x * (1.0 + jnp.tanh(0.7978845608028654 * (x + 0.044715 * (x * x * x))))


def _row_blocks(nb, t):
    if t >= ROW_BLOCK:
        return [(slice(b, b + 1), t0, ROW_BLOCK) for b in range(nb) for t0 in range(0, t, ROW_BLOCK)]
    bb = ROW_BLOCK // t
    return [(slice(b0, b0 + bb), 0, t) for b0 in range(0, nb, bb)]


def _load_big_weights(layer, w_in_hbm, w_br_hbm, w_out_hbm, w_in, w_br, w_out, stage, stage_b, sem):
    n_in, n_out = IN_COLS // COL_TILE, D_MODEL // COL_TILE
    jobs = [("in", j) for j in range(n_in)] + [("out", j) for j in range(n_out)] + [("br", i) for i in range(N_BRANCH)]

    def copy(job, slot):
        kind, j = job
        if kind == "br":
            return pltpu.make_async_copy(w_br_hbm.at[layer, j], stage_b.at[slot], sem.at[slot])
        src = w_in_hbm if kind == "in" else w_out_hbm
        return pltpu.make_async_copy(src.at[layer, :, pl.ds(j * COL_TILE, COL_TILE)], stage.at[slot], sem.at[slot])

    copy(jobs[0], 0).start()
    for n, job in enumerate(jobs):
        slot = n % 2
        if n + 1 < len(jobs):
            copy(jobs[n + 1], 1 - slot).start()
        copy(job, slot).wait()
        kind, j = job
        if kind == "in":
            tile = stage[slot]
            w_in[j] = (tile * 0.5 if j * COL_TILE >= GATE_COL0 else tile).astype(BF16)
        elif kind == "out":
            w_out[j] = stage[slot].astype(BF16)
        else:
            for c in range(n_out):
                w_br[j * n_out + c] = (stage_b[slot, :, c * COL_TILE:(c + 1) * COL_TILE] * 0.5).astype(BF16)


def _mixer_kernel(cfg, *refs):
    nb, t, tm = cfg.nb, cfg.t, cfg.nb * cfg.t
    refs = list(refs)
    x_ref = refs.pop(0)
    if cfg.has_state:
        pool_st, conv_st, sconv_st = refs.pop(0), refs.pop(0), refs.pop(0)
    (n1g, pool_w, pool_sc, cdw, cdb, clg, clb, scw, slg, slb, wcat, sbias, n2g, w_rt, b_rt,
     w_in_hbm, w_br_hbm, w_out_hbm) = refs[:18]
    refs = refs[18:]
    if cfg.has_state:
        refs = refs[2:]
    h1_o, route_o, pool_o, conv_o, sconv_o, v_o = refs[:6]
    (pool_ext, sum_a, sum_b, conv_ext, conv_sh, sc_ext, buf_a, buf_b, buf_c, gate_buf, xb_buf,
     w_in, w_br, w_out, stage, stage_b, w_sem) = refs[6:]

    first_step = pl.program_id(0) == 0
    if not cfg.has_state:
        first_step = first_step & (pl.program_id(1) == 0)
    pl.when(first_step)(functools.partial(
        _load_big_weights, cfg.layer, w_in_hbm, w_br_hbm, w_out_hbm, w_in, w_br, w_out, stage, stage_b, w_sem))

    if cfg.has_state:
        seq_pos0 = cfg.start_pos
        pool_ext[:, 0:POOL_PAD - SUBLANES, :] = jnp.zeros((nb, POOL_PAD - SUBLANES, BRANCH_W), F32)
        pool_ext[:, POOL_PAD - POOL_HIST:POOL_PAD, :] = pool_st[...]
        conv_ext[:, CONF_PAD - CONF_HIST:CONF_PAD, :] = conv_st[...]
        sc_ext[:, SCONV_PAD - SCONV_HIST:SCONV_PAD, :] = sconv_st[...]
    else:
        c = pl.program_id(1)
        seq_pos0 = cfg.start_pos + c * t

        @pl.when(c == 0)
        def _():
            pool_ext[:, 0:POOL_PAD, :] = jnp.zeros((nb, POOL_PAD, BRANCH_W), F32)
            conv_ext[:, 0:CONF_PAD, :] = jnp.zeros((nb, CONF_PAD, BRANCH_W), F32)
            sc_ext[:, 0:SCONV_PAD, :] = jnp.zeros((nb, SCONV_PAD, BRANCH_W), F32)

    x = x_ref[...]
    xb_buf[...] = _rms(x, n1g[...]).astype(BF16)

    def proj(lo, hi):
        assert lo % COL_TILE == 0 and hi == lo + COL_TILE
        return jnp.dot(xb_buf[...], w_in[lo // COL_TILE], preferred_element_type=F32)

    a_pool = proj(0, 256)
    pool_ext[:, POOL_PAD:, :] = a_pool.reshape(nb, t, BRANCH_W)
    glu = proj(256, 512) * _sigmoid(proj(512, 768))
    conv_ext[:, CONF_PAD:, :] = glu.reshape(nb, t, BRANCH_W)
    z = proj(1024, 1280) * proj(1280, 1536)
    sc_ext[:, SCONV_PAD:, :] = z.reshape(nb, t, BRANCH_W)

    pl_len = POOL_PAD + t
    sum_b[:, 8:pl_len, :] = pool_ext[:, 8:pl_len, :] + pool_ext[:, 7:pl_len - 1, :]
    sum_a[:, 16:pl_len, :] = sum_b[:, 16:pl_len, :] + sum_b[:, 14:pl_len - 2, :]
    sum_b[:, 24:pl_len, :] = sum_a[:, 24:pl_len, :] + sum_a[:, 20:pl_len - 4, :]
    for sh in range(SUBLANES):
        n_rows = t + SUBLANES * ((CONF_WIDTH - 1 - sh) // SUBLANES)
        first = CONF_PAD - CONF_HIST + sh
        conv_sh[sh, :, 0:n_rows, :] = conv_ext[:, first:first + n_rows, :]

    blocks = _row_blocks(nb, t)
    gate_cols = (N_BRANCH * D_MODEL) // len(blocks)
    bias_tile = cdb[...]
    conv_zeros = []
    for bi, (bs, t0, tb) in enumerate(blocks):
        bb = bs.stop - bs.start
        shp = (bb, tb, BRANCH_W)
        lane = lax.broadcasted_iota(jnp.int32, shp, 2)
        pos = seq_pos0 + t0 + lax.broadcasted_iota(jnp.int32, shp, 1)
        r0 = POOL_PAD + t0
        cur = pool_ext[bs, r0:r0 + tb, :]
        s2 = cur + pool_ext[bs, r0 - 1:r0 - 1 + tb, :]
        s4 = sum_a[bs, r0:r0 + tb, :]
        s8 = sum_b[bs, r0:r0 + tb, :]
        s16 = s8 + sum_b[bs, r0 - SUBLANES:r0 - SUBLANES + tb, :]
        win_sum = jnp.where(lane < 64, s2, jnp.where(lane < 128, s4, jnp.where(lane < 192, s8, s16)))
        win = jnp.where(lane < 64, 2, jnp.where(lane < 128, 4, jnp.where(lane < 192, 8, 16)))
        cnt = jnp.minimum(pos + 1, win).astype(F32)
        buf_a[bs, t0:t0 + tb, :] = win_sum / cnt - cur
        tiles = (ROW_BLOCK // SUBLANES, SUBLANES, BRANCH_W)
        acc = jnp.broadcast_to(bias_tile[None], tiles)
        for k in range(CONF_WIDTH):
            q, sh = divmod(k, SUBLANES)
            r = t0 + SUBLANES * q
            acc = acc + conv_sh[sh, bs, r:r + tb, :].reshape(tiles) * cdw[k][None]
        buf_b[bs, t0:t0 + tb, :] = acc.reshape(shp)
        conv_zeros.append(_zero_like_bits(acc[0]))
        base = SCONV_PAD - SCONV_HIST + t0
        acc = sc_ext[bs, base:base + tb, :].reshape(tiles) * scw[0][None]
        for k in range(1, SCONV_WIDTH):
            acc = acc + sc_ext[bs, base + k:base + k + tb, :].reshape(tiles) * scw[k][None]
        buf_c[bs, t0:t0 + tb, :] = acc.reshape(shp)
        if bi >= GATE_LEAD:
            z = conv_zeros[bi - GATE_LEAD]
            corner = xb_buf[0:2 * SUBLANES, 0:BRANCH_W].astype(F32) + jnp.concatenate([z, z], axis=0)
            xb_buf[0:2 * SUBLANES, 0:BRANCH_W] = corner.astype(BF16)
        for g0 in range(bi * gate_cols, (bi + 1) * gate_cols, COL_TILE):
            g = jnp.tanh(proj(GATE_COL0 + g0, GATE_COL0 + g0 + COL_TILE)) + 1.0
            gate_buf[:, g0:g0 + COL_TILE] = g
        bias_tile = cdb[...] + _zero_like_bits(g[0:SUBLANES, :])

    pool_o[...] = pool_ext[:, POOL_PAD + t - POOL_HIST:POOL_PAD + t, :]
    conv_o[...] = conv_ext[:, CONF_PAD + t - CONF_HIST:CONF_PAD + t, :]
    sconv_o[...] = sc_ext[:, SCONV_PAD + t - SCONV_HIST:SCONV_PAD + t, :]
    if not cfg.has_state:
        pool_ext[:, 0:POOL_PAD, :] = pool_ext[:, t:t + POOL_PAD, :]
        conv_ext[:, 0:CONF_PAD, :] = conv_ext[:, t:t + CONF_PAD, :]
        sc_ext[:, 0:SCONV_PAD, :] = sc_ext[:, t:t + SCONV_PAD, :]

    pooled = buf_a[...].reshape(tm, BRANCH_W).astype(BF16)
    br_a = jnp.dot(pooled, pool_w[...], preferred_element_type=F32) * pool_sc[...]

    cb = _ln(buf_b[...].reshape(tm, BRANCH_W), clg[...], clb[...])
    br_b = cb * _sigmoid(cb)

    br_c = proj(768, 1024) * buf_c[...].reshape(tm, BRANCH_W)

    gu = _gelu_tanh(proj(1536, 1792))
    v = _ln(_gelu_tanh(proj(1792, 2048)), slg[...], slb[...])
    if cfg.has_state:
        v_o[...] = v.reshape(nb, t, BRANCH_W)
    else:
        v_o[...] = v[tm - cfg.v_rows:, :].reshape(1, cfg.v_rows, BRANCH_W)
    head = lax.broadcasted_iota(jnp.int32, (CHUNK, BRANCH_W), 1) // SGU_HW
    mixed = []
    for j in range(tm // CHUNK):
        vj = v[j * CHUNK:(j + 1) * CHUNK, :]
        stacked = jnp.concatenate([jnp.where(head == h, vj, 0.0) for h in range(SGU_HEADS)], axis=0)
        mixed.append(jnp.dot(wcat[...], stacked.astype(BF16), preferred_element_type=F32) + sbias[...])
    br_d = gu * jnp.concatenate(mixed, axis=0)

    n_ct = D_MODEL // COL_TILE
    brs = [br.astype(BF16) for br in (br_a, br_b, br_c, br_d)]
    for c in range(n_ct):
        part = None
        for i in range(N_BRANCH):
            g0 = i * D_MODEL + c * COL_TILE
            term = gate_buf[:, g0:g0 + COL_TILE] * jnp.dot(
                brs[i], w_br[i * n_ct + c], preferred_element_type=F32)
            part = term if part is None else part + term
        xb_buf[:, c * COL_TILE:(c + 1) * COL_TILE] = part.astype(BF16)
    for c in range(n_ct):
        cs = slice(c * COL_TILE, (c + 1) * COL_TILE)
        h1_o[:, cs] = x_ref[:, cs] + jnp.dot(xb_buf[...], w_out[c], preferred_element_type=F32)

    xn2 = _rms(h1_o[...], n2g[...])
    x_hi = xn2.astype(BF16)
    x_lo = (xn2 - x_hi.astype(F32)).astype(BF16)
    logits = (jnp.dot(x_hi, w_rt[0], preferred_element_type=F32)
              + (jnp.dot(x_lo, w_rt[0], preferred_element_type=F32)
                 + jnp.dot(x_hi, w_rt[1], preferred_element_type=F32))) + b_rt[...]
    lane = lax.broadcasted_iota(jnp.int32, (tm, LANES), 1)
    is_g = (lane >= ROUTE_GROUP_LANE0) & (lane < ROUTE_GROUP_LANE0 + N_GROUPS)
    glog = jnp.where(is_g, logits, NEG_BIG)
    gmax = jnp.max(glog, axis=-1, keepdims=True)
    lane_f = lane.astype(F32)
    no_lane = float(4 * LANES)
    gsel = jnp.min(jnp.where(glog == gmax, lane_f, no_lane), axis=-1, keepdims=True) - ROUTE_GROUP_LANE0
    pg = 1.0 / jnp.sum(jnp.where(is_g, jnp.exp(glog - gmax), 0.0), axis=-1, keepdims=True)
    in_grp = (lane < N_EXPERTS) & ((lane // EXPERTS_PER_GROUP) == gsel.astype(jnp.int32))
    el = jnp.where(in_grp, logits, NEG_BIG)
    m1 = jnp.max(el, axis=-1, keepdims=True)
    i1 = jnp.min(jnp.where(in_grp & (el == m1), lane_f, no_lane), axis=-1, keepdims=True)
    rest = in_grp & (lane_f != i1)
    el2 = jnp.where(rest, logits, NEG_BIG)
    m2 = jnp.max(el2, axis=-1, keepdims=True)
    i2 = jnp.min(jnp.where(rest & (el2 == m2), lane_f, no_lane), axis=-1, keepdims=True)
    e21 = jnp.exp(m2 - m1)
    w1 = pg / (1.0 + e21)
    w2 = pg * e21 / (1.0 + e21)
    route_o[...] = jnp.where(lane == 0, i1, jnp.where(lane == 1, i2,
                             jnp.where(lane == 2, w1, jnp.where(lane == 3, w2, 0.0))))


def _const_spec(shape):
    nd = len(shape)
    return pl.BlockSpec(shape, lambda *_: (0,) * nd, pipeline_mode=pl.Buffered(1))


def _mixer_call(cfg, n_seq, n_rows_total, x2d, x_tile_off, states, weights, big_weights, aliased):
    nb, t = cfg.nb, cfg.t
    tm = nb * t
    if cfg.has_state:
        grid = (n_seq // nb,)
        tok = lambda i: (x_tile_off + i, 0)
        out_tok = lambda i: (cfg.row_off + i, 0)
        seq3 = lambda i: (i, 0, 0)
    else:
        grid = (n_seq, 2048 // t)
        nt = grid[1]
        tok = lambda b, c: (x_tile_off + b * nt + c, 0)
        out_tok = lambda b, c: (cfg.row_off + b * nt + c, 0)
        seq3 = lambda b, c: (b, 0, 0)

    in_specs = [pl.BlockSpec((tm, D_MODEL), tok)]
    args = [x2d]
    if cfg.has_state:
        for s in states:
            in_specs.append(pl.BlockSpec((nb,) + s.shape[1:], seq3))
            args.append(s)
    for w in weights:
        in_specs.append(_const_spec(w.shape))
        args.append(w)
    for w in big_weights:
        in_specs.append(pl.BlockSpec(memory_space=pl.ANY))
        args.append(w)
    io_alias = {}
    if cfg.has_state:
        for k, a in enumerate(aliased):
            in_specs.append(pl.BlockSpec(memory_space=pl.ANY))
            io_alias[len(args)] = k
            args.append(a)

    out_shape = [
        jax.ShapeDtypeStruct((n_rows_total, D_MODEL), F32),
        jax.ShapeDtypeStruct((n_rows_total, LANES), F32),
        jax.ShapeDtypeStruct((n_seq, POOL_HIST, BRANCH_W), F32),
        jax.ShapeDtypeStruct((n_seq, CONF_HIST, BRANCH_W), F32),
        jax.ShapeDtypeStruct((n_seq, SCONV_HIST, BRANCH_W), F32),
        jax.ShapeDtypeStruct((n_seq, cfg.v_rows, BRANCH_W), F32),
    ]
    out_specs = [
        pl.BlockSpec((tm, D_MODEL), out_tok),
        pl.BlockSpec((tm, LANES), out_tok),
        pl.BlockSpec((nb, POOL_HIST, BRANCH_W), seq3),
        pl.BlockSpec((nb, CONF_HIST, BRANCH_W), seq3),
        pl.BlockSpec((nb, SCONV_HIST, BRANCH_W), seq3),
        pl.BlockSpec((nb, cfg.v_rows, BRANCH_W), seq3),
    ]
    scratch = [
        pltpu.VMEM((nb, POOL_PAD + t, BRANCH_W), F32),
        pltpu.VMEM((nb, POOL_PAD + t, BRANCH_W), F32),
        pltpu.VMEM((nb, POOL_PAD + t, BRANCH_W), F32),
        pltpu.VMEM((nb, CONF_PAD + t, BRANCH_W), F32),
        pltpu.VMEM((SUBLANES, nb, t + CONF_PAD - SUBLANES, BRANCH_W), F32),
        pltpu.VMEM((nb, SCONV_PAD + t, BRANCH_W), F32),
        pltpu.VMEM((nb, t, BRANCH_W), F32),
        pltpu.VMEM((nb, t, BRANCH_W), F32),
        pltpu.VMEM((nb, t, BRANCH_W), F32),
        pltpu.VMEM((tm, N_BRANCH * D_MODEL), F32),
        pltpu.VMEM((tm, D_MODEL), BF16),
        pltpu.VMEM((IN_COLS // COL_TILE, D_MODEL, COL_TILE), BF16),
        pltpu.VMEM((N_BRANCH * D_MODEL // COL_TILE, BRANCH_W, COL_TILE), BF16),
        pltpu.VMEM((D_MODEL // COL_TILE, D_MODEL, COL_TILE), BF16),
        pltpu.VMEM((2, D_MODEL, COL_TILE), F32),
        pltpu.VMEM((2, BRANCH_W, D_MODEL), F32),
        pltpu.SemaphoreType.DMA((2,)),
    ]
    return pl.pallas_call(
        functools.partial(_mixer_kernel, cfg),
        grid=grid,
        in_specs=in_specs,
        out_specs=out_specs,
        out_shape=out_shape,
        scratch_shapes=scratch,
        input_output_aliases=io_alias,
        compiler_params=pltpu.CompilerParams(
            dimension_semantics=("arbitrary",) * len(grid), vmem_limit_bytes=VMEM_LIMIT),
        name="mixer_sample" if cfg.has_state else "mixer_prompt",
    )(*args)


def _sort_kernel(h1_ref, route_ref, n2g, xl_o, pos_o, nch_o):
    t = SORT_TILE
    xn = _rms(h1_ref[...], n2g[...]).astype(BF16)
    r = route_ref[...]
    lane_i = lax.broadcasted_iota(jnp.int32, (t, LANES), 1)
    lane = lane_i.astype(F32)
    hit1 = lane == r[:, 0:1]
    hit2 = lane == r[:, 1:2]
    onehot = jnp.where(hit1 | hit2, 1.0, 0.0)
    row = lax.broadcasted_iota(jnp.int32, (t, t), 0)
    col = lax.broadcasted_iota(jnp.int32, (t, t), 1)
    strict_lower = jnp.where(col < row, 1.0, 0.0).astype(BF16)
    before = jnp.dot(strict_lower, onehot.astype(BF16), preferred_element_type=F32)
    cnt = jnp.sum(onehot, axis=0, keepdims=True)
    nch = jnp.floor((cnt + (SEG_PAD - 1)) * (1.0 / SEG_PAD))
    er = lax.broadcasted_iota(jnp.int32, (LANES, LANES), 0)
    ec = lax.broadcasted_iota(jnp.int32, (LANES, LANES), 1)
    strict_upper = jnp.where(er < ec, 1.0, 0.0).astype(BF16)
    nch8 = jnp.broadcast_to(nch, (SUBLANES, LANES)).astype(BF16)
    seg0 = jnp.dot(nch8, strict_upper, preferred_element_type=F32)[0:1, :] * SEG_PAD
    where = seg0 + before
    pos0 = jnp.sum(jnp.where(hit1, where, 0.0), axis=-1, keepdims=True)
    pos1 = jnp.sum(jnp.where(hit2, where, 0.0), axis=-1, keepdims=True)
    pos = jnp.where(lane_i == 0, pos0, jnp.where(lane_i == 1, pos1, -1.0))
    pos_o[...] = pos
    nch_o[...] = jnp.broadcast_to(nch, (1, SUBLANES, LANES))
    pos_t = jnp.transpose(pos)
    p0 = pos_t[0:1, :]
    p1 = pos_t[1:2, :]
    for blk in range(N_PERM_BLOCKS):
        dst = (lax.broadcasted_iota(jnp.int32, (PERM_BLOCK, t), 0) + blk * PERM_BLOCK).astype(F32)
        perm = jnp.where((dst == p0) | (dst == p1), 1.0, 0.0).astype(BF16)
        xl_o[0, blk * PERM_BLOCK:(blk + 1) * PERM_BLOCK, :] = jnp.dot(
            perm, xn, preferred_element_type=F32).astype(BF16)


def _sort_call(h1, route, n2g):
    n = h1.shape[0]
    nt = n // SORT_TILE
    return pl.pallas_call(
        _sort_kernel,
        grid=(nt,),
        in_specs=[pl.BlockSpec((SORT_TILE, D_MODEL), lambda i: (i, 0)),
                  pl.BlockSpec((SORT_TILE, LANES), lambda i: (i, 0)),
                  pl.BlockSpec((1, D_MODEL), lambda i: (0, 0))],
        out_specs=[pl.BlockSpec((1, LOCAL_ROWS, D_MODEL), lambda i: (i, 0, 0)),
                   pl.BlockSpec((SORT_TILE, LANES), lambda i: (i, 0)),
                   pl.BlockSpec((1, SUBLANES, LANES), lambda i: (i, 0, 0))],
        out_shape=[jax.ShapeDtypeStruct((nt + 1, LOCAL_ROWS, D_MODEL), BF16),
                   jax.ShapeDtypeStruct((n, LANES), F32),
                   jax.ShapeDtypeStruct((nt, SUBLANES, LANES), F32)],
        compiler_params=pltpu.CompilerParams(dimension_semantics=("arbitrary",), vmem_limit_bytes=VMEM_LIMIT),
        name="moe_sort",
    )(h1, route, n2g)


def _chunk_copy(src, src_row, dst, dst_row, sem):
    return pltpu.make_async_copy(src.at[pl.ds(src_row, SEG_PAD), :], dst.at[pl.ds(dst_row, SEG_PAD), :], sem)


def _expert_kernel(trash_row0, se_ref, nv_ref, src_ref, dst_ref, xl_in, wg, wu, wd, xl_io, *scratch):
    del se_ref, xl_in
    ns = EXP_SLOTS
    xbuf, ybuf, (sem_in, sem_out) = scratch[:ns], scratch[ns:2 * ns], scratch[2 * ns:]
    s = pl.program_id(0)
    nv = nv_ref[0]

    def gather(step, sl):
        return [_chunk_copy(xl_io, pl.multiple_of(src_ref[step * CHUNKS_PER_STEP + j], SEG_PAD),
                            xbuf[sl], j * SEG_PAD, sem_in.at[sl]) for j in range(CHUNKS_PER_STEP)]

    def write_back(step, sl):
        return [_chunk_copy(ybuf[sl], j * SEG_PAD, xl_io,
                            pl.multiple_of(dst_ref[step * CHUNKS_PER_STEP + j], SEG_PAD), sem_out.at[sl])
                for j in range(CHUNKS_PER_STEP)]

    def spare_write(sl):
        return [_chunk_copy(ybuf[sl], j * SEG_PAD, xl_io, trash_row0 + sl * EXP_TILE + j * SEG_PAD,
                            sem_out.at[sl]) for j in range(CHUNKS_PER_STEP)]

    @pl.when(s == 0)
    def _():
        for ahead in range(ns - 1):
            for c in gather(ahead, ahead):
                c.start()
        for sl in range(ns):
            ybuf[sl][...] = jnp.zeros_like(ybuf[sl])
            for c in spare_write(sl):
                c.start()

    def live_step(sl):
        for c in gather(s + ns - 1, (sl + ns - 1) % ns):
            c.start()
        for c in gather(s, sl):
            c.wait()
        xb = xbuf[sl][...]
        a = jnp.dot(xb, wg[0, 0].astype(BF16), preferred_element_type=F32)
        b = jnp.dot(xb, wu[0, 0].astype(BF16), preferred_element_type=F32)
        hh = (a * _sigmoid(a) * b).astype(BF16)
        y = jnp.dot(hh, wd[0, 0].astype(BF16), preferred_element_type=F32).astype(BF16)
        for c in write_back(s, sl):
            c.wait()
        ybuf[sl][...] = y
        for c in write_back(s, sl):
            c.start()

    def drain_step(sl):
        for ahead in range(ns - 1):
            for c in gather(s + ahead, (sl + ahead) % ns):
                c.wait()
        for every in range(ns):
            for c in spare_write(every):
                c.wait()

    for sl in range(ns):
        pl.when((s < nv) & (s % ns == sl))(functools.partial(live_step, sl))
        pl.when((s == nv) & (s % ns == sl))(functools.partial(drain_step, sl))


def _expert_call(xl, trash_row0, step_expert, n_valid, chunk_src, chunk_dst, layer, w_gate, w_up, w_down):
    n_steps = step_expert.shape[0] - (EXP_SLOTS - 1)
    wmap = lambda s, se, *_: (layer, se[s], 0, 0)
    grid_spec = pltpu.PrefetchScalarGridSpec(
        num_scalar_prefetch=4,
        grid=(n_steps,),
        in_specs=[pl.BlockSpec(memory_space=pl.ANY),
                  pl.BlockSpec((1, 1, D_MODEL, D_EXPERT), wmap),
                  pl.BlockSpec((1, 1, D_MODEL, D_EXPERT), wmap),
                  pl.BlockSpec((1, 1, D_EXPERT, D_MODEL), wmap)],
        out_specs=pl.BlockSpec(memory_space=pl.ANY),
        scratch_shapes=[pltpu.VMEM((EXP_TILE, D_MODEL), BF16) for _ in range(2 * EXP_SLOTS)] + [
                        pltpu.SemaphoreType.DMA((EXP_SLOTS,)), pltpu.SemaphoreType.DMA((EXP_SLOTS,))],
    )
    return pl.pallas_call(
        functools.partial(_expert_kernel, trash_row0),
        grid_spec=grid_spec,
        out_shape=jax.ShapeDtypeStruct(xl.shape, BF16),
        input_output_aliases={4: 0},
        compiler_params=pltpu.CompilerParams(dimension_semantics=("arbitrary",), has_side_effects=True),
        name="moe_experts",
    )(step_expert, n_valid, chunk_src, chunk_dst, xl, w_gate, w_up, w_down)


def _combine_kernel(final_norm, n_first, h1_ref, route_ref, pos_ref, fg, yl_ref, *outs):
    t = SORT_TILE
    r = route_ref[...]
    p = pos_ref[...]
    w1, w2 = r[:, 2:3], r[:, 3:4]
    pos0, pos1 = p[:, 0:1], p[:, 1:2]
    acc = h1_ref[...]
    for blk in range(N_PERM_BLOCKS):
        src = (lax.broadcasted_iota(jnp.int32, (t, PERM_BLOCK), 1) + blk * PERM_BLOCK).astype(F32)
        pw = jnp.where(src == pos0, w1, 0.0) + jnp.where(src == pos1, w2, 0.0)
        acc = acc + jnp.dot(pw.astype(BF16), yl_ref[0, blk * PERM_BLOCK:(blk + 1) * PERM_BLOCK, :],
                            preferred_element_type=F32)
    if not final_norm:
        outs[0][...] = acc
        return
    y = _rms(acc, fg[...])
    i = pl.program_id(0)

    @pl.when(i < n_first)
    def _():
        outs[0][...] = y

    @pl.when(i >= n_first)
    def _():
        outs[1][...] = y


def _combine_call(h1, route, pos, yl, final_g, final_norm, n_first_rows):
    n = h1.shape[0]
    nt = n // SORT_TILE
    n_first = n_first_rows // SORT_TILE
    tok = lambda i: (i, 0)
    if final_norm:
        out_shape = [jax.ShapeDtypeStruct((n_first_rows, D_MODEL), F32),
                     jax.ShapeDtypeStruct((n - n_first_rows, D_MODEL), F32)]
        out_specs = [pl.BlockSpec((SORT_TILE, D_MODEL), lambda i: (jnp.minimum(i, n_first - 1), 0)),
                     pl.BlockSpec((SORT_TILE, D_MODEL), lambda i: (jnp.maximum(i - n_first, 0), 0))]
    else:
        out_shape = [jax.ShapeDtypeStruct((n, D_MODEL), F32)]
        out_specs = [pl.BlockSpec((SORT_TILE, D_MODEL), tok)]
    return pl.pallas_call(
        functools.partial(_combine_kernel, final_norm, n_first),
        grid=(nt,),
        in_specs=[pl.BlockSpec((SORT_TILE, D_MODEL), tok),
                  pl.BlockSpec((SORT_TILE, LANES), tok),
                  pl.BlockSpec((SORT_TILE, LANES), tok),
                  pl.BlockSpec((1, D_MODEL), lambda i: (0, 0)),
                  pl.BlockSpec((1, LOCAL_ROWS, D_MODEL), lambda i: (i, 0, 0))],
        out_specs=out_specs,
        out_shape=out_shape,
        compiler_params=pltpu.CompilerParams(dimension_semantics=("arbitrary",), vmem_limit_bytes=VMEM_LIMIT),
        name="moe_combine",
    )(h1, route, pos, final_g, yl)


def _expert_tables(nch, n_steps, trash_row0):
    nt = nch.shape[0]
    cps = CHUNKS_PER_STEP
    seg_row0 = (jnp.cumsum(nch, axis=1) - nch) * SEG_PAD
    first = jnp.cumsum(nch, axis=0) - nch
    tot = jnp.sum(nch, axis=0)
    steps = (tot + cps - 1) // cps
    step_end = jnp.cumsum(steps)
    n_valid = step_end[-1:]
    s_ids = jnp.arange(n_steps, dtype=jnp.int32)
    step_expert = jnp.minimum(jnp.sum((step_end[None, :] <= s_ids[:, None]).astype(jnp.int32), axis=1),
                              N_EXPERTS - 1)
    sel = (step_expert[:, None] == jnp.arange(N_EXPERTS, dtype=jnp.int32)[None, :]).astype(jnp.int32)
    step0 = sel @ (step_end - steps)
    tot_s = sel @ tot
    first_s = sel @ first.T
    row0_s = sel @ seg_row0.T
    k = (s_ids - step0)[:, None] * cps + jnp.arange(cps, dtype=jnp.int32)[None, :]
    ok = (k < tot_s[:, None]) & (s_ids < n_valid[0])[:, None]
    tile = jnp.sum((first_s[:, None, :] <= k[:, :, None]).astype(jnp.int32), axis=2) - 1
    tsel = (tile[:, :, None] == jnp.arange(nt, dtype=jnp.int32)[None, None, :]).astype(jnp.int32)
    first_k = jnp.sum(tsel * first_s[:, None, :], axis=2)
    row0_k = jnp.sum(tsel * row0_s[:, None, :], axis=2)
    row = tile * LOCAL_ROWS + row0_k + (k - first_k) * SEG_PAD
    live = (s_ids < n_valid[0])[:, None]
    src = jnp.where(live, jnp.where(ok, row, row[:, 0:1]), 0)
    spare = trash_row0 + (s_ids % EXP_SLOTS)[:, None] * EXP_TILE + jnp.arange(cps, dtype=jnp.int32)[None, :] * SEG_PAD
    dst = jnp.where(ok, row, spare)
    return (step_expert, n_valid.astype(jnp.int32), src.reshape(-1).astype(jnp.int32),
            dst.reshape(-1).astype(jnp.int32))


def _moe(h1, route, n2g, layer, w_gate, w_up, w_down, final_g, final_norm, n_first_rows):
    n = h1.shape[0]
    nt = n // SORT_TILE
    xl, pos, nch = _sort_call(h1, route, n2g)
    nch = nch[:, 0, :N_EXPERTS].astype(jnp.int32)
    max_chunks = (2 * n) // SEG_PAD + nt * N_EXPERTS
    n_steps = max_chunks // CHUNKS_PER_STEP + N_EXPERTS + 1
    trash_row0 = nt * LOCAL_ROWS
    step_expert, n_valid, chunk_src, chunk_dst = _expert_tables(nch, n_steps + EXP_SLOTS - 1, trash_row0)
    yl = _expert_call(xl.reshape((nt + 1) * LOCAL_ROWS, D_MODEL), trash_row0, step_expert, n_valid,
                      chunk_src, chunk_dst, layer, w_gate, w_up, w_down)
    return _combine_call(h1, route, pos, yl.reshape(nt + 1, LOCAL_ROWS, D_MODEL), final_g, final_norm, n_first_rows)


def _block_diag(pw):
    out = jnp.zeros((BRANCH_W, BRANCH_W), pw.dtype)
    for g in range(len(POOL_WINDOWS)):
        out = out.at[g * POOL_GW:(g + 1) * POOL_GW, g * POOL_GW:(g + 1) * POOL_GW].set(pw[g])
    return out


def kernel(x_prompt, x_sample, state_pool, state_conv, state_sconv, norm1_g, w_in, pool_w, pool_scale, conf_dw, conf_dw_b, conf_ln_g, conf_ln_b, sconv_w, sgu_ln_g, sgu_ln_b, sgu_ws, sgu_b, w_branch, w_out, norm2_g, router_g, router_g_b, router_e, router_e_b, w_gate, w_up, w_down, final_g):
    depth = w_in.shape[0]
    bp, seq, _ = x_prompt.shape
    bs, dseq, _ = x_sample.shape
    n_p, n_s = bp * seq, bs * dseq
    n = n_p + n_s
    past_len = 16384
    assert seq % MIX_TILE == 0 and n_s % MIX_TILE_S == 0 and n_p % MIX_TILE_S == 0 and dseq == SUBLANES
    assert MIX_TILE % CHUNK == 0 and MIX_TILE_S % CHUNK == 0

    cfg_p = MixCfg(layer=0, nb=1, t=MIX_TILE, has_state=False, start_pos=0, row_off=0, v_rows=CHUNK)
    cfg_s = MixCfg(layer=0, nb=MIX_TILE_S // dseq, t=dseq, has_state=True, start_pos=past_len,
                   row_off=n_p // MIX_TILE_S, v_rows=dseq)
    big_weights = (w_in, w_branch, w_out)

    row = lambda a: a.reshape(1, -1)
    rep8 = lambda a: jnp.broadcast_to(a[..., None, :], a.shape[:-1] + (SUBLANES, a.shape[-1]))
    tril = jnp.tril(jnp.ones((CHUNK, CHUNK), F32))
    eye_blk = jnp.kron(jnp.eye(CHUNK // dseq, dtype=F32), jnp.ones((dseq, dseq), F32))
    final_row = row(final_g)

    h_p, h_s = x_prompt.reshape(n_p, D_MODEL), x_sample.reshape(n_s, D_MODEL)
    off_p, off_s = 0, 0
    states_out = []
    for l in range(depth):
        ws_p = sgu_ws[l] * tril[None]
        ws_small = jnp.tile(sgu_ws[l][:, :dseq, :dseq], (1, CHUNK // dseq, CHUNK // dseq))
        ws_s = ws_small * (tril * eye_blk)[None]
        cat = lambda w: jnp.concatenate([w[h] for h in range(SGU_HEADS)], axis=1).astype(BF16)
        bias_p = jnp.repeat(sgu_b[l].T, SGU_HW, axis=1)
        bias_s = jnp.tile(jnp.repeat(sgu_b[l][:, :dseq].T, SGU_HW, axis=1), (CHUNK // dseq, 1))
        lane_pad = LANES - N_EXPERTS - N_GROUPS
        w_rt32 = jnp.pad(jnp.concatenate([router_e[l], router_g[l]], axis=1), ((0, 0), (0, lane_pad)))
        w_rt_hi = w_rt32.astype(BF16)
        w_rt = jnp.stack([w_rt_hi, (w_rt32 - w_rt_hi.astype(F32)).astype(BF16)])
        b_rt = jnp.pad(jnp.concatenate([router_e_b[l], router_g_b[l]]), (0, lane_pad)).reshape(1, LANES)

        def weights(wcat, sbias):
            return [row(norm1_g[l]), _block_diag(pool_w[l]).astype(BF16), row(pool_scale[l]),
                    rep8(conf_dw[l]), rep8(conf_dw_b[l]), row(conf_ln_g[l]), row(conf_ln_b[l]), rep8(sconv_w[l]),
                    row(sgu_ln_g[l]), row(sgu_ln_b[l]), wcat, sbias, row(norm2_g[l]), w_rt, b_rt]

        h1, route, pool_p, conv_p, sconv_p, v_p = _mixer_call(
            cfg_p._replace(layer=l), bp, n, h_p, off_p, None, weights(cat(ws_p), bias_p), big_weights, None)
        h1, route, pool_s, conv_s, sconv_s, v_s = _mixer_call(
            cfg_s._replace(layer=l), bs, n, h_s, off_s, (state_pool[l], state_conv[l], state_sconv[l]),
            weights(cat(ws_s), bias_s), big_weights, (h1, route))
        states_out.append((pool_p, conv_p, sconv_p, v_p, pool_s, conv_s, sconv_s, v_s))

        outs = _moe(h1, route, row(norm2_g[l]), l, w_gate, w_up, w_down, final_row, l == depth - 1, n_p)
        h_p = h_s = outs[0]
        off_p, off_s = 0, n_p // MIX_TILE_S

    y_prompt = outs[0].reshape(bp, seq, D_MODEL)
    y_sample = outs[1].reshape(bs, dseq, D_MODEL)
    st = [jnp.stack([s[k] for s in states_out]) for k in range(8)]
    return (y_prompt, y_sample, st[0], st[1], st[2], st[3], st[4], st[5], st[6], st[7])
```

```python
import functools
from typing import NamedTuple

import jax
import jax.numpy as jnp
from jax import lax
from jax.experimental import pallas as pl
from jax.experimental.pallas import tpu as pltpu

F32 = jnp.float32
BF16 = jnp.bfloat16

D_MODEL = 1024
BRANCH_W = 256
N_BRANCH = 4
POOL_WINDOWS = (2, 4, 8, 16)
POOL_GW = 64
POOL_HIST = 15
CONF_WIDTH = 31
CONF_HIST = 30
SCONV_WIDTH = 3
SCONV_HIST = 2
CHUNK = 128
SGU_HEADS = 4
SGU_HW = 64
N_GROUPS = 4
EXPERTS_PER_GROUP = 8
N_EXPERTS = 32
D_EXPERT = 256
RMS_EPS = 1e-6
LN_EPS = 1e-5
IN_COLS = 6144
GATE_COL0 = 2048

COL_TILE = 256
LANES = 128
SUBLANES = 8
POOL_PAD = 32
CONF_PAD = 32
SCONV_PAD = 8
ROW_BLOCK = 32
ROUTE_GROUP_LANE0 = 32
NEG_BIG = -3.0e38

MIX_TILE = 512
MIX_TILE_S = 256
SORT_TILE = 512
SEG_PAD = 16
PERM_BLOCK = 256
N_PERM_BLOCKS = -(-(2 * SORT_TILE + N_EXPERTS * (SEG_PAD - 1)) // PERM_BLOCK)
LOCAL_ROWS = N_PERM_BLOCKS * PERM_BLOCK
EXP_TILE = 512
CHUNKS_PER_STEP = EXP_TILE // SEG_PAD
EXP_SLOTS = 3
assert EXP_SLOTS * EXP_TILE <= LOCAL_ROWS
VMEM_LIMIT = 56 * 1024 * 1024


class MixCfg(NamedTuple):
    layer: int
    nb: int
    t: int
    has_state: bool
    start_pos: int
    row_off: int
    v_rows: int


def _rms(x, g):
    return x * lax.rsqrt(jnp.mean(x * x, axis=-1, keepdims=True) + RMS_EPS) * g


def _ln(x, g, b):
    mu = jnp.mean(x, axis=-1, keepdims=True)
    xc = x - mu
    return xc * lax.rsqrt(jnp.mean(xc * xc, axis=-1, keepdims=True) + LN_EPS) * g + b


def _sigmoid(x):
    return 0.5 * jnp.tanh(0.5 * x) + 0.5


def _zero_like_bits(x):
    bits = lax.bitcast_convert_type(x, jnp.uint32)
    return ((bits >> 16) >> 16).astype(jnp.int32).astype(F32)


def _gelu_tanh(x):
    return 0.5 * x * (1.0 + jnp.tanh(0.7978845608028654 * (x + 0.044715 * (x * x * x))))


def _row_blocks(nb, t):
    if t >= ROW_BLOCK:
        return [(slice(b, b + 1), t0, ROW_BLOCK) for b in range(nb) for t0 in range(0, t, ROW_BLOCK)]
    bb = ROW_BLOCK // t
    return [(slice(b0, b0 + bb), 0, t) for b0 in range(0, nb, bb)]


def _load_big_weights(layer, w_in_hbm, w_br_hbm, w_out_hbm, w_in, w_br, w_out, stage, stage_b, sem):
    n_in, n_out = IN_COLS // COL_TILE, D_MODEL // COL_TILE
    jobs = [("in", j) for j in range(n_in)] + [("out", j) for j in range(n_out)] + [("br", i) for i in range(N_BRANCH)]

    def copy(job, slot):
        kind, j = job
        if kind == "br":
            return pltpu.make_async_copy(w_br_hbm.at[layer, j], stage_b.at[slot], sem.at[slot])
        src = w_in_hbm if kind == "in" else w_out_hbm
        return pltpu.make_async_copy(src.at[layer, :, pl.ds(j * COL_TILE, COL_TILE)], stage.at[slot], sem.at[slot])

    copy(jobs[0], 0).start()
    for n, job in enumerate(jobs):
        slot = n % 2
        if n + 1 < len(jobs):
            copy(jobs[n + 1], 1 - slot).start()
        copy(job, slot).wait()
        kind, j = job
        if kind == "in":
            tile = stage[slot]
            w_in[j] = (tile * 0.5 if j * COL_TILE >= GATE_COL0 else tile).astype(BF16)
        elif kind == "out":
            w_out[j] = stage[slot].astype(BF16)
        else:
            for c in range(n_out):
                w_br[j * n_out + c] = (stage_b[slot, :, c * COL_TILE:(c + 1) * COL_TILE] * 0.5).astype(BF16)


def _mixer_kernel(cfg, *refs):
    nb, t, tm = cfg.nb, cfg.t, cfg.nb * cfg.t
    refs = list(refs)
    x_ref = refs.pop(0)
    if cfg.has_state:
        pool_st, conv_st, sconv_st = refs.pop(0), refs.pop(0), refs.pop(0)
    (n1g, pool_w, pool_sc, cdw, cdb, clg, clb, scw, slg, slb, wcat, sbias, n2g, w_rt, b_rt,
     w_in_hbm, w_br_hbm, w_out_hbm) = refs[:18]
    refs = refs[18:]
    if cfg.has_state:
        refs = refs[2:]
    h1_o, route_o, pool_o, conv_o, sconv_o, v_o = refs[:6]
    (pool_ext, sum_a, sum_b, conv_ext, conv_sh, sc_ext, buf_a, buf_b, buf_c, gate_buf, xb_buf,
     w_in, w_br, w_out, stage, stage_b, w_sem) = refs[6:]

    first_step = pl.program_id(0) == 0
    if not cfg.has_state:
        first_step = first_step & (pl.program_id(1) == 0)
    pl.when(first_step)(functools.partial(
        _load_big_weights, cfg.layer, w_in_hbm, w_br_hbm, w_out_hbm, w_in, w_br, w_out, stage, stage_b, w_sem))

    if cfg.has_state:
        seq_pos0 = cfg.start_pos
        pool_ext[:, 0:POOL_PAD - SUBLANES, :] = jnp.zeros((nb, POOL_PAD - SUBLANES, BRANCH_W), F32)
        pool_ext[:, POOL_PAD - POOL_HIST:POOL_PAD, :] = pool_st[...]
        conv_ext[:, CONF_PAD - CONF_HIST:CONF_PAD, :] = conv_st[...]
        sc_ext[:, SCONV_PAD - SCONV_HIST:SCONV_PAD, :] = sconv_st[...]
    else:
        c = pl.program_id(1)
        seq_pos0 = cfg.start_pos + c * t

        @pl.when(c == 0)
        def _():
            pool_ext[:, 0:POOL_PAD, :] = jnp.zeros((nb, POOL_PAD, BRANCH_W), F32)
            conv_ext[:, 0:CONF_PAD, :] = jnp.zeros((nb, CONF_PAD, BRANCH_W), F32)
            sc_ext[:, 0:SCONV_PAD, :] = jnp.zeros((nb, SCONV_PAD, BRANCH_W), F32)

    x = x_ref[...]
    xb_buf[...] = _rms(x, n1g[...]).astype(BF16)

    def proj(lo, hi):
        assert lo % COL_TILE == 0 and hi == lo + COL_TILE
        return jnp.dot(xb_buf[...], w_in[lo // COL_TILE], preferred_element_type=F32)

    a_pool = proj(0, 256)
    pool_ext[:, POOL_PAD:, :] = a_pool.reshape(nb, t, BRANCH_W)
    glu = proj(256, 512) * _sigmoid(proj(512, 768))
    conv_ext[:, CONF_PAD:, :] = glu.reshape(nb, t, BRANCH_W)
    z = proj(1024, 1280) * proj(1280, 1536)
    sc_ext[:, SCONV_PAD:, :] = z.reshape(nb, t, BRANCH_W)

    pl_len = POOL_PAD + t
    sum_b[:, 8:pl_len, :] = pool_ext[:, 8:pl_len, :] + pool_ext[:, 7:pl_len - 1, :]
    sum_a[:, 16:pl_len, :] = sum_b[:, 16:pl_len, :] + sum_b[:, 14:pl_len - 2, :]
    sum_b[:, 24:pl_len, :] = sum_a[:, 24:pl_len, :] + sum_a[:, 20:pl_len - 4, :]
    for sh in range(SUBLANES):
        n_rows = t + SUBLANES * ((CONF_WIDTH - 1 - sh) // SUBLANES)
        first = CONF_PAD - CONF_HIST + sh
        conv_sh[sh, :, 0:n_rows, :] = conv_ext[:, first:first + n_rows, :]

    def window_block(bs, t0, tb, order_zero):
        bb = bs.stop - bs.start
        shp = (bb, tb, BRANCH_W)
        lane = lax.broadcasted_iota(jnp.int32, shp, 2)
        pos = seq_pos0 + t0 + lax.broadcasted_iota(jnp.int32, shp, 1)
        r0 = POOL_PAD + t0
        cur = pool_ext[bs, r0:r0 + tb, :]
        s2 = cur + pool_ext[bs, r0 - 1:r0 - 1 + tb, :]
        s4 = sum_a[bs, r0:r0 + tb, :]
        s8 = sum_b[bs, r0:r0 + tb, :]
        s16 = s8 + sum_b[bs, r0 - SUBLANES:r0 - SUBLANES + tb, :]
        win_sum = jnp.where(lane < 64, s2, jnp.where(lane < 128, s4, jnp.where(lane < 192, s8, s16)))
        win = jnp.where(lane < 64, 2, jnp.where(lane < 128, 4, jnp.where(lane < 192, 8, 16)))
        cnt = jnp.minimum(pos + 1, win).astype(F32)
        buf_a[bs, t0:t0 + tb, :] = win_sum / cnt - cur
        tiles = (ROW_BLOCK // SUBLANES, SUBLANES, BRANCH_W)
        acc = jnp.broadcast_to((cdb[...] + order_zero)[None], tiles)
        for k in range(CONF_WIDTH):
            q, sh = divmod(k, SUBLANES)
            r = t0 + SUBLANES * q
            acc = acc + conv_sh[sh, bs, r:r + tb, :].reshape(tiles) * cdw[k][None]
        buf_b[bs, t0:t0 + tb, :] = acc.reshape(shp)
        base = SCONV_PAD - SCONV_HIST + t0
        acc = sc_ext[bs, base:base + tb, :].reshape(tiles) * scw[0][None]
        for k in range(1, SCONV_WIDTH):
            acc = acc + sc_ext[bs, base + k:base + k + tb, :].reshape(tiles) * scw[k][None]
        buf_c[bs, t0:t0 + tb, :] = acc.reshape(shp)

    def gate_chunk(k):
        g0 = k * COL_TILE
        g = jnp.tanh(proj(GATE_COL0 + g0, GATE_COL0 + g0 + COL_TILE)) + 1.0
        gate_buf[:, g0:g0 + COL_TILE] = g
        return _zero_like_bits(g[0:SUBLANES, :])

    br = {}

    def item_windows(blks):
        def run(zero):
            for bs, t0, tb in blks:
                window_block(bs, t0, tb, zero)
        return run

    def item_state(zero):
        del zero
        pool_o[...] = pool_ext[:, POOL_PAD + t - POOL_HIST:POOL_PAD + t, :]
        conv_o[...] = conv_ext[:, CONF_PAD + t - CONF_HIST:CONF_PAD + t, :]
        sconv_o[...] = sc_ext[:, SCONV_PAD + t - SCONV_HIST:SCONV_PAD + t, :]
        if not cfg.has_state:
            pool_ext[:, 0:POOL_PAD, :] = pool_ext[:, t:t + POOL_PAD, :]
            conv_ext[:, 0:CONF_PAD, :] = conv_ext[:, t:t + CONF_PAD, :]
            sc_ext[:, 0:SCONV_PAD, :] = sc_ext[:, t:t + SCONV_PAD, :]

    def item_a(zero):
        pooled = buf_a[...].reshape(tm, BRANCH_W).astype(BF16)
        scale = pool_sc[...] + zero[0:1, :]
        br["a"] = (jnp.dot(pooled, pool_w[...], preferred_element_type=F32) * scale).astype(BF16)

    def item_b(zero):
        cb = _ln(buf_b[...].reshape(tm, BRANCH_W), clg[...] + zero[0:1, :], clb[...])
        br["b"] = (cb * _sigmoid(cb)).astype(BF16)

    def item_c(zero):
        br["c"] = (proj(768, 1024) * (buf_c[...].reshape(tm, BRANCH_W) + zero[0:1, :])).astype(BF16)

    def item_u(zero):
        br["u"] = _gelu_tanh(proj(1536, 1792) + zero[0:1, :])

    def item_v(zero):
        v = _ln(_gelu_tanh(proj(1792, 2048)), slg[...] + zero[0:1, :], slb[...])
        if cfg.has_state:
            v_o[...] = v.reshape(nb, t, BRANCH_W)
        else:
            v_o[...] = v[tm - cfg.v_rows:, :].reshape(1, cfg.v_rows, BRANCH_W)
        br["v"] = v

    def item_d(zero):
        head = lax.broadcasted_iota(jnp.int32, (CHUNK, BRANCH_W), 1) // SGU_HW
        bias = sbias[...] + jnp.concatenate([zero] * (CHUNK // SUBLANES), axis=0)
        mixed = []
        for j in range(tm // CHUNK):
            vj = br["v"][j * CHUNK:(j + 1) * CHUNK, :]
            stacked = jnp.concatenate([jnp.where(head == h, vj, 0.0) for h in range(SGU_HEADS)], axis=0)
            mixed.append(jnp.dot(wcat[...], stacked.astype(BF16), preferred_element_type=F32) + bias)
        br["d"] = (br["u"] * jnp.concatenate(mixed, axis=0)).astype(BF16)

    blocks = _row_blocks(nb, t)
    n_gate = N_BRANCH * D_MODEL // COL_TILE
    per_item = -(-len(blocks) // (n_gate // 2))
    items = [item_windows(blocks[i:i + per_item]) for i in range(0, len(blocks), per_item)]
    items += [item_state, item_c, item_b, item_u, item_v, item_a, item_d]
    assert len(items) <= n_gate
    zero = jnp.zeros((SUBLANES, BRANCH_W), F32)
    for k in range(n_gate):
        next_zero = gate_chunk(k)
        if k < len(items):
            items[k](zero)
        zero = next_zero

    n_ct = D_MODEL // COL_TILE
    brs = [br["a"], br["b"], br["c"], br["d"]]
    for c in range(n_ct):
        part = None
        for i in range(N_BRANCH):
            g0 = i * D_MODEL + c * COL_TILE
            term = gate_buf[:, g0:g0 + COL_TILE] * jnp.dot(
                brs[i], w_br[i * n_ct + c], preferred_element_type=F32)
            part = term if part is None else part + term
        xb_buf[:, c * COL_TILE:(c + 1) * COL_TILE] = part.astype(BF16)
    for c in range(n_ct):
        cs = slice(c * COL_TILE, (c + 1) * COL_TILE)
        h1_o[:, cs] = x_ref[:, cs] + jnp.dot(xb_buf[...], w_out[c], preferred_element_type=F32)

    xn2 = _rms(h1_o[...], n2g[...])
    x_hi = xn2.astype(BF16)
    x_lo = (xn2 - x_hi.astype(F32)).astype(BF16)
    logits = (jnp.dot(x_hi, w_rt[0], preferred_element_type=F32)
              + (jnp.dot(x_lo, w_rt[0], preferred_element_type=F32)
                 + jnp.dot(x_hi, w_rt[1], preferred_element_type=F32))) + b_rt[...]
    lane = lax.broadcasted_iota(jnp.int32, (tm, LANES), 1)
    is_g = (lane >= ROUTE_GROUP_LANE0) & (lane < ROUTE_GROUP_LANE0 + N_GROUPS)
    glog = jnp.where(is_g, logits, NEG_BIG)
    gmax = jnp.max(glog, axis=-1, keepdims=True)
    lane_f = lane.astype(F32)
    no_lane = float(4 * LANES)
    gsel = jnp.min(jnp.where(glog == gmax, lane_f, no_lane), axis=-1, keepdims=True) - ROUTE_GROUP_LANE0
    pg = 1.0 / jnp.sum(jnp.where(is_g, jnp.exp(glog - gmax), 0.0), axis=-1, keepdims=True)
    in_grp = (lane < N_EXPERTS) & ((lane // EXPERTS_PER_GROUP) == gsel.astype(jnp.int32))
    el = jnp.where(in_grp, logits, NEG_BIG)
    m1 = jnp.max(el, axis=-1, keepdims=True)
    i1 = jnp.min(jnp.where(in_grp & (el == m1), lane_f, no_lane), axis=-1, keepdims=True)
    rest = in_grp & (lane_f != i1)
    el2 = jnp.where(rest, logits, NEG_BIG)
    m2 = jnp.max(el2, axis=-1, keepdims=True)
    i2 = jnp.min(jnp.where(rest & (el2 == m2), lane_f, no_lane), axis=-1, keepdims=True)
    e21 = jnp.exp(m2 - m1)
    w1 = pg / (1.0 + e21)
    w2 = pg * e21 / (1.0 + e21)
    route_o[...] = jnp.where(lane == 0, i1, jnp.where(lane == 1, i2,
                             jnp.where(lane == 2, w1, jnp.where(lane == 3, w2, 0.0))))


def _const_spec(shape):
    nd = len(shape)
    return pl.BlockSpec(shape, lambda *_: (0,) * nd, pipeline_mode=pl.Buffered(1))


def _mixer_call(cfg, n_seq, n_rows_total, x2d, x_tile_off, states, weights, big_weights, aliased):
    nb, t = cfg.nb, cfg.t
    tm = nb * t
    if cfg.has_state:
        grid = (n_seq // nb,)
        tok = lambda i: (x_tile_off + i, 0)
        out_tok = lambda i: (cfg.row_off + i, 0)
        seq3 = lambda i: (i, 0, 0)
    else:
        grid = (n_seq, 2048 // t)
        nt = grid[1]
        tok = lambda b, c: (x_tile_off + b * nt + c, 0)
        out_tok = lambda b, c: (cfg.row_off + b * nt + c, 0)
        seq3 = lambda b, c: (b, 0, 0)

    in_specs = [pl.BlockSpec((tm, D_MODEL), tok)]
    args = [x2d]
    if cfg.has_state:
        for s in states:
            in_specs.append(pl.BlockSpec((nb,) + s.shape[1:], seq3))
            args.append(s)
    for w in weights:
        in_specs.append(_const_spec(w.shape))
        args.append(w)
    for w in big_weights:
        in_specs.append(pl.BlockSpec(memory_space=pl.ANY))
        args.append(w)
    io_alias = {}
    if cfg.has_state:
        for k, a in enumerate(aliased):
            in_specs.append(pl.BlockSpec(memory_space=pl.ANY))
            io_alias[len(args)] = k
            args.append(a)

    out_shape = [
        jax.ShapeDtypeStruct((n_rows_total, D_MODEL), F32),
        jax.ShapeDtypeStruct((n_rows_total, LANES), F32),
        jax.ShapeDtypeStruct((n_seq, POOL_HIST, BRANCH_W), F32),
        jax.ShapeDtypeStruct((n_seq, CONF_HIST, BRANCH_W), F32),
        jax.ShapeDtypeStruct((n_seq, SCONV_HIST, BRANCH_W), F32),
        jax.ShapeDtypeStruct((n_seq, cfg.v_rows, BRANCH_W), F32),
    ]
    out_specs = [
        pl.BlockSpec((tm, D_MODEL), out_tok),
        pl.BlockSpec((tm, LANES), out_tok),
        pl.BlockSpec((nb, POOL_HIST, BRANCH_W), seq3),
        pl.BlockSpec((nb, CONF_HIST, BRANCH_W), seq3),
        pl.BlockSpec((nb, SCONV_HIST, BRANCH_W), seq3),
        pl.BlockSpec((nb, cfg.v_rows, BRANCH_W), seq3),
    ]
    scratch = [
        pltpu.VMEM((nb, POOL_PAD + t, BRANCH_W), F32),
        pltpu.VMEM((nb, POOL_PAD + t, BRANCH_W), F32),
        pltpu.VMEM((nb, POOL_PAD + t, BRANCH_W), F32),
        pltpu.VMEM((nb, CONF_PAD + t, BRANCH_W), F32),
        pltpu.VMEM((SUBLANES, nb, t + CONF_PAD - SUBLANES, BRANCH_W), F32),
        pltpu.VMEM((nb, SCONV_PAD + t, BRANCH_W), F32),
        pltpu.VMEM((nb, t, BRANCH_W), F32),
        pltpu.VMEM((nb, t, BRANCH_W), F32),
        pltpu.VMEM((nb, t, BRANCH_W), F32),
        pltpu.VMEM((tm, N_BRANCH * D_MODEL), F32),
        pltpu.VMEM((tm, D_MODEL), BF16),
        pltpu.VMEM((IN_COLS // COL_TILE, D_MODEL, COL_TILE), BF16),
        pltpu.VMEM((N_BRANCH * D_MODEL // COL_TILE, BRANCH_W, COL_TILE), BF16),
        pltpu.VMEM((D_MODEL // COL_TILE, D_MODEL, COL_TILE), BF16),
        pltpu.VMEM((2, D_MODEL, COL_TILE), F32),
        pltpu.VMEM((2, BRANCH_W, D_MODEL), F32),
        pltpu.SemaphoreType.DMA((2,)),
    ]
    return pl.pallas_call(
        functools.partial(_mixer_kernel, cfg),
        grid=grid,
        in_specs=in_specs,
        out_specs=out_specs,
        out_shape=out_shape,
        scratch_shapes=scratch,
        input_output_aliases=io_alias,
        compiler_params=pltpu.CompilerParams(
            dimension_semantics=("arbitrary",) * len(grid), vmem_limit_bytes=VMEM_LIMIT),
        name="mixer_sample" if cfg.has_state else "mixer_prompt",
    )(*args)


def _sort_kernel(h1_ref, route_ref, n2g, xl_o, pos_o, nch_o):
    t = SORT_TILE
    xn = _rms(h1_ref[...], n2g[...]).astype(BF16)
    r = route_ref[...]
    lane_i = lax.broadcasted_iota(jnp.int32, (t, LANES), 1)
    lane = lane_i.astype(F32)
    hit1 = lane == r[:, 0:1]
    hit2 = lane == r[:, 1:2]
    onehot = jnp.where(hit1 | hit2, 1.0, 0.0)
    row = lax.broadcasted_iota(jnp.int32, (t, t), 0)
    col = lax.broadcasted_iota(jnp.int32, (t, t), 1)
    strict_lower = jnp.where(col < row, 1.0, 0.0).astype(BF16)
    before = jnp.dot(strict_lower, onehot.astype(BF16), preferred_element_type=F32)
    cnt = jnp.sum(onehot, axis=0, keepdims=True)
    nch = jnp.floor((cnt + (SEG_PAD - 1)) * (1.0 / SEG_PAD))
    er = lax.broadcasted_iota(jnp.int32, (LANES, LANES), 0)
    ec = lax.broadcasted_iota(jnp.int32, (LANES, LANES), 1)
    strict_upper = jnp.where(er < ec, 1.0, 0.0).astype(BF16)
    nch8 = jnp.broadcast_to(nch, (SUBLANES, LANES)).astype(BF16)
    seg0 = jnp.dot(nch8, strict_upper, preferred_element_type=F32)[0:1, :] * SEG_PAD
    where = seg0 + before
    pos0 = jnp.sum(jnp.where(hit1, where, 0.0), axis=-1, keepdims=True)
    pos1 = jnp.sum(jnp.where(hit2, where, 0.0), axis=-1, keepdims=True)
    pos = jnp.where(lane_i == 0, pos0, jnp.where(lane_i == 1, pos1, -1.0))
    pos_o[...] = pos
    nch_o[...] = jnp.broadcast_to(nch, (1, SUBLANES, LANES))
    pos_t = jnp.transpose(pos)
    p0 = pos_t[0:1, :]
    p1 = pos_t[1:2, :]
    used_rows = jnp.sum(nch) * SEG_PAD

    def perm_block(blk):
        dst = (lax.broadcasted_iota(jnp.int32, (PERM_BLOCK, t), 0) + blk * PERM_BLOCK).astype(F32)
        perm = jnp.where((dst == p0) | (dst == p1), 1.0, 0.0).astype(BF16)
        xl_o[0, blk * PERM_BLOCK:(blk + 1) * PERM_BLOCK, :] = jnp.dot(
            perm, xn, preferred_element_type=F32).astype(BF16)

    for blk in range(N_PERM_BLOCKS - 1):
        perm_block(blk)
    pl.when(used_rows > (N_PERM_BLOCKS - 1) * PERM_BLOCK)(functools.partial(perm_block, N_PERM_BLOCKS - 1))


def _sort_call(h1, route, n2g):
    n = h1.shape[0]
    nt = n // SORT_TILE
    return pl.pallas_call(
        _sort_kernel,
        grid=(nt,),
        in_specs=[pl.BlockSpec((SORT_TILE, D_MODEL), lambda i: (i, 0)),
                  pl.BlockSpec((SORT_TILE, LANES), lambda i: (i, 0)),
                  pl.BlockSpec((1, D_MODEL), lambda i: (0, 0))],
        out_specs=[pl.BlockSpec((1, LOCAL_ROWS, D_MODEL), lambda i: (i, 0, 0)),
                   pl.BlockSpec((SORT_TILE, LANES), lambda i: (i, 0)),
                   pl.BlockSpec((1, SUBLANES, LANES), lambda i: (i, 0, 0))],
        out_shape=[jax.ShapeDtypeStruct((nt + 1, LOCAL_ROWS, D_MODEL), BF16),
                   jax.ShapeDtypeStruct((n, LANES), F32),
                   jax.ShapeDtypeStruct((nt, SUBLANES, LANES), F32)],
        compiler_params=pltpu.CompilerParams(dimension_semantics=("arbitrary",), vmem_limit_bytes=VMEM_LIMIT),
        name="moe_sort",
    )(h1, route, n2g)


def _chunk_copy(src, src_row, dst, dst_row, sem):
    return pltpu.make_async_copy(src.at[pl.ds(src_row, SEG_PAD), :], dst.at[pl.ds(dst_row, SEG_PAD), :], sem)


def _expert_kernel(trash_row0, se_ref, nv_ref, src_ref, dst_ref, xl_in, wg, wu, wd, xl_io, *scratch):
    del se_ref, xl_in
    ns = EXP_SLOTS
    xbuf, ybuf, (sem_in, sem_out) = scratch[:ns], scratch[ns:2 * ns], scratch[2 * ns:]
    s = pl.program_id(0)
    nv = nv_ref[0]

    def gather(step, sl):
        return [_chunk_copy(xl_io, pl.multiple_of(src_ref[step * CHUNKS_PER_STEP + j], SEG_PAD),
                            xbuf[sl], j * SEG_PAD, sem_in.at[sl]) for j in range(CHUNKS_PER_STEP)]

    def write_back(step, sl):
        return [_chunk_copy(ybuf[sl], j * SEG_PAD, xl_io,
                            pl.multiple_of(dst_ref[step * CHUNKS_PER_STEP + j], SEG_PAD), sem_out.at[sl])
                for j in range(CHUNKS_PER_STEP)]

    def spare_write(sl):
        return [_chunk_copy(ybuf[sl], j * SEG_PAD, xl_io, trash_row0 + sl * EXP_TILE + j * SEG_PAD,
                            sem_out.at[sl]) for j in range(CHUNKS_PER_STEP)]

    @pl.when(s == 0)
    def _():
        for ahead in range(ns - 1):
            for c in gather(ahead, ahead):
                c.start()
        for sl in range(ns):
            ybuf[sl][...] = jnp.zeros_like(ybuf[sl])
            for c in spare_write(sl):
                c.start()

    def live_step(sl):
        for c in gather(s + ns - 1, (sl + ns - 1) % ns):
            c.start()
        for c in gather(s, sl):
            c.wait()
        xb = xbuf[sl][...]
        a = jnp.dot(xb, wg[0, 0].astype(BF16), preferred_element_type=F32)
        b = jnp.dot(xb, wu[0, 0].astype(BF16), preferred_element_type=F32)
        hh = (a * _sigmoid(a) * b).astype(BF16)
        y = jnp.dot(hh, wd[0, 0].astype(BF16), preferred_element_type=F32).astype(BF16)
        for c in write_back(s, sl):
            c.wait()
        ybuf[sl][...] = y
        for c in write_back(s, sl):
            c.start()

    def drain_step(sl):
        for ahead in range(ns - 1):
            for c in gather(s + ahead, (sl + ahead) % ns):
                c.wait()
        for every in range(ns):
            for c in spare_write(every):
                c.wait()

    for sl in range(ns):
        pl.when((s < nv) & (s % ns == sl))(functools.partial(live_step, sl))
        pl.when((s == nv) & (s % ns == sl))(functools.partial(drain_step, sl))


def _expert_call(xl, trash_row0, step_expert, n_valid, chunk_src, chunk_dst, layer, w_gate, w_up, w_down):
    n_steps = step_expert.shape[0] - (EXP_SLOTS - 1)
    wmap = lambda s, se, *_: (layer, se[s], 0, 0)
    grid_spec = pltpu.PrefetchScalarGridSpec(
        num_scalar_prefetch=4,
        grid=(n_steps,),
        in_specs=[pl.BlockSpec(memory_space=pl.ANY),
                  pl.BlockSpec((1, 1, D_MODEL, D_EXPERT), wmap),
                  pl.BlockSpec((1, 1, D_MODEL, D_EXPERT), wmap),
                  pl.BlockSpec((1, 1, D_EXPERT, D_MODEL), wmap)],
        out_specs=pl.BlockSpec(memory_space=pl.ANY),
        scratch_shapes=[pltpu.VMEM((EXP_TILE, D_MODEL), BF16) for _ in range(2 * EXP_SLOTS)] + [
                        pltpu.SemaphoreType.DMA((EXP_SLOTS,)), pltpu.SemaphoreType.DMA((EXP_SLOTS,))],
    )
    return pl.pallas_call(
        functools.partial(_expert_kernel, trash_row0),
        grid_spec=grid_spec,
        out_shape=jax.ShapeDtypeStruct(xl.shape, BF16),
        input_output_aliases={4: 0},
        compiler_params=pltpu.CompilerParams(dimension_semantics=("arbitrary",), has_side_effects=True),
        name="moe_experts",
    )(step_expert, n_valid, chunk_src, chunk_dst, xl, w_gate, w_up, w_down)


def _combine_kernel(final_norm, n_first, h1_ref, route_ref, pos_ref, fg, yl_ref, *outs):
    t = SORT_TILE
    r = route_ref[...]
    p = pos_ref[...]
    w1, w2 = r[:, 2:3], r[:, 3:4]
    pos0, pos1 = p[:, 0:1], p[:, 1:2]
    *outs, acc_ref = outs

    def gathered(blk):
        src = (lax.broadcasted_iota(jnp.int32, (t, PERM_BLOCK), 1) + blk * PERM_BLOCK).astype(F32)
        pw = jnp.where(src == pos0, w1, 0.0) + jnp.where(src == pos1, w2, 0.0)
        return jnp.dot(pw.astype(BF16), yl_ref[0, blk * PERM_BLOCK:(blk + 1) * PERM_BLOCK, :],
                       preferred_element_type=F32)

    acc = h1_ref[...]
    for blk in range(N_PERM_BLOCKS - 1):
        acc = acc + gathered(blk)
    acc_ref[...] = acc

    @pl.when(jnp.max(p) >= (N_PERM_BLOCKS - 1) * PERM_BLOCK)
    def _():
        acc_ref[...] += gathered(N_PERM_BLOCKS - 1)

    if not final_norm:
        outs[0][...] = acc_ref[...]
        return
    y = _rms(acc_ref[...], fg[...])
    i = pl.program_id(0)

    @pl.when(i < n_first)
    def _():
        outs[0][...] = y

    @pl.when(i >= n_first)
    def _():
        outs[1][...] = y


def _combine_call(h1, route, pos, yl, final_g, final_norm, n_first_rows):
    n = h1.shape[0]
    nt = n // SORT_TILE
    n_first = n_first_rows // SORT_TILE
    tok = lambda i: (i, 0)
    if final_norm:
        out_shape = [jax.ShapeDtypeStruct((n_first_rows, D_MODEL), F32),
                     jax.ShapeDtypeStruct((n - n_first_rows, D_MODEL), F32)]
        out_specs = [pl.BlockSpec((SORT_TILE, D_MODEL), lambda i: (jnp.minimum(i, n_first - 1), 0)),
                     pl.BlockSpec((SORT_TILE, D_MODEL), lambda i: (jnp.maximum(i - n_first, 0), 0))]
    else:
        out_shape = [jax.ShapeDtypeStruct((n, D_MODEL), F32)]
        out_specs = [pl.BlockSpec((SORT_TILE, D_MODEL), tok)]
    return pl.pallas_call(
        functools.partial(_combine_kernel, final_norm, n_first),
        grid=(nt,),
        in_specs=[pl.BlockSpec((SORT_TILE, D_MODEL), tok),
                  pl.BlockSpec((SORT_TILE, LANES), tok),
                  pl.BlockSpec((SORT_TILE, LANES), tok),
                  pl.BlockSpec((1, D_MODEL), lambda i: (0, 0)),
                  pl.BlockSpec((1, LOCAL_ROWS, D_MODEL), lambda i: (i, 0, 0))],
        out_specs=out_specs,
        out_shape=out_shape,
        scratch_shapes=[pltpu.VMEM((SORT_TILE, D_MODEL), F32)],
        compiler_params=pltpu.CompilerParams(dimension_semantics=("arbitrary",), vmem_limit_bytes=VMEM_LIMIT),
        name="moe_combine",
    )(h1, route, pos, final_g, yl)


def _expert_tables(nch, n_steps, trash_row0):
    nt = nch.shape[0]
    cps = CHUNKS_PER_STEP
    seg_row0 = (jnp.cumsum(nch, axis=1) - nch) * SEG_PAD
    first = jnp.cumsum(nch, axis=0) - nch
    tot = jnp.sum(nch, axis=0)
    steps = (tot + cps - 1) // cps
    step_end = jnp.cumsum(steps)
    n_valid = step_end[-1:]
    s_ids = jnp.arange(n_steps, dtype=jnp.int32)
    step_expert = jnp.minimum(jnp.sum((step_end[None, :] <= s_ids[:, None]).astype(jnp.int32), axis=1),
                              N_EXPERTS - 1)
    sel = (step_expert[:, None] == jnp.arange(N_EXPERTS, dtype=jnp.int32)[None, :]).astype(jnp.int32)
    step0 = sel @ (step_end - steps)
    tot_s = sel @ tot
    first_s = sel @ first.T
    row0_s = sel @ seg_row0.T
    k = (s_ids - step0)[:, None] * cps + jnp.arange(cps, dtype=jnp.int32)[None, :]
    ok = (k < tot_s[:, None]) & (s_ids < n_valid[0])[:, None]
    tile = jnp.sum((first_s[:, None, :] <= k[:, :, None]).astype(jnp.int32), axis=2) - 1
    tsel = (tile[:, :, None] == jnp.arange(nt, dtype=jnp.int32)[None, None, :]).astype(jnp.int32)
    first_k = jnp.sum(tsel * first_s[:, None, :], axis=2)
    row0_k = jnp.sum(tsel * row0_s[:, None, :], axis=2)
    row = tile * LOCAL_ROWS + row0_k + (k - first_k) * SEG_PAD
    live = (s_ids < n_valid[0])[:, None]
    src = jnp.where(live, jnp.where(ok, row, row[:, 0:1]), 0)
    spare = trash_row0 + (s_ids % EXP_SLOTS)[:, None] * EXP_TILE + jnp.arange(cps, dtype=jnp.int32)[None, :] * SEG_PAD
    dst = jnp.where(ok, row, spare)
    return (step_expert, n_valid.astype(jnp.int32), src.reshape(-1).astype(jnp.int32),
            dst.reshape(-1).astype(jnp.int32))


def _moe(h1, route, n2g, layer, w_gate, w_up, w_down, final_g, final_norm, n_first_rows):
    n = h1.shape[0]
    nt = n // SORT_TILE
    xl, pos, nch = _sort_call(h1, route, n2g)
    nch = nch[:, 0, :N_EXPERTS].astype(jnp.int32)
    max_chunks = (2 * n) // SEG_PAD + nt * N_EXPERTS
    n_steps = max_chunks // CHUNKS_PER_STEP + N_EXPERTS + 1
    trash_row0 = nt * LOCAL_ROWS
    step_expert, n_valid, chunk_src, chunk_dst = _expert_tables(nch, n_steps + EXP_SLOTS - 1, trash_row0)
    yl = _expert_call(xl.reshape((nt + 1) * LOCAL_ROWS, D_MODEL), trash_row0, step_expert, n_valid,
                      chunk_src, chunk_dst, layer, w_gate, w_up, w_down)
    return _combine_call(h1, route, pos, yl.reshape(nt + 1, LOCAL_ROWS, D_MODEL), final_g, final_norm, n_first_rows)


def _block_diag(pw):
    out = jnp.zeros((BRANCH_W, BRANCH_W), pw.dtype)
    for g in range(len(POOL_WINDOWS)):
        out = out.at[g * POOL_GW:(g + 1) * POOL_GW, g * POOL_GW:(g + 1) * POOL_GW].set(pw[g])
    return out


def kernel(x_prompt, x_sample, state_pool, state_conv, state_sconv, norm1_g, w_in, pool_w, pool_scale, conf_dw, conf_dw_b, conf_ln_g, conf_ln_b, sconv_w, sgu_ln_g, sgu_ln_b, sgu_ws, sgu_b, w_branch, w_out, norm2_g, router_g, router_g_b, router_e, router_e_b, w_gate, w_up, w_down, final_g):
    depth = w_in.shape[0]
    bp, seq, _ = x_prompt.shape
    bs, dseq, _ = x_sample.shape
    n_p, n_s = bp * seq, bs * dseq
    n = n_p + n_s
    past_len = 16384
    assert seq % MIX_TILE == 0 and n_s % MIX_TILE_S == 0 and n_p % MIX_TILE_S == 0 and dseq == SUBLANES
    assert MIX_TILE % CHUNK == 0 and MIX_TILE_S % CHUNK == 0

    cfg_p = MixCfg(layer=0, nb=1, t=MIX_TILE, has_state=False, start_pos=0, row_off=0, v_rows=CHUNK)
    cfg_s = MixCfg(layer=0, nb=MIX_TILE_S // dseq, t=dseq, has_state=True, start_pos=past_len,
                   row_off=n_p // MIX_TILE_S, v_rows=dseq)
    big_weights = (w_in, w_branch, w_out)

    row = lambda a: a.reshape(1, -1)
    rep8 = lambda a: jnp.broadcast_to(a[..., None, :], a.shape[:-1] + (SUBLANES, a.shape[-1]))
    tril = jnp.tril(jnp.ones((CHUNK, CHUNK), F32))
    eye_blk = jnp.kron(jnp.eye(CHUNK // dseq, dtype=F32), jnp.ones((dseq, dseq), F32))
    final_row = row(final_g)

    h_p, h_s = x_prompt.reshape(n_p, D_MODEL), x_sample.reshape(n_s, D_MODEL)
    off_p, off_s = 0, 0
    states_out = []
    for l in range(depth):
        ws_p = sgu_ws[l] * tril[None]
        ws_small = jnp.tile(sgu_ws[l][:, :dseq, :dseq], (1, CHUNK // dseq, CHUNK // dseq))
        ws_s = ws_small * (tril * eye_blk)[None]
        cat = lambda w: jnp.concatenate([w[h] for h in range(SGU_HEADS)], axis=1).astype(BF16)
        bias_p = jnp.repeat(sgu_b[l].T, SGU_HW, axis=1)
        bias_s = jnp.tile(jnp.repeat(sgu_b[l][:, :dseq].T, SGU_HW, axis=1), (CHUNK // dseq, 1))
        lane_pad = LANES - N_EXPERTS - N_GROUPS
        w_rt32 = jnp.pad(jnp.concatenate([router_e[l], router_g[l]], axis=1), ((0, 0), (0, lane_pad)))
        w_rt_hi = w_rt32.astype(BF16)
        w_rt = jnp.stack([w_rt_hi, (w_rt32 - w_rt_hi.astype(F32)).astype(BF16)])
        b_rt = jnp.pad(jnp.concatenate([router_e_b[l], router_g_b[l]]), (0, lane_pad)).reshape(1, LANES)

        def weights(wcat, sbias):
            return [row(norm1_g[l]), _block_diag(pool_w[l]).astype(BF16), row(pool_scale[l]),
                    rep8(conf_dw[l]), rep8(conf_dw_b[l]), row(conf_ln_g[l]), row(conf_ln_b[l]), rep8(sconv_w[l]),
                    row(sgu_ln_g[l]), row(sgu_ln_b[l]), wcat, sbias, row(norm2_g[l]), w_rt, b_rt]

        h1, route, pool_p, conv_p, sconv_p, v_p = _mixer_call(
            cfg_p._replace(layer=l), bp, n, h_p, off_p, None, weights(cat(ws_p), bias_p), big_weights, None)
        h1, route, pool_s, conv_s, sconv_s, v_s = _mixer_call(
            cfg_s._replace(layer=l), bs, n, h_s, off_s, (state_pool[l], state_conv[l], state_sconv[l]),
            weights(cat(ws_s), bias_s), big_weights, (h1, route))
        states_out.append((pool_p, conv_p, sconv_p, v_p, pool_s, conv_s, sconv_s, v_s))

        outs = _moe(h1, route, row(norm2_g[l]), l, w_gate, w_up, w_down, final_row, l == depth - 1, n_p)
        h_p = h_s = outs[0]
        off_p, off_s = 0, n_p // MIX_TILE_S

    y_prompt = outs[0].reshape(bp, seq, D_MODEL)
    y_sample = outs[1].reshape(bs, dseq, D_MODEL)
    st = [jnp.stack([s[k] for s in states_out]) for k in range(8)]
    return (y_prompt, y_sample, st[0], st[1], st[2], st[3], st[4], st[5], st[6], st[7])
```

```python
import functools
from typing import NamedTuple

import jax
import jax.numpy as jnp
from jax import lax
from jax.experimental import pallas as pl
from jax.experimental.pallas import tpu as pltpu

F32 = jnp.float32
BF16 = jnp.bfloat16

D_MODEL = 1024
BRANCH_W = 256
N_BRANCH = 4
POOL_WINDOWS = (2, 4, 8, 16)
POOL_GW = 64
POOL_HIST = 15
CONF_WIDTH = 31
CONF_HIST = 30
SCONV_WIDTH = 3
SCONV_HIST = 2
CHUNK = 128
SGU_HEADS = 4
SGU_HW = 64
N_GROUPS = 4
EXPERTS_PER_GROUP = 8
N_EXPERTS = 32
D_EXPERT = 256
RMS_EPS = 1e-6
LN_EPS = 1e-5
IN_COLS = 6144
GATE_COL0 = 2048

COL_TILE = 256
LANES = 128
SUBLANES = 8
POOL_PAD = 32
CONF_PAD = 32
SCONV_PAD = 8
ROW_BLOCK = 32
ROUTE_GROUP_LANE0 = 32
NEG_BIG = -3.0e38

MIX_TILE = 512
MIX_TILE_S = 256
SORT_TILE = 512
SEG_PAD = 16
PERM_BLOCK = 256
N_PERM_BLOCKS = -(-(2 * SORT_TILE + N_EXPERTS * (SEG_PAD - 1)) // PERM_BLOCK)
LOCAL_ROWS = N_PERM_BLOCKS * PERM_BLOCK
EXP_TILE = 512
CHUNKS_PER_STEP = EXP_TILE // SEG_PAD
EXP_SLOTS = 3
assert EXP_SLOTS * EXP_TILE <= LOCAL_ROWS
VMEM_LIMIT = 56 * 1024 * 1024


class MixCfg(NamedTuple):
    layer: int
    nb: int
    t: int
    has_state: bool
    start_pos: int
    row_off: int
    v_rows: int


def _rms(x, g):
    return x * lax.rsqrt(jnp.mean(x * x, axis=-1, keepdims=True) + RMS_EPS) * g


def _ln(x, g, b):
    mu = jnp.mean(x, axis=-1, keepdims=True)
    xc = x - mu
    return xc * lax.rsqrt(jnp.mean(xc * xc, axis=-1, keepdims=True) + LN_EPS) * g + b


def _sigmoid(x):
    return 0.5 * jnp.tanh(0.5 * x) + 0.5


def _zero_like_bits(x):
    bits = lax.bitcast_convert_type(x, jnp.uint32)
    return ((bits >> 16) >> 16).astype(jnp.int32).astype(F32)


def _gelu_tanh(x):
    return 0.5 * x * (1.0 + jnp.tanh(0.7978845608028654 * (x + 0.044715 * (x * x * x))))


def _row_blocks(nb, t):
    if t >= ROW_BLOCK:
        return [(slice(b, b + 1), t0, ROW_BLOCK) for b in range(nb) for t0 in range(0, t, ROW_BLOCK)]
    bb = ROW_BLOCK // t
    return [(slice(b0, b0 + bb), 0, t) for b0 in range(0, nb, bb)]


def _load_big_weights(layer, w_in_hbm, w_br_hbm, w_out_hbm, w_in, w_br, w_out, stage, stage_b, sem):
    n_in, n_out = IN_COLS // COL_TILE, D_MODEL // COL_TILE
    jobs = [("in", j) for j in range(n_in)] + [("out", j) for j in range(n_out)] + [("br", i) for i in range(N_BRANCH)]

    def copy(job, slot):
        kind, j = job
        if kind == "br":
            return pltpu.make_async_copy(w_br_hbm.at[layer, j], stage_b.at[slot], sem.at[slot])
        src = w_in_hbm if kind == "in" else w_out_hbm
        return pltpu.make_async_copy(src.at[layer, :, pl.ds(j * COL_TILE, COL_TILE)], stage.at[slot], sem.at[slot])

    copy(jobs[0], 0).start()
    for n, job in enumerate(jobs):
        slot = n % 2
        if n + 1 < len(jobs):
            copy(jobs[n + 1], 1 - slot).start()
        copy(job, slot).wait()
        kind, j = job
        if kind == "in":
            tile = stage[slot]
            w_in[j] = (tile * 0.5 if j * COL_TILE >= GATE_COL0 else tile).astype(BF16)
        elif kind == "out":
            w_out[j] = stage[slot].astype(BF16)
        else:
            for c in range(n_out):
                w_br[j * n_out + c] = (stage_b[slot, :, c * COL_TILE:(c + 1) * COL_TILE] * 0.5).astype(BF16)


def _mixer_kernel(cfg, *refs):
    nb, t, tm = cfg.nb, cfg.t, cfg.nb * cfg.t
    refs = list(refs)
    x_ref = refs.pop(0)
    if cfg.has_state:
        pool_st, conv_st, sconv_st = refs.pop(0), refs.pop(0), refs.pop(0)
    (n1g, pool_w, pool_sc, cdw, cdb, clg, clb, scw, slg, slb, wcat, sbias, n2g, w_rt, b_rt,
     w_in_hbm, w_br_hbm, w_out_hbm) = refs[:18]
    refs = refs[18:]
    if cfg.has_state:
        refs = refs[3:]
    h1_o, route_o, xn2_o, pool_o, conv_o, sconv_o, v_o = refs[:7]
    (pool_ext, sum_a, sum_b, conv_ext, conv_sh, sc_ext, buf_a, buf_b, buf_c, gate_buf, xb_buf,
     w_in, w_br, w_out, stage, stage_b, w_sem) = refs[7:]

    first_step = pl.program_id(0) == 0
    if not cfg.has_state:
        first_step = first_step & (pl.program_id(1) == 0)
    pl.when(first_step)(functools.partial(
        _load_big_weights, cfg.layer, w_in_hbm, w_br_hbm, w_out_hbm, w_in, w_br, w_out, stage, stage_b, w_sem))

    if cfg.has_state:
        seq_pos0 = cfg.start_pos
        pool_ext[:, 0:POOL_PAD - SUBLANES, :] = jnp.zeros((nb, POOL_PAD - SUBLANES, BRANCH_W), F32)
        pool_ext[:, POOL_PAD - POOL_HIST:POOL_PAD, :] = pool_st[...]
        conv_ext[:, CONF_PAD - CONF_HIST:CONF_PAD, :] = conv_st[...]
        sc_ext[:, SCONV_PAD - SCONV_HIST:SCONV_PAD, :] = sconv_st[...]
    else:
        c = pl.program_id(1)
        seq_pos0 = cfg.start_pos + c * t

        @pl.when(c == 0)
        def _():
            pool_ext[:, 0:POOL_PAD, :] = jnp.zeros((nb, POOL_PAD, BRANCH_W), F32)
            conv_ext[:, 0:CONF_PAD, :] = jnp.zeros((nb, CONF_PAD, BRANCH_W), F32)
            sc_ext[:, 0:SCONV_PAD, :] = jnp.zeros((nb, SCONV_PAD, BRANCH_W), F32)

    x = x_ref[...]
    xb_buf[...] = _rms(x, n1g[...]).astype(BF16)

    def proj(lo, hi):
        assert lo % COL_TILE == 0 and hi == lo + COL_TILE
        return jnp.dot(xb_buf[...], w_in[lo // COL_TILE], preferred_element_type=F32)

    a_pool = proj(0, 256)
    pool_ext[:, POOL_PAD:, :] = a_pool.reshape(nb, t, BRANCH_W)
    glu = proj(256, 512) * _sigmoid(proj(512, 768))
    conv_ext[:, CONF_PAD:, :] = glu.reshape(nb, t, BRANCH_W)
    z = proj(1024, 1280) * proj(1280, 1536)
    sc_ext[:, SCONV_PAD:, :] = z.reshape(nb, t, BRANCH_W)

    pl_len = POOL_PAD + t
    sum_b[:, 8:pl_len, :] = pool_ext[:, 8:pl_len, :] + pool_ext[:, 7:pl_len - 1, :]
    sum_a[:, 16:pl_len, :] = sum_b[:, 16:pl_len, :] + sum_b[:, 14:pl_len - 2, :]
    sum_b[:, 24:pl_len, :] = sum_a[:, 24:pl_len, :] + sum_a[:, 20:pl_len - 4, :]
    for sh in range(SUBLANES):
        n_rows = t + SUBLANES * ((CONF_WIDTH - 1 - sh) // SUBLANES)
        first = CONF_PAD - CONF_HIST + sh
        conv_sh[sh, :, 0:n_rows, :] = conv_ext[:, first:first + n_rows, :]

    def window_block(bs, t0, tb, order_zero):
        bb = bs.stop - bs.start
        shp = (bb, tb, BRANCH_W)
        lane = lax.broadcasted_iota(jnp.int32, shp, 2)
        pos = seq_pos0 + t0 + lax.broadcasted_iota(jnp.int32, shp, 1)
        r0 = POOL_PAD + t0
        cur = pool_ext[bs, r0:r0 + tb, :]
        s2 = cur + pool_ext[bs, r0 - 1:r0 - 1 + tb, :]
        s4 = sum_a[bs, r0:r0 + tb, :]
        s8 = sum_b[bs, r0:r0 + tb, :]
        s16 = s8 + sum_b[bs, r0 - SUBLANES:r0 - SUBLANES + tb, :]
        win_sum = jnp.where(lane < 64, s2, jnp.where(lane < 128, s4, jnp.where(lane < 192, s8, s16)))
        win = jnp.where(lane < 64, 2, jnp.where(lane < 128, 4, jnp.where(lane < 192, 8, 16)))
        cnt = jnp.minimum(pos + 1, win).astype(F32)
        buf_a[bs, t0:t0 + tb, :] = win_sum / cnt - cur
        tiles = (ROW_BLOCK // SUBLANES, SUBLANES, BRANCH_W)
        acc = jnp.broadcast_to((cdb[...] + order_zero)[None], tiles)
        for k in range(CONF_WIDTH):
            q, sh = divmod(k, SUBLANES)
            r = t0 + SUBLANES * q
            acc = acc + conv_sh[sh, bs, r:r + tb, :].reshape(tiles) * cdw[k][None]
        buf_b[bs, t0:t0 + tb, :] = acc.reshape(shp)
        base = SCONV_PAD - SCONV_HIST + t0
        acc = sc_ext[bs, base:base + tb, :].reshape(tiles) * scw[0][None]
        for k in range(1, SCONV_WIDTH):
            acc = acc + sc_ext[bs, base + k:base + k + tb, :].reshape(tiles) * scw[k][None]
        buf_c[bs, t0:t0 + tb, :] = acc.reshape(shp)

    def gate_chunk(k):
        g0 = k * COL_TILE
        g = jnp.tanh(proj(GATE_COL0 + g0, GATE_COL0 + g0 + COL_TILE)) + 1.0
        gate_buf[:, g0:g0 + COL_TILE] = g
        return _zero_like_bits(g[0:SUBLANES, :])

    br = {}

    def item_windows(blks):
        def run(zero):
            for bs, t0, tb in blks:
                window_block(bs, t0, tb, zero)
        return run

    def item_state(zero):
        del zero
        pool_o[...] = pool_ext[:, POOL_PAD + t - POOL_HIST:POOL_PAD + t, :]
        conv_o[...] = conv_ext[:, CONF_PAD + t - CONF_HIST:CONF_PAD + t, :]
        sconv_o[...] = sc_ext[:, SCONV_PAD + t - SCONV_HIST:SCONV_PAD + t, :]
        if not cfg.has_state:
            pool_ext[:, 0:POOL_PAD, :] = pool_ext[:, t:t + POOL_PAD, :]
            conv_ext[:, 0:CONF_PAD, :] = conv_ext[:, t:t + CONF_PAD, :]
            sc_ext[:, 0:SCONV_PAD, :] = sc_ext[:, t:t + SCONV_PAD, :]

    def item_a(zero):
        pooled = buf_a[...].reshape(tm, BRANCH_W).astype(BF16)
        scale = pool_sc[...] + zero[0:1, :]
        br["a"] = (jnp.dot(pooled, pool_w[...], preferred_element_type=F32) * scale).astype(BF16)

    def item_b(zero):
        cb = _ln(buf_b[...].reshape(tm, BRANCH_W), clg[...] + zero[0:1, :], clb[...])
        br["b"] = (cb * _sigmoid(cb)).astype(BF16)

    def item_c(zero):
        br["c"] = (proj(768, 1024) * (buf_c[...].reshape(tm, BRANCH_W) + zero[0:1, :])).astype(BF16)

    def item_u(zero):
        br["u"] = _gelu_tanh(proj(1536, 1792) + zero[0:1, :])

    def item_v(zero):
        v = _ln(_gelu_tanh(proj(1792, 2048)), slg[...] + zero[0:1, :], slb[...])
        if cfg.has_state:
            v_o[...] = v.reshape(nb, t, BRANCH_W)
        else:
            v_o[...] = v[tm - cfg.v_rows:, :].reshape(1, cfg.v_rows, BRANCH_W)
        br["v"] = v

    def item_d(zero):
        head = lax.broadcasted_iota(jnp.int32, (CHUNK, BRANCH_W), 1) // SGU_HW
        bias = sbias[...] + jnp.concatenate([zero] * (CHUNK // SUBLANES), axis=0)
        mixed = []
        for j in range(tm // CHUNK):
            vj = br["v"][j * CHUNK:(j + 1) * CHUNK, :]
            stacked = jnp.concatenate([jnp.where(head == h, vj, 0.0) for h in range(SGU_HEADS)], axis=0)
            mixed.append(jnp.dot(wcat[...], stacked.astype(BF16), preferred_element_type=F32) + bias)
        br["d"] = (br["u"] * jnp.concatenate(mixed, axis=0)).astype(BF16)

    blocks = _row_blocks(nb, t)
    n_gate = N_BRANCH * D_MODEL // COL_TILE
    per_item = -(-len(blocks) // (n_gate // 2))
    items = [item_windows(blocks[i:i + per_item]) for i in range(0, len(blocks), per_item)]
    items += [item_state, item_c, item_b, item_u, item_v, item_a, item_d]
    assert len(items) <= n_gate
    zero = jnp.zeros((SUBLANES, BRANCH_W), F32)
    for k in range(n_gate):
        next_zero = gate_chunk(k)
        if k < len(items):
            items[k](zero)
        zero = next_zero

    n_ct = D_MODEL // COL_TILE
    brs = [br["a"], br["b"], br["c"], br["d"]]
    for c in range(n_ct):
        part = None
        for i in range(N_BRANCH):
            g0 = i * D_MODEL + c * COL_TILE
            term = gate_buf[:, g0:g0 + COL_TILE] * jnp.dot(
                brs[i], w_br[i * n_ct + c], preferred_element_type=F32)
            part = term if part is None else part + term
        xb_buf[:, c * COL_TILE:(c + 1) * COL_TILE] = part.astype(BF16)
    for c in range(n_ct):
        cs = slice(c * COL_TILE, (c + 1) * COL_TILE)
        h1_o[:, cs] = x_ref[:, cs] + jnp.dot(xb_buf[...], w_out[c], preferred_element_type=F32)

    xn2 = _rms(h1_o[...], n2g[...])
    x_hi = xn2.astype(BF16)
    xn2_o[...] = x_hi
    x_lo = (xn2 - x_hi.astype(F32)).astype(BF16)
    hi_both = jnp.dot(x_hi, w_rt[...], preferred_element_type=F32)
    lo_hi = jnp.dot(x_lo, w_rt[:, 0:LANES], preferred_element_type=F32)
    logits = (hi_both[:, 0:LANES] + (lo_hi + hi_both[:, LANES:2 * LANES])) + b_rt[...]
    lane = lax.broadcasted_iota(jnp.int32, (tm, LANES), 1)
    is_g = (lane >= ROUTE_GROUP_LANE0) & (lane < ROUTE_GROUP_LANE0 + N_GROUPS)
    glog = jnp.where(is_g, logits, NEG_BIG)
    gmax = jnp.max(glog, axis=-1, keepdims=True)
    lane_f = lane.astype(F32)
    no_lane = float(4 * LANES)
    gsel = jnp.min(jnp.where(glog == gmax, lane_f, no_lane), axis=-1, keepdims=True) - ROUTE_GROUP_LANE0
    pg = 1.0 / jnp.sum(jnp.where(is_g, jnp.exp(glog - gmax), 0.0), axis=-1, keepdims=True)
    in_grp = (lane < N_EXPERTS) & ((lane // EXPERTS_PER_GROUP) == gsel.astype(jnp.int32))
    el = jnp.where(in_grp, logits, NEG_BIG)
    m1 = jnp.max(el, axis=-1, keepdims=True)
    i1 = jnp.min(jnp.where(in_grp & (el == m1), lane_f, no_lane), axis=-1, keepdims=True)
    rest = in_grp & (lane_f != i1)
    el2 = jnp.where(rest, logits, NEG_BIG)
    m2 = jnp.max(el2, axis=-1, keepdims=True)
    i2 = jnp.min(jnp.where(rest & (el2 == m2), lane_f, no_lane), axis=-1, keepdims=True)
    e21 = jnp.exp(m2 - m1)
    w1 = pg / (1.0 + e21)
    w2 = pg * e21 / (1.0 + e21)
    route_o[...] = jnp.where(lane == 0, i1, jnp.where(lane == 1, i2,
                             jnp.where(lane == 2, w1, jnp.where(lane == 3, w2, 0.0))))


def _const_spec(shape):
    nd = len(shape)
    return pl.BlockSpec(shape, lambda *_: (0,) * nd, pipeline_mode=pl.Buffered(1))


def _mixer_call(cfg, n_seq, n_rows_total, x2d, x_tile_off, states, weights, big_weights, aliased):
    nb, t = cfg.nb, cfg.t
    tm = nb * t
    if cfg.has_state:
        grid = (n_seq // nb,)
        tok = lambda i: (x_tile_off + i, 0)
        out_tok = lambda i: (cfg.row_off + i, 0)
        seq3 = lambda i: (i, 0, 0)
    else:
        grid = (n_seq, 2048 // t)
        nt = grid[1]
        tok = lambda b, c: (x_tile_off + b * nt + c, 0)
        out_tok = lambda b, c: (cfg.row_off + b * nt + c, 0)
        seq3 = lambda b, c: (b, 0, 0)

    in_specs = [pl.BlockSpec((tm, D_MODEL), tok)]
    args = [x2d]
    if cfg.has_state:
        for s in states:
            in_specs.append(pl.BlockSpec((nb,) + s.shape[1:], seq3))
            args.append(s)
    for w in weights:
        in_specs.append(_const_spec(w.shape))
        args.append(w)
    for w in big_weights:
        in_specs.append(pl.BlockSpec(memory_space=pl.ANY))
        args.append(w)
    io_alias = {}
    if cfg.has_state:
        for k, a in enumerate(aliased):
            in_specs.append(pl.BlockSpec(memory_space=pl.ANY))
            io_alias[len(args)] = k
            args.append(a)

    out_shape = [
        jax.ShapeDtypeStruct((n_rows_total, D_MODEL), F32),
        jax.ShapeDtypeStruct((n_rows_total, LANES), F32),
        jax.ShapeDtypeStruct((n_rows_total, D_MODEL), BF16),
        jax.ShapeDtypeStruct((n_seq, POOL_HIST, BRANCH_W), F32),
        jax.ShapeDtypeStruct((n_seq, CONF_HIST, BRANCH_W), F32),
        jax.ShapeDtypeStruct((n_seq, SCONV_HIST, BRANCH_W), F32),
        jax.ShapeDtypeStruct((n_seq, cfg.v_rows, BRANCH_W), F32),
    ]
    out_specs = [
        pl.BlockSpec((tm, D_MODEL), out_tok),
        pl.BlockSpec((tm, LANES), out_tok),
        pl.BlockSpec((tm, D_MODEL), out_tok),
        pl.BlockSpec((nb, POOL_HIST, BRANCH_W), seq3),
        pl.BlockSpec((nb, CONF_HIST, BRANCH_W), seq3),
        pl.BlockSpec((nb, SCONV_HIST, BRANCH_W), seq3),
        pl.BlockSpec((nb, cfg.v_rows, BRANCH_W), seq3),
    ]
    scratch = [
        pltpu.VMEM((nb, POOL_PAD + t, BRANCH_W), F32),
        pltpu.VMEM((nb, POOL_PAD + t, BRANCH_W), F32),
        pltpu.VMEM((nb, POOL_PAD + t, BRANCH_W), F32),
        pltpu.VMEM((nb, CONF_PAD + t, BRANCH_W), F32),
        pltpu.VMEM((SUBLANES, nb, t + CONF_PAD - SUBLANES, BRANCH_W), F32),
        pltpu.VMEM((nb, SCONV_PAD + t, BRANCH_W), F32),
        pltpu.VMEM((nb, t, BRANCH_W), F32),
        pltpu.VMEM((nb, t, BRANCH_W), F32),
        pltpu.VMEM((nb, t, BRANCH_W), F32),
        pltpu.VMEM((tm, N_BRANCH * D_MODEL), F32),
        pltpu.VMEM((tm, D_MODEL), BF16),
        pltpu.VMEM((IN_COLS // COL_TILE, D_MODEL, COL_TILE), BF16),
        pltpu.VMEM((N_BRANCH * D_MODEL // COL_TILE, BRANCH_W, COL_TILE), BF16),
        pltpu.VMEM((D_MODEL // COL_TILE, D_MODEL, COL_TILE), BF16),
        pltpu.VMEM((2, D_MODEL, COL_TILE), F32),
        pltpu.VMEM((2, BRANCH_W, D_MODEL), F32),
        pltpu.SemaphoreType.DMA((2,)),
    ]
    return pl.pallas_call(
        functools.partial(_mixer_kernel, cfg),
        grid=grid,
        in_specs=in_specs,
        out_specs=out_specs,
        out_shape=out_shape,
        scratch_shapes=scratch,
        input_output_aliases=io_alias,
        compiler_params=pltpu.CompilerParams(
            dimension_semantics=("arbitrary",) * len(grid), vmem_limit_bytes=VMEM_LIMIT),
        name="mixer_sample" if cfg.has_state else "mixer_prompt",
    )(*args)


def _sort_kernel(xn_ref, route_ref, xl_o, pos_o, nch_o):
    t = SORT_TILE
    xn = xn_ref[...]
    r = route_ref[...]
    lane_i = lax.broadcasted_iota(jnp.int32, (t, LANES), 1)
    lane = lane_i.astype(F32)
    hit1 = lane == r[:, 0:1]
    hit2 = lane == r[:, 1:2]
    onehot = jnp.where(hit1 | hit2, 1.0, 0.0)
    row = lax.broadcasted_iota(jnp.int32, (t, t), 0)
    col = lax.broadcasted_iota(jnp.int32, (t, t), 1)
    strict_lower = jnp.where(col < row, 1.0, 0.0).astype(BF16)
    before = jnp.dot(strict_lower, onehot.astype(BF16), preferred_element_type=F32)
    cnt = jnp.sum(onehot, axis=0, keepdims=True)
    nch = jnp.floor((cnt + (SEG_PAD - 1)) * (1.0 / SEG_PAD))
    er = lax.broadcasted_iota(jnp.int32, (LANES, LANES), 0)
    ec = lax.broadcasted_iota(jnp.int32, (LANES, LANES), 1)
    strict_upper = jnp.where(er < ec, 1.0, 0.0).astype(BF16)
    nch8 = jnp.broadcast_to(nch, (SUBLANES, LANES)).astype(BF16)
    seg0 = jnp.dot(nch8, strict_upper, preferred_element_type=F32)[0:1, :] * SEG_PAD
    where = seg0 + before
    pos0 = jnp.sum(jnp.where(hit1, where, 0.0), axis=-1, keepdims=True)
    pos1 = jnp.sum(jnp.where(hit2, where, 0.0), axis=-1, keepdims=True)
    pos = jnp.where(lane_i == 0, pos0, jnp.where(lane_i == 1, pos1, -1.0))
    pos_o[...] = pos
    nch_o[...] = jnp.broadcast_to(nch, (1, SUBLANES, LANES))
    pos_t = jnp.transpose(pos)
    p0 = pos_t[0:1, :]
    p1 = pos_t[1:2, :]
    used_rows = jnp.sum(nch) * SEG_PAD

    def perm_block(blk):
        dst = (lax.broadcasted_iota(jnp.int32, (PERM_BLOCK, t), 0) + blk * PERM_BLOCK).astype(F32)
        perm = jnp.where((dst == p0) | (dst == p1), 1.0, 0.0).astype(BF16)
        xl_o[0, blk * PERM_BLOCK:(blk + 1) * PERM_BLOCK, :] = jnp.dot(
            perm, xn, preferred_element_type=F32).astype(BF16)

    for blk in range(N_PERM_BLOCKS - 1):
        perm_block(blk)
    pl.when(used_rows > (N_PERM_BLOCKS - 1) * PERM_BLOCK)(functools.partial(perm_block, N_PERM_BLOCKS - 1))


def _sort_call(xn2, route):
    n = xn2.shape[0]
    nt = n // SORT_TILE
    return pl.pallas_call(
        _sort_kernel,
        grid=(nt,),
        in_specs=[pl.BlockSpec((SORT_TILE, D_MODEL), lambda i: (i, 0)),
                  pl.BlockSpec((SORT_TILE, LANES), lambda i: (i, 0))],
        out_specs=[pl.BlockSpec((1, LOCAL_ROWS, D_MODEL), lambda i: (i, 0, 0)),
                   pl.BlockSpec((SORT_TILE, LANES), lambda i: (i, 0)),
                   pl.BlockSpec((1, SUBLANES, LANES), lambda i: (i, 0, 0))],
        out_shape=[jax.ShapeDtypeStruct((nt + 1, LOCAL_ROWS, D_MODEL), BF16),
                   jax.ShapeDtypeStruct((n, LANES), F32),
                   jax.ShapeDtypeStruct((nt, SUBLANES, LANES), F32)],
        compiler_params=pltpu.CompilerParams(dimension_semantics=("arbitrary",), vmem_limit_bytes=VMEM_LIMIT),
        name="moe_sort",
    )(xn2, route)


def _chunk_copy(src, src_row, dst, dst_row, sem):
    return pltpu.make_async_copy(src.at[pl.ds(src_row, SEG_PAD), :], dst.at[pl.ds(dst_row, SEG_PAD), :], sem)


def _expert_kernel(trash_row0, se_ref, nv_ref, src_ref, dst_ref, xl_in, wg, wu, wd, xl_io, *scratch):
    del se_ref, xl_in
    ns = EXP_SLOTS
    xbuf, ybuf, (sem_in, sem_out) = scratch[:ns], scratch[ns:2 * ns], scratch[2 * ns:]
    s = pl.program_id(0)
    nv = nv_ref[0]

    def gather(step, sl):
        return [_chunk_copy(xl_io, pl.multiple_of(src_ref[step * CHUNKS_PER_STEP + j], SEG_PAD),
                            xbuf[sl], j * SEG_PAD, sem_in.at[sl]) for j in range(CHUNKS_PER_STEP)]

    def write_back(step, sl):
        return [_chunk_copy(ybuf[sl], j * SEG_PAD, xl_io,
                            pl.multiple_of(dst_ref[step * CHUNKS_PER_STEP + j], SEG_PAD), sem_out.at[sl])
                for j in range(CHUNKS_PER_STEP)]

    def spare_write(sl):
        return [_chunk_copy(ybuf[sl], j * SEG_PAD, xl_io, trash_row0 + sl * EXP_TILE + j * SEG_PAD,
                            sem_out.at[sl]) for j in range(CHUNKS_PER_STEP)]

    @pl.when(s == 0)
    def _():
        for ahead in range(ns - 1):
            for c in gather(ahead, ahead):
                c.start()
        for sl in range(ns):
            ybuf[sl][...] = jnp.zeros_like(ybuf[sl])
            for c in spare_write(sl):
                c.start()

    def live_step(sl):
        for c in gather(s + ns - 1, (sl + ns - 1) % ns):
            c.start()
        for c in gather(s, sl):
            c.wait()
        xb = xbuf[sl][...]
        a = jnp.dot(xb, wg[0, 0].astype(BF16), preferred_element_type=F32)
        b = jnp.dot(xb, wu[0, 0].astype(BF16), preferred_element_type=F32)
        hh = (a * _sigmoid(a) * b).astype(BF16)
        y = jnp.dot(hh, wd[0, 0].astype(BF16), preferred_element_type=F32).astype(BF16)
        for c in write_back(s, sl):
            c.wait()
        ybuf[sl][...] = y
        for c in write_back(s, sl):
            c.start()

    def drain_step(sl):
        for ahead in range(ns - 1):
            for c in gather(s + ahead, (sl + ahead) % ns):
                c.wait()
        for every in range(ns):
            for c in spare_write(every):
                c.wait()

    for sl in range(ns):
        pl.when((s < nv) & (s % ns == sl))(functools.partial(live_step, sl))
        pl.when((s == nv) & (s % ns == sl))(functools.partial(drain_step, sl))


def _expert_call(xl, trash_row0, step_expert, n_valid, chunk_src, chunk_dst, layer, w_gate, w_up, w_down):
    n_steps = step_expert.shape[0] - (EXP_SLOTS - 1)
    wmap = lambda s, se, *_: (layer, se[s], 0, 0)
    grid_spec = pltpu.PrefetchScalarGridSpec(
        num_scalar_prefetch=4,
        grid=(n_steps,),
        in_specs=[pl.BlockSpec(memory_space=pl.ANY),
                  pl.BlockSpec((1, 1, D_MODEL, D_EXPERT), wmap),
                  pl.BlockSpec((1, 1, D_MODEL, D_EXPERT), wmap),
                  pl.BlockSpec((1, 1, D_EXPERT, D_MODEL), wmap)],
        out_specs=pl.BlockSpec(memory_space=pl.ANY),
        scratch_shapes=[pltpu.VMEM((EXP_TILE, D_MODEL), BF16) for _ in range(2 * EXP_SLOTS)] + [
                        pltpu.SemaphoreType.DMA((EXP_SLOTS,)), pltpu.SemaphoreType.DMA((EXP_SLOTS,))],
    )
    return pl.pallas_call(
        functools.partial(_expert_kernel, trash_row0),
        grid_spec=grid_spec,
        out_shape=jax.ShapeDtypeStruct(xl.shape, BF16),
        input_output_aliases={4: 0},
        compiler_params=pltpu.CompilerParams(dimension_semantics=("arbitrary",), has_side_effects=True),
        name="moe_experts",
    )(step_expert, n_valid, chunk_src, chunk_dst, xl, w_gate, w_up, w_down)


def _combine_kernel(final_norm, n_first, h1_ref, route_ref, pos_ref, fg, yl_ref, *outs):
    t = SORT_TILE
    r = route_ref[...]
    p = pos_ref[...]
    w1, w2 = r[:, 2:3], r[:, 3:4]
    pos0, pos1 = p[:, 0:1], p[:, 1:2]
    *outs, acc_ref = outs

    def gathered(blk):
        src = (lax.broadcasted_iota(jnp.int32, (t, PERM_BLOCK), 1) + blk * PERM_BLOCK).astype(F32)
        pw = jnp.where(src == pos0, w1, 0.0) + jnp.where(src == pos1, w2, 0.0)
        return jnp.dot(pw.astype(BF16), yl_ref[0, blk * PERM_BLOCK:(blk + 1) * PERM_BLOCK, :],
                       preferred_element_type=F32)

    acc = h1_ref[...]
    for blk in range(N_PERM_BLOCKS - 1):
        acc = acc + gathered(blk)
    acc_ref[...] = acc

    @pl.when(jnp.max(p) >= (N_PERM_BLOCKS - 1) * PERM_BLOCK)
    def _():
        acc_ref[...] += gathered(N_PERM_BLOCKS - 1)

    if not final_norm:
        outs[0][...] = acc_ref[...]
        return
    y = _rms(acc_ref[...], fg[...])
    i = pl.program_id(0)

    @pl.when(i < n_first)
    def _():
        outs[0][...] = y

    @pl.when(i >= n_first)
    def _():
        outs[1][...] = y


def _combine_call(h1, route, pos, yl, final_g, final_norm, n_first_rows):
    n = h1.shape[0]
    nt = n // SORT_TILE
    n_first = n_first_rows // SORT_TILE
    tok = lambda i: (i, 0)
    if final_norm:
        out_shape = [jax.ShapeDtypeStruct((n_first_rows, D_MODEL), F32),
                     jax.ShapeDtypeStruct((n - n_first_rows, D_MODEL), F32)]
        out_specs = [pl.BlockSpec((SORT_TILE, D_MODEL), lambda i: (jnp.minimum(i, n_first - 1), 0)),
                     pl.BlockSpec((SORT_TILE, D_MODEL), lambda i: (jnp.maximum(i - n_first, 0), 0))]
    else:
        out_shape = [jax.ShapeDtypeStruct((n, D_MODEL), F32)]
        out_specs = [pl.BlockSpec((SORT_TILE, D_MODEL), tok)]
    return pl.pallas_call(
        functools.partial(_combine_kernel, final_norm, n_first),
        grid=(nt,),
        in_specs=[pl.BlockSpec((SORT_TILE, D_MODEL), tok),
                  pl.BlockSpec((SORT_TILE, LANES), tok),
                  pl.BlockSpec((SORT_TILE, LANES), tok),
                  pl.BlockSpec((1, D_MODEL), lambda i: (0, 0)),
                  pl.BlockSpec((1, LOCAL_ROWS, D_MODEL), lambda i: (i, 0, 0))],
        out_specs=out_specs,
        out_shape=out_shape,
        scratch_shapes=[pltpu.VMEM((SORT_TILE, D_MODEL), F32)],
        compiler_params=pltpu.CompilerParams(dimension_semantics=("arbitrary",), vmem_limit_bytes=VMEM_LIMIT),
        name="moe_combine",
    )(h1, route, pos, final_g, yl)


def _expert_tables(nch, n_steps, trash_row0):
    nt = nch.shape[0]
    cps = CHUNKS_PER_STEP
    seg_row0 = (jnp.cumsum(nch, axis=1) - nch) * SEG_PAD
    first = jnp.cumsum(nch, axis=0) - nch
    tot = jnp.sum(nch, axis=0)
    steps = (tot + cps - 1) // cps
    step_end = jnp.cumsum(steps)
    n_valid = step_end[-1:]
    s_ids = jnp.arange(n_steps, dtype=jnp.int32)
    step_expert = jnp.minimum(jnp.sum((step_end[None, :] <= s_ids[:, None]).astype(jnp.int32), axis=1),
                              N_EXPERTS - 1)
    sel = (step_expert[:, None] == jnp.arange(N_EXPERTS, dtype=jnp.int32)[None, :]).astype(jnp.int32)
    step0 = sel @ (step_end - steps)
    tot_s = sel @ tot
    first_s = sel @ first.T
    row0_s = sel @ seg_row0.T
    k = (s_ids - step0)[:, None] * cps + jnp.arange(cps, dtype=jnp.int32)[None, :]
    ok = (k < tot_s[:, None]) & (s_ids < n_valid[0])[:, None]
    tile = jnp.sum((first_s[:, None, :] <= k[:, :, None]).astype(jnp.int32), axis=2) - 1
    tsel = (tile[:, :, None] == jnp.arange(nt, dtype=jnp.int32)[None, None, :]).astype(jnp.int32)
    first_k = jnp.sum(tsel * first_s[:, None, :], axis=2)
    row0_k = jnp.sum(tsel * row0_s[:, None, :], axis=2)
    row = tile * LOCAL_ROWS + row0_k + (k - first_k) * SEG_PAD
    live = (s_ids < n_valid[0])[:, None]
    src = jnp.where(live, jnp.where(ok, row, row[:, 0:1]), 0)
    spare = trash_row0 + (s_ids % EXP_SLOTS)[:, None] * EXP_TILE + jnp.arange(cps, dtype=jnp.int32)[None, :] * SEG_PAD
    dst = jnp.where(ok, row, spare)
    return (step_expert, n_valid.astype(jnp.int32), src.reshape(-1).astype(jnp.int32),
            dst.reshape(-1).astype(jnp.int32))


def _moe(h1, route, xn2, layer, w_gate, w_up, w_down, final_g, final_norm, n_first_rows):
    n = h1.shape[0]
    nt = n // SORT_TILE
    xl, pos, nch = _sort_call(xn2, route)
    nch = nch[:, 0, :N_EXPERTS].astype(jnp.int32)
    max_chunks = (2 * n) // SEG_PAD + nt * N_EXPERTS
    n_steps = max_chunks // CHUNKS_PER_STEP + N_EXPERTS + 1
    trash_row0 = nt * LOCAL_ROWS
    step_expert, n_valid, chunk_src, chunk_dst = _expert_tables(nch, n_steps + EXP_SLOTS - 1, trash_row0)
    yl = _expert_call(xl.reshape((nt + 1) * LOCAL_ROWS, D_MODEL), trash_row0, step_expert, n_valid,
                      chunk_src, chunk_dst, layer, w_gate, w_up, w_down)
    return _combine_call(h1, route, pos, yl.reshape(nt + 1, LOCAL_ROWS, D_MODEL), final_g, final_norm, n_first_rows)


def _block_diag(pw):
    out = jnp.zeros((BRANCH_W, BRANCH_W), pw.dtype)
    for g in range(len(POOL_WINDOWS)):
        out = out.at[g * POOL_GW:(g + 1) * POOL_GW, g * POOL_GW:(g + 1) * POOL_GW].set(pw[g])
    return out


def kernel(x_prompt, x_sample, state_pool, state_conv, state_sconv, norm1_g, w_in, pool_w, pool_scale, conf_dw, conf_dw_b, conf_ln_g, conf_ln_b, sconv_w, sgu_ln_g, sgu_ln_b, sgu_ws, sgu_b, w_branch, w_out, norm2_g, router_g, router_g_b, router_e, router_e_b, w_gate, w_up, w_down, final_g):
    depth = w_in.shape[0]
    bp, seq, _ = x_prompt.shape
    bs, dseq, _ = x_sample.shape
    n_p, n_s = bp * seq, bs * dseq
    n = n_p + n_s
    past_len = 16384
    assert seq % MIX_TILE == 0 and n_s % MIX_TILE_S == 0 and n_p % MIX_TILE_S == 0 and dseq == SUBLANES
    assert MIX_TILE % CHUNK == 0 and MIX_TILE_S % CHUNK == 0

    cfg_p = MixCfg(layer=0, nb=1, t=MIX_TILE, has_state=False, start_pos=0, row_off=0, v_rows=CHUNK)
    cfg_s = MixCfg(layer=0, nb=MIX_TILE_S // dseq, t=dseq, has_state=True, start_pos=past_len,
                   row_off=n_p // MIX_TILE_S, v_rows=dseq)
    big_weights = (w_in, w_branch, w_out)

    row = lambda a: a.reshape(1, -1)
    rep8 = lambda a: jnp.broadcast_to(a[..., None, :], a.shape[:-1] + (SUBLANES, a.shape[-1]))
    tril = jnp.tril(jnp.ones((CHUNK, CHUNK), F32))
    eye_blk = jnp.kron(jnp.eye(CHUNK // dseq, dtype=F32), jnp.ones((dseq, dseq), F32))
    final_row = row(final_g)

    h_p, h_s = x_prompt.reshape(n_p, D_MODEL), x_sample.reshape(n_s, D_MODEL)
    off_p, off_s = 0, 0
    states_out = []
    for l in range(depth):
        ws_p = sgu_ws[l] * tril[None]
        ws_small = jnp.tile(sgu_ws[l][:, :dseq, :dseq], (1, CHUNK // dseq, CHUNK // dseq))
        ws_s = ws_small * (tril * eye_blk)[None]
        cat = lambda w: jnp.concatenate([w[h] for h in range(SGU_HEADS)], axis=1).astype(BF16)
        bias_p = jnp.repeat(sgu_b[l].T, SGU_HW, axis=1)
        bias_s = jnp.tile(jnp.repeat(sgu_b[l][:, :dseq].T, SGU_HW, axis=1), (CHUNK // dseq, 1))
        lane_pad = LANES - N_EXPERTS - N_GROUPS
        w_rt32 = jnp.pad(jnp.concatenate([router_e[l], router_g[l]], axis=1), ((0, 0), (0, lane_pad)))
        w_rt_hi = w_rt32.astype(BF16)
        w_rt = jnp.concatenate([w_rt_hi, (w_rt32 - w_rt_hi.astype(F32)).astype(BF16)], axis=1)
        b_rt = jnp.pad(jnp.concatenate([router_e_b[l], router_g_b[l]]), (0, lane_pad)).reshape(1, LANES)

        def weights(wcat, sbias):
            return [row(norm1_g[l]), _block_diag(pool_w[l]).astype(BF16), row(pool_scale[l]),
                    rep8(conf_dw[l]), rep8(conf_dw_b[l]), row(conf_ln_g[l]), row(conf_ln_b[l]), rep8(sconv_w[l]),
                    row(sgu_ln_g[l]), row(sgu_ln_b[l]), wcat, sbias, row(norm2_g[l]), w_rt, b_rt]

        h1, route, xn2, pool_p, conv_p, sconv_p, v_p = _mixer_call(
            cfg_p._replace(layer=l), bp, n, h_p, off_p, None, weights(cat(ws_p), bias_p), big_weights, None)
        h1, route, xn2, pool_s, conv_s, sconv_s, v_s = _mixer_call(
            cfg_s._replace(layer=l), bs, n, h_s, off_s, (state_pool[l], state_conv[l], state_sconv[l]),
            weights(cat(ws_s), bias_s), big_weights, (h1, route, xn2))
        states_out.append((pool_p, conv_p, sconv_p, v_p, pool_s, conv_s, sconv_s, v_s))

        outs = _moe(h1, route, xn2, l, w_gate, w_up, w_down, final_row, l == depth - 1, n_p)
        h_p = h_s = outs[0]
        off_p, off_s = 0, n_p // MIX_TILE_S

    y_prompt = outs[0].reshape(bp, seq, D_MODEL)
    y_sample = outs[1].reshape(bs, dseq, D_MODEL)
    st = [jnp.stack([s[k] for s in states_out]) for k in range(8)]
    return (y_prompt, y_sample, st[0], st[1], st[2], st[3], st[4], st[5], st[6], st[7])
```

```python
import functools
from typing import NamedTuple

import jax
import jax.numpy as jnp
from jax import lax
from jax.experimental import pallas as pl
from jax.experimental.pallas import tpu as pltpu

F32 = jnp.float32
BF16 = jnp.bfloat16

D_MODEL = 1024
BRANCH_W = 256
N_BRANCH = 4
POOL_WINDOWS = (2, 4, 8, 16)
POOL_GW = 64
POOL_HIST = 15
CONF_WIDTH = 31
CONF_HIST = 30
SCONV_WIDTH = 3
SCONV_HIST = 2
CHUNK = 128
SGU_HEADS = 4
SGU_HW = 64
N_GROUPS = 4
EXPERTS_PER_GROUP = 8
N_EXPERTS = 32
D_EXPERT = 256
RMS_EPS = 1e-6
LN_EPS = 1e-5
IN_COLS = 6144
GATE_COL0 = 2048

COL_TILE = 256
LANES = 128
SUBLANES = 8
POOL_PAD = 32
CONF_PAD = 32
SCONV_PAD = 8
ROW_BLOCK = 32
ROUTE_GROUP_LANE0 = 32
NEG_BIG = -3.0e38

MIX_TILE = 512
MIX_TILE_S = 256
SORT_TILE = 512
SEG_PAD = 16
PERM_BLOCK = 256
N_PERM_BLOCKS = -(-(2 * SORT_TILE + N_EXPERTS * (SEG_PAD - 1)) // PERM_BLOCK)
LOCAL_ROWS = N_PERM_BLOCKS * PERM_BLOCK
EXP_TILE = 512
CHUNKS_PER_STEP = EXP_TILE // SEG_PAD
EXP_SLOTS = 4
SPARE_BLOCKS = -(-EXP_SLOTS * EXP_TILE // LOCAL_ROWS)
VMEM_LIMIT = 56 * 1024 * 1024


class MixCfg(NamedTuple):
    layer: int
    nb: int
    t: int
    has_state: bool
    start_pos: int
    row_off: int
    v_rows: int


def _rms(x, g):
    return x * lax.rsqrt(jnp.mean(x * x, axis=-1, keepdims=True) + RMS_EPS) * g


def _ln(x, g, b):
    mu = jnp.mean(x, axis=-1, keepdims=True)
    xc = x - mu
    return xc * lax.rsqrt(jnp.mean(xc * xc, axis=-1, keepdims=True) + LN_EPS) * g + b


def _sigmoid(x):
    return 0.5 * jnp.tanh(0.5 * x) + 0.5


def _zero_like_bits(x):
    bits = lax.bitcast_convert_type(x, jnp.uint32)
    return ((bits >> 16) >> 16).astype(jnp.int32).astype(F32)


def _gelu_tanh(x):
    return 0.5 * x * (1.0 + jnp.tanh(0.7978845608028654 * (x + 0.044715 * (x * x * x))))


def _row_blocks(nb, t):
    if t >= ROW_BLOCK:
        return [(slice(b, b + 1), t0, ROW_BLOCK) for b in range(nb) for t0 in range(0, t, ROW_BLOCK)]
    bb = ROW_BLOCK // t
    return [(slice(b0, b0 + bb), 0, t) for b0 in range(0, nb, bb)]


def _load_big_weights(layer, w_in_hbm, w_br_hbm, w_out_hbm, w_in, w_br, w_out, stage, stage_b, sem):
    n_in, n_out = IN_COLS // COL_TILE, D_MODEL // COL_TILE
    jobs = [("in", j) for j in range(n_in)] + [("out", j) for j in range(n_out)] + [("br", i) for i in range(N_BRANCH)]

    def copy(job, slot):
        kind, j = job
        if kind == "br":
            return pltpu.make_async_copy(w_br_hbm.at[layer, j], stage_b.at[slot], sem.at[slot])
        src = w_in_hbm if kind == "in" else w_out_hbm
        return pltpu.make_async_copy(src.at[layer, :, pl.ds(j * COL_TILE, COL_TILE)], stage.at[slot], sem.at[slot])

    copy(jobs[0], 0).start()
    for n, job in enumerate(jobs):
        slot = n % 2
        if n + 1 < len(jobs):
            copy(jobs[n + 1], 1 - slot).start()
        copy(job, slot).wait()
        kind, j = job
        if kind == "in":
            tile = stage[slot]
            w_in[j] = (tile * 0.5 if j * COL_TILE >= GATE_COL0 else tile).astype(BF16)
        elif kind == "out":
            w_out[j] = stage[slot].astype(BF16)
        else:
            for c in range(n_out):
                w_br[j * n_out + c] = (stage_b[slot, :, c * COL_TILE:(c + 1) * COL_TILE] * 0.5).astype(BF16)


def _mixer_kernel(cfg, *refs):
    nb, t, tm = cfg.nb, cfg.t, cfg.nb * cfg.t
    refs = list(refs)
    x_ref = refs.pop(0)
    if cfg.has_state:
        pool_st, conv_st, sconv_st = (refs.pop(0).at[0] for _ in range(3))
    (n1g, pool_w, pool_sc, cdw, cdb, clg, clb, scw, slg, slb, wcat, sbias, n2g, w_rt, b_rt) = (
        r.at[0] for r in refs[:15])
    w_in_hbm, w_br_hbm, w_out_hbm = refs[15:18]
    refs = refs[18:]
    if cfg.has_state:
        refs = refs[3:]
    h1_o, route_o, xn2_o, pool_o, conv_o, sconv_o, v_o = refs[:7]
    (pool_ext, sum_a, sum_b, conv_ext, conv_sh, sc_ext, buf_a, buf_b, buf_c, gate_buf, xb_buf,
     w_in, w_br, w_out, stage, stage_b, w_sem) = refs[7:]

    first_step = pl.program_id(0) == 0
    if not cfg.has_state:
        first_step = first_step & (pl.program_id(1) == 0)
    pl.when(first_step)(functools.partial(
        _load_big_weights, cfg.layer, w_in_hbm, w_br_hbm, w_out_hbm, w_in, w_br, w_out, stage, stage_b, w_sem))

    if cfg.has_state:
        seq_pos0 = cfg.start_pos
        pool_ext[:, 0:POOL_PAD - SUBLANES, :] = jnp.zeros((nb, POOL_PAD - SUBLANES, BRANCH_W), F32)
        pool_ext[:, POOL_PAD - POOL_HIST:POOL_PAD, :] = pool_st[...]
        conv_ext[:, CONF_PAD - CONF_HIST:CONF_PAD, :] = conv_st[...]
        sc_ext[:, SCONV_PAD - SCONV_HIST:SCONV_PAD, :] = sconv_st[...]
    else:
        c = pl.program_id(1)
        seq_pos0 = cfg.start_pos + c * t

        @pl.when(c == 0)
        def _():
            pool_ext[:, 0:POOL_PAD, :] = jnp.zeros((nb, POOL_PAD, BRANCH_W), F32)
            conv_ext[:, 0:CONF_PAD, :] = jnp.zeros((nb, CONF_PAD, BRANCH_W), F32)
            sc_ext[:, 0:SCONV_PAD, :] = jnp.zeros((nb, SCONV_PAD, BRANCH_W), F32)

    x = x_ref[...]
    xb_buf[...] = _rms(x, n1g[...]).astype(BF16)

    def proj(lo, hi):
        assert lo % COL_TILE == 0 and hi == lo + COL_TILE
        return jnp.dot(xb_buf[...], w_in[lo // COL_TILE], preferred_element_type=F32)

    a_pool = proj(0, 256)
    pool_ext[:, POOL_PAD:, :] = a_pool.reshape(nb, t, BRANCH_W)
    glu = proj(256, 512) * _sigmoid(proj(512, 768))
    conv_ext[:, CONF_PAD:, :] = glu.reshape(nb, t, BRANCH_W)
    z = proj(1024, 1280) * proj(1280, 1536)
    sc_ext[:, SCONV_PAD:, :] = z.reshape(nb, t, BRANCH_W)

    pl_len = POOL_PAD + t
    sum_b[:, 8:pl_len, :] = pool_ext[:, 8:pl_len, :] + pool_ext[:, 7:pl_len - 1, :]
    sum_a[:, 16:pl_len, :] = sum_b[:, 16:pl_len, :] + sum_b[:, 14:pl_len - 2, :]
    sum_b[:, 24:pl_len, :] = sum_a[:, 24:pl_len, :] + sum_a[:, 20:pl_len - 4, :]
    for sh in range(SUBLANES):
        n_rows = t + SUBLANES * ((CONF_WIDTH - 1 - sh) // SUBLANES)
        first = CONF_PAD - CONF_HIST + sh
        conv_sh[sh, :, 0:n_rows, :] = conv_ext[:, first:first + n_rows, :]

    def window_block(bs, t0, tb, order_zero):
        bb = bs.stop - bs.start
        shp = (bb, tb, BRANCH_W)
        lane = lax.broadcasted_iota(jnp.int32, shp, 2)
        pos = seq_pos0 + t0 + lax.broadcasted_iota(jnp.int32, shp, 1)
        r0 = POOL_PAD + t0
        cur = pool_ext[bs, r0:r0 + tb, :]
        s2 = cur + pool_ext[bs, r0 - 1:r0 - 1 + tb, :]
        s4 = sum_a[bs, r0:r0 + tb, :]
        s8 = sum_b[bs, r0:r0 + tb, :]
        s16 = s8 + sum_b[bs, r0 - SUBLANES:r0 - SUBLANES + tb, :]
        win_sum = jnp.where(lane < 64, s2, jnp.where(lane < 128, s4, jnp.where(lane < 192, s8, s16)))
        win = jnp.where(lane < 64, 2, jnp.where(lane < 128, 4, jnp.where(lane < 192, 8, 16)))
        cnt = jnp.minimum(pos + 1, win).astype(F32)
        buf_a[bs, t0:t0 + tb, :] = win_sum / cnt - cur
        tiles = (ROW_BLOCK // SUBLANES, SUBLANES, BRANCH_W)
        acc = jnp.broadcast_to((cdb[...] + order_zero)[None], tiles)
        for k in range(CONF_WIDTH):
            q, sh = divmod(k, SUBLANES)
            r = t0 + SUBLANES * q
            acc = acc + conv_sh[sh, bs, r:r + tb, :].reshape(tiles) * cdw[k][None]
        buf_b[bs, t0:t0 + tb, :] = acc.reshape(shp)
        base = SCONV_PAD - SCONV_HIST + t0
        acc = sc_ext[bs, base:base + tb, :].reshape(tiles) * scw[0][None]
        for k in range(1, SCONV_WIDTH):
            acc = acc + sc_ext[bs, base + k:base + k + tb, :].reshape(tiles) * scw[k][None]
        buf_c[bs, t0:t0 + tb, :] = acc.reshape(shp)

    def gate_chunk(k):
        g0 = k * COL_TILE
        g = jnp.tanh(proj(GATE_COL0 + g0, GATE_COL0 + g0 + COL_TILE)) + 1.0
        gate_buf[:, g0:g0 + COL_TILE] = g
        return _zero_like_bits(g[0:SUBLANES, :])

    br = {}

    def item_windows(blks):
        def run(zero):
            for bs, t0, tb in blks:
                window_block(bs, t0, tb, zero)
        return run

    def item_state(zero):
        del zero
        pool_o[...] = pool_ext[:, POOL_PAD + t - POOL_HIST:POOL_PAD + t, :]
        conv_o[...] = conv_ext[:, CONF_PAD + t - CONF_HIST:CONF_PAD + t, :]
        sconv_o[...] = sc_ext[:, SCONV_PAD + t - SCONV_HIST:SCONV_PAD + t, :]
        if not cfg.has_state:
            pool_ext[:, 0:POOL_PAD, :] = pool_ext[:, t:t + POOL_PAD, :]
            conv_ext[:, 0:CONF_PAD, :] = conv_ext[:, t:t + CONF_PAD, :]
            sc_ext[:, 0:SCONV_PAD, :] = sc_ext[:, t:t + SCONV_PAD, :]

    def item_a(zero):
        pooled = buf_a[...].reshape(tm, BRANCH_W).astype(BF16)
        scale = pool_sc[...] + zero[0:1, :]
        br["a"] = (jnp.dot(pooled, pool_w[...], preferred_element_type=F32) * scale).astype(BF16)

    def item_b(zero):
        cb = _ln(buf_b[...].reshape(tm, BRANCH_W), clg[...] + zero[0:1, :], clb[...])
        br["b"] = (cb * _sigmoid(cb)).astype(BF16)

    def item_c(zero):
        br["c"] = (proj(768, 1024) * (buf_c[...].reshape(tm, BRANCH_W) + zero[0:1, :])).astype(BF16)

    def item_u(zero):
        br["u"] = _gelu_tanh(proj(1536, 1792) + zero[0:1, :])

    def item_v(zero):
        v = _ln(_gelu_tanh(proj(1792, 2048)), slg[...] + zero[0:1, :], slb[...])
        if cfg.has_state:
            v_o[...] = v.reshape(nb, t, BRANCH_W)
        else:
            v_o[...] = v[tm - cfg.v_rows:, :].reshape(1, cfg.v_rows, BRANCH_W)
        br["v"] = v

    def item_d(zero):
        head = lax.broadcasted_iota(jnp.int32, (CHUNK, BRANCH_W), 1) // SGU_HW
        bias = sbias[...] + jnp.concatenate([zero] * (CHUNK // SUBLANES), axis=0)
        mixed = []
        for j in range(tm // CHUNK):
            vj = br["v"][j * CHUNK:(j + 1) * CHUNK, :]
            stacked = jnp.concatenate([jnp.where(head == h, vj, 0.0) for h in range(SGU_HEADS)], axis=0)
            mixed.append(jnp.dot(wcat[...], stacked.astype(BF16), preferred_element_type=F32) + bias)
        br["d"] = (br["u"] * jnp.concatenate(mixed, axis=0)).astype(BF16)

    blocks = _row_blocks(nb, t)
    n_gate = N_BRANCH * D_MODEL // COL_TILE
    per_item = -(-len(blocks) // (n_gate // 2))
    items = [item_windows(blocks[i:i + per_item]) for i in range(0, len(blocks), per_item)]
    items += [item_state, item_c, item_b, item_u, item_v, item_a, item_d]
    assert len(items) <= n_gate
    zero = jnp.zeros((SUBLANES, BRANCH_W), F32)
    for k in range(n_gate):
        next_zero = gate_chunk(k)
        if k < len(items):
            items[k](zero)
        zero = next_zero

    n_ct = D_MODEL // COL_TILE
    brs = [br["a"], br["b"], br["c"], br["d"]]
    for c in range(n_ct):
        part = None
        for i in range(N_BRANCH):
            g0 = i * D_MODEL + c * COL_TILE
            term = gate_buf[:, g0:g0 + COL_TILE] * jnp.dot(
                brs[i], w_br[i * n_ct + c], preferred_element_type=F32)
            part = term if part is None else part + term
        xb_buf[:, c * COL_TILE:(c + 1) * COL_TILE] = part.astype(BF16)
    for c in range(n_ct):
        cs = slice(c * COL_TILE, (c + 1) * COL_TILE)
        h1_o[:, cs] = x_ref[:, cs] + jnp.dot(xb_buf[...], w_out[c], preferred_element_type=F32)

    xn2 = _rms(h1_o[...], n2g[...])
    x_hi = xn2.astype(BF16)
    xn2_o[...] = x_hi
    x_lo = (xn2 - x_hi.astype(F32)).astype(BF16)
    hi_both = jnp.dot(x_hi, w_rt[...], preferred_element_type=F32)
    lo_hi = jnp.dot(x_lo, w_rt[:, 0:LANES], preferred_element_type=F32)
    logits = (hi_both[:, 0:LANES] + (lo_hi + hi_both[:, LANES:2 * LANES])) + b_rt[...]
    lane = lax.broadcasted_iota(jnp.int32, (tm, LANES), 1)
    is_g = (lane >= ROUTE_GROUP_LANE0) & (lane < ROUTE_GROUP_LANE0 + N_GROUPS)
    glog = jnp.where(is_g, logits, NEG_BIG)
    gmax = jnp.max(glog, axis=-1, keepdims=True)
    lane_f = lane.astype(F32)
    no_lane = float(4 * LANES)
    gsel = jnp.min(jnp.where(glog == gmax, lane_f, no_lane), axis=-1, keepdims=True) - ROUTE_GROUP_LANE0
    pg = 1.0 / jnp.sum(jnp.where(is_g, jnp.exp(glog - gmax), 0.0), axis=-1, keepdims=True)
    in_grp = (lane < N_EXPERTS) & ((lane // EXPERTS_PER_GROUP) == gsel.astype(jnp.int32))
    el = jnp.where(in_grp, logits, NEG_BIG)
    m1 = jnp.max(el, axis=-1, keepdims=True)
    i1 = jnp.min(jnp.where(in_grp & (el == m1), lane_f, no_lane), axis=-1, keepdims=True)
    rest = in_grp & (lane_f != i1)
    el2 = jnp.where(rest, logits, NEG_BIG)
    m2 = jnp.max(el2, axis=-1, keepdims=True)
    i2 = jnp.min(jnp.where(rest & (el2 == m2), lane_f, no_lane), axis=-1, keepdims=True)
    e21 = jnp.exp(m2 - m1)
    w1 = pg / (1.0 + e21)
    w2 = pg * e21 / (1.0 + e21)
    route_o[...] = jnp.where(lane == 0, i1, jnp.where(lane == 1, i2,
                             jnp.where(lane == 2, w1, jnp.where(lane == 3, w2, 0.0))))


def _layer_spec(shape, layer):
    nd = len(shape)
    return pl.BlockSpec((1,) + shape[1:], lambda *_: (layer,) + (0,) * (nd - 1), pipeline_mode=pl.Buffered(1))


def _mixer_call(cfg, n_seq, n_rows_total, x2d, x_tile_off, states, weights, big_weights, aliased):
    nb, t = cfg.nb, cfg.t
    tm = nb * t
    if cfg.has_state:
        grid = (n_seq // nb,)
        tok = lambda i: (x_tile_off + i, 0)
        out_tok = lambda i: (cfg.row_off + i, 0)
        seq3 = lambda i: (i, 0, 0)
    else:
        grid = (n_seq, 2048 // t)
        nt = grid[1]
        tok = lambda b, c: (x_tile_off + b * nt + c, 0)
        out_tok = lambda b, c: (cfg.row_off + b * nt + c, 0)
        seq3 = lambda b, c: (b, 0, 0)

    in_specs = [pl.BlockSpec((tm, D_MODEL), tok)]
    args = [x2d]
    if cfg.has_state:
        for s in states:
            in_specs.append(pl.BlockSpec((1, nb) + s.shape[2:], lambda i: (cfg.layer, i, 0, 0)))
            args.append(s)
    for w in weights:
        in_specs.append(_layer_spec(w.shape, cfg.layer))
        args.append(w)
    for w in big_weights:
        in_specs.append(pl.BlockSpec(memory_space=pl.ANY))
        args.append(w)
    io_alias = {}
    if cfg.has_state:
        for k, a in enumerate(aliased):
            in_specs.append(pl.BlockSpec(memory_space=pl.ANY))
            io_alias[len(args)] = k
            args.append(a)

    out_shape = [
        jax.ShapeDtypeStruct((n_rows_total, D_MODEL), F32),
        jax.ShapeDtypeStruct((n_rows_total, LANES), F32),
        jax.ShapeDtypeStruct((n_rows_total, D_MODEL), BF16),
        jax.ShapeDtypeStruct((n_seq, POOL_HIST, BRANCH_W), F32),
        jax.ShapeDtypeStruct((n_seq, CONF_HIST, BRANCH_W), F32),
        jax.ShapeDtypeStruct((n_seq, SCONV_HIST, BRANCH_W), F32),
        jax.ShapeDtypeStruct((n_seq, cfg.v_rows, BRANCH_W), F32),
    ]
    out_specs = [
        pl.BlockSpec((tm, D_MODEL), out_tok),
        pl.BlockSpec((tm, LANES), out_tok),
        pl.BlockSpec((tm, D_MODEL), out_tok),
        pl.BlockSpec((nb, POOL_HIST, BRANCH_W), seq3),
        pl.BlockSpec((nb, CONF_HIST, BRANCH_W), seq3),
        pl.BlockSpec((nb, SCONV_HIST, BRANCH_W), seq3),
        pl.BlockSpec((nb, cfg.v_rows, BRANCH_W), seq3),
    ]
    scratch = [
        pltpu.VMEM((nb, POOL_PAD + t, BRANCH_W), F32),
        pltpu.VMEM((nb, POOL_PAD + t, BRANCH_W), F32),
        pltpu.VMEM((nb, POOL_PAD + t, BRANCH_W), F32),
        pltpu.VMEM((nb, CONF_PAD + t, BRANCH_W), F32),
        pltpu.VMEM((SUBLANES, nb, t + CONF_PAD - SUBLANES, BRANCH_W), F32),
        pltpu.VMEM((nb, SCONV_PAD + t, BRANCH_W), F32),
        pltpu.VMEM((nb, t, BRANCH_W), F32),
        pltpu.VMEM((nb, t, BRANCH_W), F32),
        pltpu.VMEM((nb, t, BRANCH_W), F32),
        pltpu.VMEM((tm, N_BRANCH * D_MODEL), F32),
        pltpu.VMEM((tm, D_MODEL), BF16),
        pltpu.VMEM((IN_COLS // COL_TILE, D_MODEL, COL_TILE), BF16),
        pltpu.VMEM((N_BRANCH * D_MODEL // COL_TILE, BRANCH_W, COL_TILE), BF16),
        pltpu.VMEM((D_MODEL // COL_TILE, D_MODEL, COL_TILE), BF16),
        pltpu.VMEM((2, D_MODEL, COL_TILE), F32),
        pltpu.VMEM((2, BRANCH_W, D_MODEL), F32),
        pltpu.SemaphoreType.DMA((2,)),
    ]
    return pl.pallas_call(
        functools.partial(_mixer_kernel, cfg),
        grid=grid,
        in_specs=in_specs,
        out_specs=out_specs,
        out_shape=out_shape,
        scratch_shapes=scratch,
        input_output_aliases=io_alias,
        compiler_params=pltpu.CompilerParams(
            dimension_semantics=("arbitrary",) * len(grid), vmem_limit_bytes=VMEM_LIMIT),
        name="mixer_sample" if cfg.has_state else "mixer_prompt",
    )(*args)


def _sort_kernel(xn_ref, route_ref, xl_o, pos_o, nch_o):
    t = SORT_TILE
    xn = xn_ref[...]
    r = route_ref[...]
    lane_i = lax.broadcasted_iota(jnp.int32, (t, LANES), 1)
    lane = lane_i.astype(F32)
    hit1 = lane == r[:, 0:1]
    hit2 = lane == r[:, 1:2]
    onehot = jnp.where(hit1 | hit2, 1.0, 0.0)
    row = lax.broadcasted_iota(jnp.int32, (t, t), 0)
    col = lax.broadcasted_iota(jnp.int32, (t, t), 1)
    strict_lower = jnp.where(col < row, 1.0, 0.0).astype(BF16)
    before = jnp.dot(strict_lower, onehot.astype(BF16), preferred_element_type=F32)
    cnt = jnp.sum(onehot, axis=0, keepdims=True)
    nch = jnp.floor((cnt + (SEG_PAD - 1)) * (1.0 / SEG_PAD))
    er = lax.broadcasted_iota(jnp.int32, (LANES, LANES), 0)
    ec = lax.broadcasted_iota(jnp.int32, (LANES, LANES), 1)
    strict_upper = jnp.where(er < ec, 1.0, 0.0).astype(BF16)
    nch8 = jnp.broadcast_to(nch, (SUBLANES, LANES)).astype(BF16)
    seg0 = jnp.dot(nch8, strict_upper, preferred_element_type=F32)[0:1, :] * SEG_PAD
    where = seg0 + before
    pos0 = jnp.sum(jnp.where(hit1, where, 0.0), axis=-1, keepdims=True)
    pos1 = jnp.sum(jnp.where(hit2, where, 0.0), axis=-1, keepdims=True)
    pos = jnp.where(lane_i == 0, pos0, jnp.where(lane_i == 1, pos1, -1.0))
    pos_o[...] = pos
    nch_o[...] = jnp.broadcast_to(nch, (1, SUBLANES, LANES))
    pos_t = jnp.transpose(pos)
    p0 = pos_t[0:1, :]
    p1 = pos_t[1:2, :]
    used_rows = jnp.sum(nch) * SEG_PAD

    def perm_block(blk):
        dst = (lax.broadcasted_iota(jnp.int32, (PERM_BLOCK, t), 0) + blk * PERM_BLOCK).astype(F32)
        perm = jnp.where((dst == p0) | (dst == p1), 1.0, 0.0).astype(BF16)
        xl_o[0, blk * PERM_BLOCK:(blk + 1) * PERM_BLOCK, :] = jnp.dot(
            perm, xn, preferred_element_type=F32).astype(BF16)

    for blk in range(N_PERM_BLOCKS - 1):
        perm_block(blk)
    pl.when(used_rows > (N_PERM_BLOCKS - 1) * PERM_BLOCK)(functools.partial(perm_block, N_PERM_BLOCKS - 1))


def _sort_call(xn2, route):
    n = xn2.shape[0]
    nt = n // SORT_TILE
    return pl.pallas_call(
        _sort_kernel,
        grid=(nt,),
        in_specs=[pl.BlockSpec((SORT_TILE, D_MODEL), lambda i: (i, 0)),
                  pl.BlockSpec((SORT_TILE, LANES), lambda i: (i, 0))],
        out_specs=[pl.BlockSpec((1, LOCAL_ROWS, D_MODEL), lambda i: (i, 0, 0)),
                   pl.BlockSpec((SORT_TILE, LANES), lambda i: (i, 0)),
                   pl.BlockSpec((1, SUBLANES, LANES), lambda i: (i, 0, 0))],
        out_shape=[jax.ShapeDtypeStruct((nt + SPARE_BLOCKS, LOCAL_ROWS, D_MODEL), BF16),
                   jax.ShapeDtypeStruct((n, LANES), F32),
                   jax.ShapeDtypeStruct((nt, SUBLANES, LANES), F32)],
        compiler_params=pltpu.CompilerParams(dimension_semantics=("arbitrary",), vmem_limit_bytes=VMEM_LIMIT),
        name="moe_sort",
    )(xn2, route)


def _chunk_copy(src, src_row, dst, dst_row, sem):
    return pltpu.make_async_copy(src.at[pl.ds(src_row, SEG_PAD), :], dst.at[pl.ds(dst_row, SEG_PAD), :], sem)


def _expert_kernel(trash_row0, se_ref, nv_ref, src_ref, dst_ref, xl_in, wg, wu, wd, xl_io, *scratch):
    del se_ref, xl_in
    ns = EXP_SLOTS
    xbuf, ybuf, (sem_in, sem_out) = scratch[:ns], scratch[ns:2 * ns], scratch[2 * ns:]
    s = pl.program_id(0)
    nv = nv_ref[0]

    def gather(step, sl):
        return [_chunk_copy(xl_io, pl.multiple_of(src_ref[step * CHUNKS_PER_STEP + j], SEG_PAD),
                            xbuf[sl], j * SEG_PAD, sem_in.at[sl]) for j in range(CHUNKS_PER_STEP)]

    def write_back(step, sl):
        return [_chunk_copy(ybuf[sl], j * SEG_PAD, xl_io,
                            pl.multiple_of(dst_ref[step * CHUNKS_PER_STEP + j], SEG_PAD), sem_out.at[sl])
                for j in range(CHUNKS_PER_STEP)]

    def spare_write(sl):
        return [_chunk_copy(ybuf[sl], j * SEG_PAD, xl_io, trash_row0 + sl * EXP_TILE + j * SEG_PAD,
                            sem_out.at[sl]) for j in range(CHUNKS_PER_STEP)]

    @pl.when(s == 0)
    def _():
        for ahead in range(ns - 1):
            for c in gather(ahead, ahead):
                c.start()
        for sl in range(ns):
            ybuf[sl][...] = jnp.zeros_like(ybuf[sl])
            for c in spare_write(sl):
                c.start()

    def live_step(sl):
        for c in gather(s + ns - 1, (sl + ns - 1) % ns):
            c.start()
        for c in gather(s, sl):
            c.wait()
        xb = xbuf[sl][...]
        a = jnp.dot(xb, wg[0, 0].astype(BF16), preferred_element_type=F32)
        b = jnp.dot(xb, wu[0, 0].astype(BF16), preferred_element_type=F32)
        hh = (a * _sigmoid(a) * b).astype(BF16)
        y = jnp.dot(hh, wd[0, 0].astype(BF16), preferred_element_type=F32).astype(BF16)
        for c in write_back(s, sl):
            c.wait()
        ybuf[sl][...] = y
        for c in write_back(s, sl):
            c.start()

    def drain_step(sl):
        for ahead in range(ns - 1):
            for c in gather(s + ahead, (sl + ahead) % ns):
                c.wait()
        for every in range(ns):
            for c in spare_write(every):
                c.wait()

    for sl in range(ns):
        pl.when((s < nv) & (s % ns == sl))(functools.partial(live_step, sl))
        pl.when((s == nv) & (s % ns == sl))(functools.partial(drain_step, sl))


def _expert_call(xl, trash_row0, step_expert, n_valid, chunk_src, chunk_dst, layer, w_gate, w_up, w_down):
    n_steps = step_expert.shape[0] - (EXP_SLOTS - 1)
    wmap = lambda s, se, *_: (layer, se[s], 0, 0)
    grid_spec = pltpu.PrefetchScalarGridSpec(
        num_scalar_prefetch=4,
        grid=(n_steps,),
        in_specs=[pl.BlockSpec(memory_space=pl.ANY),
                  pl.BlockSpec((1, 1, D_MODEL, D_EXPERT), wmap),
                  pl.BlockSpec((1, 1, D_MODEL, D_EXPERT), wmap),
                  pl.BlockSpec((1, 1, D_EXPERT, D_MODEL), wmap)],
        out_specs=pl.BlockSpec(memory_space=pl.ANY),
        scratch_shapes=[pltpu.VMEM((EXP_TILE, D_MODEL), BF16) for _ in range(2 * EXP_SLOTS)] + [
                        pltpu.SemaphoreType.DMA((EXP_SLOTS,)), pltpu.SemaphoreType.DMA((EXP_SLOTS,))],
    )
    return pl.pallas_call(
        functools.partial(_expert_kernel, trash_row0),
        grid_spec=grid_spec,
        out_shape=jax.ShapeDtypeStruct(xl.shape, BF16),
        input_output_aliases={4: 0},
        compiler_params=pltpu.CompilerParams(dimension_semantics=("arbitrary",), has_side_effects=True),
        name="moe_experts",
    )(step_expert, n_valid, chunk_src, chunk_dst, xl, w_gate, w_up, w_down)


def _combine_kernel(final_norm, n_first, h1_ref, route_ref, pos_ref, fg, yl_ref, *outs):
    t = SORT_TILE
    r = route_ref[...]
    p = pos_ref[...]
    w1, w2 = r[:, 2:3], r[:, 3:4]
    pos0, pos1 = p[:, 0:1], p[:, 1:2]
    *outs, acc_ref = outs

    def gathered(blk):
        src = (lax.broadcasted_iota(jnp.int32, (t, PERM_BLOCK), 1) + blk * PERM_BLOCK).astype(F32)
        pw = jnp.where(src == pos0, w1, 0.0) + jnp.where(src == pos1, w2, 0.0)
        return jnp.dot(pw.astype(BF16), yl_ref[0, blk * PERM_BLOCK:(blk + 1) * PERM_BLOCK, :],
                       preferred_element_type=F32)

    acc = h1_ref[...]
    for blk in range(N_PERM_BLOCKS - 1):
        acc = acc + gathered(blk)
    acc_ref[...] = acc

    @pl.when(jnp.max(p) >= (N_PERM_BLOCKS - 1) * PERM_BLOCK)
    def _():
        acc_ref[...] += gathered(N_PERM_BLOCKS - 1)

    if not final_norm:
        outs[0][...] = acc_ref[...]
        return
    y = _rms(acc_ref[...], fg[...])
    i = pl.program_id(0)

    @pl.when(i < n_first)
    def _():
        outs[0][...] = y

    @pl.when(i >= n_first)
    def _():
        outs[1][...] = y


def _combine_call(h1, route, pos, yl, final_g, final_norm, n_first_rows):
    n = h1.shape[0]
    nt = n // SORT_TILE
    n_first = n_first_rows // SORT_TILE
    tok = lambda i: (i, 0)
    if final_norm:
        out_shape = [jax.ShapeDtypeStruct((n_first_rows, D_MODEL), F32),
                     jax.ShapeDtypeStruct((n - n_first_rows, D_MODEL), F32)]
        out_specs = [pl.BlockSpec((SORT_TILE, D_MODEL), lambda i: (jnp.minimum(i, n_first - 1), 0)),
                     pl.BlockSpec((SORT_TILE, D_MODEL), lambda i: (jnp.maximum(i - n_first, 0), 0))]
    else:
        out_shape = [jax.ShapeDtypeStruct((n, D_MODEL), F32)]
        out_specs = [pl.BlockSpec((SORT_TILE, D_MODEL), tok)]
    return pl.pallas_call(
        functools.partial(_combine_kernel, final_norm, n_first),
        grid=(nt,),
        in_specs=[pl.BlockSpec((SORT_TILE, D_MODEL), tok),
                  pl.BlockSpec((SORT_TILE, LANES), tok),
                  pl.BlockSpec((SORT_TILE, LANES), tok),
                  pl.BlockSpec((1, D_MODEL), lambda i: (0, 0)),
                  pl.BlockSpec((1, LOCAL_ROWS, D_MODEL), lambda i: (i, 0, 0))],
        out_specs=out_specs,
        out_shape=out_shape,
        scratch_shapes=[pltpu.VMEM((SORT_TILE, D_MODEL), F32)],
        compiler_params=pltpu.CompilerParams(dimension_semantics=("arbitrary",), vmem_limit_bytes=VMEM_LIMIT),
        name="moe_combine",
    )(h1, route, pos, final_g, yl)


def _expert_tables(nch, n_steps, trash_row0):
    nt = nch.shape[0]
    cps = CHUNKS_PER_STEP
    seg_row0 = (jnp.cumsum(nch, axis=1) - nch) * SEG_PAD
    first = jnp.cumsum(nch, axis=0) - nch
    tot = jnp.sum(nch, axis=0)
    steps = (tot + cps - 1) // cps
    step_end = jnp.cumsum(steps)
    n_valid = step_end[-1:]
    s_ids = jnp.arange(n_steps, dtype=jnp.int32)
    step_expert = jnp.minimum(jnp.sum((step_end[None, :] <= s_ids[:, None]).astype(jnp.int32), axis=1),
                              N_EXPERTS - 1)
    sel = (step_expert[:, None] == jnp.arange(N_EXPERTS, dtype=jnp.int32)[None, :]).astype(jnp.int32)
    step0 = sel @ (step_end - steps)
    tot_s = sel @ tot
    first_s = sel @ first.T
    row0_s = sel @ seg_row0.T
    k = (s_ids - step0)[:, None] * cps + jnp.arange(cps, dtype=jnp.int32)[None, :]
    ok = (k < tot_s[:, None]) & (s_ids < n_valid[0])[:, None]
    tile = jnp.sum((first_s[:, None, :] <= k[:, :, None]).astype(jnp.int32), axis=2) - 1
    tsel = (tile[:, :, None] == jnp.arange(nt, dtype=jnp.int32)[None, None, :]).astype(jnp.int32)
    first_k = jnp.sum(tsel * first_s[:, None, :], axis=2)
    row0_k = jnp.sum(tsel * row0_s[:, None, :], axis=2)
    row = tile * LOCAL_ROWS + row0_k + (k - first_k) * SEG_PAD
    live = (s_ids < n_valid[0])[:, None]
    src = jnp.where(live, jnp.where(ok, row, row[:, 0:1]), 0)
    spare = trash_row0 + (s_ids % EXP_SLOTS)[:, None] * EXP_TILE + jnp.arange(cps, dtype=jnp.int32)[None, :] * SEG_PAD
    dst = jnp.where(ok, row, spare)
    return (step_expert, n_valid.astype(jnp.int32), src.reshape(-1).astype(jnp.int32),
            dst.reshape(-1).astype(jnp.int32))


def _moe(h1, route, xn2, layer, w_gate, w_up, w_down, final_g, final_norm, n_first_rows):
    n = h1.shape[0]
    nt = n // SORT_TILE
    xl, pos, nch = _sort_call(xn2, route)
    nch = nch[:, 0, :N_EXPERTS].astype(jnp.int32)
    max_chunks = (2 * n) // SEG_PAD + nt * N_EXPERTS
    n_steps = max_chunks // CHUNKS_PER_STEP + N_EXPERTS + 1
    trash_row0 = nt * LOCAL_ROWS
    step_expert, n_valid, chunk_src, chunk_dst = _expert_tables(nch, n_steps + EXP_SLOTS - 1, trash_row0)
    yl = _expert_call(xl.reshape((nt + SPARE_BLOCKS) * LOCAL_ROWS, D_MODEL), trash_row0, step_expert, n_valid,
                      chunk_src, chunk_dst, layer, w_gate, w_up, w_down)
    return _combine_call(h1, route, pos, yl.reshape(nt + SPARE_BLOCKS, LOCAL_ROWS, D_MODEL), final_g, final_norm,
                         n_first_rows)


def kernel(x_prompt, x_sample, state_pool, state_conv, state_sconv, norm1_g, w_in, pool_w, pool_scale, conf_dw, conf_dw_b, conf_ln_g, conf_ln_b, sconv_w, sgu_ln_g, sgu_ln_b, sgu_ws, sgu_b, w_branch, w_out, norm2_g, router_g, router_g_b, router_e, router_e_b, w_gate, w_up, w_down, final_g):
    depth = w_in.shape[0]
    bp, seq, _ = x_prompt.shape
    bs, dseq, _ = x_sample.shape
    n_p, n_s = bp * seq, bs * dseq
    n = n_p + n_s
    past_len = 16384
    assert seq % MIX_TILE == 0 and n_s % MIX_TILE_S == 0 and n_p % MIX_TILE_S == 0 and dseq == SUBLANES
    assert MIX_TILE % CHUNK == 0 and MIX_TILE_S % CHUNK == 0

    cfg_p = MixCfg(layer=0, nb=1, t=MIX_TILE, has_state=False, start_pos=0, row_off=0, v_rows=CHUNK)
    cfg_s = MixCfg(layer=0, nb=MIX_TILE_S // dseq, t=dseq, has_state=True, start_pos=past_len,
                   row_off=n_p // MIX_TILE_S, v_rows=dseq)
    big_weights = (w_in, w_branch, w_out)

    row = lambda a: a[:, None, :]
    rep8 = lambda a: jnp.broadcast_to(a[..., None, :], a.shape[:-1] + (SUBLANES, a.shape[-1]))
    tril = jnp.tril(jnp.ones((CHUNK, CHUNK), F32))
    eye_blk = jnp.kron(jnp.eye(CHUNK // dseq, dtype=F32), jnp.ones((dseq, dseq), F32))
    cat = lambda w: jnp.concatenate([w[:, h] for h in range(SGU_HEADS)], axis=2).astype(BF16)
    wcat_p = cat(sgu_ws * tril)
    wcat_s = cat(jnp.tile(sgu_ws[:, :, :dseq, :dseq], (1, 1, CHUNK // dseq, CHUNK // dseq)) * (tril * eye_blk))
    bias_p = jnp.repeat(jnp.swapaxes(sgu_b, 1, 2), SGU_HW, axis=2)
    bias_s = jnp.tile(jnp.repeat(jnp.swapaxes(sgu_b[:, :, :dseq], 1, 2), SGU_HW, axis=2), (1, CHUNK // dseq, 1))
    lane_pad = LANES - N_EXPERTS - N_GROUPS
    w_rt32 = jnp.pad(jnp.concatenate([router_e, router_g], axis=2), ((0, 0), (0, 0), (0, lane_pad)))
    w_rt_hi = w_rt32.astype(BF16)
    w_rt = jnp.concatenate([w_rt_hi, (w_rt32 - w_rt_hi.astype(F32)).astype(BF16)], axis=2)
    b_rt = row(jnp.pad(jnp.concatenate([router_e_b, router_g_b], axis=1), ((0, 0), (0, lane_pad))))
    same_group = jnp.eye(len(POOL_WINDOWS), dtype=F32)[None, :, None, :, None]
    pool_bd = (pool_w[:, :, :, None, :] * same_group).reshape(depth, BRANCH_W, BRANCH_W).astype(BF16)

    def weights(wcat, sbias):
        return [row(norm1_g), pool_bd, row(pool_scale), rep8(conf_dw), rep8(conf_dw_b), row(conf_ln_g),
                row(conf_ln_b), rep8(sconv_w), row(sgu_ln_g), row(sgu_ln_b), wcat, sbias, row(norm2_g), w_rt, b_rt]

    weights_p, weights_s = weights(wcat_p, bias_p), weights(wcat_s, bias_s)
    final_row = final_g.reshape(1, -1)

    h_p, h_s = x_prompt.reshape(n_p, D_MODEL), x_sample.reshape(n_s, D_MODEL)
    off_p, off_s = 0, 0
    states_out = []
    for l in range(depth):
        h1, route, xn2, pool_p, conv_p, sconv_p, v_p = _mixer_call(
            cfg_p._replace(layer=l), bp, n, h_p, off_p, None, weights_p, big_weights, None)
        h1, route, xn2, pool_s, conv_s, sconv_s, v_s = _mixer_call(
            cfg_s._replace(layer=l), bs, n, h_s, off_s, (state_pool, state_conv, state_sconv),
            weights_s, big_weights, (h1, route, xn2))
        states_out.append((pool_p, conv_p, sconv_p, v_p, pool_s, conv_s, sconv_s, v_s))

        outs = _moe(h1, route, xn2, l, w_gate, w_up, w_down, final_row, l == depth - 1, n_p)
        h_p = h_s = outs[0]
        off_p, off_s = 0, n_p // MIX_TILE_S

    y_prompt = outs[0].reshape(bp, seq, D_MODEL)
    y_sample = outs[1].reshape(bs, dseq, D_MODEL)
    st = [jnp.stack([s[k] for s in states_out]) for k in range(8)]
    return (y_prompt, y_sample, st[0], st[1], st[2], st[3], st[4], st[5], st[6], st[7])
```

```python
import functools
from typing import NamedTuple

import jax
import jax.numpy as jnp
from jax import lax
from jax.experimental import pallas as pl
from jax.experimental.pallas import tpu as pltpu

F32 = jnp.float32
BF16 = jnp.bfloat16

D_MODEL = 1024
BRANCH_W = 256
N_BRANCH = 4
POOL_WINDOWS = (2, 4, 8, 16)
POOL_GW = 64
POOL_HIST = 15
CONF_WIDTH = 31
CONF_HIST = 30
SCONV_WIDTH = 3
SCONV_HIST = 2
CHUNK = 128
SGU_HEADS = 4
SGU_HW = 64
N_GROUPS = 4
EXPERTS_PER_GROUP = 8
N_EXPERTS = 32
D_EXPERT = 256
RMS_EPS = 1e-6
LN_EPS = 1e-5
IN_COLS = 6144
GATE_COL0 = 2048

COL_TILE = 256
LANES = 128
SUBLANES = 8
POOL_PAD = 32
CONF_PAD = 32
SCONV_PAD = 8
ROW_BLOCK = 32
ROUTE_GROUP_LANE0 = 32
NEG_BIG = -3.0e38

MIX_TILE = 512
MIX_TILE_S = 256
SORT_TILE = 512
SEG_PAD = 16
PERM_BLOCK = 256
N_PERM_BLOCKS = -(-(2 * SORT_TILE + N_EXPERTS * (SEG_PAD - 1)) // PERM_BLOCK)
LOCAL_ROWS = N_PERM_BLOCKS * PERM_BLOCK
EXP_TILE = 512
CHUNKS_PER_STEP = EXP_TILE // SEG_PAD
EXP_SLOTS = 3
SPARE_BLOCKS = -(-EXP_SLOTS * EXP_TILE // LOCAL_ROWS)
VMEM_LIMIT = 56 * 1024 * 1024


class MixCfg(NamedTuple):
    layer: int
    nb: int
    t: int
    has_state: bool
    start_pos: int
    row_off: int
    v_rows: int


def _rms(x, g):
    return x * lax.rsqrt(jnp.mean(x * x, axis=-1, keepdims=True) + RMS_EPS) * g


def _ln(x, g, b):
    mu = jnp.mean(x, axis=-1, keepdims=True)
    xc = x - mu
    return xc * lax.rsqrt(jnp.mean(xc * xc, axis=-1, keepdims=True) + LN_EPS) * g + b


def _sigmoid(x):
    return 0.5 * jnp.tanh(0.5 * x) + 0.5


def _zero_like_bits(x):
    bits = lax.bitcast_convert_type(x, jnp.uint32)
    return ((bits >> 16) >> 16).astype(jnp.int32).astype(F32)


def _gelu_tanh(x):
    return 0.5 * x * (1.0 + jnp.tanh(0.7978845608028654 * (x + 0.044715 * (x * x * x))))


def _row_blocks(nb, t):
    if t >= ROW_BLOCK:
        return [(slice(b, b + 1), t0, ROW_BLOCK) for b in range(nb) for t0 in range(0, t, ROW_BLOCK)]
    bb = ROW_BLOCK // t
    return [(slice(b0, b0 + bb), 0, t) for b0 in range(0, nb, bb)]


def _load_big_weights(layer, w_in_hbm, w_br_hbm, w_out_hbm, w_in, w_br, w_out, stage, stage_b, sem):
    n_in, n_out = IN_COLS // COL_TILE, D_MODEL // COL_TILE
    jobs = [("in", j) for j in range(n_in)] + [("out", j) for j in range(n_out)] + [("br", i) for i in range(N_BRANCH)]

    def copy(job, slot):
        kind, j = job
        if kind == "br":
            return pltpu.make_async_copy(w_br_hbm.at[layer, j], stage_b.at[slot], sem.at[slot])
        src = w_in_hbm if kind == "in" else w_out_hbm
        return pltpu.make_async_copy(src.at[layer, :, pl.ds(j * COL_TILE, COL_TILE)], stage.at[slot], sem.at[slot])

    copy(jobs[0], 0).start()
    for n, job in enumerate(jobs):
        slot = n % 2
        if n + 1 < len(jobs):
            copy(jobs[n + 1], 1 - slot).start()
        copy(job, slot).wait()
        kind, j = job
        if kind == "in":
            tile = stage[slot]
            w_in[j] = (tile * 0.5 if j * COL_TILE >= GATE_COL0 else tile).astype(BF16)
        elif kind == "out":
            w_out[j] = stage[slot].astype(BF16)
        else:
            for c in range(n_out):
                w_br[j * n_out + c] = (stage_b[slot, :, c * COL_TILE:(c + 1) * COL_TILE] * 0.5).astype(BF16)


def _mixer_kernel(cfg, *refs):
    nb, t, tm = cfg.nb, cfg.t, cfg.nb * cfg.t
    refs = list(refs)
    x_ref = refs.pop(0)
    if cfg.has_state:
        pool_st, conv_st, sconv_st = (refs.pop(0).at[0] for _ in range(3))
    (n1g, pool_w, pool_sc, cdw, cdb, clg, clb, scw, slg, slb, wcat, sbias, n2g, w_rt, b_rt) = (
        r.at[0] for r in refs[:15])
    w_in_hbm, w_br_hbm, w_out_hbm = refs[15:18]
    refs = refs[18:]
    if cfg.has_state:
        refs = refs[3:]
    h1_o, route_o, xn2_o, pool_o, conv_o, sconv_o, v_o = refs[:7]
    (pool_ext, sum_a, sum_b, conv_ext, conv_sh, sc_ext, buf_a, buf_b, buf_c, gate_buf, xb_buf,
     w_in, w_br, w_out, stage, stage_b, w_sem) = refs[7:]

    first_step = pl.program_id(0) == 0
    if not cfg.has_state:
        first_step = first_step & (pl.program_id(1) == 0)
    pl.when(first_step)(functools.partial(
        _load_big_weights, cfg.layer, w_in_hbm, w_br_hbm, w_out_hbm, w_in, w_br, w_out, stage, stage_b, w_sem))

    if cfg.has_state:
        seq_pos0 = cfg.start_pos
        pool_ext[:, 0:POOL_PAD - SUBLANES, :] = jnp.zeros((nb, POOL_PAD - SUBLANES, BRANCH_W), F32)
        pool_ext[:, POOL_PAD - POOL_HIST:POOL_PAD, :] = pool_st[...]
        conv_ext[:, CONF_PAD - CONF_HIST:CONF_PAD, :] = conv_st[...]
        sc_ext[:, SCONV_PAD - SCONV_HIST:SCONV_PAD, :] = sconv_st[...]
    else:
        c = pl.program_id(1)
        seq_pos0 = cfg.start_pos + c * t

        @pl.when(c == 0)
        def _():
            pool_ext[:, 0:POOL_PAD, :] = jnp.zeros((nb, POOL_PAD, BRANCH_W), F32)
            conv_ext[:, 0:CONF_PAD, :] = jnp.zeros((nb, CONF_PAD, BRANCH_W), F32)
            sc_ext[:, 0:SCONV_PAD, :] = jnp.zeros((nb, SCONV_PAD, BRANCH_W), F32)

    x = x_ref[...]
    xb_buf[...] = _rms(x, n1g[...]).astype(BF16)

    def proj(lo, hi):
        assert lo % COL_TILE == 0 and hi == lo + COL_TILE
        return jnp.dot(xb_buf[...], w_in[lo // COL_TILE], preferred_element_type=F32)

    a_pool = proj(0, 256)
    pool_ext[:, POOL_PAD:, :] = a_pool.reshape(nb, t, BRANCH_W)
    glu = proj(256, 512) * _sigmoid(proj(512, 768))
    conv_ext[:, CONF_PAD:, :] = glu.reshape(nb, t, BRANCH_W)
    z = proj(1024, 1280) * proj(1280, 1536)
    sc_ext[:, SCONV_PAD:, :] = z.reshape(nb, t, BRANCH_W)

    pl_len = POOL_PAD + t
    sum_b[:, 8:pl_len, :] = pool_ext[:, 8:pl_len, :] + pool_ext[:, 7:pl_len - 1, :]
    sum_a[:, 16:pl_len, :] = sum_b[:, 16:pl_len, :] + sum_b[:, 14:pl_len - 2, :]
    sum_b[:, 24:pl_len, :] = sum_a[:, 24:pl_len, :] + sum_a[:, 20:pl_len - 4, :]
    for sh in range(SUBLANES):
        n_rows = t + SUBLANES * ((CONF_WIDTH - 1 - sh) // SUBLANES)
        first = CONF_PAD - CONF_HIST + sh
        conv_sh[sh, :, 0:n_rows, :] = conv_ext[:, first:first + n_rows, :]

    def window_block(bs, t0, tb, order_zero):
        bb = bs.stop - bs.start
        shp = (bb, tb, BRANCH_W)
        lane = lax.broadcasted_iota(jnp.int32, shp, 2)
        pos = seq_pos0 + t0 + lax.broadcasted_iota(jnp.int32, shp, 1)
        r0 = POOL_PAD + t0
        cur = pool_ext[bs, r0:r0 + tb, :]
        s2 = cur + pool_ext[bs, r0 - 1:r0 - 1 + tb, :]
        s4 = sum_a[bs, r0:r0 + tb, :]
        s8 = sum_b[bs, r0:r0 + tb, :]
        s16 = s8 + sum_b[bs, r0 - SUBLANES:r0 - SUBLANES + tb, :]
        win_sum = jnp.where(lane < 64, s2, jnp.where(lane < 128, s4, jnp.where(lane < 192, s8, s16)))
        win = jnp.where(lane < 64, 2, jnp.where(lane < 128, 4, jnp.where(lane < 192, 8, 16)))
        cnt = jnp.minimum(pos + 1, win).astype(F32)
        buf_a[bs, t0:t0 + tb, :] = win_sum / cnt - cur
        tiles = (ROW_BLOCK // SUBLANES, SUBLANES, BRANCH_W)
        acc = jnp.broadcast_to((cdb[...] + order_zero)[None], tiles)
        for k in range(CONF_WIDTH):
            q, sh = divmod(k, SUBLANES)
            r = t0 + SUBLANES * q
            acc = acc + conv_sh[sh, bs, r:r + tb, :].reshape(tiles) * cdw[k][None]
        buf_b[bs, t0:t0 + tb, :] = acc.reshape(shp)
        base = SCONV_PAD - SCONV_HIST + t0
        acc = sc_ext[bs, base:base + tb, :].reshape(tiles) * scw[0][None]
        for k in range(1, SCONV_WIDTH):
            acc = acc + sc_ext[bs, base + k:base + k + tb, :].reshape(tiles) * scw[k][None]
        buf_c[bs, t0:t0 + tb, :] = acc.reshape(shp)

    def gate_chunk(k):
        g0 = k * COL_TILE
        g = jnp.tanh(proj(GATE_COL0 + g0, GATE_COL0 + g0 + COL_TILE)) + 1.0
        gate_buf[:, g0:g0 + COL_TILE] = g
        return _zero_like_bits(g[0:SUBLANES, :])

    br = {}

    def item_windows(blks):
        def run(zero):
            for bs, t0, tb in blks:
                window_block(bs, t0, tb, zero)
        return run

    def item_state(zero):
        del zero
        pool_o[...] = pool_ext[:, POOL_PAD + t - POOL_HIST:POOL_PAD + t, :]
        conv_o[...] = conv_ext[:, CONF_PAD + t - CONF_HIST:CONF_PAD + t, :]
        sconv_o[...] = sc_ext[:, SCONV_PAD + t - SCONV_HIST:SCONV_PAD + t, :]
        if not cfg.has_state:
            pool_ext[:, 0:POOL_PAD, :] = pool_ext[:, t:t + POOL_PAD, :]
            conv_ext[:, 0:CONF_PAD, :] = conv_ext[:, t:t + CONF_PAD, :]
            sc_ext[:, 0:SCONV_PAD, :] = sc_ext[:, t:t + SCONV_PAD, :]

    def item_a(zero):
        pooled = buf_a[...].reshape(tm, BRANCH_W).astype(BF16)
        scale = pool_sc[...] + zero[0:1, :]
        br["a"] = (jnp.dot(pooled, pool_w[...], preferred_element_type=F32) * scale).astype(BF16)

    def item_b(zero):
        cb = _ln(buf_b[...].reshape(tm, BRANCH_W), clg[...] + zero[0:1, :], clb[...])
        br["b"] = (cb * _sigmoid(cb)).astype(BF16)

    def item_c(zero):
        br["c"] = (proj(768, 1024) * (buf_c[...].reshape(tm, BRANCH_W) + zero[0:1, :])).astype(BF16)

    def item_u(zero):
        br["u"] = _gelu_tanh(proj(1536, 1792) + zero[0:1, :])

    def item_v(zero):
        v = _ln(_gelu_tanh(proj(1792, 2048)), slg[...] + zero[0:1, :], slb[...])
        if cfg.has_state:
            v_o[...] = v.reshape(nb, t, BRANCH_W)
        else:
            v_o[...] = v[tm - cfg.v_rows:, :].reshape(1, cfg.v_rows, BRANCH_W)
        br["v"] = v

    def item_d(zero):
        head = lax.broadcasted_iota(jnp.int32, (CHUNK, BRANCH_W), 1) // SGU_HW
        bias = sbias[...] + jnp.concatenate([zero] * (CHUNK // SUBLANES), axis=0)
        mixed = []
        for j in range(tm // CHUNK):
            vj = br["v"][j * CHUNK:(j + 1) * CHUNK, :]
            stacked = jnp.concatenate([jnp.where(head == h, vj, 0.0) for h in range(SGU_HEADS)], axis=0)
            mixed.append(jnp.dot(wcat[...], stacked.astype(BF16), preferred_element_type=F32) + bias)
        br["d"] = (br["u"] * jnp.concatenate(mixed, axis=0)).astype(BF16)

    blocks = _row_blocks(nb, t)
    n_gate = N_BRANCH * D_MODEL // COL_TILE
    per_item = -(-len(blocks) // (n_gate // 2))
    items = [item_windows(blocks[i:i + per_item]) for i in range(0, len(blocks), per_item)]
    items += [item_state, item_c, item_b, item_u, item_v, item_a, item_d]
    assert len(items) <= n_gate
    zero = jnp.zeros((SUBLANES, BRANCH_W), F32)
    for k in range(n_gate):
        next_zero = gate_chunk(k)
        if k < len(items):
            items[k](zero)
        zero = next_zero

    n_ct = D_MODEL // COL_TILE
    brs = [br["a"], br["b"], br["c"], br["d"]]
    for c in range(n_ct):
        part = None
        for i in range(N_BRANCH):
            g0 = i * D_MODEL + c * COL_TILE
            term = gate_buf[:, g0:g0 + COL_TILE] * jnp.dot(
                brs[i], w_br[i * n_ct + c], preferred_element_type=F32)
            part = term if part is None else part + term
        xb_buf[:, c * COL_TILE:(c + 1) * COL_TILE] = part.astype(BF16)
    for c in range(n_ct):
        cs = slice(c * COL_TILE, (c + 1) * COL_TILE)
        h1_o[:, cs] = x_ref[:, cs] + jnp.dot(xb_buf[...], w_out[c], preferred_element_type=F32)

    xn2 = _rms(h1_o[...], n2g[...])
    x_hi = xn2.astype(BF16)
    xn2_o[...] = x_hi
    x_lo = (xn2 - x_hi.astype(F32)).astype(BF16)
    hi_both = jnp.dot(x_hi, w_rt[...], preferred_element_type=F32)
    lo_hi = jnp.dot(x_lo, w_rt[:, 0:LANES], preferred_element_type=F32)
    logits = (hi_both[:, 0:LANES] + (lo_hi + hi_both[:, LANES:2 * LANES])) + b_rt[...]
    lane = lax.broadcasted_iota(jnp.int32, (tm, LANES), 1)
    is_g = (lane >= ROUTE_GROUP_LANE0) & (lane < ROUTE_GROUP_LANE0 + N_GROUPS)
    glog = jnp.where(is_g, logits, NEG_BIG)
    gmax = jnp.max(glog, axis=-1, keepdims=True)
    lane_f = lane.astype(F32)
    no_lane = float(4 * LANES)
    gsel = jnp.min(jnp.where(glog == gmax, lane_f, no_lane), axis=-1, keepdims=True) - ROUTE_GROUP_LANE0
    pg = 1.0 / jnp.sum(jnp.where(is_g, jnp.exp(glog - gmax), 0.0), axis=-1, keepdims=True)
    in_grp = (lane < N_EXPERTS) & ((lane // EXPERTS_PER_GROUP) == gsel.astype(jnp.int32))
    el = jnp.where(in_grp, logits, NEG_BIG)
    m1 = jnp.max(el, axis=-1, keepdims=True)
    is_m1 = in_grp & (el == m1)
    i1 = jnp.min(jnp.where(is_m1, lane_f, no_lane), axis=-1, keepdims=True)
    n_m1 = jnp.sum(jnp.where(is_m1, 1.0, 0.0), axis=-1, keepdims=True)
    below = jnp.max(jnp.where(in_grp & (el < m1), logits, NEG_BIG), axis=-1, keepdims=True)
    m2 = jnp.where(n_m1 >= 2.0, m1, below)
    i2 = jnp.min(jnp.where(in_grp & (el == m2) & (lane_f != i1), lane_f, no_lane), axis=-1, keepdims=True)
    e21 = jnp.exp(m2 - m1)
    w1 = pg / (1.0 + e21)
    w2 = pg * e21 / (1.0 + e21)
    route_o[...] = jnp.where(lane == 0, i1, jnp.where(lane == 1, i2,
                             jnp.where(lane == 2, w1, jnp.where(lane == 3, w2, 0.0))))


def _layer_spec(shape, layer):
    nd = len(shape)
    return pl.BlockSpec((1,) + shape[1:], lambda *_: (layer,) + (0,) * (nd - 1), pipeline_mode=pl.Buffered(1))


def _mixer_call(cfg, n_seq, n_rows_total, x2d, x_tile_off, states, weights, big_weights, aliased):
    nb, t = cfg.nb, cfg.t
    tm = nb * t
    if cfg.has_state:
        grid = (n_seq // nb,)
        tok = lambda i: (x_tile_off + i, 0)
        out_tok = lambda i: (cfg.row_off + i, 0)
        seq3 = lambda i: (i, 0, 0)
    else:
        grid = (n_seq, 2048 // t)
        nt = grid[1]
        tok = lambda b, c: (x_tile_off + b * nt + c, 0)
        out_tok = lambda b, c: (cfg.row_off + b * nt + c, 0)
        seq3 = lambda b, c: (b, 0, 0)

    in_specs = [pl.BlockSpec((tm, D_MODEL), tok)]
    args = [x2d]
    if cfg.has_state:
        for s in states:
            in_specs.append(pl.BlockSpec((1, nb) + s.shape[2:], lambda i: (cfg.layer, i, 0, 0)))
            args.append(s)
    for w in weights:
        in_specs.append(_layer_spec(w.shape, cfg.layer))
        args.append(w)
    for w in big_weights:
        in_specs.append(pl.BlockSpec(memory_space=pl.ANY))
        args.append(w)
    io_alias = {}
    if cfg.has_state:
        for k, a in enumerate(aliased):
            in_specs.append(pl.BlockSpec(memory_space=pl.ANY))
            io_alias[len(args)] = k
            args.append(a)

    out_shape = [
        jax.ShapeDtypeStruct((n_rows_total, D_MODEL), F32),
        jax.ShapeDtypeStruct((n_rows_total, LANES), F32),
        jax.ShapeDtypeStruct((n_rows_total, D_MODEL), BF16),
        jax.ShapeDtypeStruct((n_seq, POOL_HIST, BRANCH_W), F32),
        jax.ShapeDtypeStruct((n_seq, CONF_HIST, BRANCH_W), F32),
        jax.ShapeDtypeStruct((n_seq, SCONV_HIST, BRANCH_W), F32),
        jax.ShapeDtypeStruct((n_seq, cfg.v_rows, BRANCH_W), F32),
    ]
    out_specs = [
        pl.BlockSpec((tm, D_MODEL), out_tok),
        pl.BlockSpec((tm, LANES), out_tok),
        pl.BlockSpec((tm, D_MODEL), out_tok),
        pl.BlockSpec((nb, POOL_HIST, BRANCH_W), seq3),
        pl.BlockSpec((nb, CONF_HIST, BRANCH_W), seq3),
        pl.BlockSpec((nb, SCONV_HIST, BRANCH_W), seq3),
        pl.BlockSpec((nb, cfg.v_rows, BRANCH_W), seq3),
    ]
    scratch = [
        pltpu.VMEM((nb, POOL_PAD + t, BRANCH_W), F32),
        pltpu.VMEM((nb, POOL_PAD + t, BRANCH_W), F32),
        pltpu.VMEM((nb, POOL_PAD + t, BRANCH_W), F32),
        pltpu.VMEM((nb, CONF_PAD + t, BRANCH_W), F32),
        pltpu.VMEM((SUBLANES, nb, t + CONF_PAD - SUBLANES, BRANCH_W), F32),
        pltpu.VMEM((nb, SCONV_PAD + t, BRANCH_W), F32),
        pltpu.VMEM((nb, t, BRANCH_W), F32),
        pltpu.VMEM((nb, t, BRANCH_W), F32),
        pltpu.VMEM((nb, t, BRANCH_W), F32),
        pltpu.VMEM((tm, N_BRANCH * D_MODEL), F32),
        pltpu.VMEM((tm, D_MODEL), BF16),
        pltpu.VMEM((IN_COLS // COL_TILE, D_MODEL, COL_TILE), BF16),
        pltpu.VMEM((N_BRANCH * D_MODEL // COL_TILE, BRANCH_W, COL_TILE), BF16),
        pltpu.VMEM((D_MODEL // COL_TILE, D_MODEL, COL_TILE), BF16),
        pltpu.VMEM((2, D_MODEL, COL_TILE), F32),
        pltpu.VMEM((2, BRANCH_W, D_MODEL), F32),
        pltpu.SemaphoreType.DMA((2,)),
    ]
    return pl.pallas_call(
        functools.partial(_mixer_kernel, cfg),
        grid=grid,
        in_specs=in_specs,
        out_specs=out_specs,
        out_shape=out_shape,
        scratch_shapes=scratch,
        input_output_aliases=io_alias,
        compiler_params=pltpu.CompilerParams(
            dimension_semantics=("arbitrary",) * len(grid), vmem_limit_bytes=VMEM_LIMIT),
        name="mixer_sample" if cfg.has_state else "mixer_prompt",
    )(*args)


def _sort_kernel(xn_ref, route_ref, xl_o, pos_o, nch_o):
    t = SORT_TILE
    xn = xn_ref[...]
    r = route_ref[...]
    lane_i = lax.broadcasted_iota(jnp.int32, (t, LANES), 1)
    lane = lane_i.astype(F32)
    hit1 = lane == r[:, 0:1]
    hit2 = lane == r[:, 1:2]
    onehot = jnp.where(hit1 | hit2, 1.0, 0.0)
    row = lax.broadcasted_iota(jnp.int32, (t, t), 0)
    col = lax.broadcasted_iota(jnp.int32, (t, t), 1)
    strict_lower = jnp.where(col < row, 1.0, 0.0).astype(BF16)
    before = jnp.dot(strict_lower, onehot.astype(BF16), preferred_element_type=F32)
    cnt = jnp.sum(onehot, axis=0, keepdims=True)
    nch = jnp.floor((cnt + (SEG_PAD - 1)) * (1.0 / SEG_PAD))
    er = lax.broadcasted_iota(jnp.int32, (LANES, LANES), 0)
    ec = lax.broadcasted_iota(jnp.int32, (LANES, LANES), 1)
    strict_upper = jnp.where(er < ec, 1.0, 0.0).astype(BF16)
    nch8 = jnp.broadcast_to(nch, (SUBLANES, LANES)).astype(BF16)
    seg0 = jnp.dot(nch8, strict_upper, preferred_element_type=F32)[0:1, :] * SEG_PAD
    where = seg0 + before
    pos0 = jnp.sum(jnp.where(hit1, where, 0.0), axis=-1, keepdims=True)
    pos1 = jnp.sum(jnp.where(hit2, where, 0.0), axis=-1, keepdims=True)
    pos = jnp.where(lane_i == 0, pos0, jnp.where(lane_i == 1, pos1, -1.0))
    pos_o[...] = pos
    nch_o[...] = jnp.broadcast_to(nch, (1, SUBLANES, LANES))
    pos_t = jnp.transpose(pos)
    p0 = pos_t[0:1, :]
    p1 = pos_t[1:2, :]
    used_rows = jnp.sum(nch) * SEG_PAD

    def perm_block(blk):
        dst = (lax.broadcasted_iota(jnp.int32, (PERM_BLOCK, t), 0) + blk * PERM_BLOCK).astype(F32)
        perm = jnp.where((dst == p0) | (dst == p1), 1.0, 0.0).astype(BF16)
        xl_o[0, blk * PERM_BLOCK:(blk + 1) * PERM_BLOCK, :] = jnp.dot(
            perm, xn, preferred_element_type=F32).astype(BF16)

    for blk in range(N_PERM_BLOCKS - 1):
        perm_block(blk)
    pl.when(used_rows > (N_PERM_BLOCKS - 1) * PERM_BLOCK)(functools.partial(perm_block, N_PERM_BLOCKS - 1))


def _sort_call(xn2, route):
    n = xn2.shape[0]
    nt = n // SORT_TILE
    return pl.pallas_call(
        _sort_kernel,
        grid=(nt,),
        in_specs=[pl.BlockSpec((SORT_TILE, D_MODEL), lambda i: (i, 0)),
                  pl.BlockSpec((SORT_TILE, LANES), lambda i: (i, 0))],
        out_specs=[pl.BlockSpec((1, LOCAL_ROWS, D_MODEL), lambda i: (i, 0, 0)),
                   pl.BlockSpec((SORT_TILE, LANES), lambda i: (i, 0)),
                   pl.BlockSpec((1, SUBLANES, LANES), lambda i: (i, 0, 0))],
        out_shape=[jax.ShapeDtypeStruct((nt + SPARE_BLOCKS, LOCAL_ROWS, D_MODEL), BF16),
                   jax.ShapeDtypeStruct((n, LANES), F32),
                   jax.ShapeDtypeStruct((nt, SUBLANES, LANES), F32)],
        compiler_params=pltpu.CompilerParams(dimension_semantics=("arbitrary",), vmem_limit_bytes=VMEM_LIMIT),
        name="moe_sort",
    )(xn2, route)


def _chunk_copy(src, src_row, dst, dst_row, sem):
    return pltpu.make_async_copy(src.at[pl.ds(src_row, SEG_PAD), :], dst.at[pl.ds(dst_row, SEG_PAD), :], sem)


def _expert_kernel(trash_row0, se_ref, nv_ref, src_ref, dst_ref, xl_in, wg, wu, wd, xl_io, *scratch):
    del se_ref, xl_in
    ns = EXP_SLOTS
    xbuf, ybuf, (sem_in, sem_out) = scratch[:ns], scratch[ns:2 * ns], scratch[2 * ns:]
    s = pl.program_id(0)
    nv = nv_ref[0]

    def gather(step, sl):
        return [_chunk_copy(xl_io, pl.multiple_of(src_ref[step * CHUNKS_PER_STEP + j], SEG_PAD),
                            xbuf[sl], j * SEG_PAD, sem_in.at[sl]) for j in range(CHUNKS_PER_STEP)]

    def write_back(step, sl):
        return [_chunk_copy(ybuf[sl], j * SEG_PAD, xl_io,
                            pl.multiple_of(dst_ref[step * CHUNKS_PER_STEP + j], SEG_PAD), sem_out.at[sl])
                for j in range(CHUNKS_PER_STEP)]

    def spare_write(sl):
        return [_chunk_copy(ybuf[sl], j * SEG_PAD, xl_io, trash_row0 + sl * EXP_TILE + j * SEG_PAD,
                            sem_out.at[sl]) for j in range(CHUNKS_PER_STEP)]

    @pl.when(s == 0)
    def _():
        for ahead in range(ns - 1):
            for c in gather(ahead, ahead):
                c.start()
        for sl in range(ns):
            ybuf[sl][...] = jnp.zeros_like(ybuf[sl])
            for c in spare_write(sl):
                c.start()

    def live_step(sl):
        for c in gather(s + ns - 1, (sl + ns - 1) % ns):
            c.start()
        for c in gather(s, sl):
            c.wait()
        xb = xbuf[sl][...]
        a = jnp.dot(xb, wg[0, 0].astype(BF16), preferred_element_type=F32)
        b = jnp.dot(xb, wu[0, 0].astype(BF16), preferred_element_type=F32)
        hh = (a * _sigmoid(a) * b).astype(BF16)
        y = jnp.dot(hh, wd[0, 0].astype(BF16), preferred_element_type=F32).astype(BF16)
        for c in write_back(s, sl):
            c.wait()
        ybuf[sl][...] = y
        for c in write_back(s, sl):
            c.start()

    def drain_step(sl):
        for ahead in range(ns - 1):
            for c in gather(s + ahead, (sl + ahead) % ns):
                c.wait()
        for every in range(ns):
            for c in spare_write(every):
                c.wait()

    for sl in range(ns):
        pl.when((s < nv) & (s % ns == sl))(functools.partial(live_step, sl))
        pl.when((s == nv) & (s % ns == sl))(functools.partial(drain_step, sl))


def _expert_call(xl, trash_row0, step_expert, n_valid, chunk_src, chunk_dst, layer, w_gate, w_up, w_down):
    n_steps = step_expert.shape[0] - (EXP_SLOTS - 1)
    wmap = lambda s, se, *_: (layer, se[s], 0, 0)
    grid_spec = pltpu.PrefetchScalarGridSpec(
        num_scalar_prefetch=4,
        grid=(n_steps,),
        in_specs=[pl.BlockSpec(memory_space=pl.ANY),
                  pl.BlockSpec((1, 1, D_MODEL, D_EXPERT), wmap),
                  pl.BlockSpec((1, 1, D_MODEL, D_EXPERT), wmap),
                  pl.BlockSpec((1, 1, D_EXPERT, D_MODEL), wmap)],
        out_specs=pl.BlockSpec(memory_space=pl.ANY),
        scratch_shapes=[pltpu.VMEM((EXP_TILE, D_MODEL), BF16) for _ in range(2 * EXP_SLOTS)] + [
                        pltpu.SemaphoreType.DMA((EXP_SLOTS,)), pltpu.SemaphoreType.DMA((EXP_SLOTS,))],
    )
    return pl.pallas_call(
        functools.partial(_expert_kernel, trash_row0),
        grid_spec=grid_spec,
        out_shape=jax.ShapeDtypeStruct(xl.shape, BF16),
        input_output_aliases={4: 0},
        compiler_params=pltpu.CompilerParams(dimension_semantics=("arbitrary",), has_side_effects=True),
        name="moe_experts",
    )(step_expert, n_valid, chunk_src, chunk_dst, xl, w_gate, w_up, w_down)


def _combine_kernel(final_norm, n_first, h1_ref, route_ref, pos_ref, fg, yl_ref, *outs):
    t = SORT_TILE
    r = route_ref[...]
    p = pos_ref[...]
    w1, w2 = r[:, 2:3], r[:, 3:4]
    pos0, pos1 = p[:, 0:1], p[:, 1:2]
    *outs, acc_ref = outs

    def gathered(blk):
        src = (lax.broadcasted_iota(jnp.int32, (t, PERM_BLOCK), 1) + blk * PERM_BLOCK).astype(F32)
        pw = jnp.where(src == pos0, w1, 0.0) + jnp.where(src == pos1, w2, 0.0)
        return jnp.dot(pw.astype(BF16), yl_ref[0, blk * PERM_BLOCK:(blk + 1) * PERM_BLOCK, :],
                       preferred_element_type=F32)

    acc = h1_ref[...]
    for blk in range(N_PERM_BLOCKS - 1):
        acc = acc + gathered(blk)
    acc_ref[...] = acc

    @pl.when(jnp.max(p) >= (N_PERM_BLOCKS - 1) * PERM_BLOCK)
    def _():
        acc_ref[...] += gathered(N_PERM_BLOCKS - 1)

    if not final_norm:
        outs[0][...] = acc_ref[...]
        return
    y = _rms(acc_ref[...], fg[...])
    i = pl.program_id(0)

    @pl.when(i < n_first)
    def _():
        outs[0][...] = y

    @pl.when(i >= n_first)
    def _():
        outs[1][...] = y


def _combine_call(h1, route, pos, yl, final_g, final_norm, n_first_rows):
    n = h1.shape[0]
    nt = n // SORT_TILE
    n_first = n_first_rows // SORT_TILE
    tok = lambda i: (i, 0)
    if final_norm:
        out_shape = [jax.ShapeDtypeStruct((n_first_rows, D_MODEL), F32),
                     jax.ShapeDtypeStruct((n - n_first_rows, D_MODEL), F32)]
        out_specs = [pl.BlockSpec((SORT_TILE, D_MODEL), lambda i: (jnp.minimum(i, n_first - 1), 0)),
                     pl.BlockSpec((SORT_TILE, D_MODEL), lambda i: (jnp.maximum(i - n_first, 0), 0))]
    else:
        out_shape = [jax.ShapeDtypeStruct((n, D_MODEL), F32)]
        out_specs = [pl.BlockSpec((SORT_TILE, D_MODEL), tok)]
    return pl.pallas_call(
        functools.partial(_combine_kernel, final_norm, n_first),
        grid=(nt,),
        in_specs=[pl.BlockSpec((SORT_TILE, D_MODEL), tok),
                  pl.BlockSpec((SORT_TILE, LANES), tok),
                  pl.BlockSpec((SORT_TILE, LANES), tok),
                  pl.BlockSpec((1, D_MODEL), lambda i: (0, 0)),
                  pl.BlockSpec((1, LOCAL_ROWS, D_MODEL), lambda i: (i, 0, 0))],
        out_specs=out_specs,
        out_shape=out_shape,
        scratch_shapes=[pltpu.VMEM((SORT_TILE, D_MODEL), F32)],
        compiler_params=pltpu.CompilerParams(dimension_semantics=("arbitrary",), vmem_limit_bytes=VMEM_LIMIT),
        name="moe_combine",
    )(h1, route, pos, final_g, yl)


def _expert_tables(nch, n_steps, trash_row0):
    nt = nch.shape[0]
    cps = CHUNKS_PER_STEP
    seg_row0 = (jnp.cumsum(nch, axis=1) - nch) * SEG_PAD
    first = jnp.cumsum(nch, axis=0) - nch
    tot = jnp.sum(nch, axis=0)
    steps = (tot + cps - 1) // cps
    step_end = jnp.cumsum(steps)
    n_valid = step_end[-1:]
    s_ids = jnp.arange(n_steps, dtype=jnp.int32)
    step_expert = jnp.minimum(jnp.sum((step_end[None, :] <= s_ids[:, None]).astype(jnp.int32), axis=1),
                              N_EXPERTS - 1)
    sel = (step_expert[:, None] == jnp.arange(N_EXPERTS, dtype=jnp.int32)[None, :]).astype(jnp.int32)
    step0 = sel @ (step_end - steps)
    tot_s = sel @ tot
    first_s = sel @ first.T
    row0_s = sel @ seg_row0.T
    k = (s_ids - step0)[:, None] * cps + jnp.arange(cps, dtype=jnp.int32)[None, :]
    ok = (k < tot_s[:, None]) & (s_ids < n_valid[0])[:, None]
    tile = jnp.sum((first_s[:, None, :] <= k[:, :, None]).astype(jnp.int32), axis=2) - 1
    tsel = (tile[:, :, None] == jnp.arange(nt, dtype=jnp.int32)[None, None, :]).astype(jnp.int32)
    first_k = jnp.sum(tsel * first_s[:, None, :], axis=2)
    row0_k = jnp.sum(tsel * row0_s[:, None, :], axis=2)
    row = tile * LOCAL_ROWS + row0_k + (k - first_k) * SEG_PAD
    live = (s_ids < n_valid[0])[:, None]
    src = jnp.where(live, jnp.where(ok, row, row[:, 0:1]), 0)
    spare = trash_row0 + (s_ids % EXP_SLOTS)[:, None] * EXP_TILE + jnp.arange(cps, dtype=jnp.int32)[None, :] * SEG_PAD
    dst = jnp.where(ok, row, spare)
    return (step_expert, n_valid.astype(jnp.int32), src.reshape(-1).astype(jnp.int32),
            dst.reshape(-1).astype(jnp.int32))


def _moe(h1, route, xn2, layer, w_gate, w_up, w_down, final_g, final_norm, n_first_rows):
    n = h1.shape[0]
    nt = n // SORT_TILE
    xl, pos, nch = _sort_call(xn2, route)
    nch = nch[:, 0, :N_EXPERTS].astype(jnp.int32)
    max_chunks = (2 * n) // SEG_PAD + nt * N_EXPERTS
    n_steps = max_chunks // CHUNKS_PER_STEP + N_EXPERTS + 1
    trash_row0 = nt * LOCAL_ROWS
    step_expert, n_valid, chunk_src, chunk_dst = _expert_tables(nch, n_steps + EXP_SLOTS - 1, trash_row0)
    yl = _expert_call(xl.reshape((nt + SPARE_BLOCKS) * LOCAL_ROWS, D_MODEL), trash_row0, step_expert, n_valid,
                      chunk_src, chunk_dst, layer, w_gate, w_up, w_down)
    return _combine_call(h1, route, pos, yl.reshape(nt + SPARE_BLOCKS, LOCAL_ROWS, D_MODEL), final_g, final_norm,
                         n_first_rows)


def kernel(x_prompt, x_sample, state_pool, state_conv, state_sconv, norm1_g, w_in, pool_w, pool_scale, conf_dw, conf_dw_b, conf_ln_g, conf_ln_b, sconv_w, sgu_ln_g, sgu_ln_b, sgu_ws, sgu_b, w_branch, w_out, norm2_g, router_g, router_g_b, router_e, router_e_b, w_gate, w_up, w_down, final_g):
    depth = w_in.shape[0]
    bp, seq, _ = x_prompt.shape
    bs, dseq, _ = x_sample.shape
    n_p, n_s = bp * seq, bs * dseq
    n = n_p + n_s
    past_len = 16384
    assert seq % MIX_TILE == 0 and n_s % MIX_TILE_S == 0 and n_p % MIX_TILE_S == 0 and dseq == SUBLANES
    assert MIX_TILE % CHUNK == 0 and MIX_TILE_S % CHUNK == 0

    cfg_p = MixCfg(layer=0, nb=1, t=MIX_TILE, has_state=False, start_pos=0, row_off=0, v_rows=CHUNK)
    cfg_s = MixCfg(layer=0, nb=MIX_TILE_S // dseq, t=dseq, has_state=True, start_pos=past_len,
                   row_off=n_p // MIX_TILE_S, v_rows=dseq)
    big_weights = (w_in, w_branch, w_out)

    row = lambda a: a[:, None, :]
    rep8 = lambda a: jnp.broadcast_to(a[..., None, :], a.shape[:-1] + (SUBLANES, a.shape[-1]))
    tril = jnp.tril(jnp.ones((CHUNK, CHUNK), F32))
    eye_blk = jnp.kron(jnp.eye(CHUNK // dseq, dtype=F32), jnp.ones((dseq, dseq), F32))
    cat = lambda w: jnp.concatenate([w[:, h] for h in range(SGU_HEADS)], axis=2).astype(BF16)
    wcat_p = cat(sgu_ws * tril)
    wcat_s = cat(jnp.tile(sgu_ws[:, :, :dseq, :dseq], (1, 1, CHUNK // dseq, CHUNK // dseq)) * (tril * eye_blk))
    bias_p = jnp.repeat(jnp.swapaxes(sgu_b, 1, 2), SGU_HW, axis=2)
    bias_s = jnp.tile(jnp.repeat(jnp.swapaxes(sgu_b[:, :, :dseq], 1, 2), SGU_HW, axis=2), (1, CHUNK // dseq, 1))
    lane_pad = LANES - N_EXPERTS - N_GROUPS
    w_rt32 = jnp.pad(jnp.concatenate([router_e, router_g], axis=2), ((0, 0), (0, 0), (0, lane_pad)))
    w_rt_hi = w_rt32.astype(BF16)
    w_rt = jnp.concatenate([w_rt_hi, (w_rt32 - w_rt_hi.astype(F32)).astype(BF16)], axis=2)
    b_rt = row(jnp.pad(jnp.concatenate([router_e_b, router_g_b], axis=1), ((0, 0), (0, lane_pad))))
    same_group = jnp.eye(len(POOL_WINDOWS), dtype=F32)[None, :, None, :, None]
    pool_bd = (pool_w[:, :, :, None, :] * same_group).reshape(depth, BRANCH_W, BRANCH_W).astype(BF16)

    def weights(wcat, sbias):
        return [row(norm1_g), pool_bd, row(pool_scale), rep8(conf_dw), rep8(conf_dw_b), row(conf_ln_g),
                row(conf_ln_b), rep8(sconv_w), row(sgu_ln_g), row(sgu_ln_b), wcat, sbias, row(norm2_g), w_rt, b_rt]

    weights_p, weights_s = weights(wcat_p, bias_p), weights(wcat_s, bias_s)
    final_row = final_g.reshape(1, -1)

    h_p, h_s = x_prompt.reshape(n_p, D_MODEL), x_sample.reshape(n_s, D_MODEL)
    off_p, off_s = 0, 0
    states_out = []
    for l in range(depth):
        h1, route, xn2, pool_p, conv_p, sconv_p, v_p = _mixer_call(
            cfg_p._replace(layer=l), bp, n, h_p, off_p, None, weights_p, big_weights, None)
        h1, route, xn2, pool_s, conv_s, sconv_s, v_s = _mixer_call(
            cfg_s._replace(layer=l), bs, n, h_s, off_s, (state_pool, state_conv, state_sconv),
            weights_s, big_weights, (h1, route, xn2))
        states_out.append((pool_p, conv_p, sconv_p, v_p, pool_s, conv_s, sconv_s, v_s))

        outs = _moe(h1, route, xn2, l, w_gate, w_up, w_down, final_row, l == depth - 1, n_p)
        h_p = h_s = outs[0]
        off_p, off_s = 0, n_p // MIX_TILE_S

    y_prompt = outs[0].reshape(bp, seq, D_MODEL)
    y_sample = outs[1].reshape(bs, dseq, D_MODEL)
    st = [jnp.stack([s[k] for s in states_out]) for k in range(8)]
    return (y_prompt, y_sample, st[0], st[1], st[2], st[3], st[4], st[5], st[6], st[7])
```

```python
import functools
from typing import NamedTuple

import jax
import jax.numpy as jnp
from jax import lax
from jax.experimental import pallas as pl
from jax.experimental.pallas import tpu as pltpu

F32 = jnp.float32
BF16 = jnp.bfloat16

D_MODEL = 1024
BRANCH_W = 256
N_BRANCH = 4
POOL_WINDOWS = (2, 4, 8, 16)
POOL_HIST = 15
CONF_WIDTH = 31
CONF_HIST = 30
SCONV_WIDTH = 3
SCONV_HIST = 2
CHUNK = 128
SGU_HEADS = 4
SGU_HW = 64
N_GROUPS = 4
EXPERTS_PER_GROUP = 8
N_EXPERTS = 32
D_EXPERT = 256
RMS_EPS = 1e-6
LN_EPS = 1e-5
IN_COLS = 6144
GATE_COL0 = 2048

COL_TILE = 256
LANES = 128
SUBLANES = 8
POOL_PAD = 32
CONF_PAD = 32
SCONV_PAD = 8
ROW_BLOCK = 32
ROUTE_GROUP_LANE0 = 32
NEG_BIG = -3.0e38

MIX_TILE = 512
MIX_TILE_S = 256
SORT_TILE = 512
SEG_PAD = 16
PERM_BLOCK = 256
N_PERM_BLOCKS = -(-(2 * SORT_TILE + N_EXPERTS * (SEG_PAD - 1)) // PERM_BLOCK)
LOCAL_ROWS = N_PERM_BLOCKS * PERM_BLOCK
EXP_TILE = 512
CHUNKS_PER_STEP = EXP_TILE // SEG_PAD
EXP_SLOTS = 3
SPARE_BLOCKS = -(-EXP_SLOTS * EXP_TILE // LOCAL_ROWS)
VMEM_LIMIT = 56 * 1024 * 1024


class MixCfg(NamedTuple):
    layer: int
    nb: int
    t: int
    has_state: bool
    start_pos: int
    row_off: int
    v_rows: int


def _rms(x, g):
    return x * lax.rsqrt(jnp.mean(x * x, axis=-1, keepdims=True) + RMS_EPS) * g


def _ln(x, g, b):
    mu = jnp.mean(x, axis=-1, keepdims=True)
    xc = x - mu
    return xc * lax.rsqrt(jnp.mean(xc * xc, axis=-1, keepdims=True) + LN_EPS) * g + b


def _sigmoid(x):
    return 0.5 * jnp.tanh(0.5 * x) + 0.5


def _zero_like_bits(x):
    bits = lax.bitcast_convert_type(x, jnp.uint32)
    return ((bits >> 16) >> 16).astype(jnp.int32).astype(F32)


def _gelu_tanh(x):
    return 0.5 * x * (1.0 + jnp.tanh(0.7978845608028654 * (x + 0.044715 * (x * x * x))))


def _row_blocks(nb, t):
    if t >= ROW_BLOCK:
        return [(slice(b, b + 1), t0, ROW_BLOCK) for b in range(nb) for t0 in range(0, t, ROW_BLOCK)]
    bb = ROW_BLOCK // t
    return [(slice(b0, b0 + bb), 0, t) for b0 in range(0, nb, bb)]


def _load_big_weights(layer, w_in_hbm, w_br_hbm, w_out_hbm, w_in, w_br, w_out, stage, stage_b, sem):
    n_in, n_out = IN_COLS // COL_TILE, D_MODEL // COL_TILE
    jobs = [("in", j) for j in range(n_in)] + [("out", j) for j in range(n_out)] + [("br", i) for i in range(N_BRANCH)]

    def copy(job, slot):
        kind, j = job
        if kind == "br":
            return pltpu.make_async_copy(w_br_hbm.at[layer, j], stage_b.at[slot], sem.at[slot])
        src = w_in_hbm if kind == "in" else w_out_hbm
        return pltpu.make_async_copy(src.at[layer, :, pl.ds(j * COL_TILE, COL_TILE)], stage.at[slot], sem.at[slot])

    copy(jobs[0], 0).start()
    for n, job in enumerate(jobs):
        slot = n % 2
        if n + 1 < len(jobs):
            copy(jobs[n + 1], 1 - slot).start()
        copy(job, slot).wait()
        kind, j = job
        if kind == "in":
            tile = stage[slot]
            w_in[j] = (tile * 0.5 if j * COL_TILE >= GATE_COL0 else tile).astype(BF16)
        elif kind == "out":
            w_out[j] = stage[slot].astype(BF16)
        else:
            for c in range(n_out):
                w_br[j * n_out + c] = (stage_b[slot, :, c * COL_TILE:(c + 1) * COL_TILE] * 0.5).astype(BF16)


def _mixer_kernel(cfg, *refs):
    nb, t, tm = cfg.nb, cfg.t, cfg.nb * cfg.t
    refs = list(refs)
    x_ref = refs.pop(0)
    if cfg.has_state:
        pool_st, conv_st, sconv_st = (refs.pop(0).at[0] for _ in range(3))
    (n1g, pool_w, pool_sc, cdw, cdb, clg, clb, scw, slg, slb, wcat, sbias, n2g, w_rt, b_rt) = (
        r.at[0] for r in refs[:15])
    w_in_hbm, w_br_hbm, w_out_hbm = refs[15:18]
    refs = refs[18:]
    if cfg.has_state:
        refs = refs[3:]
    h1_o, route_o, xn2_o, pool_o, conv_o, sconv_o, v_o = refs[:7]
    (pool_ext, sum_a, sum_b, conv_ext, conv_sh, sc_ext, buf_a, buf_b, buf_c, gate_buf, xb_buf,
     w_in, w_br, w_out, stage, stage_b, w_sem) = refs[7:]

    first_step = pl.program_id(0) == 0
    if not cfg.has_state:
        first_step = first_step & (pl.program_id(1) == 0)
    pl.when(first_step)(functools.partial(
        _load_big_weights, cfg.layer, w_in_hbm, w_br_hbm, w_out_hbm, w_in, w_br, w_out, stage, stage_b, w_sem))

    if cfg.has_state:
        seq_pos0 = cfg.start_pos
        pool_ext[:, 0:POOL_PAD - SUBLANES, :] = jnp.zeros((nb, POOL_PAD - SUBLANES, BRANCH_W), F32)
        pool_ext[:, POOL_PAD - POOL_HIST:POOL_PAD, :] = pool_st[...]
        conv_ext[:, CONF_PAD - CONF_HIST:CONF_PAD, :] = conv_st[...]
        sc_ext[:, SCONV_PAD - SCONV_HIST:SCONV_PAD, :] = sconv_st[...]
    else:
        c = pl.program_id(1)
        seq_pos0 = cfg.start_pos + c * t

        @pl.when(c == 0)
        def _():
            pool_ext[:, 0:POOL_PAD, :] = jnp.zeros((nb, POOL_PAD, BRANCH_W), F32)
            conv_ext[:, 0:CONF_PAD, :] = jnp.zeros((nb, CONF_PAD, BRANCH_W), F32)
            sc_ext[:, 0:SCONV_PAD, :] = jnp.zeros((nb, SCONV_PAD, BRANCH_W), F32)

    x = x_ref[...]
    xb_buf[...] = _rms(x, n1g[...]).astype(BF16)

    def proj(lo, hi):
        assert lo % COL_TILE == 0 and hi == lo + COL_TILE
        return jnp.dot(xb_buf[...], w_in[lo // COL_TILE], preferred_element_type=F32)

    a_pool = proj(0, 256)
    pool_ext[:, POOL_PAD:, :] = a_pool.reshape(nb, t, BRANCH_W)
    glu = proj(256, 512) * _sigmoid(proj(512, 768))
    conv_ext[:, CONF_PAD:, :] = glu.reshape(nb, t, BRANCH_W)
    z = proj(1024, 1280) * proj(1280, 1536)
    sc_ext[:, SCONV_PAD:, :] = z.reshape(nb, t, BRANCH_W)

    pl_len = POOL_PAD + t
    sum_b[:, 8:pl_len, :] = pool_ext[:, 8:pl_len, :] + pool_ext[:, 7:pl_len - 1, :]
    sum_a[:, 16:pl_len, :] = sum_b[:, 16:pl_len, :] + sum_b[:, 14:pl_len - 2, :]
    sum_b[:, 24:pl_len, :] = sum_a[:, 24:pl_len, :] + sum_a[:, 20:pl_len - 4, :]
    for sh in range(SUBLANES):
        n_rows = t + SUBLANES * ((CONF_WIDTH - 1 - sh) // SUBLANES)
        first = CONF_PAD - CONF_HIST + sh
        conv_sh[sh, :, 0:n_rows, :] = conv_ext[:, first:first + n_rows, :]

    def window_block(bs, t0, tb, order_zero):
        bb = bs.stop - bs.start
        shp = (bb, tb, BRANCH_W)
        lane = lax.broadcasted_iota(jnp.int32, shp, 2)
        pos = seq_pos0 + t0 + lax.broadcasted_iota(jnp.int32, shp, 1)
        r0 = POOL_PAD + t0
        cur = pool_ext[bs, r0:r0 + tb, :]
        s2 = cur + pool_ext[bs, r0 - 1:r0 - 1 + tb, :]
        s4 = sum_a[bs, r0:r0 + tb, :]
        s8 = sum_b[bs, r0:r0 + tb, :]
        s16 = s8 + sum_b[bs, r0 - SUBLANES:r0 - SUBLANES + tb, :]
        win_sum = jnp.where(lane < 64, s2, jnp.where(lane < 128, s4, jnp.where(lane < 192, s8, s16)))
        win = jnp.where(lane < 64, 2, jnp.where(lane < 128, 4, jnp.where(lane < 192, 8, 16)))
        cnt = jnp.minimum(pos + 1, win).astype(F32)
        buf_a[bs, t0:t0 + tb, :] = win_sum / cnt - cur
        tiles = (ROW_BLOCK // SUBLANES, SUBLANES, BRANCH_W)
        acc = jnp.broadcast_to((cdb[...] + order_zero)[None], tiles)
        for k in range(CONF_WIDTH):
            q, sh = divmod(k, SUBLANES)
            r = t0 + SUBLANES * q
            acc = acc + conv_sh[sh, bs, r:r + tb, :].reshape(tiles) * cdw[k][None]
        buf_b[bs, t0:t0 + tb, :] = acc.reshape(shp)
        base = SCONV_PAD - SCONV_HIST + t0
        acc = sc_ext[bs, base:base + tb, :].reshape(tiles) * scw[0][None]
        for k in range(1, SCONV_WIDTH):
            acc = acc + sc_ext[bs, base + k:base + k + tb, :].reshape(tiles) * scw[k][None]
        buf_c[bs, t0:t0 + tb, :] = acc.reshape(shp)

    def gate_chunk(k):
        g0 = k * COL_TILE
        g = jnp.tanh(proj(GATE_COL0 + g0, GATE_COL0 + g0 + COL_TILE)) + 1.0
        gate_buf[:, g0:g0 + COL_TILE] = g
        return _zero_like_bits(g[0:SUBLANES, :])

    br = {}

    def item_windows(blks):
        def run(zero):
            for bs, t0, tb in blks:
                window_block(bs, t0, tb, zero)
        return run

    def item_state(zero):
        del zero
        pool_o[...] = pool_ext[:, POOL_PAD + t - POOL_HIST:POOL_PAD + t, :]
        conv_o[...] = conv_ext[:, CONF_PAD + t - CONF_HIST:CONF_PAD + t, :]
        sconv_o[...] = sc_ext[:, SCONV_PAD + t - SCONV_HIST:SCONV_PAD + t, :]
        if not cfg.has_state:
            pool_ext[:, 0:POOL_PAD, :] = pool_ext[:, t:t + POOL_PAD, :]
            conv_ext[:, 0:CONF_PAD, :] = conv_ext[:, t:t + CONF_PAD, :]
            sc_ext[:, 0:SCONV_PAD, :] = sc_ext[:, t:t + SCONV_PAD, :]

    def item_a(zero):
        pooled = buf_a[...].reshape(tm, BRANCH_W).astype(BF16)
        scale = pool_sc[...] + zero[0:1, :]
        br["a"] = (jnp.dot(pooled, pool_w[...], preferred_element_type=F32) * scale).astype(BF16)

    def item_b(zero):
        cb = _ln(buf_b[...].reshape(tm, BRANCH_W), clg[...] + zero[0:1, :], clb[...])
        br["b"] = (cb * _sigmoid(cb)).astype(BF16)

    def item_c(zero):
        br["c"] = (proj(768, 1024) * (buf_c[...].reshape(tm, BRANCH_W) + zero[0:1, :])).astype(BF16)

    def item_u(zero):
        br["u"] = _gelu_tanh(proj(1536, 1792) + zero[0:1, :])

    def item_v(zero):
        v = _ln(_gelu_tanh(proj(1792, 2048)), slg[...] + zero[0:1, :], slb[...])
        if cfg.has_state:
            v_o[...] = v.reshape(nb, t, BRANCH_W)
        else:
            v_o[...] = v[tm - cfg.v_rows:, :].reshape(1, cfg.v_rows, BRANCH_W)
        br["v"] = v

    def item_d(zero):
        head = lax.broadcasted_iota(jnp.int32, (CHUNK, BRANCH_W), 1) // SGU_HW
        bias = sbias[...] + jnp.concatenate([zero] * (CHUNK // SUBLANES), axis=0)
        mixed = []
        for j in range(tm // CHUNK):
            vj = br["v"][j * CHUNK:(j + 1) * CHUNK, :]
            stacked = jnp.concatenate([jnp.where(head == h, vj, 0.0) for h in range(SGU_HEADS)], axis=0)
            mixed.append(jnp.dot(wcat[...], stacked.astype(BF16), preferred_element_type=F32) + bias)
        br["d"] = (br["u"] * jnp.concatenate(mixed, axis=0)).astype(BF16)

    blocks = _row_blocks(nb, t)
    n_gate = N_BRANCH * D_MODEL // COL_TILE
    per_item = -(-len(blocks) // (n_gate // 2))
    items = [item_windows(blocks[i:i + per_item]) for i in range(0, len(blocks), per_item)]
    items += [item_state, item_c, item_b, item_u, item_v, item_a, item_d]
    assert len(items) <= n_gate
    zero = jnp.zeros((SUBLANES, BRANCH_W), F32)
    for k in range(n_gate):
        next_zero = gate_chunk(k)
        if k < len(items):
            items[k](zero)
        zero = next_zero

    n_ct = D_MODEL // COL_TILE
    brs = [br["a"], br["b"], br["c"], br["d"]]
    for c in range(n_ct):
        part = None
        for i in range(N_BRANCH):
            g0 = i * D_MODEL + c * COL_TILE
            term = gate_buf[:, g0:g0 + COL_TILE] * jnp.dot(
                brs[i], w_br[i * n_ct + c], preferred_element_type=F32)
            part = term if part is None else part + term
        xb_buf[:, c * COL_TILE:(c + 1) * COL_TILE] = part.astype(BF16)
    for c in range(n_ct):
        cs = slice(c * COL_TILE, (c + 1) * COL_TILE)
        h1_o[:, cs] = x_ref[:, cs] + jnp.dot(xb_buf[...], w_out[c], preferred_element_type=F32)

    xn2 = _rms(h1_o[...], n2g[...])
    x_hi = xn2.astype(BF16)
    xn2_o[...] = x_hi
    x_lo = (xn2 - x_hi.astype(F32)).astype(BF16)
    hi_both = jnp.dot(x_hi, w_rt[...], preferred_element_type=F32)
    lo_hi = jnp.dot(x_lo, w_rt[:, 0:LANES], preferred_element_type=F32)
    logits = (hi_both[:, 0:LANES] + (lo_hi + hi_both[:, LANES:2 * LANES])) + b_rt[...]
    lane = lax.broadcasted_iota(jnp.int32, (tm, LANES), 1)
    is_g = (lane >= ROUTE_GROUP_LANE0) & (lane < ROUTE_GROUP_LANE0 + N_GROUPS)
    glog = jnp.where(is_g, logits, NEG_BIG)
    gmax = jnp.max(glog, axis=-1, keepdims=True)
    lane_f = lane.astype(F32)
    no_lane = float(4 * LANES)
    gsel = jnp.min(jnp.where(glog == gmax, lane_f, no_lane), axis=-1, keepdims=True) - ROUTE_GROUP_LANE0
    pg = 1.0 / jnp.sum(jnp.where(is_g, jnp.exp(glog - gmax), 0.0), axis=-1, keepdims=True)
    in_grp = (lane < N_EXPERTS) & ((lane // EXPERTS_PER_GROUP) == gsel.astype(jnp.int32))
    el = jnp.where(in_grp, logits, NEG_BIG)
    m1 = jnp.max(el, axis=-1, keepdims=True)
    i1 = jnp.min(jnp.where(in_grp & (el == m1), lane_f, no_lane), axis=-1, keepdims=True)
    rest = in_grp & (lane_f != i1)
    el2 = jnp.where(rest, logits, NEG_BIG)
    m2 = jnp.max(el2, axis=-1, keepdims=True)
    i2 = jnp.min(jnp.where(rest & (el2 == m2), lane_f, no_lane), axis=-1, keepdims=True)
    e21 = jnp.exp(m2 - m1)
    w1 = pg / (1.0 + e21)
    w2 = pg * e21 / (1.0 + e21)
    route_o[...] = jnp.where(lane == 0, i1, jnp.where(lane == 1, i2,
                             jnp.where(lane == 2, w1, jnp.where(lane == 3, w2, 0.0))))


def _layer_spec(shape, layer):
    nd = len(shape)
    return pl.BlockSpec((1,) + shape[1:], lambda *_: (layer,) + (0,) * (nd - 1), pipeline_mode=pl.Buffered(1))


def _mixer_call(cfg, n_seq, n_rows_total, x2d, x_tile_off, states, weights, big_weights, aliased):
    nb, t = cfg.nb, cfg.t
    tm = nb * t
    if cfg.has_state:
        grid = (n_seq // nb,)
        tok = lambda i: (x_tile_off + i, 0)
        out_tok = lambda i: (cfg.row_off + i, 0)
        seq3 = lambda i: (i, 0, 0)
    else:
        grid = (n_seq, 2048 // t)
        nt = grid[1]
        tok = lambda b, c: (x_tile_off + b * nt + c, 0)
        out_tok = lambda b, c: (cfg.row_off + b * nt + c, 0)
        seq3 = lambda b, c: (b, 0, 0)

    in_specs = [pl.BlockSpec((tm, D_MODEL), tok)]
    args = [x2d]
    if cfg.has_state:
        for s in states:
            in_specs.append(pl.BlockSpec((1, nb) + s.shape[2:], lambda i: (cfg.layer, i, 0, 0)))
            args.append(s)
    for w in weights:
        in_specs.append(_layer_spec(w.shape, cfg.layer))
        args.append(w)
    for w in big_weights:
        in_specs.append(pl.BlockSpec(memory_space=pl.ANY))
        args.append(w)
    io_alias = {}
    if cfg.has_state:
        for k, a in enumerate(aliased):
            in_specs.append(pl.BlockSpec(memory_space=pl.ANY))
            io_alias[len(args)] = k
            args.append(a)

    out_shape = [
        jax.ShapeDtypeStruct((n_rows_total, D_MODEL), F32),
        jax.ShapeDtypeStruct((n_rows_total, LANES), F32),
        jax.ShapeDtypeStruct((n_rows_total, D_MODEL), BF16),
        jax.ShapeDtypeStruct((n_seq, POOL_HIST, BRANCH_W), F32),
        jax.ShapeDtypeStruct((n_seq, CONF_HIST, BRANCH_W), F32),
        jax.ShapeDtypeStruct((n_seq, SCONV_HIST, BRANCH_W), F32),
        jax.ShapeDtypeStruct((n_seq, cfg.v_rows, BRANCH_W), F32),
    ]
    out_specs = [
        pl.BlockSpec((tm, D_MODEL), out_tok),
        pl.BlockSpec((tm, LANES), out_tok),
        pl.BlockSpec((tm, D_MODEL), out_tok),
        pl.BlockSpec((nb, POOL_HIST, BRANCH_W), seq3),
        pl.BlockSpec((nb, CONF_HIST, BRANCH_W), seq3),
        pl.BlockSpec((nb, SCONV_HIST, BRANCH_W), seq3),
        pl.BlockSpec((nb, cfg.v_rows, BRANCH_W), seq3),
    ]
    scratch = [
        pltpu.VMEM((nb, POOL_PAD + t, BRANCH_W), F32),
        pltpu.VMEM((nb, POOL_PAD + t, BRANCH_W), F32),
        pltpu.VMEM((nb, POOL_PAD + t, BRANCH_W), F32),
        pltpu.VMEM((nb, CONF_PAD + t, BRANCH_W), F32),
        pltpu.VMEM((SUBLANES, nb, t + CONF_PAD - SUBLANES, BRANCH_W), F32),
        pltpu.VMEM((nb, SCONV_PAD + t, BRANCH_W), F32),
        pltpu.VMEM((nb, t, BRANCH_W), F32),
        pltpu.VMEM((nb, t, BRANCH_W), F32),
        pltpu.VMEM((nb, t, BRANCH_W), F32),
        pltpu.VMEM((tm, N_BRANCH * D_MODEL), F32),
        pltpu.VMEM((tm, D_MODEL), BF16),
        pltpu.VMEM((IN_COLS // COL_TILE, D_MODEL, COL_TILE), BF16),
        pltpu.VMEM((N_BRANCH * D_MODEL // COL_TILE, BRANCH_W, COL_TILE), BF16),
        pltpu.VMEM((D_MODEL // COL_TILE, D_MODEL, COL_TILE), BF16),
        pltpu.VMEM((2, D_MODEL, COL_TILE), F32),
        pltpu.VMEM((2, BRANCH_W, D_MODEL), F32),
        pltpu.SemaphoreType.DMA((2,)),
    ]
    return pl.pallas_call(
        functools.partial(_mixer_kernel, cfg),
        grid=grid,
        in_specs=in_specs,
        out_specs=out_specs,
        out_shape=out_shape,
        scratch_shapes=scratch,
        input_output_aliases=io_alias,
        compiler_params=pltpu.CompilerParams(
            dimension_semantics=("arbitrary",) * len(grid), vmem_limit_bytes=VMEM_LIMIT),
        name="mixer_sample" if cfg.has_state else "mixer_prompt",
    )(*args)


def _sort_kernel(xn_ref, route_ref, xl_o, pos_o, nch_o):
    t = SORT_TILE
    xn = xn_ref[...]
    r = route_ref[...]
    lane_i = lax.broadcasted_iota(jnp.int32, (t, LANES), 1)
    lane = lane_i.astype(F32)
    hit1 = lane == r[:, 0:1]
    hit2 = lane == r[:, 1:2]
    onehot = jnp.where(hit1 | hit2, 1.0, 0.0)
    row = lax.broadcasted_iota(jnp.int32, (t, t), 0)
    col = lax.broadcasted_iota(jnp.int32, (t, t), 1)
    strict_lower = jnp.where(col < row, 1.0, 0.0).astype(BF16)
    before = jnp.dot(strict_lower, onehot.astype(BF16), preferred_element_type=F32)
    cnt = jnp.sum(onehot, axis=0, keepdims=True)
    nch = jnp.floor((cnt + (SEG_PAD - 1)) * (1.0 / SEG_PAD))
    er = lax.broadcasted_iota(jnp.int32, (LANES, LANES), 0)
    ec = lax.broadcasted_iota(jnp.int32, (LANES, LANES), 1)
    strict_upper = jnp.where(er < ec, 1.0, 0.0).astype(BF16)
    nch8 = jnp.broadcast_to(nch, (SUBLANES, LANES)).astype(BF16)
    seg0 = jnp.dot(nch8, strict_upper, preferred_element_type=F32)[0:1, :] * SEG_PAD
    where = seg0 + before
    pos0 = jnp.sum(jnp.where(hit1, where, 0.0), axis=-1, keepdims=True)
    pos1 = jnp.sum(jnp.where(hit2, where, 0.0), axis=-1, keepdims=True)
    pos = jnp.where(lane_i == 0, pos0, jnp.where(lane_i == 1, pos1, -1.0))
    pos_o[...] = pos
    nch_o[...] = jnp.broadcast_to(nch, (1, SUBLANES, LANES))
    pos_t = jnp.transpose(pos)
    p0 = pos_t[0:1, :]
    p1 = pos_t[1:2, :]
    used_rows = jnp.sum(nch) * SEG_PAD

    def perm_block(blk):
        dst = (lax.broadcasted_iota(jnp.int32, (PERM_BLOCK, t), 0) + blk * PERM_BLOCK).astype(F32)
        perm = jnp.where((dst == p0) | (dst == p1), 1.0, 0.0).astype(BF16)
        xl_o[0, blk * PERM_BLOCK:(blk + 1) * PERM_BLOCK, :] = jnp.dot(
            perm, xn, preferred_element_type=F32).astype(BF16)

    for blk in range(N_PERM_BLOCKS - 1):
        perm_block(blk)
    pl.when(used_rows > (N_PERM_BLOCKS - 1) * PERM_BLOCK)(functools.partial(perm_block, N_PERM_BLOCKS - 1))


def _sort_call(xn2, route):
    n = xn2.shape[0]
    nt = n // SORT_TILE
    return pl.pallas_call(
        _sort_kernel,
        grid=(nt,),
        in_specs=[pl.BlockSpec((SORT_TILE, D_MODEL), lambda i: (i, 0)),
                  pl.BlockSpec((SORT_TILE, LANES), lambda i: (i, 0))],
        out_specs=[pl.BlockSpec((1, LOCAL_ROWS, D_MODEL), lambda i: (i, 0, 0)),
                   pl.BlockSpec((SORT_TILE, LANES), lambda i: (i, 0)),
                   pl.BlockSpec((1, SUBLANES, LANES), lambda i: (i, 0, 0))],
        out_shape=[jax.ShapeDtypeStruct((nt + SPARE_BLOCKS, LOCAL_ROWS, D_MODEL), BF16),
                   jax.ShapeDtypeStruct((n, LANES), F32),
                   jax.ShapeDtypeStruct((nt, SUBLANES, LANES), F32)],
        compiler_params=pltpu.CompilerParams(dimension_semantics=("arbitrary",), vmem_limit_bytes=VMEM_LIMIT),
        name="moe_sort",
    )(xn2, route)


def _chunk_copy(src, src_row, dst, dst_row, sem):
    return pltpu.make_async_copy(src.at[pl.ds(src_row, SEG_PAD), :], dst.at[pl.ds(dst_row, SEG_PAD), :], sem)


def _expert_kernel(trash_row0, se_ref, nv_ref, src_ref, dst_ref, xl_in, wg, wu, wd, xl_io, *scratch):
    del se_ref, xl_in
    ns = EXP_SLOTS
    xbuf, ybuf, (sem_in, sem_out) = scratch[:ns], scratch[ns:2 * ns], scratch[2 * ns:]
    s = pl.program_id(0)
    nv = nv_ref[0]

    def gather(step, sl):
        return [_chunk_copy(xl_io, pl.multiple_of(src_ref[step * CHUNKS_PER_STEP + j], SEG_PAD),
                            xbuf[sl], j * SEG_PAD, sem_in.at[sl]) for j in range(CHUNKS_PER_STEP)]

    def write_back(step, sl):
        return [_chunk_copy(ybuf[sl], j * SEG_PAD, xl_io,
                            pl.multiple_of(dst_ref[step * CHUNKS_PER_STEP + j], SEG_PAD), sem_out.at[sl])
                for j in range(CHUNKS_PER_STEP)]

    def spare_write(sl):
        return [_chunk_copy(ybuf[sl], j * SEG_PAD, xl_io, trash_row0 + sl * EXP_TILE + j * SEG_PAD,
                            sem_out.at[sl]) for j in range(CHUNKS_PER_STEP)]

    @pl.when(s == 0)
    def _():
        for ahead in range(ns - 1):
            for c in gather(ahead, ahead):
                c.start()
        for sl in range(ns):
            ybuf[sl][...] = jnp.zeros_like(ybuf[sl])
            for c in spare_write(sl):
                c.start()

    def live_step(sl):
        for c in gather(s + ns - 1, (sl + ns - 1) % ns):
            c.start()
        for c in gather(s, sl):
            c.wait()
        xb = xbuf[sl][...]
        a = jnp.dot(xb, wg[0, 0].astype(BF16), preferred_element_type=F32)
        b = jnp.dot(xb, wu[0, 0].astype(BF16), preferred_element_type=F32)
        hh = (a * _sigmoid(a) * b).astype(BF16)
        y = jnp.dot(hh, wd[0, 0].astype(BF16), preferred_element_type=F32).astype(BF16)
        for c in write_back(s, sl):
            c.wait()
        ybuf[sl][...] = y
        for c in write_back(s, sl):
            c.start()

    def drain_step(sl):
        for ahead in range(ns - 1):
            for c in gather(s + ahead, (sl + ahead) % ns):
                c.wait()
        for every in range(ns):
            for c in spare_write(every):
                c.wait()

    for sl in range(ns):
        pl.when((s < nv) & (s % ns == sl))(functools.partial(live_step, sl))
        pl.when((s == nv) & (s % ns == sl))(functools.partial(drain_step, sl))


def _expert_call(xl, trash_row0, step_expert, n_valid, chunk_src, chunk_dst, layer, w_gate, w_up, w_down):
    n_steps = step_expert.shape[0] - (EXP_SLOTS - 1)
    wmap = lambda s, se, *_: (layer, se[s], 0, 0)
    grid_spec = pltpu.PrefetchScalarGridSpec(
        num_scalar_prefetch=4,
        grid=(n_steps,),
        in_specs=[pl.BlockSpec(memory_space=pl.ANY),
                  pl.BlockSpec((1, 1, D_MODEL, D_EXPERT), wmap),
                  pl.BlockSpec((1, 1, D_MODEL, D_EXPERT), wmap),
                  pl.BlockSpec((1, 1, D_EXPERT, D_MODEL), wmap)],
        out_specs=pl.BlockSpec(memory_space=pl.ANY),
        scratch_shapes=[pltpu.VMEM((EXP_TILE, D_MODEL), BF16) for _ in range(2 * EXP_SLOTS)] + [
                        pltpu.SemaphoreType.DMA((EXP_SLOTS,)), pltpu.SemaphoreType.DMA((EXP_SLOTS,))],
    )
    return pl.pallas_call(
        functools.partial(_expert_kernel, trash_row0),
        grid_spec=grid_spec,
        out_shape=jax.ShapeDtypeStruct(xl.shape, BF16),
        input_output_aliases={4: 0},
        compiler_params=pltpu.CompilerParams(dimension_semantics=("arbitrary",), has_side_effects=True),
        name="moe_experts",
    )(step_expert, n_valid, chunk_src, chunk_dst, xl, w_gate, w_up, w_down)


def _combine_kernel(final_norm, n_first, h1_ref, route_ref, pos_ref, fg, yl_ref, *outs):
    t = SORT_TILE
    r = route_ref[...]
    p = pos_ref[...]
    w1, w2 = r[:, 2:3], r[:, 3:4]
    pos0, pos1 = p[:, 0:1], p[:, 1:2]
    *outs, acc_ref = outs

    def gathered(blk):
        src = (lax.broadcasted_iota(jnp.int32, (t, PERM_BLOCK), 1) + blk * PERM_BLOCK).astype(F32)
        pw = jnp.where(src == pos0, w1, 0.0) + jnp.where(src == pos1, w2, 0.0)
        return jnp.dot(pw.astype(BF16), yl_ref[0, blk * PERM_BLOCK:(blk + 1) * PERM_BLOCK, :],
                       preferred_element_type=F32)

    acc = h1_ref[...]
    for blk in range(N_PERM_BLOCKS - 1):
        acc = acc + gathered(blk)
    acc_ref[...] = acc

    @pl.when(jnp.max(p) >= (N_PERM_BLOCKS - 1) * PERM_BLOCK)
    def _():
        acc_ref[...] += gathered(N_PERM_BLOCKS - 1)

    if not final_norm:
        outs[0][...] = acc_ref[...]
        return
    y = _rms(acc_ref[...], fg[...])
    i = pl.program_id(0)

    @pl.when(i < n_first)
    def _():
        outs[0][...] = y

    @pl.when(i >= n_first)
    def _():
        outs[1][...] = y


def _combine_call(h1, route, pos, yl, final_g, final_norm, n_first_rows):
    n = h1.shape[0]
    nt = n // SORT_TILE
    n_first = n_first_rows // SORT_TILE
    tok = lambda i: (i, 0)
    if final_norm:
        out_shape = [jax.ShapeDtypeStruct((n_first_rows, D_MODEL), F32),
                     jax.ShapeDtypeStruct((n - n_first_rows, D_MODEL), F32)]
        out_specs = [pl.BlockSpec((SORT_TILE, D_MODEL), lambda i: (jnp.minimum(i, n_first - 1), 0)),
                     pl.BlockSpec((SORT_TILE, D_MODEL), lambda i: (jnp.maximum(i - n_first, 0), 0))]
    else:
        out_shape = [jax.ShapeDtypeStruct((n, D_MODEL), F32)]
        out_specs = [pl.BlockSpec((SORT_TILE, D_MODEL), tok)]
    return pl.pallas_call(
        functools.partial(_combine_kernel, final_norm, n_first),
        grid=(nt,),
        in_specs=[pl.BlockSpec((SORT_TILE, D_MODEL), tok),
                  pl.BlockSpec((SORT_TILE, LANES), tok),
                  pl.BlockSpec((SORT_TILE, LANES), tok),
                  pl.BlockSpec((1, D_MODEL), lambda i: (0, 0)),
                  pl.BlockSpec((1, LOCAL_ROWS, D_MODEL), lambda i: (i, 0, 0))],
        out_specs=out_specs,
        out_shape=out_shape,
        scratch_shapes=[pltpu.VMEM((SORT_TILE, D_MODEL), F32)],
        compiler_params=pltpu.CompilerParams(dimension_semantics=("arbitrary",), vmem_limit_bytes=VMEM_LIMIT),
        name="moe_combine",
    )(h1, route, pos, final_g, yl)


def _expert_tables(nch, n_steps, trash_row0):
    nt = nch.shape[0]
    cps = CHUNKS_PER_STEP
    seg_row0 = (jnp.cumsum(nch, axis=1) - nch) * SEG_PAD
    first = jnp.cumsum(nch, axis=0) - nch
    tot = jnp.sum(nch, axis=0)
    steps = (tot + cps - 1) // cps
    step_end = jnp.cumsum(steps)
    n_valid = step_end[-1:]
    s_ids = jnp.arange(n_steps, dtype=jnp.int32)
    step_expert = jnp.minimum(jnp.sum((step_end[None, :] <= s_ids[:, None]).astype(jnp.int32), axis=1),
                              N_EXPERTS - 1)
    sel = (step_expert[:, None] == jnp.arange(N_EXPERTS, dtype=jnp.int32)[None, :]).astype(jnp.int32)
    step0 = sel @ (step_end - steps)
    tot_s = sel @ tot
    first_s = sel @ first.T
    row0_s = sel @ seg_row0.T
    k = (s_ids - step0)[:, None] * cps + jnp.arange(cps, dtype=jnp.int32)[None, :]
    ok = (k < tot_s[:, None]) & (s_ids < n_valid[0])[:, None]
    tile = jnp.sum((first_s[:, None, :] <= k[:, :, None]).astype(jnp.int32), axis=2) - 1
    tsel = (tile[:, :, None] == jnp.arange(nt, dtype=jnp.int32)[None, None, :]).astype(jnp.int32)
    first_k = jnp.sum(tsel * first_s[:, None, :], axis=2)
    row0_k = jnp.sum(tsel * row0_s[:, None, :], axis=2)
    row = tile * LOCAL_ROWS + row0_k + (k - first_k) * SEG_PAD
    live = (s_ids < n_valid[0])[:, None]
    src = jnp.where(live, jnp.where(ok, row, row[:, 0:1]), 0)
    spare = trash_row0 + (s_ids % EXP_SLOTS)[:, None] * EXP_TILE + jnp.arange(cps, dtype=jnp.int32)[None, :] * SEG_PAD
    dst = jnp.where(ok, row, spare)
    return (step_expert, n_valid.astype(jnp.int32), src.reshape(-1).astype(jnp.int32),
            dst.reshape(-1).astype(jnp.int32))


def _moe(h1, route, xn2, layer, w_gate, w_up, w_down, final_g, final_norm, n_first_rows):
    n = h1.shape[0]
    nt = n // SORT_TILE
    xl, pos, nch = _sort_call(xn2, route)
    nch = nch[:, 0, :N_EXPERTS].astype(jnp.int32)
    max_chunks = (2 * n) // SEG_PAD + nt * N_EXPERTS
    n_steps = max_chunks // CHUNKS_PER_STEP + N_EXPERTS + 1
    trash_row0 = nt * LOCAL_ROWS
    step_expert, n_valid, chunk_src, chunk_dst = _expert_tables(nch, n_steps + EXP_SLOTS - 1, trash_row0)
    yl = _expert_call(xl.reshape((nt + SPARE_BLOCKS) * LOCAL_ROWS, D_MODEL), trash_row0, step_expert, n_valid,
                      chunk_src, chunk_dst, layer, w_gate, w_up, w_down)
    return _combine_call(h1, route, pos, yl.reshape(nt + SPARE_BLOCKS, LOCAL_ROWS, D_MODEL), final_g, final_norm,
                         n_first_rows)


def kernel(x_prompt, x_sample, state_pool, state_conv, state_sconv, norm1_g, w_in, pool_w, pool_scale, conf_dw, conf_dw_b, conf_ln_g, conf_ln_b, sconv_w, sgu_ln_g, sgu_ln_b, sgu_ws, sgu_b, w_branch, w_out, norm2_g, router_g, router_g_b, router_e, router_e_b, w_gate, w_up, w_down, final_g):
    depth = w_in.shape[0]
    bp, seq, _ = x_prompt.shape
    bs, dseq, _ = x_sample.shape
    n_p, n_s = bp * seq, bs * dseq
    n = n_p + n_s
    past_len = 16384
    assert seq % MIX_TILE == 0 and n_s % MIX_TILE_S == 0 and n_p % MIX_TILE_S == 0 and dseq == SUBLANES
    assert MIX_TILE % CHUNK == 0 and MIX_TILE_S % CHUNK == 0

    cfg_p = MixCfg(layer=0, nb=1, t=MIX_TILE, has_state=False, start_pos=0, row_off=0, v_rows=CHUNK)
    cfg_s = MixCfg(layer=0, nb=MIX_TILE_S // dseq, t=dseq, has_state=True, start_pos=past_len,
                   row_off=n_p // MIX_TILE_S, v_rows=dseq)
    big_weights = (w_in, w_branch, w_out)

    row = lambda a: a[:, None, :]
    rep8 = lambda a: jnp.broadcast_to(a[..., None, :], a.shape[:-1] + (SUBLANES, a.shape[-1]))
    tril = jnp.tril(jnp.ones((CHUNK, CHUNK), F32))
    eye_blk = jnp.kron(jnp.eye(CHUNK // dseq, dtype=F32), jnp.ones((dseq, dseq), F32))
    cat = lambda w: jnp.concatenate([w[:, h] for h in range(SGU_HEADS)], axis=2).astype(BF16)
    wcat_p = cat(sgu_ws * tril)
    wcat_s = cat(jnp.tile(sgu_ws[:, :, :dseq, :dseq], (1, 1, CHUNK // dseq, CHUNK // dseq)) * (tril * eye_blk))
    bias_p = jnp.repeat(jnp.swapaxes(sgu_b, 1, 2), SGU_HW, axis=2)
    bias_s = jnp.tile(jnp.repeat(jnp.swapaxes(sgu_b[:, :, :dseq], 1, 2), SGU_HW, axis=2), (1, CHUNK // dseq, 1))
    lane_pad = LANES - N_EXPERTS - N_GROUPS
    w_rt32 = jnp.pad(jnp.concatenate([router_e, router_g], axis=2), ((0, 0), (0, 0), (0, lane_pad)))
    w_rt_hi = w_rt32.astype(BF16)
    w_rt = jnp.concatenate([w_rt_hi, (w_rt32 - w_rt_hi.astype(F32)).astype(BF16)], axis=2)
    b_rt = row(jnp.pad(jnp.concatenate([router_e_b, router_g_b], axis=1), ((0, 0), (0, lane_pad))))
    same_group = jnp.eye(len(POOL_WINDOWS), dtype=F32)[None, :, None, :, None]
    pool_bd = (pool_w[:, :, :, None, :] * same_group).reshape(depth, BRANCH_W, BRANCH_W).astype(BF16)

    def weights(wcat, sbias):
        return [row(norm1_g), pool_bd, row(pool_scale), rep8(conf_dw), rep8(conf_dw_b), row(conf_ln_g),
                row(conf_ln_b), rep8(sconv_w), row(sgu_ln_g), row(sgu_ln_b), wcat, sbias, row(norm2_g), w_rt, b_rt]

    weights_p, weights_s = weights(wcat_p, bias_p), weights(wcat_s, bias_s)
    final_row = final_g.reshape(1, -1)

    h_p, h_s = x_prompt.reshape(n_p, D_MODEL), x_sample.reshape(n_s, D_MODEL)
    off_p, off_s = 0, 0
    states_out = []
    for l in range(depth):
        h1, route, xn2, pool_p, conv_p, sconv_p, v_p = _mixer_call(
            cfg_p._replace(layer=l), bp, n, h_p, off_p, None, weights_p, big_weights, None)
        h1, route, xn2, pool_s, conv_s, sconv_s, v_s = _mixer_call(
            cfg_s._replace(layer=l), bs, n, h_s, off_s, (state_pool, state_conv, state_sconv),
            weights_s, big_weights, (h1, route, xn2))
        states_out.append((pool_p, conv_p, sconv_p, v_p, pool_s, conv_s, sconv_s, v_s))

        outs = _moe(h1, route, xn2, l, w_gate, w_up, w_down, final_row, l == depth - 1, n_p)
        h_p = h_s = outs[0]
        off_p, off_s = 0, n_p // MIX_TILE_S

    y_prompt = outs[0].reshape(bp, seq, D_MODEL)
    y_sample = outs[1].reshape(bs, dseq, D_MODEL)
    st = [jnp.stack([s[k] for s in states_out]) for k in range(8)]
    return (y_prompt, y_sample, st[0], st[1], st[2], st[3], st[4], st[5], st[6], st[7])
```

```python
import functools
from typing import NamedTuple

import jax
import jax.numpy as jnp
from jax import lax
from jax.experimental import pallas as pl
from jax.experimental.pallas import tpu as pltpu

F32 = jnp.float32
BF16 = jnp.bfloat16

D_MODEL = 1024
BRANCH_W = 256
N_BRANCH = 4
POOL_WINDOWS = (2, 4, 8, 16)
POOL_HIST = 15
CONF_WIDTH = 31
CONF_HIST = 30
SCONV_WIDTH = 3
SCONV_HIST = 2
CHUNK = 128
SGU_HEADS = 4
SGU_HW = 64
N_GROUPS = 4
EXPERTS_PER_GROUP = 8
N_EXPERTS = 32
D_EXPERT = 256
RMS_EPS = 1e-6
LN_EPS = 1e-5
IN_COLS = 6144
GATE_COL0 = 2048

COL_TILE = 256
LANES = 128
SUBLANES = 8
POOL_PAD = 32
CONF_PAD = 32
SCONV_PAD = 8
ROW_BLOCK = 32
W_IN_ROWS = 64
ROUTE_GROUP_LANE0 = 32
NEG_BIG = -3.0e38

MIX_TILE = 512
MIX_TILE_S = 256
SORT_TILE = 512
SEG_PAD = 16
PERM_BLOCK = 256
N_PERM_BLOCKS = -(-(2 * SORT_TILE + N_EXPERTS * (SEG_PAD - 1)) // PERM_BLOCK)
LOCAL_ROWS = N_PERM_BLOCKS * PERM_BLOCK
EXP_TILE = 512
CHUNKS_PER_STEP = EXP_TILE // SEG_PAD
EXP_SLOTS = 3
SPARE_BLOCKS = -(-EXP_SLOTS * EXP_TILE // LOCAL_ROWS)
VMEM_LIMIT = 56 * 1024 * 1024


class MixCfg(NamedTuple):
    layer: int
    nb: int
    t: int
    has_state: bool
    start_pos: int
    row_off: int
    v_rows: int
    seq_len: int


def _rms(x, g):
    return x * lax.rsqrt(jnp.mean(x * x, axis=-1, keepdims=True) + RMS_EPS) * g


def _ln(x, g, b):
    mu = jnp.mean(x, axis=-1, keepdims=True)
    xc = x - mu
    return xc * lax.rsqrt(jnp.mean(xc * xc, axis=-1, keepdims=True) + LN_EPS) * g + b


def _sigmoid(x):
    return 0.5 * jnp.tanh(0.5 * x) + 0.5


def _zero_like_bits(x):
    bits = lax.bitcast_convert_type(x, jnp.uint32)
    return ((bits >> 16) >> 16).astype(jnp.int32).astype(F32)


def _gelu_tanh(x):
    return 0.5 * x * (1.0 + jnp.tanh(0.7978845608028654 * (x + 0.044715 * (x * x * x))))


def _row_blocks(nb, t):
    if t >= ROW_BLOCK:
        return [(slice(b, b + 1), t0, ROW_BLOCK) for b in range(nb) for t0 in range(0, t, ROW_BLOCK)]
    bb = ROW_BLOCK // t
    return [(slice(b0, b0 + bb), 0, t) for b0 in range(0, nb, bb)]


def _load_big_weights(layer, w_in_hbm, w_br_hbm, w_out_hbm, w_in, w_br, w_out, stage, stage_b, sem):
    n_in, n_out = IN_COLS // COL_TILE, D_MODEL // COL_TILE
    jobs = ([("in", r) for r in range(0, D_MODEL, W_IN_ROWS)] + [("out", r) for r in range(0, D_MODEL, BRANCH_W)]
            + [("br", i) for i in range(N_BRANCH)])

    def copy(job, slot):
        kind, j = job
        if kind == "in":
            return pltpu.make_async_copy(w_in_hbm.at[layer, pl.ds(j, W_IN_ROWS), :], stage.at[slot], sem.at[slot])
        src = w_br_hbm.at[layer, j] if kind == "br" else w_out_hbm.at[layer, pl.ds(j, BRANCH_W), :]
        return pltpu.make_async_copy(src, stage_b.at[slot], sem.at[slot])

    def col(buf, slot, c):
        return buf[slot, :, c * COL_TILE:(c + 1) * COL_TILE]

    copy(jobs[0], 0).start()
    for n, job in enumerate(jobs):
        slot = n % 2
        if n + 1 < len(jobs):
            copy(jobs[n + 1], 1 - slot).start()
        copy(job, slot).wait()
        kind, j = job
        if kind == "in":
            for c in range(n_in):
                tile = col(stage, slot, c)
                w_in[c, j:j + W_IN_ROWS, :] = (tile * 0.5 if c * COL_TILE >= GATE_COL0 else tile).astype(BF16)
        elif kind == "out":
            for c in range(n_out):
                w_out[c, j:j + BRANCH_W, :] = col(stage_b, slot, c).astype(BF16)
        else:
            for c in range(n_out):
                w_br[j * n_out + c] = (col(stage_b, slot, c) * 0.5).astype(BF16)


def _mixer_kernel(cfg, *refs):
    nb, t, tm = cfg.nb, cfg.t, cfg.nb * cfg.t
    refs = list(refs)
    x_ref = refs.pop(0)
    if cfg.has_state:
        pool_st, conv_st, sconv_st = (refs.pop(0).at[0] for _ in range(3))
    (n1g, pool_w, pool_sc, cdw, cdb, clg, clb, scw, slg, slb, wcat, sbias, n2g, w_rt, b_rt) = (
        r.at[0] for r in refs[:15])
    w_in_hbm, w_br_hbm, w_out_hbm = refs[15:18]
    refs = refs[18:]
    if cfg.has_state:
        refs = refs[3:]
    h1_o, route_o, xn2_o, pool_o, conv_o, sconv_o, v_o = refs[:7]
    (pool_ext, sum_a, sum_b, conv_ext, conv_sh, sc_ext, buf_a, buf_b, buf_c, gate_buf, xb_buf,
     w_in, w_br, w_out, stage, stage_b, w_sem) = refs[7:]

    first_step = pl.program_id(0) == 0
    if not cfg.has_state:
        first_step = first_step & (pl.program_id(1) == 0)
    pl.when(first_step)(functools.partial(
        _load_big_weights, cfg.layer, w_in_hbm, w_br_hbm, w_out_hbm, w_in, w_br, w_out, stage, stage_b, w_sem))

    if cfg.has_state:
        seq_pos0 = cfg.start_pos
        pool_ext[:, 0:POOL_PAD - SUBLANES, :] = jnp.zeros((nb, POOL_PAD - SUBLANES, BRANCH_W), F32)
        pool_ext[:, POOL_PAD - POOL_HIST:POOL_PAD, :] = pool_st[...]
        conv_ext[:, CONF_PAD - CONF_HIST:CONF_PAD, :] = conv_st[...]
        sc_ext[:, SCONV_PAD - SCONV_HIST:SCONV_PAD, :] = sconv_st[...]
    else:
        c = pl.program_id(1)
        seq_pos0 = cfg.start_pos + c * t

        @pl.when(c == 0)
        def _():
            pool_ext[:, 0:POOL_PAD, :] = jnp.zeros((nb, POOL_PAD, BRANCH_W), F32)
            conv_ext[:, 0:CONF_PAD, :] = jnp.zeros((nb, CONF_PAD, BRANCH_W), F32)
            sc_ext[:, 0:SCONV_PAD, :] = jnp.zeros((nb, SCONV_PAD, BRANCH_W), F32)

    x = x_ref[...]
    xb_buf[...] = _rms(x, n1g[...]).astype(BF16)

    def proj(lo, hi):
        assert lo % COL_TILE == 0 and hi == lo + COL_TILE
        return jnp.dot(xb_buf[...], w_in[lo // COL_TILE], preferred_element_type=F32)

    a_pool = proj(0, 256)
    pool_ext[:, POOL_PAD:, :] = a_pool.reshape(nb, t, BRANCH_W)
    glu = proj(256, 512) * _sigmoid(proj(512, 768))
    conv_ext[:, CONF_PAD:, :] = glu.reshape(nb, t, BRANCH_W)
    z = proj(1024, 1280) * proj(1280, 1536)
    sc_ext[:, SCONV_PAD:, :] = z.reshape(nb, t, BRANCH_W)

    pl_len = POOL_PAD + t
    sum_b[:, 8:pl_len, :] = pool_ext[:, 8:pl_len, :] + pool_ext[:, 7:pl_len - 1, :]
    sum_a[:, 16:pl_len, :] = sum_b[:, 16:pl_len, :] + sum_b[:, 14:pl_len - 2, :]
    sum_b[:, 24:pl_len, :] = sum_a[:, 24:pl_len, :] + sum_a[:, 20:pl_len - 4, :]
    for sh in range(SUBLANES):
        n_rows = t + SUBLANES * ((CONF_WIDTH - 1 - sh) // SUBLANES)
        first = CONF_PAD - CONF_HIST + sh
        conv_sh[sh, :, 0:n_rows, :] = conv_ext[:, first:first + n_rows, :]

    def window_block(bs, t0, tb, order_zero):
        bb = bs.stop - bs.start
        shp = (bb, tb, BRANCH_W)
        lane = lax.broadcasted_iota(jnp.int32, shp, 2)
        pos = seq_pos0 + t0 + lax.broadcasted_iota(jnp.int32, shp, 1)
        r0 = POOL_PAD + t0
        cur = pool_ext[bs, r0:r0 + tb, :]
        s2 = cur + pool_ext[bs, r0 - 1:r0 - 1 + tb, :]
        s4 = sum_a[bs, r0:r0 + tb, :]
        s8 = sum_b[bs, r0:r0 + tb, :]
        s16 = s8 + sum_b[bs, r0 - SUBLANES:r0 - SUBLANES + tb, :]
        win_sum = jnp.where(lane < 64, s2, jnp.where(lane < 128, s4, jnp.where(lane < 192, s8, s16)))
        win = jnp.where(lane < 64, 2, jnp.where(lane < 128, 4, jnp.where(lane < 192, 8, 16)))
        cnt = jnp.minimum(pos + 1, win).astype(F32)
        buf_a[bs, t0:t0 + tb, :] = win_sum / cnt - cur
        tiles = (ROW_BLOCK // SUBLANES, SUBLANES, BRANCH_W)
        acc = jnp.broadcast_to((cdb[...] + order_zero)[None], tiles)
        for k in range(CONF_WIDTH):
            q, sh = divmod(k, SUBLANES)
            r = t0 + SUBLANES * q
            acc = acc + conv_sh[sh, bs, r:r + tb, :].reshape(tiles) * cdw[k][None]
        buf_b[bs, t0:t0 + tb, :] = acc.reshape(shp)
        base = SCONV_PAD - SCONV_HIST + t0
        acc = sc_ext[bs, base:base + tb, :].reshape(tiles) * scw[0][None]
        for k in range(1, SCONV_WIDTH):
            acc = acc + sc_ext[bs, base + k:base + k + tb, :].reshape(tiles) * scw[k][None]
        buf_c[bs, t0:t0 + tb, :] = acc.reshape(shp)

    def gate_chunk(k):
        g0 = k * COL_TILE
        g = jnp.tanh(proj(GATE_COL0 + g0, GATE_COL0 + g0 + COL_TILE)) + 1.0
        gate_buf[:, g0:g0 + COL_TILE] = g
        return _zero_like_bits(g[0:SUBLANES, :])

    br = {}

    def item_windows(blks):
        def run(zero):
            for bs, t0, tb in blks:
                window_block(bs, t0, tb, zero)
        return run

    def item_state(zero):
        del zero
        pool_o[...] = pool_ext[:, POOL_PAD + t - POOL_HIST:POOL_PAD + t, :]
        conv_o[...] = conv_ext[:, CONF_PAD + t - CONF_HIST:CONF_PAD + t, :]
        sconv_o[...] = sc_ext[:, SCONV_PAD + t - SCONV_HIST:SCONV_PAD + t, :]
        if not cfg.has_state:
            pool_ext[:, 0:POOL_PAD, :] = pool_ext[:, t:t + POOL_PAD, :]
            conv_ext[:, 0:CONF_PAD, :] = conv_ext[:, t:t + CONF_PAD, :]
            sc_ext[:, 0:SCONV_PAD, :] = sc_ext[:, t:t + SCONV_PAD, :]

    def item_a(zero):
        pooled = buf_a[...].reshape(tm, BRANCH_W).astype(BF16)
        scale = pool_sc[...] + zero[0:1, :]
        br["a"] = (jnp.dot(pooled, pool_w[...], preferred_element_type=F32) * scale).astype(BF16)

    def item_b(zero):
        cb = _ln(buf_b[...].reshape(tm, BRANCH_W), clg[...] + zero[0:1, :], clb[...])
        br["b"] = (cb * _sigmoid(cb)).astype(BF16)

    def item_c(zero):
        br["c"] = (proj(768, 1024) * (buf_c[...].reshape(tm, BRANCH_W) + zero[0:1, :])).astype(BF16)

    def item_u(zero):
        br["u"] = _gelu_tanh(proj(1536, 1792) + zero[0:1, :])

    def item_v(zero):
        v = _ln(_gelu_tanh(proj(1792, 2048)), slg[...] + zero[0:1, :], slb[...])
        if cfg.has_state:
            v_o[...] = v.reshape(nb, t, BRANCH_W)
        else:
            v_o[...] = v[tm - cfg.v_rows:, :].reshape(1, cfg.v_rows, BRANCH_W)
        br["v"] = v

    def item_d(zero):
        head = lax.broadcasted_iota(jnp.int32, (CHUNK, BRANCH_W), 1) // SGU_HW
        bias = sbias[...] + jnp.concatenate([zero] * (CHUNK // SUBLANES), axis=0)
        mixed = []
        for j in range(tm // CHUNK):
            vj = br["v"][j * CHUNK:(j + 1) * CHUNK, :]
            stacked = jnp.concatenate([jnp.where(head == h, vj, 0.0) for h in range(SGU_HEADS)], axis=0)
            mixed.append(jnp.dot(wcat[...], stacked.astype(BF16), preferred_element_type=F32) + bias)
        br["d"] = (br["u"] * jnp.concatenate(mixed, axis=0)).astype(BF16)

    blocks = _row_blocks(nb, t)
    n_gate = N_BRANCH * D_MODEL // COL_TILE
    per_item = -(-len(blocks) // (n_gate // 2))
    items = [item_windows(blocks[i:i + per_item]) for i in range(0, len(blocks), per_item)]
    items += [item_state, item_c, item_b, item_u, item_v, item_a, item_d]
    assert len(items) <= n_gate
    zero = jnp.zeros((SUBLANES, BRANCH_W), F32)
    for k in range(n_gate):
        next_zero = gate_chunk(k)
        if k < len(items):
            items[k](zero)
        zero = next_zero

    n_ct = D_MODEL // COL_TILE
    brs = [br["a"], br["b"], br["c"], br["d"]]
    for c in range(n_ct):
        part = None
        for i in range(N_BRANCH):
            g0 = i * D_MODEL + c * COL_TILE
            term = gate_buf[:, g0:g0 + COL_TILE] * jnp.dot(
                brs[i], w_br[i * n_ct + c], preferred_element_type=F32)
            part = term if part is None else part + term
        xb_buf[:, c * COL_TILE:(c + 1) * COL_TILE] = part.astype(BF16)
    for c in range(n_ct):
        cs = slice(c * COL_TILE, (c + 1) * COL_TILE)
        h1_o[:, cs] = x_ref[:, cs] + jnp.dot(xb_buf[...], w_out[c], preferred_element_type=F32)

    xn2 = _rms(h1_o[...], n2g[...])
    x_hi = xn2.astype(BF16)
    xn2_o[...] = x_hi
    x_lo = (xn2 - x_hi.astype(F32)).astype(BF16)
    hi_both = jnp.dot(x_hi, w_rt[...], preferred_element_type=F32)
    lo_hi = jnp.dot(x_lo, w_rt[:, 0:LANES], preferred_element_type=F32)
    logits = (hi_both[:, 0:LANES] + (lo_hi + hi_both[:, LANES:2 * LANES])) + b_rt[...]
    lane = lax.broadcasted_iota(jnp.int32, (tm, LANES), 1)
    is_g = (lane >= ROUTE_GROUP_LANE0) & (lane < ROUTE_GROUP_LANE0 + N_GROUPS)
    glog = jnp.where(is_g, logits, NEG_BIG)
    gmax = jnp.max(glog, axis=-1, keepdims=True)
    lane_f = lane.astype(F32)
    no_lane = float(4 * LANES)
    gsel = jnp.min(jnp.where(glog == gmax, lane_f, no_lane), axis=-1, keepdims=True) - ROUTE_GROUP_LANE0
    pg = 1.0 / jnp.sum(jnp.where(is_g, jnp.exp(glog - gmax), 0.0), axis=-1, keepdims=True)
    in_grp = (lane < N_EXPERTS) & ((lane // EXPERTS_PER_GROUP) == gsel.astype(jnp.int32))
    el = jnp.where(in_grp, logits, NEG_BIG)
    m1 = jnp.max(el, axis=-1, keepdims=True)
    i1 = jnp.min(jnp.where(in_grp & (el == m1), lane_f, no_lane), axis=-1, keepdims=True)
    rest = in_grp & (lane_f != i1)
    el2 = jnp.where(rest, logits, NEG_BIG)
    m2 = jnp.max(el2, axis=-1, keepdims=True)
    i2 = jnp.min(jnp.where(rest & (el2 == m2), lane_f, no_lane), axis=-1, keepdims=True)
    e21 = jnp.exp(m2 - m1)
    w1 = pg / (1.0 + e21)
    w2 = pg * e21 / (1.0 + e21)
    route_o[...] = jnp.where(lane == 0, i1, jnp.where(lane == 1, i2,
                             jnp.where(lane == 2, w1, jnp.where(lane == 3, w2, 0.0))))


def _layer_spec(shape, layer):
    nd = len(shape)
    return pl.BlockSpec((1,) + shape[1:], lambda *_: (layer,) + (0,) * (nd - 1), pipeline_mode=pl.Buffered(1))


def _mixer_call(cfg, n_seq, n_rows_total, x2d, x_tile_off, states, weights, big_weights, aliased):
    nb, t = cfg.nb, cfg.t
    tm = nb * t
    if cfg.has_state:
        grid = (n_seq // nb,)
        tok = lambda i: (x_tile_off + i, 0)
        out_tok = lambda i: (cfg.row_off + i, 0)
        seq3 = lambda i: (i, 0, 0)
    else:
        grid = (n_seq, cfg.seq_len // t)
        nt = grid[1]
        tok = lambda b, c: (x_tile_off + b * nt + c, 0)
        out_tok = lambda b, c: (cfg.row_off + b * nt + c, 0)
        seq3 = lambda b, c: (b, 0, 0)

    in_specs = [pl.BlockSpec((tm, D_MODEL), tok)]
    args = [x2d]
    if cfg.has_state:
        for s in states:
            in_specs.append(pl.BlockSpec((1, nb) + s.shape[2:], lambda i: (cfg.layer, i, 0, 0)))
            args.append(s)
    for w in weights:
        in_specs.append(_layer_spec(w.shape, cfg.layer))
        args.append(w)
    for w in big_weights:
        in_specs.append(pl.BlockSpec(memory_space=pl.ANY))
        args.append(w)
    io_alias = {}
    if cfg.has_state:
        for k, a in enumerate(aliased):
            in_specs.append(pl.BlockSpec(memory_space=pl.ANY))
            io_alias[len(args)] = k
            args.append(a)

    out_shape = [
        jax.ShapeDtypeStruct((n_rows_total, D_MODEL), F32),
        jax.ShapeDtypeStruct((n_rows_total, LANES), F32),
        jax.ShapeDtypeStruct((n_rows_total, D_MODEL), BF16),
        jax.ShapeDtypeStruct((n_seq, POOL_HIST, BRANCH_W), F32),
        jax.ShapeDtypeStruct((n_seq, CONF_HIST, BRANCH_W), F32),
        jax.ShapeDtypeStruct((n_seq, SCONV_HIST, BRANCH_W), F32),
        jax.ShapeDtypeStruct((n_seq, cfg.v_rows, BRANCH_W), F32),
    ]
    out_specs = [
        pl.BlockSpec((tm, D_MODEL), out_tok),
        pl.BlockSpec((tm, LANES), out_tok),
        pl.BlockSpec((tm, D_MODEL), out_tok),
        pl.BlockSpec((nb, POOL_HIST, BRANCH_W), seq3),
        pl.BlockSpec((nb, CONF_HIST, BRANCH_W), seq3),
        pl.BlockSpec((nb, SCONV_HIST, BRANCH_W), seq3),
        pl.BlockSpec((nb, cfg.v_rows, BRANCH_W), seq3),
    ]
    scratch = [
        pltpu.VMEM((nb, POOL_PAD + t, BRANCH_W), F32),
        pltpu.VMEM((nb, POOL_PAD + t, BRANCH_W), F32),
        pltpu.VMEM((nb, POOL_PAD + t, BRANCH_W), F32),
        pltpu.VMEM((nb, CONF_PAD + t, BRANCH_W), F32),
        pltpu.VMEM((SUBLANES, nb, t + CONF_PAD - SUBLANES, BRANCH_W), F32),
        pltpu.VMEM((nb, SCONV_PAD + t, BRANCH_W), F32),
        pltpu.VMEM((nb, t, BRANCH_W), F32),
        pltpu.VMEM((nb, t, BRANCH_W), F32),
        pltpu.VMEM((nb, t, BRANCH_W), F32),
        pltpu.VMEM((tm, N_BRANCH * D_MODEL), F32),
        pltpu.VMEM((tm, D_MODEL), BF16),
        pltpu.VMEM((IN_COLS // COL_TILE, D_MODEL, COL_TILE), BF16),
        pltpu.VMEM((N_BRANCH * D_MODEL // COL_TILE, BRANCH_W, COL_TILE), BF16),
        pltpu.VMEM((D_MODEL // COL_TILE, D_MODEL, COL_TILE), BF16),
        pltpu.VMEM((2, W_IN_ROWS, IN_COLS), F32),
        pltpu.VMEM((2, BRANCH_W, D_MODEL), F32),
        pltpu.SemaphoreType.DMA((2,)),
    ]
    return pl.pallas_call(
        functools.partial(_mixer_kernel, cfg),
        grid=grid,
        in_specs=in_specs,
        out_specs=out_specs,
        out_shape=out_shape,
        scratch_shapes=scratch,
        input_output_aliases=io_alias,
        compiler_params=pltpu.CompilerParams(
            dimension_semantics=("arbitrary",) * len(grid), vmem_limit_bytes=VMEM_LIMIT),
        name="mixer_sample" if cfg.has_state else "mixer_prompt",
    )(*args)


def _sort_kernel(xn_ref, route_ref, xl_o, pos_o, nch_o):
    t = SORT_TILE
    xn = xn_ref[...]
    r = route_ref[...]
    lane_i = lax.broadcasted_iota(jnp.int32, (t, LANES), 1)
    lane = lane_i.astype(F32)
    hit1 = lane == r[:, 0:1]
    hit2 = lane == r[:, 1:2]
    onehot = jnp.where(hit1 | hit2, 1.0, 0.0)
    row = lax.broadcasted_iota(jnp.int32, (t, t), 0)
    col = lax.broadcasted_iota(jnp.int32, (t, t), 1)
    strict_lower = jnp.where(col < row, 1.0, 0.0).astype(BF16)
    before = jnp.dot(strict_lower, onehot.astype(BF16), preferred_element_type=F32)
    cnt = jnp.sum(onehot, axis=0, keepdims=True)
    nch = jnp.floor((cnt + (SEG_PAD - 1)) * (1.0 / SEG_PAD))
    er = lax.broadcasted_iota(jnp.int32, (LANES, LANES), 0)
    ec = lax.broadcasted_iota(jnp.int32, (LANES, LANES), 1)
    strict_upper = jnp.where(er < ec, 1.0, 0.0).astype(BF16)
    nch8 = jnp.broadcast_to(nch, (SUBLANES, LANES)).astype(BF16)
    seg0 = jnp.dot(nch8, strict_upper, preferred_element_type=F32)[0:1, :] * SEG_PAD
    where = seg0 + before
    pos0 = jnp.sum(jnp.where(hit1, where, 0.0), axis=-1, keepdims=True)
    pos1 = jnp.sum(jnp.where(hit2, where, 0.0), axis=-1, keepdims=True)
    pos = jnp.where(lane_i == 0, pos0, jnp.where(lane_i == 1, pos1, -1.0))
    pos_o[...] = pos
    nch_o[...] = jnp.broadcast_to(nch, (1, SUBLANES, LANES))
    pos_t = jnp.transpose(pos)
    p0 = pos_t[0:1, :]
    p1 = pos_t[1:2, :]
    used_rows = jnp.sum(nch) * SEG_PAD

    def perm_block(blk):
        dst = (lax.broadcasted_iota(jnp.int32, (PERM_BLOCK, t), 0) + blk * PERM_BLOCK).astype(F32)
        perm = jnp.where((dst == p0) | (dst == p1), 1.0, 0.0).astype(BF16)
        xl_o[0, blk * PERM_BLOCK:(blk + 1) * PERM_BLOCK, :] = jnp.dot(
            perm, xn, preferred_element_type=F32).astype(BF16)

    for blk in range(N_PERM_BLOCKS - 1):
        perm_block(blk)
    pl.when(used_rows > (N_PERM_BLOCKS - 1) * PERM_BLOCK)(functools.partial(perm_block, N_PERM_BLOCKS - 1))


def _sort_call(xn2, route):
    n = xn2.shape[0]
    nt = n // SORT_TILE
    return pl.pallas_call(
        _sort_kernel,
        grid=(nt,),
        in_specs=[pl.BlockSpec((SORT_TILE, D_MODEL), lambda i: (i, 0)),
                  pl.BlockSpec((SORT_TILE, LANES), lambda i: (i, 0))],
        out_specs=[pl.BlockSpec((1, LOCAL_ROWS, D_MODEL), lambda i: (i, 0, 0)),
                   pl.BlockSpec((SORT_TILE, LANES), lambda i: (i, 0)),
                   pl.BlockSpec((1, SUBLANES, LANES), lambda i: (i, 0, 0))],
        out_shape=[jax.ShapeDtypeStruct((nt + SPARE_BLOCKS, LOCAL_ROWS, D_MODEL), BF16),
                   jax.ShapeDtypeStruct((n, LANES), F32),
                   jax.ShapeDtypeStruct((nt, SUBLANES, LANES), F32)],
        compiler_params=pltpu.CompilerParams(dimension_semantics=("arbitrary",), vmem_limit_bytes=VMEM_LIMIT),
        name="moe_sort",
    )(xn2, route)


def _chunk_copy(src, src_row, dst, dst_row, sem):
    return pltpu.make_async_copy(src.at[pl.ds(src_row, SEG_PAD), :], dst.at[pl.ds(dst_row, SEG_PAD), :], sem)


def _expert_kernel(trash_row0, se_ref, nv_ref, src_ref, dst_ref, xl_in, wg, wu, wd, xl_io, *scratch):
    del se_ref, xl_in
    ns = EXP_SLOTS
    xbuf, ybuf, (sem_in, sem_out) = scratch[:ns], scratch[ns:2 * ns], scratch[2 * ns:]
    s = pl.program_id(0)
    nv = nv_ref[0]

    def gather(step, sl):
        return [_chunk_copy(xl_io, pl.multiple_of(src_ref[step * CHUNKS_PER_STEP + j], SEG_PAD),
                            xbuf[sl], j * SEG_PAD, sem_in.at[sl]) for j in range(CHUNKS_PER_STEP)]

    def write_back(step, sl):
        return [_chunk_copy(ybuf[sl], j * SEG_PAD, xl_io,
                            pl.multiple_of(dst_ref[step * CHUNKS_PER_STEP + j], SEG_PAD), sem_out.at[sl])
                for j in range(CHUNKS_PER_STEP)]

    def spare_write(sl):
        return [_chunk_copy(ybuf[sl], j * SEG_PAD, xl_io, trash_row0 + sl * EXP_TILE + j * SEG_PAD,
                            sem_out.at[sl]) for j in range(CHUNKS_PER_STEP)]

    @pl.when(s == 0)
    def _():
        for ahead in range(ns - 1):
            for c in gather(ahead, ahead):
                c.start()
        for sl in range(ns):
            ybuf[sl][...] = jnp.zeros_like(ybuf[sl])
            for c in spare_write(sl):
                c.start()

    def live_step(sl):
        for c in gather(s + ns - 1, (sl + ns - 1) % ns):
            c.start()
        for c in gather(s, sl):
            c.wait()
        xb = xbuf[sl][...]
        a = jnp.dot(xb, wg[0, 0].astype(BF16), preferred_element_type=F32)
        b = jnp.dot(xb, wu[0, 0].astype(BF16), preferred_element_type=F32)
        hh = (a * _sigmoid(a) * b).astype(BF16)
        y = jnp.dot(hh, wd[0, 0].astype(BF16), preferred_element_type=F32).astype(BF16)
        for c in write_back(s, sl):
            c.wait()
        ybuf[sl][...] = y
        for c in write_back(s, sl):
            c.start()

    def drain_step(sl):
        for ahead in range(ns - 1):
            for c in gather(s + ahead, (sl + ahead) % ns):
                c.wait()
        for every in range(ns):
            for c in spare_write(every):
                c.wait()

    for sl in range(ns):
        pl.when((s < nv) & (s % ns == sl))(functools.partial(live_step, sl))
        pl.when((s == nv) & (s % ns == sl))(functools.partial(drain_step, sl))


def _expert_call(xl, trash_row0, step_expert, n_valid, chunk_src, chunk_dst, layer, w_gate, w_up, w_down):
    n_steps = step_expert.shape[0] - (EXP_SLOTS - 1)
    wmap = lambda s, se, *_: (layer, se[s], 0, 0)
    grid_spec = pltpu.PrefetchScalarGridSpec(
        num_scalar_prefetch=4,
        grid=(n_steps,),
        in_specs=[pl.BlockSpec(memory_space=pl.ANY),
                  pl.BlockSpec((1, 1, D_MODEL, D_EXPERT), wmap),
                  pl.BlockSpec((1, 1, D_MODEL, D_EXPERT), wmap),
                  pl.BlockSpec((1, 1, D_EXPERT, D_MODEL), wmap)],
        out_specs=pl.BlockSpec(memory_space=pl.ANY),
        scratch_shapes=[pltpu.VMEM((EXP_TILE, D_MODEL), BF16) for _ in range(2 * EXP_SLOTS)] + [
                        pltpu.SemaphoreType.DMA((EXP_SLOTS,)), pltpu.SemaphoreType.DMA((EXP_SLOTS,))],
    )
    return pl.pallas_call(
        functools.partial(_expert_kernel, trash_row0),
        grid_spec=grid_spec,
        out_shape=jax.ShapeDtypeStruct(xl.shape, BF16),
        input_output_aliases={4: 0},
        compiler_params=pltpu.CompilerParams(dimension_semantics=("arbitrary",), has_side_effects=True),
        name="moe_experts",
    )(step_expert, n_valid, chunk_src, chunk_dst, xl, w_gate, w_up, w_down)


def _combine_kernel(final_norm, n_first, h1_ref, route_ref, pos_ref, fg, yl_ref, *outs):
    t = SORT_TILE
    r = route_ref[...]
    p = pos_ref[...]
    w1, w2 = r[:, 2:3], r[:, 3:4]
    pos0, pos1 = p[:, 0:1], p[:, 1:2]
    *outs, acc_ref = outs

    def gathered(blk):
        src = (lax.broadcasted_iota(jnp.int32, (t, PERM_BLOCK), 1) + blk * PERM_BLOCK).astype(F32)
        pw = jnp.where(src == pos0, w1, 0.0) + jnp.where(src == pos1, w2, 0.0)
        return jnp.dot(pw.astype(BF16), yl_ref[0, blk * PERM_BLOCK:(blk + 1) * PERM_BLOCK, :],
                       preferred_element_type=F32)

    acc = h1_ref[...]
    for blk in range(N_PERM_BLOCKS - 1):
        acc = acc + gathered(blk)
    acc_ref[...] = acc

    @pl.when(jnp.max(p) >= (N_PERM_BLOCKS - 1) * PERM_BLOCK)
    def _():
        acc_ref[...] += gathered(N_PERM_BLOCKS - 1)

    if not final_norm:
        outs[0][...] = acc_ref[...]
        return
    y = _rms(acc_ref[...], fg[...])
    i = pl.program_id(0)

    @pl.when(i < n_first)
    def _():
        outs[0][...] = y

    @pl.when(i >= n_first)
    def _():
        outs[1][...] = y


def _combine_call(h1, route, pos, yl, final_g, final_norm, n_first_rows):
    n = h1.shape[0]
    nt = n // SORT_TILE
    n_first = n_first_rows // SORT_TILE
    tok = lambda i: (i, 0)
    if final_norm:
        out_shape = [jax.ShapeDtypeStruct((n_first_rows, D_MODEL), F32),
                     jax.ShapeDtypeStruct((n - n_first_rows, D_MODEL), F32)]
        out_specs = [pl.BlockSpec((SORT_TILE, D_MODEL), lambda i: (jnp.minimum(i, n_first - 1), 0)),
                     pl.BlockSpec((SORT_TILE, D_MODEL), lambda i: (jnp.maximum(i - n_first, 0), 0))]
    else:
        out_shape = [jax.ShapeDtypeStruct((n, D_MODEL), F32)]
        out_specs = [pl.BlockSpec((SORT_TILE, D_MODEL), tok)]
    return pl.pallas_call(
        functools.partial(_combine_kernel, final_norm, n_first),
        grid=(nt,),
        in_specs=[pl.BlockSpec((SORT_TILE, D_MODEL), tok),
                  pl.BlockSpec((SORT_TILE, LANES), tok),
                  pl.BlockSpec((SORT_TILE, LANES), tok),
                  pl.BlockSpec((1, D_MODEL), lambda i: (0, 0)),
                  pl.BlockSpec((1, LOCAL_ROWS, D_MODEL), lambda i: (i, 0, 0))],
        out_specs=out_specs,
        out_shape=out_shape,
        scratch_shapes=[pltpu.VMEM((SORT_TILE, D_MODEL), F32)],
        compiler_params=pltpu.CompilerParams(dimension_semantics=("arbitrary",), vmem_limit_bytes=VMEM_LIMIT),
        name="moe_combine",
    )(h1, route, pos, final_g, yl)


def _expert_tables(nch, n_steps, trash_row0):
    nt = nch.shape[0]
    cps = CHUNKS_PER_STEP
    seg_row0 = (jnp.cumsum(nch, axis=1) - nch) * SEG_PAD
    first = jnp.cumsum(nch, axis=0) - nch
    tot = jnp.sum(nch, axis=0)
    steps = (tot + cps - 1) // cps
    step_end = jnp.cumsum(steps)
    n_valid = step_end[-1:]
    s_ids = jnp.arange(n_steps, dtype=jnp.int32)
    step_expert = jnp.minimum(jnp.sum((step_end[None, :] <= s_ids[:, None]).astype(jnp.int32), axis=1),
                              N_EXPERTS - 1)
    sel = (step_expert[:, None] == jnp.arange(N_EXPERTS, dtype=jnp.int32)[None, :]).astype(jnp.int32)
    step0 = sel @ (step_end - steps)
    tot_s = sel @ tot
    first_s = sel @ first.T
    row0_s = sel @ seg_row0.T
    k = (s_ids - step0)[:, None] * cps + jnp.arange(cps, dtype=jnp.int32)[None, :]
    ok = (k < tot_s[:, None]) & (s_ids < n_valid[0])[:, None]
    tile = jnp.sum((first_s[:, None, :] <= k[:, :, None]).astype(jnp.int32), axis=2) - 1
    tsel = (tile[:, :, None] == jnp.arange(nt, dtype=jnp.int32)[None, None, :]).astype(jnp.int32)
    first_k = jnp.sum(tsel * first_s[:, None, :], axis=2)
    row0_k = jnp.sum(tsel * row0_s[:, None, :], axis=2)
    row = tile * LOCAL_ROWS + row0_k + (k - first_k) * SEG_PAD
    live = (s_ids < n_valid[0])[:, None]
    src = jnp.where(live, jnp.where(ok, row, row[:, 0:1]), 0)
    spare = trash_row0 + (s_ids % EXP_SLOTS)[:, None] * EXP_TILE + jnp.arange(cps, dtype=jnp.int32)[None, :] * SEG_PAD
    dst = jnp.where(ok, row, spare)
    return (step_expert, n_valid.astype(jnp.int32), src.reshape(-1).astype(jnp.int32),
            dst.reshape(-1).astype(jnp.int32))


def _moe(h1, route, xn2, layer, w_gate, w_up, w_down, final_g, final_norm, n_first_rows):
    n = h1.shape[0]
    nt = n // SORT_TILE
    xl, pos, nch = _sort_call(xn2, route)
    nch = nch[:, 0, :N_EXPERTS].astype(jnp.int32)
    max_chunks = (2 * n) // SEG_PAD + nt * N_EXPERTS
    n_steps = max_chunks // CHUNKS_PER_STEP + N_EXPERTS + 1
    trash_row0 = nt * LOCAL_ROWS
    step_expert, n_valid, chunk_src, chunk_dst = _expert_tables(nch, n_steps + EXP_SLOTS - 1, trash_row0)
    yl = _expert_call(xl.reshape((nt + SPARE_BLOCKS) * LOCAL_ROWS, D_MODEL), trash_row0, step_expert, n_valid,
                      chunk_src, chunk_dst, layer, w_gate, w_up, w_down)
    return _combine_call(h1, route, pos, yl.reshape(nt + SPARE_BLOCKS, LOCAL_ROWS, D_MODEL), final_g, final_norm,
                         n_first_rows)


def kernel(x_prompt, x_sample, state_pool, state_conv, state_sconv, norm1_g, w_in, pool_w, pool_scale, conf_dw, conf_dw_b, conf_ln_g, conf_ln_b, sconv_w, sgu_ln_g, sgu_ln_b, sgu_ws, sgu_b, w_branch, w_out, norm2_g, router_g, router_g_b, router_e, router_e_b, w_gate, w_up, w_down, final_g):
    depth = w_in.shape[0]
    bp, seq, _ = x_prompt.shape
    bs, dseq, _ = x_sample.shape
    n_p, n_s = bp * seq, bs * dseq
    n = n_p + n_s
    past_len = 16384
    assert seq % MIX_TILE == 0 and n_s % MIX_TILE_S == 0 and n_p % MIX_TILE_S == 0 and dseq == SUBLANES
    assert MIX_TILE % CHUNK == 0 and MIX_TILE_S % CHUNK == 0

    cfg_p = MixCfg(layer=0, nb=1, t=MIX_TILE, has_state=False, start_pos=0, row_off=0, v_rows=CHUNK, seq_len=seq)
    cfg_s = MixCfg(layer=0, nb=MIX_TILE_S // dseq, t=dseq, has_state=True, start_pos=past_len,
                   row_off=n_p // MIX_TILE_S, v_rows=dseq, seq_len=dseq)
    big_weights = (w_in, w_branch, w_out)

    row = lambda a: a[:, None, :]
    rep8 = lambda a: jnp.broadcast_to(a[..., None, :], a.shape[:-1] + (SUBLANES, a.shape[-1]))
    tril = jnp.tril(jnp.ones((CHUNK, CHUNK), F32))
    eye_blk = jnp.kron(jnp.eye(CHUNK // dseq, dtype=F32), jnp.ones((dseq, dseq), F32))
    cat = lambda w: jnp.concatenate([w[:, h] for h in range(SGU_HEADS)], axis=2).astype(BF16)
    wcat_p = cat(sgu_ws * tril)
    wcat_s = cat(jnp.tile(sgu_ws[:, :, :dseq, :dseq], (1, 1, CHUNK // dseq, CHUNK // dseq)) * (tril * eye_blk))
    bias_p = jnp.repeat(jnp.swapaxes(sgu_b, 1, 2), SGU_HW, axis=2)
    bias_s = jnp.tile(jnp.repeat(jnp.swapaxes(sgu_b[:, :, :dseq], 1, 2), SGU_HW, axis=2), (1, CHUNK // dseq, 1))
    lane_pad = LANES - N_EXPERTS - N_GROUPS
    w_rt32 = jnp.pad(jnp.concatenate([router_e, router_g], axis=2), ((0, 0), (0, 0), (0, lane_pad)))
    w_rt_hi = w_rt32.astype(BF16)
    w_rt = jnp.concatenate([w_rt_hi, (w_rt32 - w_rt_hi.astype(F32)).astype(BF16)], axis=2)
    b_rt = row(jnp.pad(jnp.concatenate([router_e_b, router_g_b], axis=1), ((0, 0), (0, lane_pad))))
    same_group = jnp.eye(len(POOL_WINDOWS), dtype=F32)[None, :, None, :, None]
    pool_bd = (pool_w[:, :, :, None, :] * same_group).reshape(depth, BRANCH_W, BRANCH_W).astype(BF16)

    def weights(wcat, sbias):
        return [row(norm1_g), pool_bd, row(pool_scale), rep8(conf_dw), rep8(conf_dw_b), row(conf_ln_g),
                row(conf_ln_b), rep8(sconv_w), row(sgu_ln_g), row(sgu_ln_b), wcat, sbias, row(norm2_g), w_rt, b_rt]

    weights_p, weights_s = weights(wcat_p, bias_p), weights(wcat_s, bias_s)
    final_row = final_g.reshape(1, -1)

    h_p, h_s = x_prompt.reshape(n_p, D_MODEL), x_sample.reshape(n_s, D_MODEL)
    off_p, off_s = 0, 0
    states_out = []
    for l in range(depth):
        h1, route, xn2, pool_p, conv_p, sconv_p, v_p = _mixer_call(
            cfg_p._replace(layer=l), bp, n, h_p, off_p, None, weights_p, big_weights, None)
        h1, route, xn2, pool_s, conv_s, sconv_s, v_s = _mixer_call(
            cfg_s._replace(layer=l), bs, n, h_s, off_s, (state_pool, state_conv, state_sconv),
            weights_s, big_weights, (h1, route, xn2))
        states_out.append((pool_p, conv_p, sconv_p, v_p, pool_s, conv_s, sconv_s, v_s))

        outs = _moe(h1, route, xn2, l, w_gate, w_up, w_down, final_row, l == depth - 1, n_p)
        h_p = h_s = outs[0]
        off_p, off_s = 0, n_p // MIX_TILE_S

    y_prompt = outs[0].reshape(bp, seq, D_MODEL)
    y_sample = outs[1].reshape(bs, dseq, D_MODEL)
    st = [jnp.stack([s[k] for s in states_out]) for k in range(8)]
    return (y_prompt, y_sample, st[0], st[1], st[2], st[3], st[4], st[5], st[6], st[7])
```

```python
import functools
from typing import NamedTuple

import jax
import jax.numpy as jnp
from jax import lax
from jax.experimental import pallas as pl
from jax.experimental.pallas import tpu as pltpu

F32 = jnp.float32
BF16 = jnp.bfloat16

D_MODEL = 1024
BRANCH_W = 256
N_BRANCH = 4
POOL_WINDOWS = (2, 4, 8, 16)
POOL_GW = 64
POOL_HIST = 15
CONF_WIDTH = 31
CONF_HIST = 30
SCONV_WIDTH = 3
SCONV_HIST = 2
CHUNK = 128
PAST_LEN = 16384
SGU_HEADS = 4
SGU_HW = 64
N_GROUPS = 4
EXPERTS_PER_GROUP = 8
N_EXPERTS = 32
D_EXPERT = 256
RMS_EPS = 1e-6
LN_EPS = 1e-5
IN_COLS = 6144
GATE_COL0 = 2048

COL_TILE = 256
LANES = 128
SUBLANES = 8
POOL_PAD = 32
CONF_PAD = 32
SCONV_PAD = 8
ROW_BLOCK = 32
W_IN_ROWS = 64
ROUTE_GROUP_LANE0 = 32
NEG_BIG = -3.0e38

MIX_TILE = 512
MIX_TILE_S = 256
SORT_TILE = 512
SEG_PAD = 16
PERM_BLOCK = 256
N_PERM_BLOCKS = -(-(2 * SORT_TILE + N_EXPERTS * (SEG_PAD - 1)) // PERM_BLOCK)
LOCAL_ROWS = N_PERM_BLOCKS * PERM_BLOCK
EXP_TILE = 512
CHUNKS_PER_STEP = EXP_TILE // SEG_PAD
EXP_SLOTS = 3
SPARE_BLOCKS = -(-EXP_SLOTS * EXP_TILE // LOCAL_ROWS)
VMEM_LIMIT = 56 * 1024 * 1024


class MixCfg(NamedTuple):
    layer: int
    nb: int
    t: int
    has_state: bool
    start_pos: int
    row_off: int
    v_rows: int
    seq_len: int


def _rms(x, g):
    return x * lax.rsqrt(jnp.mean(x * x, axis=-1, keepdims=True) + RMS_EPS) * g


def _ln(x, g, b):
    mu = jnp.mean(x, axis=-1, keepdims=True)
    xc = x - mu
    return xc * lax.rsqrt(jnp.mean(xc * xc, axis=-1, keepdims=True) + LN_EPS) * g + b


def _sigmoid(x):
    return 0.5 * jnp.tanh(0.5 * x) + 0.5


def _zero_like_bits(x):
    bits = lax.bitcast_convert_type(x, jnp.uint32)
    return ((bits >> 16) >> 16).astype(jnp.int32).astype(F32)


def _gelu_tanh(x):
    return 0.5 * x * (1.0 + jnp.tanh(0.7978845608028654 * (x + 0.044715 * (x * x * x))))


def _row_blocks(nb, t):
    if t >= ROW_BLOCK:
        return [(slice(b, b + 1), t0, ROW_BLOCK) for b in range(nb) for t0 in range(0, t, ROW_BLOCK)]
    bb = ROW_BLOCK // t
    return [(slice(b0, b0 + bb), 0, t) for b0 in range(0, nb, bb)]


def _load_big_weights(layer, w_in_hbm, w_br_hbm, w_out_hbm, w_in, w_br, w_out, stage, stage_b, sem):
    n_in, n_out = IN_COLS // COL_TILE, D_MODEL // COL_TILE
    jobs = ([("in", r) for r in range(0, D_MODEL, W_IN_ROWS)] + [("out", r) for r in range(0, D_MODEL, BRANCH_W)]
            + [("br", i) for i in range(N_BRANCH)])

    def copy(job, slot):
        kind, j = job
        if kind == "in":
            return pltpu.make_async_copy(w_in_hbm.at[layer, pl.ds(j, W_IN_ROWS), :], stage.at[slot], sem.at[slot])
        src = w_br_hbm.at[layer, j] if kind == "br" else w_out_hbm.at[layer, pl.ds(j, BRANCH_W), :]
        return pltpu.make_async_copy(src, stage_b.at[slot], sem.at[slot])

    def col(buf, slot, c):
        return buf[slot, :, c * COL_TILE:(c + 1) * COL_TILE]

    copy(jobs[0], 0).start()
    for n, job in enumerate(jobs):
        slot = n % 2
        if n + 1 < len(jobs):
            copy(jobs[n + 1], 1 - slot).start()
        copy(job, slot).wait()
        kind, j = job
        if kind == "in":
            for c in range(n_in):
                tile = col(stage, slot, c)
                w_in[c, j:j + W_IN_ROWS, :] = (tile * 0.5 if c * COL_TILE >= GATE_COL0 else tile).astype(BF16)
        elif kind == "out":
            for c in range(n_out):
                w_out[c, j:j + BRANCH_W, :] = col(stage_b, slot, c).astype(BF16)
        else:
            for c in range(n_out):
                w_br[j * n_out + c] = (col(stage_b, slot, c) * 0.5).astype(BF16)


def _mixer_kernel(cfg, *refs):
    nb, t, tm = cfg.nb, cfg.t, cfg.nb * cfg.t
    refs = list(refs)
    x_ref = refs.pop(0)
    if cfg.has_state:
        pool_st, conv_st, sconv_st = (refs.pop(0).at[0] for _ in range(3))
    (n1g, pool_w, pool_sc, cdw, cdb, clg, clb, scw, slg, slb, wcat, sbias, n2g, w_rt, b_rt) = (
        r.at[0] for r in refs[:15])
    w_in_hbm, w_br_hbm, w_out_hbm = refs[15:18]
    refs = refs[18:]
    if cfg.has_state:
        refs = refs[3:]
    h1_o, route_o, xn2_o, pool_o, conv_o, sconv_o, v_o = refs[:7]
    (pool_ext, sum_a, sum_b, conv_ext, conv_sh, sc_ext, buf_a, buf_b, buf_c, gate_buf, xb_buf,
     w_in, w_br, w_out, stage, stage_b, w_sem) = refs[7:]

    first_step = pl.program_id(0) == 0
    if not cfg.has_state:
        first_step = first_step & (pl.program_id(1) == 0)
    pl.when(first_step)(functools.partial(
        _load_big_weights, cfg.layer, w_in_hbm, w_br_hbm, w_out_hbm, w_in, w_br, w_out, stage, stage_b, w_sem))

    if cfg.has_state:
        seq_pos0 = cfg.start_pos
        pool_ext[:, 0:POOL_PAD - SUBLANES, :] = jnp.zeros((nb, POOL_PAD - SUBLANES, BRANCH_W), F32)
        pool_ext[:, POOL_PAD - POOL_HIST:POOL_PAD, :] = pool_st[...]
        conv_ext[:, CONF_PAD - CONF_HIST:CONF_PAD, :] = conv_st[...]
        sc_ext[:, SCONV_PAD - SCONV_HIST:SCONV_PAD, :] = sconv_st[...]
    else:
        c = pl.program_id(1)
        seq_pos0 = cfg.start_pos + c * t

        @pl.when(c == 0)
        def _():
            pool_ext[:, 0:POOL_PAD, :] = jnp.zeros((nb, POOL_PAD, BRANCH_W), F32)
            conv_ext[:, 0:CONF_PAD, :] = jnp.zeros((nb, CONF_PAD, BRANCH_W), F32)
            sc_ext[:, 0:SCONV_PAD, :] = jnp.zeros((nb, SCONV_PAD, BRANCH_W), F32)

    x = x_ref[...]
    xb_buf[...] = _rms(x, n1g[...]).astype(BF16)

    def proj(lo, hi):
        assert lo % COL_TILE == 0 and hi == lo + COL_TILE
        return jnp.dot(xb_buf[...], w_in[lo // COL_TILE], preferred_element_type=F32)

    a_pool = proj(0, 256)
    pool_ext[:, POOL_PAD:, :] = a_pool.reshape(nb, t, BRANCH_W)
    glu = proj(256, 512) * _sigmoid(proj(512, 768))
    conv_ext[:, CONF_PAD:, :] = glu.reshape(nb, t, BRANCH_W)
    z = proj(1024, 1280) * proj(1280, 1536)
    sc_ext[:, SCONV_PAD:, :] = z.reshape(nb, t, BRANCH_W)

    pl_len = POOL_PAD + t
    sum_b[:, 8:pl_len, :] = pool_ext[:, 8:pl_len, :] + pool_ext[:, 7:pl_len - 1, :]
    sum_a[:, 16:pl_len, :] = sum_b[:, 16:pl_len, :] + sum_b[:, 14:pl_len - 2, :]
    sum_b[:, 24:pl_len, :] = sum_a[:, 24:pl_len, :] + sum_a[:, 20:pl_len - 4, :]
    for sh in range(SUBLANES):
        n_rows = t + SUBLANES * ((CONF_WIDTH - 1 - sh) // SUBLANES)
        first = CONF_PAD - CONF_HIST + sh
        conv_sh[sh, :, 0:n_rows, :] = conv_ext[:, first:first + n_rows, :]

    def window_block(bs, t0, tb, order_zero):
        bb = bs.stop - bs.start
        shp = (bb, tb, BRANCH_W)
        lane = lax.broadcasted_iota(jnp.int32, shp, 2)
        pos = seq_pos0 + t0 + lax.broadcasted_iota(jnp.int32, shp, 1)
        r0 = POOL_PAD + t0
        cur = pool_ext[bs, r0:r0 + tb, :]
        s2 = cur + pool_ext[bs, r0 - 1:r0 - 1 + tb, :]
        s4 = sum_a[bs, r0:r0 + tb, :]
        s8 = sum_b[bs, r0:r0 + tb, :]
        s16 = s8 + sum_b[bs, r0 - SUBLANES:r0 - SUBLANES + tb, :]
        g1, g2, g3 = (lane < g * POOL_GW for g in (1, 2, 3))
        win_sum = jnp.where(g1, s2, jnp.where(g2, s4, jnp.where(g3, s8, s16)))
        win = jnp.where(g1, POOL_WINDOWS[0], jnp.where(g2, POOL_WINDOWS[1],
                                                         jnp.where(g3, POOL_WINDOWS[2], POOL_WINDOWS[3])))
        cnt = jnp.minimum(pos + 1, win).astype(F32)
        buf_a[bs, t0:t0 + tb, :] = win_sum / cnt - cur
        tiles = (ROW_BLOCK // SUBLANES, SUBLANES, BRANCH_W)
        acc = jnp.broadcast_to((cdb[...] + order_zero)[None], tiles)
        for k in range(CONF_WIDTH):
            q, sh = divmod(k, SUBLANES)
            r = t0 + SUBLANES * q
            acc = acc + conv_sh[sh, bs, r:r + tb, :].reshape(tiles) * cdw[k][None]
        buf_b[bs, t0:t0 + tb, :] = acc.reshape(shp)
        base = SCONV_PAD - SCONV_HIST + t0
        acc = sc_ext[bs, base:base + tb, :].reshape(tiles) * scw[0][None]
        for k in range(1, SCONV_WIDTH):
            acc = acc + sc_ext[bs, base + k:base + k + tb, :].reshape(tiles) * scw[k][None]
        buf_c[bs, t0:t0 + tb, :] = acc.reshape(shp)

    def gate_chunk(k):
        g0 = k * COL_TILE
        g = jnp.tanh(proj(GATE_COL0 + g0, GATE_COL0 + g0 + COL_TILE)) + 1.0
        gate_buf[:, g0:g0 + COL_TILE] = g
        return _zero_like_bits(g[0:SUBLANES, :])

    br = {}

    def item_windows(blks):
        def run(zero):
            for bs, t0, tb in blks:
                window_block(bs, t0, tb, zero)
        return run

    def item_state(zero):
        del zero
        pool_o[...] = pool_ext[:, POOL_PAD + t - POOL_HIST:POOL_PAD + t, :]
        conv_o[...] = conv_ext[:, CONF_PAD + t - CONF_HIST:CONF_PAD + t, :]
        sconv_o[...] = sc_ext[:, SCONV_PAD + t - SCONV_HIST:SCONV_PAD + t, :]
        if not cfg.has_state:
            pool_ext[:, 0:POOL_PAD, :] = pool_ext[:, t:t + POOL_PAD, :]
            conv_ext[:, 0:CONF_PAD, :] = conv_ext[:, t:t + CONF_PAD, :]
            sc_ext[:, 0:SCONV_PAD, :] = sc_ext[:, t:t + SCONV_PAD, :]

    def item_a(zero):
        pooled = buf_a[...].reshape(tm, BRANCH_W).astype(BF16)
        scale = pool_sc[...] + zero[0:1, :]
        br["a"] = (jnp.dot(pooled, pool_w[...], preferred_element_type=F32) * scale).astype(BF16)

    def item_b(zero):
        cb = _ln(buf_b[...].reshape(tm, BRANCH_W), clg[...] + zero[0:1, :], clb[...])
        br["b"] = (cb * _sigmoid(cb)).astype(BF16)

    def item_c(zero):
        br["c"] = (proj(768, 1024) * (buf_c[...].reshape(tm, BRANCH_W) + zero[0:1, :])).astype(BF16)

    def item_u(zero):
        br["u"] = _gelu_tanh(proj(1536, 1792) + zero[0:1, :])

    def item_v(zero):
        v = _ln(_gelu_tanh(proj(1792, 2048)), slg[...] + zero[0:1, :], slb[...])
        if cfg.has_state:
            v_o[...] = v.reshape(nb, t, BRANCH_W)
        else:
            v_o[...] = v[tm - cfg.v_rows:, :].reshape(1, cfg.v_rows, BRANCH_W)
        br["v"] = v

    def item_d(zero):
        head = lax.broadcasted_iota(jnp.int32, (CHUNK, BRANCH_W), 1) // SGU_HW
        bias = sbias[...] + jnp.concatenate([zero] * (CHUNK // SUBLANES), axis=0)
        mixed = []
        for j in range(tm // CHUNK):
            vj = br["v"][j * CHUNK:(j + 1) * CHUNK, :]
            stacked = jnp.concatenate([jnp.where(head == h, vj, 0.0) for h in range(SGU_HEADS)], axis=0)
            mixed.append(jnp.dot(wcat[...], stacked.astype(BF16), preferred_element_type=F32) + bias)
        br["d"] = (br["u"] * jnp.concatenate(mixed, axis=0)).astype(BF16)

    blocks = _row_blocks(nb, t)
    n_gate = N_BRANCH * D_MODEL // COL_TILE
    per_item = -(-len(blocks) // (n_gate // 2))
    items = [item_windows(blocks[i:i + per_item]) for i in range(0, len(blocks), per_item)]
    items += [item_state, item_c, item_b, item_u, item_v, item_a, item_d]
    assert len(items) <= n_gate
    zero = jnp.zeros((SUBLANES, BRANCH_W), F32)
    for k in range(n_gate):
        next_zero = gate_chunk(k)
        if k < len(items):
            items[k](zero)
        zero = next_zero

    n_ct = D_MODEL // COL_TILE
    brs = [br["a"], br["b"], br["c"], br["d"]]
    for c in range(n_ct):
        part = None
        for i in range(N_BRANCH):
            g0 = i * D_MODEL + c * COL_TILE
            term = gate_buf[:, g0:g0 + COL_TILE] * jnp.dot(
                brs[i], w_br[i * n_ct + c], preferred_element_type=F32)
            part = term if part is None else part + term
        xb_buf[:, c * COL_TILE:(c + 1) * COL_TILE] = part.astype(BF16)
    for c in range(n_ct):
        cs = slice(c * COL_TILE, (c + 1) * COL_TILE)
        h1_o[:, cs] = x_ref[:, cs] + jnp.dot(xb_buf[...], w_out[c], preferred_element_type=F32)

    xn2 = _rms(h1_o[...], n2g[...])
    x_hi = xn2.astype(BF16)
    xn2_o[...] = x_hi
    x_lo = (xn2 - x_hi.astype(F32)).astype(BF16)
    hi_both = jnp.dot(x_hi, w_rt[...], preferred_element_type=F32)
    lo_hi = jnp.dot(x_lo, w_rt[:, 0:LANES], preferred_element_type=F32)
    logits = (hi_both[:, 0:LANES] + (lo_hi + hi_both[:, LANES:2 * LANES])) + b_rt[...]
    lane = lax.broadcasted_iota(jnp.int32, (tm, LANES), 1)
    is_g = (lane >= ROUTE_GROUP_LANE0) & (lane < ROUTE_GROUP_LANE0 + N_GROUPS)
    glog = jnp.where(is_g, logits, NEG_BIG)
    gmax = jnp.max(glog, axis=-1, keepdims=True)
    lane_f = lane.astype(F32)
    no_lane = float(4 * LANES)
    gsel = jnp.min(jnp.where(glog == gmax, lane_f, no_lane), axis=-1, keepdims=True) - ROUTE_GROUP_LANE0
    pg = 1.0 / jnp.sum(jnp.where(is_g, jnp.exp(glog - gmax), 0.0), axis=-1, keepdims=True)
    in_grp = (lane < N_EXPERTS) & ((lane // EXPERTS_PER_GROUP) == gsel.astype(jnp.int32))
    el = jnp.where(in_grp, logits, NEG_BIG)
    m1 = jnp.max(el, axis=-1, keepdims=True)
    i1 = jnp.min(jnp.where(in_grp & (el == m1), lane_f, no_lane), axis=-1, keepdims=True)
    rest = in_grp & (lane_f != i1)
    el2 = jnp.where(rest, logits, NEG_BIG)
    m2 = jnp.max(el2, axis=-1, keepdims=True)
    i2 = jnp.min(jnp.where(rest & (el2 == m2), lane_f, no_lane), axis=-1, keepdims=True)
    e21 = jnp.exp(m2 - m1)
    w1 = pg / (1.0 + e21)
    w2 = pg * e21 / (1.0 + e21)
    route_o[...] = jnp.where(lane == 0, i1, jnp.where(lane == 1, i2,
                             jnp.where(lane == 2, w1, jnp.where(lane == 3, w2, 0.0))))


def _layer_spec(shape, layer):
    nd = len(shape)
    return pl.BlockSpec((1,) + shape[1:], lambda *_: (layer,) + (0,) * (nd - 1), pipeline_mode=pl.Buffered(1))


def _mixer_call(cfg, n_seq, n_rows_total, x2d, x_tile_off, states, weights, big_weights, aliased):
    nb, t = cfg.nb, cfg.t
    tm = nb * t
    if cfg.has_state:
        grid = (n_seq // nb,)
        tok = lambda i: (x_tile_off + i, 0)
        out_tok = lambda i: (cfg.row_off + i, 0)
        seq3 = lambda i: (i, 0, 0)
    else:
        grid = (n_seq, cfg.seq_len // t)
        nt = grid[1]
        tok = lambda b, c: (x_tile_off + b * nt + c, 0)
        out_tok = lambda b, c: (cfg.row_off + b * nt + c, 0)
        seq3 = lambda b, c: (b, 0, 0)

    in_specs = [pl.BlockSpec((tm, D_MODEL), tok)]
    args = [x2d]
    if cfg.has_state:
        for s in states:
            in_specs.append(pl.BlockSpec((1, nb) + s.shape[2:], lambda i: (cfg.layer, i, 0, 0)))
            args.append(s)
    for w in weights:
        in_specs.append(_layer_spec(w.shape, cfg.layer))
        args.append(w)
    for w in big_weights:
        in_specs.append(pl.BlockSpec(memory_space=pl.ANY))
        args.append(w)
    io_alias = {}
    if cfg.has_state:
        for k, a in enumerate(aliased):
            in_specs.append(pl.BlockSpec(memory_space=pl.ANY))
            io_alias[len(args)] = k
            args.append(a)

    out_shape = [
        jax.ShapeDtypeStruct((n_rows_total, D_MODEL), F32),
        jax.ShapeDtypeStruct((n_rows_total, LANES), F32),
        jax.ShapeDtypeStruct((n_rows_total, D_MODEL), BF16),
        jax.ShapeDtypeStruct((n_seq, POOL_HIST, BRANCH_W), F32),
        jax.ShapeDtypeStruct((n_seq, CONF_HIST, BRANCH_W), F32),
        jax.ShapeDtypeStruct((n_seq, SCONV_HIST, BRANCH_W), F32),
        jax.ShapeDtypeStruct((n_seq, cfg.v_rows, BRANCH_W), F32),
    ]
    out_specs = [
        pl.BlockSpec((tm, D_MODEL), out_tok),
        pl.BlockSpec((tm, LANES), out_tok),
        pl.BlockSpec((tm, D_MODEL), out_tok),
        pl.BlockSpec((nb, POOL_HIST, BRANCH_W), seq3),
        pl.BlockSpec((nb, CONF_HIST, BRANCH_W), seq3),
        pl.BlockSpec((nb, SCONV_HIST, BRANCH_W), seq3),
        pl.BlockSpec((nb, cfg.v_rows, BRANCH_W), seq3),
    ]
    scratch = [
        pltpu.VMEM((nb, POOL_PAD + t, BRANCH_W), F32),
        pltpu.VMEM((nb, POOL_PAD + t, BRANCH_W), F32),
        pltpu.VMEM((nb, POOL_PAD + t, BRANCH_W), F32),
        pltpu.VMEM((nb, CONF_PAD + t, BRANCH_W), F32),
        pltpu.VMEM((SUBLANES, nb, t + CONF_PAD - SUBLANES, BRANCH_W), F32),
        pltpu.VMEM((nb, SCONV_PAD + t, BRANCH_W), F32),
        pltpu.VMEM((nb, t, BRANCH_W), F32),
        pltpu.VMEM((nb, t, BRANCH_W), F32),
        pltpu.VMEM((nb, t, BRANCH_W), F32),
        pltpu.VMEM((tm, N_BRANCH * D_MODEL), F32),
        pltpu.VMEM((tm, D_MODEL), BF16),
        pltpu.VMEM((IN_COLS // COL_TILE, D_MODEL, COL_TILE), BF16),
        pltpu.VMEM((N_BRANCH * D_MODEL // COL_TILE, BRANCH_W, COL_TILE), BF16),
        pltpu.VMEM((D_MODEL // COL_TILE, D_MODEL, COL_TILE), BF16),
        pltpu.VMEM((2, W_IN_ROWS, IN_COLS), F32),
        pltpu.VMEM((2, BRANCH_W, D_MODEL), F32),
        pltpu.SemaphoreType.DMA((2,)),
    ]
    return pl.pallas_call(
        functools.partial(_mixer_kernel, cfg),
        grid=grid,
        in_specs=in_specs,
        out_specs=out_specs,
        out_shape=out_shape,
        scratch_shapes=scratch,
        input_output_aliases=io_alias,
        compiler_params=pltpu.CompilerParams(
            dimension_semantics=("arbitrary",) * len(grid), vmem_limit_bytes=VMEM_LIMIT),
        name="mixer_sample" if cfg.has_state else "mixer_prompt",
    )(*args)


def _sort_kernel(xn_ref, route_ref, xl_o, pos_o, nch_o):
    t = SORT_TILE
    xn = xn_ref[...]
    r = route_ref[...]
    lane_i = lax.broadcasted_iota(jnp.int32, (t, LANES), 1)
    lane = lane_i.astype(F32)
    hit1 = lane == r[:, 0:1]
    hit2 = lane == r[:, 1:2]
    onehot = jnp.where(hit1 | hit2, 1.0, 0.0)
    row = lax.broadcasted_iota(jnp.int32, (t, t), 0)
    col = lax.broadcasted_iota(jnp.int32, (t, t), 1)
    strict_lower = jnp.where(col < row, 1.0, 0.0).astype(BF16)
    before = jnp.dot(strict_lower, onehot.astype(BF16), preferred_element_type=F32)
    cnt = jnp.sum(onehot, axis=0, keepdims=True)
    nch = jnp.floor((cnt + (SEG_PAD - 1)) * (1.0 / SEG_PAD))
    er = lax.broadcasted_iota(jnp.int32, (LANES, LANES), 0)
    ec = lax.broadcasted_iota(jnp.int32, (LANES, LANES), 1)
    strict_upper = jnp.where(er < ec, 1.0, 0.0).astype(BF16)
    nch8 = jnp.broadcast_to(nch, (SUBLANES, LANES)).astype(BF16)
    seg0 = jnp.dot(nch8, strict_upper, preferred_element_type=F32)[0:1, :] * SEG_PAD
    where = seg0 + before
    pos0 = jnp.sum(jnp.where(hit1, where, 0.0), axis=-1, keepdims=True)
    pos1 = jnp.sum(jnp.where(hit2, where, 0.0), axis=-1, keepdims=True)
    pos = jnp.where(lane_i == 0, pos0, jnp.where(lane_i == 1, pos1, -1.0))
    pos_o[...] = pos
    nch_o[...] = jnp.broadcast_to(nch, (1, SUBLANES, LANES))
    pos_t = jnp.transpose(pos)
    p0 = pos_t[0:1, :]
    p1 = pos_t[1:2, :]
    used_rows = jnp.sum(nch) * SEG_PAD

    def perm_block(blk):
        dst = (lax.broadcasted_iota(jnp.int32, (PERM_BLOCK, t), 0) + blk * PERM_BLOCK).astype(F32)
        perm = jnp.where((dst == p0) | (dst == p1), 1.0, 0.0).astype(BF16)
        xl_o[0, blk * PERM_BLOCK:(blk + 1) * PERM_BLOCK, :] = jnp.dot(
            perm, xn, preferred_element_type=F32).astype(BF16)

    for blk in range(N_PERM_BLOCKS - 1):
        perm_block(blk)
    pl.when(used_rows > (N_PERM_BLOCKS - 1) * PERM_BLOCK)(functools.partial(perm_block, N_PERM_BLOCKS - 1))


def _sort_call(xn2, route):
    n = xn2.shape[0]
    nt = n // SORT_TILE
    return pl.pallas_call(
        _sort_kernel,
        grid=(nt,),
        in_specs=[pl.BlockSpec((SORT_TILE, D_MODEL), lambda i: (i, 0)),
                  pl.BlockSpec((SORT_TILE, LANES), lambda i: (i, 0))],
        out_specs=[pl.BlockSpec((1, LOCAL_ROWS, D_MODEL), lambda i: (i, 0, 0)),
                   pl.BlockSpec((SORT_TILE, LANES), lambda i: (i, 0)),
                   pl.BlockSpec((1, SUBLANES, LANES), lambda i: (i, 0, 0))],
        out_shape=[jax.ShapeDtypeStruct((nt + SPARE_BLOCKS, LOCAL_ROWS, D_MODEL), BF16),
                   jax.ShapeDtypeStruct((n, LANES), F32),
                   jax.ShapeDtypeStruct((nt, SUBLANES, LANES), F32)],
        compiler_params=pltpu.CompilerParams(dimension_semantics=("arbitrary",), vmem_limit_bytes=VMEM_LIMIT),
        name="moe_sort",
    )(xn2, route)


def _chunk_copy(src, src_row, dst, dst_row, sem):
    return pltpu.make_async_copy(src.at[pl.ds(src_row, SEG_PAD), :], dst.at[pl.ds(dst_row, SEG_PAD), :], sem)


def _expert_kernel(layer, trash_row0, se_ref, nv_ref, src_ref, dst_ref, wfirst_ref, wpar_ref, wnext_ref,
                   xl_in, wg_hbm, wu_hbm, wd_hbm, xl_io, *scratch):
    del xl_in
    ns = EXP_SLOTS
    xbuf, ybuf = scratch[:ns], scratch[ns:2 * ns]
    sem_in, sem_out, wg, wu, wd, w_sem = scratch[2 * ns:]
    s = pl.program_id(0)
    nv = nv_ref[0]

    def weight_copies(expert, par):
        return [pltpu.make_async_copy(hbm.at[layer, expert], buf.at[par], w_sem.at[par])
                for hbm, buf in ((wg_hbm, wg), (wu_hbm, wu), (wd_hbm, wd))]

    def gather(step, sl):
        return [_chunk_copy(xl_io, pl.multiple_of(src_ref[step * CHUNKS_PER_STEP + j], SEG_PAD),
                            xbuf[sl], j * SEG_PAD, sem_in.at[sl]) for j in range(CHUNKS_PER_STEP)]

    def write_back(step, sl):
        return [_chunk_copy(ybuf[sl], j * SEG_PAD, xl_io,
                            pl.multiple_of(dst_ref[step * CHUNKS_PER_STEP + j], SEG_PAD), sem_out.at[sl])
                for j in range(CHUNKS_PER_STEP)]

    def spare_write(sl):
        return [_chunk_copy(ybuf[sl], j * SEG_PAD, xl_io, trash_row0 + sl * EXP_TILE + j * SEG_PAD,
                            sem_out.at[sl]) for j in range(CHUNKS_PER_STEP)]

    @pl.when(s == 0)
    def _():
        for c in weight_copies(se_ref[0], 0):
            c.start()
        for ahead in range(ns - 1):
            for c in gather(ahead, ahead):
                c.start()
        for sl in range(ns):
            ybuf[sl][...] = jnp.zeros_like(ybuf[sl])
            for c in spare_write(sl):
                c.start()

    def live_step(sl):
        par = wpar_ref[s]

        @pl.when(wfirst_ref[s] == 1)
        def _():
            for c in weight_copies(se_ref[s], par):
                c.wait()

            @pl.when(wnext_ref[s] >= 0)
            def _():
                for c in weight_copies(wnext_ref[s], 1 - par):
                    c.start()

        for c in gather(s + ns - 1, (sl + ns - 1) % ns):
            c.start()
        for c in gather(s, sl):
            c.wait()
        xb = xbuf[sl][...]
        a = jnp.dot(xb, wg[par].astype(BF16), preferred_element_type=F32)
        b = jnp.dot(xb, wu[par].astype(BF16), preferred_element_type=F32)
        hh = (a * _sigmoid(a) * b).astype(BF16)
        y = jnp.dot(hh, wd[par].astype(BF16), preferred_element_type=F32).astype(BF16)
        for c in write_back(s, sl):
            c.wait()
        ybuf[sl][...] = y
        for c in write_back(s, sl):
            c.start()

    def drain_step(sl):
        for ahead in range(ns - 1):
            for c in gather(s + ahead, (sl + ahead) % ns):
                c.wait()
        for every in range(ns):
            for c in spare_write(every):
                c.wait()

    for sl in range(ns):
        pl.when((s < nv) & (s % ns == sl))(functools.partial(live_step, sl))
        pl.when((s == nv) & (s % ns == sl))(functools.partial(drain_step, sl))


def _expert_call(xl, trash_row0, tables, layer, w_gate, w_up, w_down):
    n_steps = tables[0].shape[0] - (EXP_SLOTS - 1)
    grid_spec = pltpu.PrefetchScalarGridSpec(
        num_scalar_prefetch=len(tables),
        grid=(n_steps,),
        in_specs=[pl.BlockSpec(memory_space=pl.ANY) for _ in range(4)],
        out_specs=pl.BlockSpec(memory_space=pl.ANY),
        scratch_shapes=[pltpu.VMEM((EXP_TILE, D_MODEL), BF16) for _ in range(2 * EXP_SLOTS)] + [
                        pltpu.SemaphoreType.DMA((EXP_SLOTS,)), pltpu.SemaphoreType.DMA((EXP_SLOTS,)),
                        pltpu.VMEM((2, D_MODEL, D_EXPERT), F32), pltpu.VMEM((2, D_MODEL, D_EXPERT), F32),
                        pltpu.VMEM((2, D_EXPERT, D_MODEL), F32), pltpu.SemaphoreType.DMA((2,))],
    )
    return pl.pallas_call(
        functools.partial(_expert_kernel, layer, trash_row0),
        grid_spec=grid_spec,
        out_shape=jax.ShapeDtypeStruct(xl.shape, BF16),
        input_output_aliases={len(tables): 0},
        compiler_params=pltpu.CompilerParams(dimension_semantics=("arbitrary",), has_side_effects=True),
        name="moe_experts",
    )(*tables, xl, w_gate, w_up, w_down)


def _combine_kernel(final_norm, n_first, h1_ref, route_ref, pos_ref, fg, yl_ref, *outs):
    t = SORT_TILE
    r = route_ref[...]
    p = pos_ref[...]
    w1, w2 = r[:, 2:3], r[:, 3:4]
    pos0, pos1 = p[:, 0:1], p[:, 1:2]
    *outs, acc_ref = outs

    def gathered(blk):
        src = (lax.broadcasted_iota(jnp.int32, (t, PERM_BLOCK), 1) + blk * PERM_BLOCK).astype(F32)
        pw = jnp.where(src == pos0, w1, 0.0) + jnp.where(src == pos1, w2, 0.0)
        return jnp.dot(pw.astype(BF16), yl_ref[0, blk * PERM_BLOCK:(blk + 1) * PERM_BLOCK, :],
                       preferred_element_type=F32)

    acc = h1_ref[...]
    for blk in range(N_PERM_BLOCKS - 1):
        acc = acc + gathered(blk)
    acc_ref[...] = acc

    @pl.when(jnp.max(p) >= (N_PERM_BLOCKS - 1) * PERM_BLOCK)
    def _():
        acc_ref[...] += gathered(N_PERM_BLOCKS - 1)

    if not final_norm:
        outs[0][...] = acc_ref[...]
        return
    y = _rms(acc_ref[...], fg[...])
    i = pl.program_id(0)

    @pl.when(i < n_first)
    def _():
        outs[0][...] = y

    @pl.when(i >= n_first)
    def _():
        outs[1][...] = y


def _combine_call(h1, route, pos, yl, final_g, final_norm, n_first_rows):
    n = h1.shape[0]
    nt = n // SORT_TILE
    n_first = n_first_rows // SORT_TILE
    tok = lambda i: (i, 0)
    if final_norm:
        out_shape = [jax.ShapeDtypeStruct((n_first_rows, D_MODEL), F32),
                     jax.ShapeDtypeStruct((n - n_first_rows, D_MODEL), F32)]
        out_specs = [pl.BlockSpec((SORT_TILE, D_MODEL), lambda i: (jnp.minimum(i, n_first - 1), 0)),
                     pl.BlockSpec((SORT_TILE, D_MODEL), lambda i: (jnp.maximum(i - n_first, 0), 0))]
    else:
        out_shape = [jax.ShapeDtypeStruct((n, D_MODEL), F32)]
        out_specs = [pl.BlockSpec((SORT_TILE, D_MODEL), tok)]
    return pl.pallas_call(
        functools.partial(_combine_kernel, final_norm, n_first),
        grid=(nt,),
        in_specs=[pl.BlockSpec((SORT_TILE, D_MODEL), tok),
                  pl.BlockSpec((SORT_TILE, LANES), tok),
                  pl.BlockSpec((SORT_TILE, LANES), tok),
                  pl.BlockSpec((1, D_MODEL), lambda i: (0, 0)),
                  pl.BlockSpec((1, LOCAL_ROWS, D_MODEL), lambda i: (i, 0, 0))],
        out_specs=out_specs,
        out_shape=out_shape,
        scratch_shapes=[pltpu.VMEM((SORT_TILE, D_MODEL), F32)],
        compiler_params=pltpu.CompilerParams(dimension_semantics=("arbitrary",), vmem_limit_bytes=VMEM_LIMIT),
        name="moe_combine",
    )(h1, route, pos, final_g, yl)


def _expert_tables(nch, n_steps, trash_row0):
    nt = nch.shape[0]
    cps = CHUNKS_PER_STEP
    seg_row0 = (jnp.cumsum(nch, axis=1) - nch) * SEG_PAD
    first = jnp.cumsum(nch, axis=0) - nch
    tot = jnp.sum(nch, axis=0)
    steps = (tot + cps - 1) // cps
    step_end = jnp.cumsum(steps)
    n_valid = step_end[-1:]
    s_ids = jnp.arange(n_steps, dtype=jnp.int32)
    step_expert = jnp.minimum(jnp.sum((step_end[None, :] <= s_ids[:, None]).astype(jnp.int32), axis=1),
                              N_EXPERTS - 1)
    sel = (step_expert[:, None] == jnp.arange(N_EXPERTS, dtype=jnp.int32)[None, :]).astype(jnp.int32)
    step0 = sel @ (step_end - steps)
    tot_s = sel @ tot
    first_s = sel @ first.T
    row0_s = sel @ seg_row0.T
    k = (s_ids - step0)[:, None] * cps + jnp.arange(cps, dtype=jnp.int32)[None, :]
    ok = (k < tot_s[:, None]) & (s_ids < n_valid[0])[:, None]
    tile = jnp.sum((first_s[:, None, :] <= k[:, :, None]).astype(jnp.int32), axis=2) - 1
    tsel = (tile[:, :, None] == jnp.arange(nt, dtype=jnp.int32)[None, None, :]).astype(jnp.int32)
    first_k = jnp.sum(tsel * first_s[:, None, :], axis=2)
    row0_k = jnp.sum(tsel * row0_s[:, None, :], axis=2)
    row = tile * LOCAL_ROWS + row0_k + (k - first_k) * SEG_PAD
    live = (s_ids < n_valid[0])[:, None]
    src = jnp.where(live, jnp.where(ok, row, row[:, 0:1]), 0)
    spare = trash_row0 + (s_ids % EXP_SLOTS)[:, None] * EXP_TILE + jnp.arange(cps, dtype=jnp.int32)[None, :] * SEG_PAD
    dst = jnp.where(ok, row, spare)
    e_ids = jnp.arange(N_EXPERTS, dtype=jnp.int32)
    has = steps > 0
    run_of_expert = jnp.cumsum(has.astype(jnp.int32)) - 1
    later = has[None, :] & (e_ids[None, :] > e_ids[:, None])
    next_live = jnp.min(jnp.where(later, e_ids[None, :], N_EXPERTS), axis=1)
    next_live = jnp.where(next_live == N_EXPERTS, -1, next_live)
    w_first = ((s_ids == step0) & live[:, 0]).astype(jnp.int32)
    w_par = (sel @ run_of_expert) % 2
    w_next = sel @ next_live
    i32 = lambda a: a.reshape(-1).astype(jnp.int32)
    return (step_expert, i32(n_valid), i32(src), i32(dst), w_first, i32(w_par), i32(w_next))


def _moe(h1, route, xn2, layer, w_gate, w_up, w_down, final_g, final_norm, n_first_rows):
    n = h1.shape[0]
    nt = n // SORT_TILE
    xl, pos, nch = _sort_call(xn2, route)
    nch = nch[:, 0, :N_EXPERTS].astype(jnp.int32)
    max_chunks = (2 * n) // SEG_PAD + nt * N_EXPERTS
    n_steps = max_chunks // CHUNKS_PER_STEP + N_EXPERTS + 1
    trash_row0 = nt * LOCAL_ROWS
    tables = _expert_tables(nch, n_steps + EXP_SLOTS - 1, trash_row0)
    yl = _expert_call(xl.reshape((nt + SPARE_BLOCKS) * LOCAL_ROWS, D_MODEL), trash_row0, tables,
                      layer, w_gate, w_up, w_down)
    return _combine_call(h1, route, pos, yl.reshape(nt + SPARE_BLOCKS, LOCAL_ROWS, D_MODEL), final_g, final_norm,
                         n_first_rows)


def kernel(x_prompt, x_sample, state_pool, state_conv, state_sconv, norm1_g, w_in, pool_w, pool_scale, conf_dw, conf_dw_b, conf_ln_g, conf_ln_b, sconv_w, sgu_ln_g, sgu_ln_b, sgu_ws, sgu_b, w_branch, w_out, norm2_g, router_g, router_g_b, router_e, router_e_b, w_gate, w_up, w_down, final_g):
    depth = w_in.shape[0]
    bp, seq, _ = x_prompt.shape
    bs, dseq, _ = x_sample.shape
    n_p, n_s = bp * seq, bs * dseq
    n = n_p + n_s
    assert seq % MIX_TILE == 0 and n_s % MIX_TILE_S == 0 and n_p % MIX_TILE_S == 0 and dseq == SUBLANES
    assert MIX_TILE % CHUNK == 0 and MIX_TILE_S % CHUNK == 0

    cfg_p = MixCfg(layer=0, nb=1, t=MIX_TILE, has_state=False, start_pos=0, row_off=0, v_rows=CHUNK, seq_len=seq)
    cfg_s = MixCfg(layer=0, nb=MIX_TILE_S // dseq, t=dseq, has_state=True, start_pos=PAST_LEN,
                   row_off=n_p // MIX_TILE_S, v_rows=dseq, seq_len=dseq)
    big_weights = (w_in, w_branch, w_out)

    row = lambda a: a[:, None, :]
    rep8 = lambda a: jnp.broadcast_to(a[..., None, :], a.shape[:-1] + (SUBLANES, a.shape[-1]))
    tril = jnp.tril(jnp.ones((CHUNK, CHUNK), F32))
    eye_blk = jnp.kron(jnp.eye(CHUNK // dseq, dtype=F32), jnp.ones((dseq, dseq), F32))
    cat = lambda w: jnp.concatenate([w[:, h] for h in range(SGU_HEADS)], axis=2).astype(BF16)
    wcat_p = cat(sgu_ws * tril)
    wcat_s = cat(jnp.tile(sgu_ws[:, :, :dseq, :dseq], (1, 1, CHUNK // dseq, CHUNK // dseq)) * (tril * eye_blk))
    bias_p = jnp.repeat(jnp.swapaxes(sgu_b, 1, 2), SGU_HW, axis=2)
    bias_s = jnp.tile(jnp.repeat(jnp.swapaxes(sgu_b[:, :, :dseq], 1, 2), SGU_HW, axis=2), (1, CHUNK // dseq, 1))
    lane_pad = LANES - N_EXPERTS - N_GROUPS
    w_rt32 = jnp.pad(jnp.concatenate([router_e, router_g], axis=2), ((0, 0), (0, 0), (0, lane_pad)))
    w_rt_hi = w_rt32.astype(BF16)
    w_rt = jnp.concatenate([w_rt_hi, (w_rt32 - w_rt_hi.astype(F32)).astype(BF16)], axis=2)
    b_rt = row(jnp.pad(jnp.concatenate([router_e_b, router_g_b], axis=1), ((0, 0), (0, lane_pad))))
    same_group = jnp.eye(len(POOL_WINDOWS), dtype=F32)[None, :, None, :, None]
    pool_bd = (pool_w[:, :, :, None, :] * same_group).reshape(depth, BRANCH_W, BRANCH_W).astype(BF16)

    def weights(wcat, sbias):
        return [row(norm1_g), pool_bd, row(pool_scale), rep8(conf_dw), rep8(conf_dw_b), row(conf_ln_g),
                row(conf_ln_b), rep8(sconv_w), row(sgu_ln_g), row(sgu_ln_b), wcat, sbias, row(norm2_g), w_rt, b_rt]

    weights_p, weights_s = weights(wcat_p, bias_p), weights(wcat_s, bias_s)
    final_row = final_g.reshape(1, -1)

    h_p, h_s = x_prompt.reshape(n_p, D_MODEL), x_sample.reshape(n_s, D_MODEL)
    off_p, off_s = 0, 0
    states_out = []
    for l in range(depth):
        h1, route, xn2, pool_p, conv_p, sconv_p, v_p = _mixer_call(
            cfg_p._replace(layer=l), bp, n, h_p, off_p, None, weights_p, big_weights, None)
        h1, route, xn2, pool_s, conv_s, sconv_s, v_s = _mixer_call(
            cfg_s._replace(layer=l), bs, n, h_s, off_s, (state_pool, state_conv, state_sconv),
            weights_s, big_weights, (h1, route, xn2))
        states_out.append((pool_p, conv_p, sconv_p, v_p, pool_s, conv_s, sconv_s, v_s))

        outs = _moe(h1, route, xn2, l, w_gate, w_up, w_down, final_row, l == depth - 1, n_p)
        h_p = h_s = outs[0]
        off_p, off_s = 0, n_p // MIX_TILE_S

    y_prompt = outs[0].reshape(bp, seq, D_MODEL)
    y_sample = outs[1].reshape(bs, dseq, D_MODEL)
    st = [jnp.stack([s[k] for s in states_out]) for k in range(8)]
    return (y_prompt, y_sample, st[0], st[1], st[2], st[3], st[4], st[5], st[6], st[7])
```

```python
import functools
from typing import NamedTuple

import jax
import jax.numpy as jnp
from jax import lax
from jax.experimental import pallas as pl
from jax.experimental.pallas import tpu as pltpu

F32 = jnp.float32
BF16 = jnp.bfloat16

D_MODEL = 1024
BRANCH_W = 256
N_BRANCH = 4
POOL_WINDOWS = (2, 4, 8, 16)
POOL_GW = 64
POOL_HIST = 15
CONF_WIDTH = 31
CONF_HIST = 30
SCONV_WIDTH = 3
SCONV_HIST = 2
CHUNK = 128
PAST_LEN = 16384
SGU_HEADS = 4
SGU_HW = 64
N_GROUPS = 4
EXPERTS_PER_GROUP = 8
N_EXPERTS = 32
D_EXPERT = 256
RMS_EPS = 1e-6
LN_EPS = 1e-5
IN_COLS = 6144
GATE_COL0 = 2048

COL_TILE = 256
LANES = 128
SUBLANES = 8
POOL_PAD = 32
CONF_PAD = 32
SCONV_PAD = 8
ROW_BLOCK = 32
W_IN_ROWS = 64
ROUTE_GROUP_LANE0 = 32
NEG_BIG = -3.0e38

MIX_TILE = 512
MIX_TILE_S = 256
SORT_TILE = 512
SEG_PAD = 16
PERM_BLOCK = 256
N_PERM_BLOCKS = -(-(2 * SORT_TILE + N_EXPERTS * (SEG_PAD - 1)) // PERM_BLOCK)
LOCAL_ROWS = N_PERM_BLOCKS * PERM_BLOCK
EXP_TILE = 512
CHUNKS_PER_STEP = EXP_TILE // SEG_PAD
EXP_SLOTS = 3
SPARE_BLOCKS = -(-EXP_SLOTS * EXP_TILE // LOCAL_ROWS)
VMEM_LIMIT = 56 * 1024 * 1024


class MixCfg(NamedTuple):
    layer: int
    nb: int
    t: int
    has_state: bool
    start_pos: int
    row_off: int
    v_rows: int
    seq_len: int


def _rms(x, g):
    return x * lax.rsqrt(jnp.mean(x * x, axis=-1, keepdims=True) + RMS_EPS) * g


def _ln(x, g, b):
    mu = jnp.mean(x, axis=-1, keepdims=True)
    xc = x - mu
    return xc * lax.rsqrt(jnp.mean(xc * xc, axis=-1, keepdims=True) + LN_EPS) * g + b


def _sigmoid(x):
    return 0.5 * jnp.tanh(0.5 * x) + 0.5


def _gelu_tanh(x):
    return 0.5 * x * (1.0 + jnp.tanh(0.7978845608028654 * (x + 0.044715 * (x * x * x))))


def _row_blocks(nb, t):
    if t >= ROW_BLOCK:
        return [(slice(b, b + 1), t0, ROW_BLOCK) for b in range(nb) for t0 in range(0, t, ROW_BLOCK)]
    bb = ROW_BLOCK // t
    return [(slice(b0, b0 + bb), 0, t) for b0 in range(0, nb, bb)]


def _load_big_weights(layer, w_in_hbm, w_br_hbm, w_out_hbm, w_in, w_br, w_out, stage, stage_b, sem):
    n_in, n_out = IN_COLS // COL_TILE, D_MODEL // COL_TILE
    jobs = ([("in", r) for r in range(0, D_MODEL, W_IN_ROWS)] + [("out", r) for r in range(0, D_MODEL, BRANCH_W)]
            + [("br", i) for i in range(N_BRANCH)])

    def copy(job, slot):
        kind, j = job
        if kind == "in":
            return pltpu.make_async_copy(w_in_hbm.at[layer, pl.ds(j, W_IN_ROWS), :], stage.at[slot], sem.at[slot])
        src = w_br_hbm.at[layer, j] if kind == "br" else w_out_hbm.at[layer, pl.ds(j, BRANCH_W), :]
        return pltpu.make_async_copy(src, stage_b.at[slot], sem.at[slot])

    def col(buf, slot, c):
        return buf[slot, :, c * COL_TILE:(c + 1) * COL_TILE]

    copy(jobs[0], 0).start()
    for n, job in enumerate(jobs):
        slot = n % 2
        if n + 1 < len(jobs):
            copy(jobs[n + 1], 1 - slot).start()
        copy(job, slot).wait()
        kind, j = job
        if kind == "in":
            for c in range(n_in):
                tile = col(stage, slot, c)
                w_in[c, j:j + W_IN_ROWS, :] = (tile * 0.5 if c * COL_TILE >= GATE_COL0 else tile).astype(BF16)
        elif kind == "out":
            for c in range(n_out):
                w_out[c, j:j + BRANCH_W, :] = col(stage_b, slot, c).astype(BF16)
        else:
            for c in range(n_out):
                w_br[j * n_out + c] = (col(stage_b, slot, c) * 0.5).astype(BF16)


def _mixer_kernel(cfg, *refs):
    nb, t, tm = cfg.nb, cfg.t, cfg.nb * cfg.t
    refs = list(refs)
    x_ref = refs.pop(0)
    if cfg.has_state:
        pool_st, conv_st, sconv_st = (refs.pop(0).at[0] for _ in range(3))
    (n1g, pool_w, pool_sc, cdw, cdb, clg, clb, scw, slg, slb, wcat, sbias, n2g, w_rt, b_rt) = (
        r.at[0] for r in refs[:15])
    w_in_hbm, w_br_hbm, w_out_hbm = refs[15:18]
    refs = refs[18:]
    if cfg.has_state:
        refs = refs[3:]
    h1_o, route_o, xn2_o, pool_o, conv_o, sconv_o, v_o = refs[:7]
    (pool_ext, sum_a, sum_b, conv_ext, conv_sh, sc_ext, buf_a, buf_b, buf_c, gate_buf, xb_buf,
     w_in, w_br, w_out, stage, stage_b, w_sem) = refs[7:]

    first_step = pl.program_id(0) == 0
    if not cfg.has_state:
        first_step = first_step & (pl.program_id(1) == 0)
    pl.when(first_step)(functools.partial(
        _load_big_weights, cfg.layer, w_in_hbm, w_br_hbm, w_out_hbm, w_in, w_br, w_out, stage, stage_b, w_sem))

    if cfg.has_state:
        seq_pos0 = cfg.start_pos
        pool_ext[:, 0:POOL_PAD - SUBLANES, :] = jnp.zeros((nb, POOL_PAD - SUBLANES, BRANCH_W), F32)
        pool_ext[:, POOL_PAD - POOL_HIST:POOL_PAD, :] = pool_st[...]
        conv_ext[:, CONF_PAD - CONF_HIST:CONF_PAD, :] = conv_st[...]
        sc_ext[:, SCONV_PAD - SCONV_HIST:SCONV_PAD, :] = sconv_st[...]
    else:
        c = pl.program_id(1)
        seq_pos0 = cfg.start_pos + c * t

        @pl.when(c == 0)
        def _():
            pool_ext[:, 0:POOL_PAD, :] = jnp.zeros((nb, POOL_PAD, BRANCH_W), F32)
            conv_ext[:, 0:CONF_PAD, :] = jnp.zeros((nb, CONF_PAD, BRANCH_W), F32)
            sc_ext[:, 0:SCONV_PAD, :] = jnp.zeros((nb, SCONV_PAD, BRANCH_W), F32)

    x = x_ref[...]
    xb_buf[...] = _rms(x, n1g[...]).astype(BF16)

    def proj(lo, hi):
        assert lo % COL_TILE == 0 and hi == lo + COL_TILE
        return jnp.dot(xb_buf[...], w_in[lo // COL_TILE], preferred_element_type=F32)

    a_pool = proj(0, 256)
    pool_ext[:, POOL_PAD:, :] = a_pool.reshape(nb, t, BRANCH_W)
    glu = proj(256, 512) * _sigmoid(proj(512, 768))
    conv_ext[:, CONF_PAD:, :] = glu.reshape(nb, t, BRANCH_W)
    z = proj(1024, 1280) * proj(1280, 1536)
    sc_ext[:, SCONV_PAD:, :] = z.reshape(nb, t, BRANCH_W)

    pl_len = POOL_PAD + t
    sum_b[:, 8:pl_len, :] = pool_ext[:, 8:pl_len, :] + pool_ext[:, 7:pl_len - 1, :]
    sum_a[:, 16:pl_len, :] = sum_b[:, 16:pl_len, :] + sum_b[:, 14:pl_len - 2, :]
    sum_b[:, 24:pl_len, :] = sum_a[:, 24:pl_len, :] + sum_a[:, 20:pl_len - 4, :]
    for sh in range(SUBLANES):
        n_rows = t + SUBLANES * ((CONF_WIDTH - 1 - sh) // SUBLANES)
        first = CONF_PAD - CONF_HIST + sh
        conv_sh[sh, :, 0:n_rows, :] = conv_ext[:, first:first + n_rows, :]

    def window_block(bs, t0, tb):
        bb = bs.stop - bs.start
        shp = (bb, tb, BRANCH_W)
        lane = lax.broadcasted_iota(jnp.int32, shp, 2)
        pos = seq_pos0 + t0 + lax.broadcasted_iota(jnp.int32, shp, 1)
        r0 = POOL_PAD + t0
        cur = pool_ext[bs, r0:r0 + tb, :]
        s2 = cur + pool_ext[bs, r0 - 1:r0 - 1 + tb, :]
        s4 = sum_a[bs, r0:r0 + tb, :]
        s8 = sum_b[bs, r0:r0 + tb, :]
        s16 = s8 + sum_b[bs, r0 - SUBLANES:r0 - SUBLANES + tb, :]
        g1, g2, g3 = (lane < g * POOL_GW for g in (1, 2, 3))
        win_sum = jnp.where(g1, s2, jnp.where(g2, s4, jnp.where(g3, s8, s16)))
        win = jnp.where(g1, POOL_WINDOWS[0], jnp.where(g2, POOL_WINDOWS[1],
                                                         jnp.where(g3, POOL_WINDOWS[2], POOL_WINDOWS[3])))
        cnt = jnp.minimum(pos + 1, win).astype(F32)
        buf_a[bs, t0:t0 + tb, :] = win_sum / cnt - cur
        tiles = (ROW_BLOCK // SUBLANES, SUBLANES, BRANCH_W)
        acc = jnp.broadcast_to(cdb[...][None], tiles)
        for k in range(CONF_WIDTH):
            q, sh = divmod(k, SUBLANES)
            r = t0 + SUBLANES * q
            acc = acc + conv_sh[sh, bs, r:r + tb, :].reshape(tiles) * cdw[k][None]
        buf_b[bs, t0:t0 + tb, :] = acc.reshape(shp)
        base = SCONV_PAD - SCONV_HIST + t0
        acc = sc_ext[bs, base:base + tb, :].reshape(tiles) * scw[0][None]
        for k in range(1, SCONV_WIDTH):
            acc = acc + sc_ext[bs, base + k:base + k + tb, :].reshape(tiles) * scw[k][None]
        buf_c[bs, t0:t0 + tb, :] = acc.reshape(shp)

    def gate_chunk(k):
        g0 = k * COL_TILE
        gate_buf[:, g0:g0 + COL_TILE] = jnp.tanh(proj(GATE_COL0 + g0, GATE_COL0 + g0 + COL_TILE)) + 1.0

    br = {}

    def item_windows(blks):
        def run():
            for bs, t0, tb in blks:
                window_block(bs, t0, tb)
        return run

    def item_state():
        pool_o[...] = pool_ext[:, POOL_PAD + t - POOL_HIST:POOL_PAD + t, :]
        conv_o[...] = conv_ext[:, CONF_PAD + t - CONF_HIST:CONF_PAD + t, :]
        sconv_o[...] = sc_ext[:, SCONV_PAD + t - SCONV_HIST:SCONV_PAD + t, :]
        if not cfg.has_state:
            pool_ext[:, 0:POOL_PAD, :] = pool_ext[:, t:t + POOL_PAD, :]
            conv_ext[:, 0:CONF_PAD, :] = conv_ext[:, t:t + CONF_PAD, :]
            sc_ext[:, 0:SCONV_PAD, :] = sc_ext[:, t:t + SCONV_PAD, :]

    def item_a():
        pooled = buf_a[...].reshape(tm, BRANCH_W).astype(BF16)
        br["a"] = (jnp.dot(pooled, pool_w[...], preferred_element_type=F32) * pool_sc[...]).astype(BF16)

    def item_b():
        cb = _ln(buf_b[...].reshape(tm, BRANCH_W), clg[...], clb[...])
        br["b"] = (cb * _sigmoid(cb)).astype(BF16)

    def item_c():
        br["c"] = (proj(768, 1024) * buf_c[...].reshape(tm, BRANCH_W)).astype(BF16)

    def item_u():
        br["u"] = _gelu_tanh(proj(1536, 1792))

    def item_v():
        v = _ln(_gelu_tanh(proj(1792, 2048)), slg[...], slb[...])
        if cfg.has_state:
            v_o[...] = v.reshape(nb, t, BRANCH_W)
        else:
            v_o[...] = v[tm - cfg.v_rows:, :].reshape(1, cfg.v_rows, BRANCH_W)
        br["v"] = v

    def item_d():
        head = lax.broadcasted_iota(jnp.int32, (CHUNK, BRANCH_W), 1) // SGU_HW
        bias = sbias[...]
        mixed = []
        for j in range(tm // CHUNK):
            vj = br["v"][j * CHUNK:(j + 1) * CHUNK, :]
            stacked = jnp.concatenate([jnp.where(head == h, vj, 0.0) for h in range(SGU_HEADS)], axis=0)
            mixed.append(jnp.dot(wcat[...], stacked.astype(BF16), preferred_element_type=F32) + bias)
        br["d"] = (br["u"] * jnp.concatenate(mixed, axis=0)).astype(BF16)

    blocks = _row_blocks(nb, t)
    n_gate = N_BRANCH * D_MODEL // COL_TILE
    per_item = -(-len(blocks) // (n_gate // 2))
    items = [item_windows(blocks[i:i + per_item]) for i in range(0, len(blocks), per_item)]
    items += [item_state, item_c, item_b, item_u, item_v, item_a, item_d]
    assert len(items) <= n_gate
    for k in range(n_gate):
        gate_chunk(k)
        if k < len(items):
            items[k]()

    n_ct = D_MODEL // COL_TILE
    brs = [br["a"], br["b"], br["c"], br["d"]]
    for c in range(n_ct):
        part = None
        for i in range(N_BRANCH):
            g0 = i * D_MODEL + c * COL_TILE
            term = gate_buf[:, g0:g0 + COL_TILE] * jnp.dot(
                brs[i], w_br[i * n_ct + c], preferred_element_type=F32)
            part = term if part is None else part + term
        xb_buf[:, c * COL_TILE:(c + 1) * COL_TILE] = part.astype(BF16)
    for c in range(n_ct):
        cs = slice(c * COL_TILE, (c + 1) * COL_TILE)
        h1_o[:, cs] = x_ref[:, cs] + jnp.dot(xb_buf[...], w_out[c], preferred_element_type=F32)

    xn2 = _rms(h1_o[...], n2g[...])
    x_hi = xn2.astype(BF16)
    xn2_o[...] = x_hi
    x_lo = (xn2 - x_hi.astype(F32)).astype(BF16)
    hi_both = jnp.dot(x_hi, w_rt[...], preferred_element_type=F32)
    lo_hi = jnp.dot(x_lo, w_rt[:, 0:LANES], preferred_element_type=F32)
    logits = (hi_both[:, 0:LANES] + (lo_hi + hi_both[:, LANES:2 * LANES])) + b_rt[...]
    lane = lax.broadcasted_iota(jnp.int32, (tm, LANES), 1)
    is_g = (lane >= ROUTE_GROUP_LANE0) & (lane < ROUTE_GROUP_LANE0 + N_GROUPS)
    glog = jnp.where(is_g, logits, NEG_BIG)
    gmax = jnp.max(glog, axis=-1, keepdims=True)
    lane_f = lane.astype(F32)
    no_lane = float(4 * LANES)
    gsel = jnp.min(jnp.where(glog == gmax, lane_f, no_lane), axis=-1, keepdims=True) - ROUTE_GROUP_LANE0
    pg = 1.0 / jnp.sum(jnp.where(is_g, jnp.exp(glog - gmax), 0.0), axis=-1, keepdims=True)
    in_grp = (lane < N_EXPERTS) & ((lane // EXPERTS_PER_GROUP) == gsel.astype(jnp.int32))
    el = jnp.where(in_grp, logits, NEG_BIG)
    m1 = jnp.max(el, axis=-1, keepdims=True)
    i1 = jnp.min(jnp.where(in_grp & (el == m1), lane_f, no_lane), axis=-1, keepdims=True)
    rest = in_grp & (lane_f != i1)
    el2 = jnp.where(rest, logits, NEG_BIG)
    m2 = jnp.max(el2, axis=-1, keepdims=True)
    i2 = jnp.min(jnp.where(rest & (el2 == m2), lane_f, no_lane), axis=-1, keepdims=True)
    e21 = jnp.exp(m2 - m1)
    w1 = pg / (1.0 + e21)
    w2 = pg * e21 / (1.0 + e21)
    route_o[...] = jnp.where(lane == 0, i1, jnp.where(lane == 1, i2,
                             jnp.where(lane == 2, w1, jnp.where(lane == 3, w2, 0.0))))


def _layer_spec(shape, layer):
    nd = len(shape)
    return pl.BlockSpec((1,) + shape[1:], lambda *_: (layer,) + (0,) * (nd - 1), pipeline_mode=pl.Buffered(1))


def _mixer_call(cfg, n_seq, n_rows_total, x2d, x_tile_off, states, weights, big_weights, aliased):
    nb, t = cfg.nb, cfg.t
    tm = nb * t
    if cfg.has_state:
        grid = (n_seq // nb,)
        tok = lambda i: (x_tile_off + i, 0)
        out_tok = lambda i: (cfg.row_off + i, 0)
        seq3 = lambda i: (i, 0, 0)
    else:
        grid = (n_seq, cfg.seq_len // t)
        nt = grid[1]
        tok = lambda b, c: (x_tile_off + b * nt + c, 0)
        out_tok = lambda b, c: (cfg.row_off + b * nt + c, 0)
        seq3 = lambda b, c: (b, 0, 0)

    in_specs = [pl.BlockSpec((tm, D_MODEL), tok)]
    args = [x2d]
    if cfg.has_state:
        for s in states:
            in_specs.append(pl.BlockSpec((1, nb) + s.shape[2:], lambda i: (cfg.layer, i, 0, 0)))
            args.append(s)
    for w in weights:
        in_specs.append(_layer_spec(w.shape, cfg.layer))
        args.append(w)
    for w in big_weights:
        in_specs.append(pl.BlockSpec(memory_space=pl.ANY))
        args.append(w)
    io_alias = {}
    if cfg.has_state:
        for k, a in enumerate(aliased):
            in_specs.append(pl.BlockSpec(memory_space=pl.ANY))
            io_alias[len(args)] = k
            args.append(a)

    out_shape = [
        jax.ShapeDtypeStruct((n_rows_total, D_MODEL), F32),
        jax.ShapeDtypeStruct((n_rows_total, LANES), F32),
        jax.ShapeDtypeStruct((n_rows_total, D_MODEL), BF16),
        jax.ShapeDtypeStruct((n_seq, POOL_HIST, BRANCH_W), F32),
        jax.ShapeDtypeStruct((n_seq, CONF_HIST, BRANCH_W), F32),
        jax.ShapeDtypeStruct((n_seq, SCONV_HIST, BRANCH_W), F32),
        jax.ShapeDtypeStruct((n_seq, cfg.v_rows, BRANCH_W), F32),
    ]
    out_specs = [
        pl.BlockSpec((tm, D_MODEL), out_tok),
        pl.BlockSpec((tm, LANES), out_tok),
        pl.BlockSpec((tm, D_MODEL), out_tok),
        pl.BlockSpec((nb, POOL_HIST, BRANCH_W), seq3),
        pl.BlockSpec((nb, CONF_HIST, BRANCH_W), seq3),
        pl.BlockSpec((nb, SCONV_HIST, BRANCH_W), seq3),
        pl.BlockSpec((nb, cfg.v_rows, BRANCH_W), seq3),
    ]
    scratch = [
        pltpu.VMEM((nb, POOL_PAD + t, BRANCH_W), F32),
        pltpu.VMEM((nb, POOL_PAD + t, BRANCH_W), F32),
        pltpu.VMEM((nb, POOL_PAD + t, BRANCH_W), F32),
        pltpu.VMEM((nb, CONF_PAD + t, BRANCH_W), F32),
        pltpu.VMEM((SUBLANES, nb, t + CONF_PAD - SUBLANES, BRANCH_W), F32),
        pltpu.VMEM((nb, SCONV_PAD + t, BRANCH_W), F32),
        pltpu.VMEM((nb, t, BRANCH_W), F32),
        pltpu.VMEM((nb, t, BRANCH_W), F32),
        pltpu.VMEM((nb, t, BRANCH_W), F32),
        pltpu.VMEM((tm, N_BRANCH * D_MODEL), F32),
        pltpu.VMEM((tm, D_MODEL), BF16),
        pltpu.VMEM((IN_COLS // COL_TILE, D_MODEL, COL_TILE), BF16),
        pltpu.VMEM((N_BRANCH * D_MODEL // COL_TILE, BRANCH_W, COL_TILE), BF16),
        pltpu.VMEM((D_MODEL // COL_TILE, D_MODEL, COL_TILE), BF16),
        pltpu.VMEM((2, W_IN_ROWS, IN_COLS), F32),
        pltpu.VMEM((2, BRANCH_W, D_MODEL), F32),
        pltpu.SemaphoreType.DMA((2,)),
    ]
    return pl.pallas_call(
        functools.partial(_mixer_kernel, cfg),
        grid=grid,
        in_specs=in_specs,
        out_specs=out_specs,
        out_shape=out_shape,
        scratch_shapes=scratch,
        input_output_aliases=io_alias,
        compiler_params=pltpu.CompilerParams(
            dimension_semantics=("arbitrary",) * len(grid), vmem_limit_bytes=VMEM_LIMIT),
        name="mixer_sample" if cfg.has_state else "mixer_prompt",
    )(*args)


def _sort_kernel(xn_ref, route_ref, xl_o, pos_o, nch_o):
    t = SORT_TILE
    xn = xn_ref[...]
    r = route_ref[...]
    lane_i = lax.broadcasted_iota(jnp.int32, (t, LANES), 1)
    lane = lane_i.astype(F32)
    hit1 = lane == r[:, 0:1]
    hit2 = lane == r[:, 1:2]
    onehot = jnp.where(hit1 | hit2, 1.0, 0.0)
    row = lax.broadcasted_iota(jnp.int32, (t, t), 0)
    col = lax.broadcasted_iota(jnp.int32, (t, t), 1)
    strict_lower = jnp.where(col < row, 1.0, 0.0).astype(BF16)
    before = jnp.dot(strict_lower, onehot.astype(BF16), preferred_element_type=F32)
    cnt = jnp.sum(onehot, axis=0, keepdims=True)
    nch = jnp.floor((cnt + (SEG_PAD - 1)) * (1.0 / SEG_PAD))
    er = lax.broadcasted_iota(jnp.int32, (LANES, LANES), 0)
    ec = lax.broadcasted_iota(jnp.int32, (LANES, LANES), 1)
    strict_upper = jnp.where(er < ec, 1.0, 0.0).astype(BF16)
    nch8 = jnp.broadcast_to(nch, (SUBLANES, LANES)).astype(BF16)
    seg0 = jnp.dot(nch8, strict_upper, preferred_element_type=F32)[0:1, :] * SEG_PAD
    where = seg0 + before
    pos0 = jnp.sum(jnp.where(hit1, where, 0.0), axis=-1, keepdims=True)
    pos1 = jnp.sum(jnp.where(hit2, where, 0.0), axis=-1, keepdims=True)
    pos = jnp.where(lane_i == 0, pos0, jnp.where(lane_i == 1, pos1, -1.0))
    pos_o[...] = pos
    nch_o[...] = jnp.broadcast_to(nch, (1, SUBLANES, LANES))
    pos_t = jnp.transpose(pos)
    p0 = pos_t[0:1, :]
    p1 = pos_t[1:2, :]
    used_rows = jnp.sum(nch) * SEG_PAD

    def perm_block(blk):
        dst = (lax.broadcasted_iota(jnp.int32, (PERM_BLOCK, t), 0) + blk * PERM_BLOCK).astype(F32)
        perm = jnp.where((dst == p0) | (dst == p1), 1.0, 0.0).astype(BF16)
        xl_o[0, blk * PERM_BLOCK:(blk + 1) * PERM_BLOCK, :] = jnp.dot(
            perm, xn, preferred_element_type=F32).astype(BF16)

    for blk in range(N_PERM_BLOCKS - 1):
        perm_block(blk)
    pl.when(used_rows > (N_PERM_BLOCKS - 1) * PERM_BLOCK)(functools.partial(perm_block, N_PERM_BLOCKS - 1))


def _sort_call(xn2, route):
    n = xn2.shape[0]
    nt = n // SORT_TILE
    return pl.pallas_call(
        _sort_kernel,
        grid=(nt,),
        in_specs=[pl.BlockSpec((SORT_TILE, D_MODEL), lambda i: (i, 0)),
                  pl.BlockSpec((SORT_TILE, LANES), lambda i: (i, 0))],
        out_specs=[pl.BlockSpec((1, LOCAL_ROWS, D_MODEL), lambda i: (i, 0, 0)),
                   pl.BlockSpec((SORT_TILE, LANES), lambda i: (i, 0)),
                   pl.BlockSpec((1, SUBLANES, LANES), lambda i: (i, 0, 0))],
        out_shape=[jax.ShapeDtypeStruct((nt + SPARE_BLOCKS, LOCAL_ROWS, D_MODEL), BF16),
                   jax.ShapeDtypeStruct((n, LANES), F32),
                   jax.ShapeDtypeStruct((nt, SUBLANES, LANES), F32)],
        compiler_params=pltpu.CompilerParams(dimension_semantics=("arbitrary",), vmem_limit_bytes=VMEM_LIMIT),
        name="moe_sort",
    )(xn2, route)


def _chunk_copy(src, src_row, dst, dst_row, sem):
    return pltpu.make_async_copy(src.at[pl.ds(src_row, SEG_PAD), :], dst.at[pl.ds(dst_row, SEG_PAD), :], sem)


def _expert_kernel(layer, trash_row0, se_ref, nv_ref, src_ref, dst_ref, wfirst_ref, wpar_ref, wnext_ref,
                   xl_in, wg_hbm, wu_hbm, wd_hbm, xl_io, *scratch):
    del xl_in
    ns = EXP_SLOTS
    xbuf, ybuf = scratch[:ns], scratch[ns:2 * ns]
    sem_in, sem_out, wg, wu, wd, w_sem = scratch[2 * ns:]
    s = pl.program_id(0)
    nv = nv_ref[0]

    def weight_copies(expert, par):
        return [pltpu.make_async_copy(hbm.at[layer, expert], buf.at[par], w_sem.at[par])
                for hbm, buf in ((wg_hbm, wg), (wu_hbm, wu), (wd_hbm, wd))]

    def gather(step, sl):
        return [_chunk_copy(xl_io, pl.multiple_of(src_ref[step * CHUNKS_PER_STEP + j], SEG_PAD),
                            xbuf[sl], j * SEG_PAD, sem_in.at[sl]) for j in range(CHUNKS_PER_STEP)]

    def write_back(step, sl):
        return [_chunk_copy(ybuf[sl], j * SEG_PAD, xl_io,
                            pl.multiple_of(dst_ref[step * CHUNKS_PER_STEP + j], SEG_PAD), sem_out.at[sl])
                for j in range(CHUNKS_PER_STEP)]

    def spare_write(sl):
        return [_chunk_copy(ybuf[sl], j * SEG_PAD, xl_io, trash_row0 + sl * EXP_TILE + j * SEG_PAD,
                            sem_out.at[sl]) for j in range(CHUNKS_PER_STEP)]

    @pl.when(s == 0)
    def _():
        for c in weight_copies(se_ref[0], 0):
            c.start()
        for ahead in range(ns - 1):
            for c in gather(ahead, ahead):
                c.start()
        for sl in range(ns):
            ybuf[sl][...] = jnp.zeros_like(ybuf[sl])
            for c in spare_write(sl):
                c.start()

    def live_step(sl):
        par = wpar_ref[s]

        @pl.when(wfirst_ref[s] == 1)
        def _():
            for c in weight_copies(se_ref[s], par):
                c.wait()

            @pl.when(wnext_ref[s] >= 0)
            def _():
                for c in weight_copies(wnext_ref[s], 1 - par):
                    c.start()

        for c in gather(s + ns - 1, (sl + ns - 1) % ns):
            c.start()
        for c in gather(s, sl):
            c.wait()
        xb = xbuf[sl][...]
        a = jnp.dot(xb, wg[par].astype(BF16), preferred_element_type=F32)
        b = jnp.dot(xb, wu[par].astype(BF16), preferred_element_type=F32)
        hh = (a * _sigmoid(a) * b).astype(BF16)
        y = jnp.dot(hh, wd[par].astype(BF16), preferred_element_type=F32).astype(BF16)
        for c in write_back(s, sl):
            c.wait()
        ybuf[sl][...] = y
        for c in write_back(s, sl):
            c.start()

    def drain_step(sl):
        for ahead in range(ns - 1):
            for c in gather(s + ahead, (sl + ahead) % ns):
                c.wait()
        for every in range(ns):
            for c in spare_write(every):
                c.wait()

    for sl in range(ns):
        pl.when((s < nv) & (s % ns == sl))(functools.partial(live_step, sl))
        pl.when((s == nv) & (s % ns == sl))(functools.partial(drain_step, sl))


def _expert_call(xl, trash_row0, tables, layer, w_gate, w_up, w_down):
    n_steps = tables[0].shape[0] - (EXP_SLOTS - 1)
    grid_spec = pltpu.PrefetchScalarGridSpec(
        num_scalar_prefetch=len(tables),
        grid=(n_steps,),
        in_specs=[pl.BlockSpec(memory_space=pl.ANY) for _ in range(4)],
        out_specs=pl.BlockSpec(memory_space=pl.ANY),
        scratch_shapes=[pltpu.VMEM((EXP_TILE, D_MODEL), BF16) for _ in range(2 * EXP_SLOTS)] + [
                        pltpu.SemaphoreType.DMA((EXP_SLOTS,)), pltpu.SemaphoreType.DMA((EXP_SLOTS,)),
                        pltpu.VMEM((2, D_MODEL, D_EXPERT), F32), pltpu.VMEM((2, D_MODEL, D_EXPERT), F32),
                        pltpu.VMEM((2, D_EXPERT, D_MODEL), F32), pltpu.SemaphoreType.DMA((2,))],
    )
    return pl.pallas_call(
        functools.partial(_expert_kernel, layer, trash_row0),
        grid_spec=grid_spec,
        out_shape=jax.ShapeDtypeStruct(xl.shape, BF16),
        input_output_aliases={len(tables): 0},
        compiler_params=pltpu.CompilerParams(dimension_semantics=("arbitrary",), has_side_effects=True),
        name="moe_experts",
    )(*tables, xl, w_gate, w_up, w_down)


def _combine_kernel(final_norm, n_first, h1_ref, route_ref, pos_ref, fg, yl_ref, *outs):
    t = SORT_TILE
    r = route_ref[...]
    p = pos_ref[...]
    w1, w2 = r[:, 2:3], r[:, 3:4]
    pos0, pos1 = p[:, 0:1], p[:, 1:2]
    *outs, acc_ref = outs

    def gathered(blk):
        src = (lax.broadcasted_iota(jnp.int32, (t, PERM_BLOCK), 1) + blk * PERM_BLOCK).astype(F32)
        pw = jnp.where(src == pos0, w1, 0.0) + jnp.where(src == pos1, w2, 0.0)
        return jnp.dot(pw.astype(BF16), yl_ref[0, blk * PERM_BLOCK:(blk + 1) * PERM_BLOCK, :],
                       preferred_element_type=F32)

    acc = h1_ref[...]
    for blk in range(N_PERM_BLOCKS - 1):
        acc = acc + gathered(blk)
    acc_ref[...] = acc

    @pl.when(jnp.max(p) >= (N_PERM_BLOCKS - 1) * PERM_BLOCK)
    def _():
        acc_ref[...] += gathered(N_PERM_BLOCKS - 1)

    if not final_norm:
        outs[0][...] = acc_ref[...]
        return
    y = _rms(acc_ref[...], fg[...])
    i = pl.program_id(0)

    @pl.when(i < n_first)
    def _():
        outs[0][...] = y

    @pl.when(i >= n_first)
    def _():
        outs[1][...] = y


def _combine_call(h1, route, pos, yl, final_g, final_norm, n_first_rows):
    n = h1.shape[0]
    nt = n // SORT_TILE
    n_first = n_first_rows // SORT_TILE
    tok = lambda i: (i, 0)
    if final_norm:
        out_shape = [jax.ShapeDtypeStruct((n_first_rows, D_MODEL), F32),
                     jax.ShapeDtypeStruct((n - n_first_rows, D_MODEL), F32)]
        out_specs = [pl.BlockSpec((SORT_TILE, D_MODEL), lambda i: (jnp.minimum(i, n_first - 1), 0)),
                     pl.BlockSpec((SORT_TILE, D_MODEL), lambda i: (jnp.maximum(i - n_first, 0), 0))]
    else:
        out_shape = [jax.ShapeDtypeStruct((n, D_MODEL), F32)]
        out_specs = [pl.BlockSpec((SORT_TILE, D_MODEL), tok)]
    return pl.pallas_call(
        functools.partial(_combine_kernel, final_norm, n_first),
        grid=(nt,),
        in_specs=[pl.BlockSpec((SORT_TILE, D_MODEL), tok),
                  pl.BlockSpec((SORT_TILE, LANES), tok),
                  pl.BlockSpec((SORT_TILE, LANES), tok),
                  pl.BlockSpec((1, D_MODEL), lambda i: (0, 0)),
                  pl.BlockSpec((1, LOCAL_ROWS, D_MODEL), lambda i: (i, 0, 0))],
        out_specs=out_specs,
        out_shape=out_shape,
        scratch_shapes=[pltpu.VMEM((SORT_TILE, D_MODEL), F32)],
        compiler_params=pltpu.CompilerParams(dimension_semantics=("arbitrary",), vmem_limit_bytes=VMEM_LIMIT),
        name="moe_combine",
    )(h1, route, pos, final_g, yl)


def _expert_tables(nch, n_steps, trash_row0):
    nt = nch.shape[0]
    cps = CHUNKS_PER_STEP
    seg_row0 = (jnp.cumsum(nch, axis=1) - nch) * SEG_PAD
    first = jnp.cumsum(nch, axis=0) - nch
    tot = jnp.sum(nch, axis=0)
    steps = (tot + cps - 1) // cps
    step_end = jnp.cumsum(steps)
    n_valid = step_end[-1:]
    s_ids = jnp.arange(n_steps, dtype=jnp.int32)
    step_expert = jnp.minimum(jnp.sum((step_end[None, :] <= s_ids[:, None]).astype(jnp.int32), axis=1),
                              N_EXPERTS - 1)
    sel = (step_expert[:, None] == jnp.arange(N_EXPERTS, dtype=jnp.int32)[None, :]).astype(jnp.int32)
    step0 = sel @ (step_end - steps)
    tot_s = sel @ tot
    first_s = sel @ first.T
    row0_s = sel @ seg_row0.T
    k = (s_ids - step0)[:, None] * cps + jnp.arange(cps, dtype=jnp.int32)[None, :]
    ok = (k < tot_s[:, None]) & (s_ids < n_valid[0])[:, None]
    tile = jnp.sum((first_s[:, None, :] <= k[:, :, None]).astype(jnp.int32), axis=2) - 1
    tsel = (tile[:, :, None] == jnp.arange(nt, dtype=jnp.int32)[None, None, :]).astype(jnp.int32)
    first_k = jnp.sum(tsel * first_s[:, None, :], axis=2)
    row0_k = jnp.sum(tsel * row0_s[:, None, :], axis=2)
    row = tile * LOCAL_ROWS + row0_k + (k - first_k) * SEG_PAD
    live = (s_ids < n_valid[0])[:, None]
    src = jnp.where(live, jnp.where(ok, row, row[:, 0:1]), 0)
    spare = trash_row0 + (s_ids % EXP_SLOTS)[:, None] * EXP_TILE + jnp.arange(cps, dtype=jnp.int32)[None, :] * SEG_PAD
    dst = jnp.where(ok, row, spare)
    e_ids = jnp.arange(N_EXPERTS, dtype=jnp.int32)
    has = steps > 0
    run_of_expert = jnp.cumsum(has.astype(jnp.int32)) - 1
    later = has[None, :] & (e_ids[None, :] > e_ids[:, None])
    next_live = jnp.min(jnp.where(later, e_ids[None, :], N_EXPERTS), axis=1)
    next_live = jnp.where(next_live == N_EXPERTS, -1, next_live)
    w_first = ((s_ids == step0) & live[:, 0]).astype(jnp.int32)
    w_par = (sel @ run_of_expert) % 2
    w_next = sel @ next_live
    i32 = lambda a: a.reshape(-1).astype(jnp.int32)
    return (step_expert, i32(n_valid), i32(src), i32(dst), w_first, i32(w_par), i32(w_next))


def _moe(h1, route, xn2, layer, w_gate, w_up, w_down, final_g, final_norm, n_first_rows):
    n = h1.shape[0]
    nt = n // SORT_TILE
    xl, pos, nch = _sort_call(xn2, route)
    nch = nch[:, 0, :N_EXPERTS].astype(jnp.int32)
    max_chunks = (2 * n) // SEG_PAD + nt * N_EXPERTS
    n_steps = max_chunks // CHUNKS_PER_STEP + N_EXPERTS + 1
    trash_row0 = nt * LOCAL_ROWS
    tables = _expert_tables(nch, n_steps + EXP_SLOTS - 1, trash_row0)
    yl = _expert_call(xl.reshape((nt + SPARE_BLOCKS) * LOCAL_ROWS, D_MODEL), trash_row0, tables,
                      layer, w_gate, w_up, w_down)
    return _combine_call(h1, route, pos, yl.reshape(nt + SPARE_BLOCKS, LOCAL_ROWS, D_MODEL), final_g, final_norm,
                         n_first_rows)


def kernel(x_prompt, x_sample, state_pool, state_conv, state_sconv, norm1_g, w_in, pool_w, pool_scale, conf_dw, conf_dw_b, conf_ln_g, conf_ln_b, sconv_w, sgu_ln_g, sgu_ln_b, sgu_ws, sgu_b, w_branch, w_out, norm2_g, router_g, router_g_b, router_e, router_e_b, w_gate, w_up, w_down, final_g):
    depth = w_in.shape[0]
    bp, seq, _ = x_prompt.shape
    bs, dseq, _ = x_sample.shape
    n_p, n_s = bp * seq, bs * dseq
    n = n_p + n_s
    assert seq % MIX_TILE == 0 and n_s % MIX_TILE_S == 0 and n_p % MIX_TILE_S == 0 and dseq == SUBLANES
    assert MIX_TILE % CHUNK == 0 and MIX_TILE_S % CHUNK == 0

    cfg_p = MixCfg(layer=0, nb=1, t=MIX_TILE, has_state=False, start_pos=0, row_off=0, v_rows=CHUNK, seq_len=seq)
    cfg_s = MixCfg(layer=0, nb=MIX_TILE_S // dseq, t=dseq, has_state=True, start_pos=PAST_LEN,
                   row_off=n_p // MIX_TILE_S, v_rows=dseq, seq_len=dseq)
    big_weights = (w_in, w_branch, w_out)

    row = lambda a: a[:, None, :]
    rep8 = lambda a: jnp.broadcast_to(a[..., None, :], a.shape[:-1] + (SUBLANES, a.shape[-1]))
    tril = jnp.tril(jnp.ones((CHUNK, CHUNK), F32))
    eye_blk = jnp.kron(jnp.eye(CHUNK // dseq, dtype=F32), jnp.ones((dseq, dseq), F32))
    cat = lambda w: jnp.concatenate([w[:, h] for h in range(SGU_HEADS)], axis=2).astype(BF16)
    wcat_p = cat(sgu_ws * tril)
    wcat_s = cat(jnp.tile(sgu_ws[:, :, :dseq, :dseq], (1, 1, CHUNK // dseq, CHUNK // dseq)) * (tril * eye_blk))
    bias_p = jnp.repeat(jnp.swapaxes(sgu_b, 1, 2), SGU_HW, axis=2)
    bias_s = jnp.tile(jnp.repeat(jnp.swapaxes(sgu_b[:, :, :dseq], 1, 2), SGU_HW, axis=2), (1, CHUNK // dseq, 1))
    lane_pad = LANES - N_EXPERTS - N_GROUPS
    w_rt32 = jnp.pad(jnp.concatenate([router_e, router_g], axis=2), ((0, 0), (0, 0), (0, lane_pad)))
    w_rt_hi = w_rt32.astype(BF16)
    w_rt = jnp.concatenate([w_rt_hi, (w_rt32 - w_rt_hi.astype(F32)).astype(BF16)], axis=2)
    b_rt = row(jnp.pad(jnp.concatenate([router_e_b, router_g_b], axis=1), ((0, 0), (0, lane_pad))))
    same_group = jnp.eye(len(POOL_WINDOWS), dtype=F32)[None, :, None, :, None]
    pool_bd = (pool_w[:, :, :, None, :] * same_group).reshape(depth, BRANCH_W, BRANCH_W).astype(BF16)

    def weights(wcat, sbias):
        return [row(norm1_g), pool_bd, row(pool_scale), rep8(conf_dw), rep8(conf_dw_b), row(conf_ln_g),
                row(conf_ln_b), rep8(sconv_w), row(sgu_ln_g), row(sgu_ln_b), wcat, sbias, row(norm2_g), w_rt, b_rt]

    weights_p, weights_s = weights(wcat_p, bias_p), weights(wcat_s, bias_s)
    final_row = final_g.reshape(1, -1)

    h_p, h_s = x_prompt.reshape(n_p, D_MODEL), x_sample.reshape(n_s, D_MODEL)
    off_p, off_s = 0, 0
    states_out = []
    for l in range(depth):
        h1, route, xn2, pool_p, conv_p, sconv_p, v_p = _mixer_call(
            cfg_p._replace(layer=l), bp, n, h_p, off_p, None, weights_p, big_weights, None)
        h1, route, xn2, pool_s, conv_s, sconv_s, v_s = _mixer_call(
            cfg_s._replace(layer=l), bs, n, h_s, off_s, (state_pool, state_conv, state_sconv),
            weights_s, big_weights, (h1, route, xn2))
        states_out.append((pool_p, conv_p, sconv_p, v_p, pool_s, conv_s, sconv_s, v_s))

        outs = _moe(h1, route, xn2, l, w_gate, w_up, w_down, final_row, l == depth - 1, n_p)
        h_p = h_s = outs[0]
        off_p, off_s = 0, n_p // MIX_TILE_S

    y_prompt = outs[0].reshape(bp, seq, D_MODEL)
    y_sample = outs[1].reshape(bs, dseq, D_MODEL)
    st = [jnp.stack([s[k] for s in states_out]) for k in range(8)]
    return (y_prompt, y_sample, st[0], st[1], st[2], st[3], st[4], st[5], st[6], st[7])
```

```python
import functools
from typing import NamedTuple

import jax
import jax.numpy as jnp
from jax import lax
from jax.experimental import pallas as pl
from jax.experimental.pallas import tpu as pltpu

F32 = jnp.float32
BF16 = jnp.bfloat16

D_MODEL = 1024
BRANCH_W = 256
N_BRANCH = 4
POOL_WINDOWS = (2, 4, 8, 16)
POOL_GW = 64
POOL_HIST = 15
CONF_WIDTH = 31
CONF_HIST = 30
SCONV_WIDTH = 3
SCONV_HIST = 2
CHUNK = 128
PAST_LEN = 16384
SGU_HEADS = 4
SGU_HW = 64
N_GROUPS = 4
EXPERTS_PER_GROUP = 8
N_EXPERTS = 32
D_EXPERT = 256
RMS_EPS = 1e-6
LN_EPS = 1e-5
IN_COLS = 6144
GATE_COL0 = 2048

COL_TILE = 256
LANES = 128
SUBLANES = 8
POOL_PAD = 32
CONF_PAD = 32
SCONV_PAD = 8
ROW_BLOCK = 32
W_IN_ROWS = 64
WINDOW_ITEMS = 4
ROUTE_GROUP_LANE0 = 32
NEG_BIG = -3.0e38

MIX_TILE = 512
MIX_TILE_S = 256
SORT_TILE = 512
SEG_PAD = 16
PERM_BLOCK = 256
N_PERM_BLOCKS = -(-(2 * SORT_TILE + N_EXPERTS * (SEG_PAD - 1)) // PERM_BLOCK)
LOCAL_ROWS = N_PERM_BLOCKS * PERM_BLOCK
EXP_TILE = 512
CHUNKS_PER_STEP = EXP_TILE // SEG_PAD
EXP_SLOTS = 3
SPARE_BLOCKS = -(-EXP_SLOTS * EXP_TILE // LOCAL_ROWS)
VMEM_LIMIT = 56 * 1024 * 1024


class MixCfg(NamedTuple):
    layer: int
    nb: int
    t: int
    has_state: bool
    start_pos: int
    row_off: int
    v_rows: int
    seq_len: int


def _rms(x, g):
    return x * lax.rsqrt(jnp.mean(x * x, axis=-1, keepdims=True) + RMS_EPS) * g


def _ln(x, g, b):
    mu = jnp.mean(x, axis=-1, keepdims=True)
    xc = x - mu
    return xc * lax.rsqrt(jnp.mean(xc * xc, axis=-1, keepdims=True) + LN_EPS) * g + b


def _sigmoid(x):
    return 0.5 * jnp.tanh(0.5 * x) + 0.5


def _gelu_tanh(x):
    return 0.5 * x * (1.0 + jnp.tanh(0.7978845608028654 * (x + 0.044715 * (x * x * x))))


def _row_blocks(nb, t):
    if t >= ROW_BLOCK:
        return [(slice(b, b + 1), t0, ROW_BLOCK) for b in range(nb) for t0 in range(0, t, ROW_BLOCK)]
    bb = ROW_BLOCK // t
    return [(slice(b0, b0 + bb), 0, t) for b0 in range(0, nb, bb)]


def _load_big_weights(layer, w_in_hbm, w_br_hbm, w_out_hbm, w_in, w_br, w_out, stage, stage_b, sem):
    n_in, n_out = IN_COLS // COL_TILE, D_MODEL // COL_TILE
    jobs = ([("in", r) for r in range(0, D_MODEL, W_IN_ROWS)] + [("out", r) for r in range(0, D_MODEL, BRANCH_W)]
            + [("br", i) for i in range(N_BRANCH)])

    def copy(job, slot):
        kind, j = job
        if kind == "in":
            return pltpu.make_async_copy(w_in_hbm.at[layer, pl.ds(j, W_IN_ROWS), :], stage.at[slot], sem.at[slot])
        src = w_br_hbm.at[layer, j] if kind == "br" else w_out_hbm.at[layer, pl.ds(j, BRANCH_W), :]
        return pltpu.make_async_copy(src, stage_b.at[slot], sem.at[slot])

    def col(buf, slot, c):
        return buf[slot, :, c * COL_TILE:(c + 1) * COL_TILE]

    copy(jobs[0], 0).start()
    for n, job in enumerate(jobs):
        slot = n % 2
        if n + 1 < len(jobs):
            copy(jobs[n + 1], 1 - slot).start()
        copy(job, slot).wait()
        kind, j = job
        if kind == "in":
            for c in range(n_in):
                tile = col(stage, slot, c)
                w_in[c, j:j + W_IN_ROWS, :] = (tile * 0.5 if c * COL_TILE >= GATE_COL0 else tile).astype(BF16)
        elif kind == "out":
            for c in range(n_out):
                w_out[c, j:j + BRANCH_W, :] = col(stage_b, slot, c).astype(BF16)
        else:
            for c in range(n_out):
                w_br[j * n_out + c] = (col(stage_b, slot, c) * 0.5).astype(BF16)


def _mixer_kernel(cfg, *refs):
    nb, t, tm = cfg.nb, cfg.t, cfg.nb * cfg.t
    refs = list(refs)
    x_ref = refs.pop(0)
    if cfg.has_state:
        pool_st, conv_st, sconv_st = (refs.pop(0).at[0] for _ in range(3))
    (n1g, pool_w, pool_sc, cdw, cdb, clg, clb, scw, slg, slb, wcat, sbias, n2g, w_rt, b_rt) = (
        r.at[0] for r in refs[:15])
    w_in_hbm, w_br_hbm, w_out_hbm = refs[15:18]
    refs = refs[18:]
    if cfg.has_state:
        refs = refs[3:]
    h1_o, route_o, xn2_o, pool_o, conv_o, sconv_o, v_o = refs[:7]
    (pool_ext, sum_a, sum_b, conv_ext, conv_sh, sc_ext, buf_a, buf_b, buf_c, gate_buf, xb_buf,
     w_in, w_br, w_out, stage, stage_b, w_sem) = refs[7:]

    first_step = pl.program_id(0) == 0
    if not cfg.has_state:
        first_step = first_step & (pl.program_id(1) == 0)
    pl.when(first_step)(functools.partial(
        _load_big_weights, cfg.layer, w_in_hbm, w_br_hbm, w_out_hbm, w_in, w_br, w_out, stage, stage_b, w_sem))

    if cfg.has_state:
        seq_pos0 = cfg.start_pos
        pool_ext[:, 0:POOL_PAD - SUBLANES, :] = jnp.zeros((nb, POOL_PAD - SUBLANES, BRANCH_W), F32)
        pool_ext[:, POOL_PAD - POOL_HIST:POOL_PAD, :] = pool_st[...]
        conv_ext[:, CONF_PAD - CONF_HIST:CONF_PAD, :] = conv_st[...]
        sc_ext[:, SCONV_PAD - SCONV_HIST:SCONV_PAD, :] = sconv_st[...]
    else:
        c = pl.program_id(1)
        seq_pos0 = cfg.start_pos + c * t

        @pl.when(c == 0)
        def _():
            pool_ext[:, 0:POOL_PAD, :] = jnp.zeros((nb, POOL_PAD, BRANCH_W), F32)
            conv_ext[:, 0:CONF_PAD, :] = jnp.zeros((nb, CONF_PAD, BRANCH_W), F32)
            sc_ext[:, 0:SCONV_PAD, :] = jnp.zeros((nb, SCONV_PAD, BRANCH_W), F32)

    x = x_ref[...]
    xb_buf[...] = _rms(x, n1g[...]).astype(BF16)

    def proj(lo, hi):
        assert lo % COL_TILE == 0 and hi == lo + COL_TILE
        return jnp.dot(xb_buf[...], w_in[lo // COL_TILE], preferred_element_type=F32)

    a_pool = proj(0, 256)
    pool_ext[:, POOL_PAD:, :] = a_pool.reshape(nb, t, BRANCH_W)
    glu = proj(256, 512) * _sigmoid(proj(512, 768))
    conv_ext[:, CONF_PAD:, :] = glu.reshape(nb, t, BRANCH_W)
    z = proj(1024, 1280) * proj(1280, 1536)
    sc_ext[:, SCONV_PAD:, :] = z.reshape(nb, t, BRANCH_W)

    pl_len = POOL_PAD + t
    sum_b[:, 8:pl_len, :] = pool_ext[:, 8:pl_len, :] + pool_ext[:, 7:pl_len - 1, :]
    sum_a[:, 16:pl_len, :] = sum_b[:, 16:pl_len, :] + sum_b[:, 14:pl_len - 2, :]
    sum_b[:, 24:pl_len, :] = sum_a[:, 24:pl_len, :] + sum_a[:, 20:pl_len - 4, :]
    for sh in range(SUBLANES):
        n_rows = t + SUBLANES * ((CONF_WIDTH - 1 - sh) // SUBLANES)
        first = CONF_PAD - CONF_HIST + sh
        conv_sh[sh, :, 0:n_rows, :] = conv_ext[:, first:first + n_rows, :]

    def window_block(bs, t0, tb):
        bb = bs.stop - bs.start
        shp = (bb, tb, BRANCH_W)
        lane = lax.broadcasted_iota(jnp.int32, shp, 2)
        pos = seq_pos0 + t0 + lax.broadcasted_iota(jnp.int32, shp, 1)
        r0 = POOL_PAD + t0
        cur = pool_ext[bs, r0:r0 + tb, :]
        s2 = cur + pool_ext[bs, r0 - 1:r0 - 1 + tb, :]
        s4 = sum_a[bs, r0:r0 + tb, :]
        s8 = sum_b[bs, r0:r0 + tb, :]
        s16 = s8 + sum_b[bs, r0 - SUBLANES:r0 - SUBLANES + tb, :]
        g1, g2, g3 = (lane < g * POOL_GW for g in (1, 2, 3))
        win_sum = jnp.where(g1, s2, jnp.where(g2, s4, jnp.where(g3, s8, s16)))
        win = jnp.where(g1, POOL_WINDOWS[0], jnp.where(g2, POOL_WINDOWS[1],
                                                         jnp.where(g3, POOL_WINDOWS[2], POOL_WINDOWS[3])))
        cnt = jnp.minimum(pos + 1, win).astype(F32)
        buf_a[bs, t0:t0 + tb, :] = win_sum / cnt - cur
        tiles = (ROW_BLOCK // SUBLANES, SUBLANES, BRANCH_W)
        acc = jnp.broadcast_to(cdb[...][None], tiles)
        for k in range(CONF_WIDTH):
            q, sh = divmod(k, SUBLANES)
            r = t0 + SUBLANES * q
            acc = acc + conv_sh[sh, bs, r:r + tb, :].reshape(tiles) * cdw[k][None]
        buf_b[bs, t0:t0 + tb, :] = acc.reshape(shp)
        base = SCONV_PAD - SCONV_HIST + t0
        acc = sc_ext[bs, base:base + tb, :].reshape(tiles) * scw[0][None]
        for k in range(1, SCONV_WIDTH):
            acc = acc + sc_ext[bs, base + k:base + k + tb, :].reshape(tiles) * scw[k][None]
        buf_c[bs, t0:t0 + tb, :] = acc.reshape(shp)

    def gate_chunk(k):
        g0 = k * COL_TILE
        gate_buf[:, g0:g0 + COL_TILE] = jnp.tanh(proj(GATE_COL0 + g0, GATE_COL0 + g0 + COL_TILE)) + 1.0

    br = {}

    def item_windows(blks):
        def run():
            for bs, t0, tb in blks:
                window_block(bs, t0, tb)
        return run

    def item_state():
        pool_o[...] = pool_ext[:, POOL_PAD + t - POOL_HIST:POOL_PAD + t, :]
        conv_o[...] = conv_ext[:, CONF_PAD + t - CONF_HIST:CONF_PAD + t, :]
        sconv_o[...] = sc_ext[:, SCONV_PAD + t - SCONV_HIST:SCONV_PAD + t, :]
        if not cfg.has_state:
            pool_ext[:, 0:POOL_PAD, :] = pool_ext[:, t:t + POOL_PAD, :]
            conv_ext[:, 0:CONF_PAD, :] = conv_ext[:, t:t + CONF_PAD, :]
            sc_ext[:, 0:SCONV_PAD, :] = sc_ext[:, t:t + SCONV_PAD, :]

    def item_a():
        pooled = buf_a[...].reshape(tm, BRANCH_W).astype(BF16)
        br["a"] = (jnp.dot(pooled, pool_w[...], preferred_element_type=F32) * pool_sc[...]).astype(BF16)

    def item_b():
        cb = _ln(buf_b[...].reshape(tm, BRANCH_W), clg[...], clb[...])
        br["b"] = (cb * _sigmoid(cb)).astype(BF16)

    def item_c():
        br["c"] = (proj(768, 1024) * buf_c[...].reshape(tm, BRANCH_W)).astype(BF16)

    def item_u():
        br["u"] = _gelu_tanh(proj(1536, 1792))

    def item_v():
        v = _ln(_gelu_tanh(proj(1792, 2048)), slg[...], slb[...])
        if cfg.has_state:
            v_o[...] = v.reshape(nb, t, BRANCH_W)
        else:
            v_o[...] = v[tm - cfg.v_rows:, :].reshape(1, cfg.v_rows, BRANCH_W)
        br["v"] = v

    def item_d():
        head = lax.broadcasted_iota(jnp.int32, (CHUNK, BRANCH_W), 1) // SGU_HW
        bias = sbias[...]
        mixed = []
        for j in range(tm // CHUNK):
            vj = br["v"][j * CHUNK:(j + 1) * CHUNK, :]
            stacked = jnp.concatenate([jnp.where(head == h, vj, 0.0) for h in range(SGU_HEADS)], axis=0)
            mixed.append(jnp.dot(wcat[...], stacked.astype(BF16), preferred_element_type=F32) + bias)
        br["d"] = (br["u"] * jnp.concatenate(mixed, axis=0)).astype(BF16)

    blocks = _row_blocks(nb, t)
    n_gate = N_BRANCH * D_MODEL // COL_TILE
    per_item = -(-len(blocks) // WINDOW_ITEMS)
    items =[item_windows(blocks[i:i + per_item]) for i in range(0, len(blocks), per_item)]
    items += [item_state, item_c, item_b, item_u, item_v, item_a, item_d]
    assert len(items) <= n_gate
    for k in range(n_gate):
        if k < len(items):
            items[k]()
        gate_chunk(k)

    n_ct = D_MODEL // COL_TILE
    brs = [br["a"], br["b"], br["c"], br["d"]]
    for c in range(n_ct):
        part = None
        for i in range(N_BRANCH):
            g0 = i * D_MODEL + c * COL_TILE
            term = gate_buf[:, g0:g0 + COL_TILE] * jnp.dot(
                brs[i], w_br[i * n_ct + c], preferred_element_type=F32)
            part = term if part is None else part + term
        xb_buf[:, c * COL_TILE:(c + 1) * COL_TILE] = part.astype(BF16)
    for c in range(n_ct):
        cs = slice(c * COL_TILE, (c + 1) * COL_TILE)
        h1_o[:, cs] = x_ref[:, cs] + jnp.dot(xb_buf[...], w_out[c], preferred_element_type=F32)

    xn2 = _rms(h1_o[...], n2g[...])
    x_hi = xn2.astype(BF16)
    xn2_o[...] = x_hi
    x_lo = (xn2 - x_hi.astype(F32)).astype(BF16)
    hi_both = jnp.dot(x_hi, w_rt[...], preferred_element_type=F32)
    lo_hi = jnp.dot(x_lo, w_rt[:, 0:LANES], preferred_element_type=F32)
    logits = (hi_both[:, 0:LANES] + (lo_hi + hi_both[:, LANES:2 * LANES])) + b_rt[...]
    lane = lax.broadcasted_iota(jnp.int32, (tm, LANES), 1)
    is_g = (lane >= ROUTE_GROUP_LANE0) & (lane < ROUTE_GROUP_LANE0 + N_GROUPS)
    glog = jnp.where(is_g, logits, NEG_BIG)
    gmax = jnp.max(glog, axis=-1, keepdims=True)
    lane_f = lane.astype(F32)
    no_lane = float(4 * LANES)
    gsel = jnp.min(jnp.where(glog == gmax, lane_f, no_lane), axis=-1, keepdims=True) - ROUTE_GROUP_LANE0
    pg = 1.0 / jnp.sum(jnp.where(is_g, jnp.exp(glog - gmax), 0.0), axis=-1, keepdims=True)
    in_grp = (lane < N_EXPERTS) & ((lane // EXPERTS_PER_GROUP) == gsel.astype(jnp.int32))
    el = jnp.where(in_grp, logits, NEG_BIG)
    m1 = jnp.max(el, axis=-1, keepdims=True)
    i1 = jnp.min(jnp.where(in_grp & (el == m1), lane_f, no_lane), axis=-1, keepdims=True)
    rest = in_grp & (lane_f != i1)
    el2 = jnp.where(rest, logits, NEG_BIG)
    m2 = jnp.max(el2, axis=-1, keepdims=True)
    i2 = jnp.min(jnp.where(rest & (el2 == m2), lane_f, no_lane), axis=-1, keepdims=True)
    e21 = jnp.exp(m2 - m1)
    w1 = pg / (1.0 + e21)
    w2 = pg * e21 / (1.0 + e21)
    route_o[...] = jnp.where(lane == 0, i1, jnp.where(lane == 1, i2,
                             jnp.where(lane == 2, w1, jnp.where(lane == 3, w2, 0.0))))


def _layer_spec(shape, layer):
    nd = len(shape)
    return pl.BlockSpec((1,) + shape[1:], lambda *_: (layer,) + (0,) * (nd - 1), pipeline_mode=pl.Buffered(1))


def _mixer_call(cfg, n_seq, n_rows_total, x2d, x_tile_off, states, weights, big_weights, aliased):
    nb, t = cfg.nb, cfg.t
    tm = nb * t
    if cfg.has_state:
        grid = (n_seq // nb,)
        tok = lambda i: (x_tile_off + i, 0)
        out_tok = lambda i: (cfg.row_off + i, 0)
        seq3 = lambda i: (i, 0, 0)
    else:
        grid = (n_seq, cfg.seq_len // t)
        nt = grid[1]
        tok = lambda b, c: (x_tile_off + b * nt + c, 0)
        out_tok = lambda b, c: (cfg.row_off + b * nt + c, 0)
        seq3 = lambda b, c: (b, 0, 0)

    in_specs = [pl.BlockSpec((tm, D_MODEL), tok)]
    args = [x2d]
    if cfg.has_state:
        for s in states:
            in_specs.append(pl.BlockSpec((1, nb) + s.shape[2:], lambda i: (cfg.layer, i, 0, 0)))
            args.append(s)
    for w in weights:
        in_specs.append(_layer_spec(w.shape, cfg.layer))
        args.append(w)
    for w in big_weights:
        in_specs.append(pl.BlockSpec(memory_space=pl.ANY))
        args.append(w)
    io_alias = {}
    if cfg.has_state:
        for k, a in enumerate(aliased):
            in_specs.append(pl.BlockSpec(memory_space=pl.ANY))
            io_alias[len(args)] = k
            args.append(a)

    out_shape = [
        jax.ShapeDtypeStruct((n_rows_total, D_MODEL), F32),
        jax.ShapeDtypeStruct((n_rows_total, LANES), F32),
        jax.ShapeDtypeStruct((n_rows_total, D_MODEL), BF16),
        jax.ShapeDtypeStruct((n_seq, POOL_HIST, BRANCH_W), F32),
        jax.ShapeDtypeStruct((n_seq, CONF_HIST, BRANCH_W), F32),
        jax.ShapeDtypeStruct((n_seq, SCONV_HIST, BRANCH_W), F32),
        jax.ShapeDtypeStruct((n_seq, cfg.v_rows, BRANCH_W), F32),
    ]
    out_specs = [
        pl.BlockSpec((tm, D_MODEL), out_tok),
        pl.BlockSpec((tm, LANES), out_tok),
        pl.BlockSpec((tm, D_MODEL), out_tok),
        pl.BlockSpec((nb, POOL_HIST, BRANCH_W), seq3),
        pl.BlockSpec((nb, CONF_HIST, BRANCH_W), seq3),
        pl.BlockSpec((nb, SCONV_HIST, BRANCH_W), seq3),
        pl.BlockSpec((nb, cfg.v_rows, BRANCH_W), seq3),
    ]
    scratch = [
        pltpu.VMEM((nb, POOL_PAD + t, BRANCH_W), F32),
        pltpu.VMEM((nb, POOL_PAD + t, BRANCH_W), F32),
        pltpu.VMEM((nb, POOL_PAD + t, BRANCH_W), F32),
        pltpu.VMEM((nb, CONF_PAD + t, BRANCH_W), F32),
        pltpu.VMEM((SUBLANES, nb, t + CONF_PAD - SUBLANES, BRANCH_W), F32),
        pltpu.VMEM((nb, SCONV_PAD + t, BRANCH_W), F32),
        pltpu.VMEM((nb, t, BRANCH_W), F32),
        pltpu.VMEM((nb, t, BRANCH_W), F32),
        pltpu.VMEM((nb, t, BRANCH_W), F32),
        pltpu.VMEM((tm, N_BRANCH * D_MODEL), F32),
        pltpu.VMEM((tm, D_MODEL), BF16),
        pltpu.VMEM((IN_COLS // COL_TILE, D_MODEL, COL_TILE), BF16),
        pltpu.VMEM((N_BRANCH * D_MODEL // COL_TILE, BRANCH_W, COL_TILE), BF16),
        pltpu.VMEM((D_MODEL // COL_TILE, D_MODEL, COL_TILE), BF16),
        pltpu.VMEM((2, W_IN_ROWS, IN_COLS), F32),
        pltpu.VMEM((2, BRANCH_W, D_MODEL), F32),
        pltpu.SemaphoreType.DMA((2,)),
    ]
    return pl.pallas_call(
        functools.partial(_mixer_kernel, cfg),
        grid=grid,
        in_specs=in_specs,
        out_specs=out_specs,
        out_shape=out_shape,
        scratch_shapes=scratch,
        input_output_aliases=io_alias,
        compiler_params=pltpu.CompilerParams(
            dimension_semantics=("arbitrary",) * len(grid), vmem_limit_bytes=VMEM_LIMIT),
        name="mixer_sample" if cfg.has_state else "mixer_prompt",
    )(*args)


def _sort_kernel(xn_ref, route_ref, xl_o, pos_o, nch_o):
    t = SORT_TILE
    xn = xn_ref[...]
    r = route_ref[...]
    lane_i = lax.broadcasted_iota(jnp.int32, (t, LANES), 1)
    lane = lane_i.astype(F32)
    hit1 = lane == r[:, 0:1]
    hit2 = lane == r[:, 1:2]
    onehot = jnp.where(hit1 | hit2, 1.0, 0.0)
    row = lax.broadcasted_iota(jnp.int32, (t, t), 0)
    col = lax.broadcasted_iota(jnp.int32, (t, t), 1)
    strict_lower = jnp.where(col < row, 1.0, 0.0).astype(BF16)
    before = jnp.dot(strict_lower, onehot.astype(BF16), preferred_element_type=F32)
    cnt = jnp.sum(onehot, axis=0, keepdims=True)
    nch = jnp.floor((cnt + (SEG_PAD - 1)) * (1.0 / SEG_PAD))
    er = lax.broadcasted_iota(jnp.int32, (LANES, LANES), 0)
    ec = lax.broadcasted_iota(jnp.int32, (LANES, LANES), 1)
    strict_upper = jnp.where(er < ec, 1.0, 0.0).astype(BF16)
    nch8 = jnp.broadcast_to(nch, (SUBLANES, LANES)).astype(BF16)
    seg0 = jnp.dot(nch8, strict_upper, preferred_element_type=F32)[0:1, :] * SEG_PAD
    where = seg0 + before
    pos0 = jnp.sum(jnp.where(hit1, where, 0.0), axis=-1, keepdims=True)
    pos1 = jnp.sum(jnp.where(hit2, where, 0.0), axis=-1, keepdims=True)
    pos = jnp.where(lane_i == 0, pos0, jnp.where(lane_i == 1, pos1, -1.0))
    pos_o[...] = pos
    nch_o[...] = jnp.broadcast_to(nch, (1, SUBLANES, LANES))
    pos_t = jnp.transpose(pos)
    p0 = pos_t[0:1, :]
    p1 = pos_t[1:2, :]
    used_rows = jnp.sum(nch) * SEG_PAD

    def perm_block(blk):
        dst = (lax.broadcasted_iota(jnp.int32, (PERM_BLOCK, t), 0) + blk * PERM_BLOCK).astype(F32)
        perm = jnp.where((dst == p0) | (dst == p1), 1.0, 0.0).astype(BF16)
        xl_o[0, blk * PERM_BLOCK:(blk + 1) * PERM_BLOCK, :] = jnp.dot(
            perm, xn, preferred_element_type=F32).astype(BF16)

    for blk in range(N_PERM_BLOCKS - 1):
        perm_block(blk)
    pl.when(used_rows > (N_PERM_BLOCKS - 1) * PERM_BLOCK)(functools.partial(perm_block, N_PERM_BLOCKS - 1))


def _sort_call(xn2, route):
    n = xn2.shape[0]
    nt = n // SORT_TILE
    return pl.pallas_call(
        _sort_kernel,
        grid=(nt,),
        in_specs=[pl.BlockSpec((SORT_TILE, D_MODEL), lambda i: (i, 0)),
                  pl.BlockSpec((SORT_TILE, LANES), lambda i: (i, 0))],
        out_specs=[pl.BlockSpec((1, LOCAL_ROWS, D_MODEL), lambda i: (i, 0, 0)),
                   pl.BlockSpec((SORT_TILE, LANES), lambda i: (i, 0)),
                   pl.BlockSpec((1, SUBLANES, LANES), lambda i: (i, 0, 0))],
        out_shape=[jax.ShapeDtypeStruct((nt + SPARE_BLOCKS, LOCAL_ROWS, D_MODEL), BF16),
                   jax.ShapeDtypeStruct((n, LANES), F32),
                   jax.ShapeDtypeStruct((nt, SUBLANES, LANES), F32)],
        compiler_params=pltpu.CompilerParams(dimension_semantics=("arbitrary",), vmem_limit_bytes=VMEM_LIMIT),
        name="moe_sort",
    )(xn2, route)


def _chunk_copy(src, src_row, dst, dst_row, sem):
    return pltpu.make_async_copy(src.at[pl.ds(src_row, SEG_PAD), :], dst.at[pl.ds(dst_row, SEG_PAD), :], sem)


def _expert_kernel(layer, trash_row0, se_ref, nv_ref, src_ref, dst_ref, wfirst_ref, wpar_ref, wnext_ref,
                   xl_in, wg_hbm, wu_hbm, wd_hbm, xl_io, *scratch):
    del xl_in
    ns = EXP_SLOTS
    xbuf, ybuf = scratch[:ns], scratch[ns:2 * ns]
    sem_in, sem_out, wg, wu, wd, w_sem = scratch[2 * ns:]
    s = pl.program_id(0)
    nv = nv_ref[0]

    def weight_copies(expert, par):
        return [pltpu.make_async_copy(hbm.at[layer, expert], buf.at[par], w_sem.at[par])
                for hbm, buf in ((wg_hbm, wg), (wu_hbm, wu), (wd_hbm, wd))]

    def gather(step, sl):
        return [_chunk_copy(xl_io, pl.multiple_of(src_ref[step * CHUNKS_PER_STEP + j], SEG_PAD),
                            xbuf[sl], j * SEG_PAD, sem_in.at[sl]) for j in range(CHUNKS_PER_STEP)]

    def write_back(step, sl):
        return [_chunk_copy(ybuf[sl], j * SEG_PAD, xl_io,
                            pl.multiple_of(dst_ref[step * CHUNKS_PER_STEP + j], SEG_PAD), sem_out.at[sl])
                for j in range(CHUNKS_PER_STEP)]

    def spare_write(sl):
        return [_chunk_copy(ybuf[sl], j * SEG_PAD, xl_io, trash_row0 + sl * EXP_TILE + j * SEG_PAD,
                            sem_out.at[sl]) for j in range(CHUNKS_PER_STEP)]

    @pl.when(s == 0)
    def _():
        for c in weight_copies(se_ref[0], 0):
            c.start()
        for ahead in range(ns - 1):
            for c in gather(ahead, ahead):
                c.start()
        for sl in range(ns):
            ybuf[sl][...] = jnp.zeros_like(ybuf[sl])
            for c in spare_write(sl):
                c.start()

    def live_step(sl):
        par = wpar_ref[s]

        @pl.when(wfirst_ref[s] == 1)
        def _():
            for c in weight_copies(se_ref[s], par):
                c.wait()

            @pl.when(wnext_ref[s] >= 0)
            def _():
                for c in weight_copies(wnext_ref[s], 1 - par):
                    c.start()

        for c in gather(s + ns - 1, (sl + ns - 1) % ns):
            c.start()
        for c in gather(s, sl):
            c.wait()
        xb = xbuf[sl][...]
        a = jnp.dot(xb, wg[par].astype(BF16), preferred_element_type=F32)
        b = jnp.dot(xb, wu[par].astype(BF16), preferred_element_type=F32)
        hh = (a * _sigmoid(a) * b).astype(BF16)
        y = jnp.dot(hh, wd[par].astype(BF16), preferred_element_type=F32).astype(BF16)
        for c in write_back(s, sl):
            c.wait()
        ybuf[sl][...] = y
        for c in write_back(s, sl):
            c.start()

    def drain_step(sl):
        for ahead in range(ns - 1):
            for c in gather(s + ahead, (sl + ahead) % ns):
                c.wait()
        for every in range(ns):
            for c in spare_write(every):
                c.wait()

    for sl in range(ns):
        pl.when((s < nv) & (s % ns == sl))(functools.partial(live_step, sl))
        pl.when((s == nv) & (s % ns == sl))(functools.partial(drain_step, sl))


def _expert_call(xl, trash_row0, tables, layer, w_gate, w_up, w_down):
    n_steps = tables[0].shape[0] - (EXP_SLOTS - 1)
    grid_spec = pltpu.PrefetchScalarGridSpec(
        num_scalar_prefetch=len(tables),
        grid=(n_steps,),
        in_specs=[pl.BlockSpec(memory_space=pl.ANY) for _ in range(4)],
        out_specs=pl.BlockSpec(memory_space=pl.ANY),
        scratch_shapes=[pltpu.VMEM((EXP_TILE, D_MODEL), BF16) for _ in range(2 * EXP_SLOTS)] + [
                        pltpu.SemaphoreType.DMA((EXP_SLOTS,)), pltpu.SemaphoreType.DMA((EXP_SLOTS,)),
                        pltpu.VMEM((2, D_MODEL, D_EXPERT), F32), pltpu.VMEM((2, D_MODEL, D_EXPERT), F32),
                        pltpu.VMEM((2, D_EXPERT, D_MODEL), F32), pltpu.SemaphoreType.DMA((2,))],
    )
    return pl.pallas_call(
        functools.partial(_expert_kernel, layer, trash_row0),
        grid_spec=grid_spec,
        out_shape=jax.ShapeDtypeStruct(xl.shape, BF16),
        input_output_aliases={len(tables): 0},
        compiler_params=pltpu.CompilerParams(dimension_semantics=("arbitrary",), has_side_effects=True),
        name="moe_experts",
    )(*tables, xl, w_gate, w_up, w_down)


def _combine_kernel(final_norm, n_first, h1_ref, route_ref, pos_ref, fg, yl_ref, *outs):
    t = SORT_TILE
    r = route_ref[...]
    p = pos_ref[...]
    w1, w2 = r[:, 2:3], r[:, 3:4]
    pos0, pos1 = p[:, 0:1], p[:, 1:2]
    *outs, acc_ref = outs

    def gathered(blk):
        src = (lax.broadcasted_iota(jnp.int32, (t, PERM_BLOCK), 1) + blk * PERM_BLOCK).astype(F32)
        pw = jnp.where(src == pos0, w1, 0.0) + jnp.where(src == pos1, w2, 0.0)
        return jnp.dot(pw.astype(BF16), yl_ref[0, blk * PERM_BLOCK:(blk + 1) * PERM_BLOCK, :],
                       preferred_element_type=F32)

    acc = h1_ref[...]
    for blk in range(N_PERM_BLOCKS - 1):
        acc = acc + gathered(blk)
    acc_ref[...] = acc

    @pl.when(jnp.max(p) >= (N_PERM_BLOCKS - 1) * PERM_BLOCK)
    def _():
        acc_ref[...] += gathered(N_PERM_BLOCKS - 1)

    if not final_norm:
        outs[0][...] = acc_ref[...]
        return
    y = _rms(acc_ref[...], fg[...])
    i = pl.program_id(0)

    @pl.when(i < n_first)
    def _():
        outs[0][...] = y

    @pl.when(i >= n_first)
    def _():
        outs[1][...] = y


def _combine_call(h1, route, pos, yl, final_g, final_norm, n_first_rows):
    n = h1.shape[0]
    nt = n // SORT_TILE
    n_first = n_first_rows // SORT_TILE
    tok = lambda i: (i, 0)
    if final_norm:
        out_shape = [jax.ShapeDtypeStruct((n_first_rows, D_MODEL), F32),
                     jax.ShapeDtypeStruct((n - n_first_rows, D_MODEL), F32)]
        out_specs = [pl.BlockSpec((SORT_TILE, D_MODEL), lambda i: (jnp.minimum(i, n_first - 1), 0)),
                     pl.BlockSpec((SORT_TILE, D_MODEL), lambda i: (jnp.maximum(i - n_first, 0), 0))]
    else:
        out_shape = [jax.ShapeDtypeStruct((n, D_MODEL), F32)]
        out_specs = [pl.BlockSpec((SORT_TILE, D_MODEL), tok)]
    return pl.pallas_call(
        functools.partial(_combine_kernel, final_norm, n_first),
        grid=(nt,),
        in_specs=[pl.BlockSpec((SORT_TILE, D_MODEL), tok),
                  pl.BlockSpec((SORT_TILE, LANES), tok),
                  pl.BlockSpec((SORT_TILE, LANES), tok),
                  pl.BlockSpec((1, D_MODEL), lambda i: (0, 0)),
                  pl.BlockSpec((1, LOCAL_ROWS, D_MODEL), lambda i: (i, 0, 0))],
        out_specs=out_specs,
        out_shape=out_shape,
        scratch_shapes=[pltpu.VMEM((SORT_TILE, D_MODEL), F32)],
        compiler_params=pltpu.CompilerParams(dimension_semantics=("arbitrary",), vmem_limit_bytes=VMEM_LIMIT),
        name="moe_combine",
    )(h1, route, pos, final_g, yl)


def _expert_tables(nch, n_steps, trash_row0):
    nt = nch.shape[0]
    cps = CHUNKS_PER_STEP
    seg_row0 = (jnp.cumsum(nch, axis=1) - nch) * SEG_PAD
    first = jnp.cumsum(nch, axis=0) - nch
    tot = jnp.sum(nch, axis=0)
    steps = (tot + cps - 1) // cps
    step_end = jnp.cumsum(steps)
    n_valid = step_end[-1:]
    s_ids = jnp.arange(n_steps, dtype=jnp.int32)
    step_expert = jnp.minimum(jnp.sum((step_end[None, :] <= s_ids[:, None]).astype(jnp.int32), axis=1),
                              N_EXPERTS - 1)
    sel = (step_expert[:, None] == jnp.arange(N_EXPERTS, dtype=jnp.int32)[None, :]).astype(jnp.int32)
    step0 = sel @ (step_end - steps)
    tot_s = sel @ tot
    first_s = sel @ first.T
    row0_s = sel @ seg_row0.T
    k = (s_ids - step0)[:, None] * cps + jnp.arange(cps, dtype=jnp.int32)[None, :]
    ok = (k < tot_s[:, None]) & (s_ids < n_valid[0])[:, None]
    tile = jnp.sum((first_s[:, None, :] <= k[:, :, None]).astype(jnp.int32), axis=2) - 1
    tsel = (tile[:, :, None] == jnp.arange(nt, dtype=jnp.int32)[None, None, :]).astype(jnp.int32)
    first_k = jnp.sum(tsel * first_s[:, None, :], axis=2)
    row0_k = jnp.sum(tsel * row0_s[:, None, :], axis=2)
    row = tile * LOCAL_ROWS + row0_k + (k - first_k) * SEG_PAD
    live = (s_ids < n_valid[0])[:, None]
    src = jnp.where(live, jnp.where(ok, row, row[:, 0:1]), 0)
    spare = trash_row0 + (s_ids % EXP_SLOTS)[:, None] * EXP_TILE + jnp.arange(cps, dtype=jnp.int32)[None, :] * SEG_PAD
    dst = jnp.where(ok, row, spare)
    e_ids = jnp.arange(N_EXPERTS, dtype=jnp.int32)
    has = steps > 0
    run_of_expert = jnp.cumsum(has.astype(jnp.int32)) - 1
    later = has[None, :] & (e_ids[None, :] > e_ids[:, None])
    next_live = jnp.min(jnp.where(later, e_ids[None, :], N_EXPERTS), axis=1)
    next_live = jnp.where(next_live == N_EXPERTS, -1, next_live)
    w_first = ((s_ids == step0) & live[:, 0]).astype(jnp.int32)
    w_par = (sel @ run_of_expert) % 2
    w_next = sel @ next_live
    i32 = lambda a: a.reshape(-1).astype(jnp.int32)
    return (step_expert, i32(n_valid), i32(src), i32(dst), w_first, i32(w_par), i32(w_next))


def _moe(h1, route, xn2, layer, w_gate, w_up, w_down, final_g, final_norm, n_first_rows):
    n = h1.shape[0]
    nt = n // SORT_TILE
    xl, pos, nch = _sort_call(xn2, route)
    nch = nch[:, 0, :N_EXPERTS].astype(jnp.int32)
    max_chunks = (2 * n) // SEG_PAD + nt * N_EXPERTS
    n_steps = max_chunks // CHUNKS_PER_STEP + N_EXPERTS + 1
    trash_row0 = nt * LOCAL_ROWS
    tables = _expert_tables(nch, n_steps + EXP_SLOTS - 1, trash_row0)
    yl = _expert_call(xl.reshape((nt + SPARE_BLOCKS) * LOCAL_ROWS, D_MODEL), trash_row0, tables,
                      layer, w_gate, w_up, w_down)
    return _combine_call(h1, route, pos, yl.reshape(nt + SPARE_BLOCKS, LOCAL_ROWS, D_MODEL), final_g, final_norm,
                         n_first_rows)


def kernel(x_prompt, x_sample, state_pool, state_conv, state_sconv, norm1_g, w_in, pool_w, pool_scale, conf_dw, conf_dw_b, conf_ln_g, conf_ln_b, sconv_w, sgu_ln_g, sgu_ln_b, sgu_ws, sgu_b, w_branch, w_out, norm2_g, router_g, router_g_b, router_e, router_e_b, w_gate, w_up, w_down, final_g):
    depth = w_in.shape[0]
    bp, seq, _ = x_prompt.shape
    bs, dseq, _ = x_sample.shape
    n_p, n_s = bp * seq, bs * dseq
    n = n_p + n_s
    assert seq % MIX_TILE == 0 and n_s % MIX_TILE_S == 0 and n_p % MIX_TILE_S == 0 and dseq == SUBLANES
    assert MIX_TILE % CHUNK == 0 and MIX_TILE_S % CHUNK == 0

    cfg_p = MixCfg(layer=0, nb=1, t=MIX_TILE, has_state=False, start_pos=0, row_off=0, v_rows=CHUNK, seq_len=seq)
    cfg_s = MixCfg(layer=0, nb=MIX_TILE_S // dseq, t=dseq, has_state=True, start_pos=PAST_LEN,
                   row_off=n_p // MIX_TILE_S, v_rows=dseq, seq_len=dseq)
    big_weights = (w_in, w_branch, w_out)

    row = lambda a: a[:, None, :]
    rep8 = lambda a: jnp.broadcast_to(a[..., None, :], a.shape[:-1] + (SUBLANES, a.shape[-1]))
    tril = jnp.tril(jnp.ones((CHUNK, CHUNK), F32))
    eye_blk = jnp.kron(jnp.eye(CHUNK // dseq, dtype=F32), jnp.ones((dseq, dseq), F32))
    cat = lambda w: jnp.concatenate([w[:, h] for h in range(SGU_HEADS)], axis=2).astype(BF16)
    wcat_p = cat(sgu_ws * tril)
    wcat_s = cat(jnp.tile(sgu_ws[:, :, :dseq, :dseq], (1, 1, CHUNK // dseq, CHUNK // dseq)) * (tril * eye_blk))
    bias_p = jnp.repeat(jnp.swapaxes(sgu_b, 1, 2), SGU_HW, axis=2)
    bias_s = jnp.tile(jnp.repeat(jnp.swapaxes(sgu_b[:, :, :dseq], 1, 2), SGU_HW, axis=2), (1, CHUNK // dseq, 1))
    lane_pad = LANES - N_EXPERTS - N_GROUPS
    w_rt32 = jnp.pad(jnp.concatenate([router_e, router_g], axis=2), ((0, 0), (0, 0), (0, lane_pad)))
    w_rt_hi = w_rt32.astype(BF16)
    w_rt = jnp.concatenate([w_rt_hi, (w_rt32 - w_rt_hi.astype(F32)).astype(BF16)], axis=2)
    b_rt = row(jnp.pad(jnp.concatenate([router_e_b, router_g_b], axis=1), ((0, 0), (0, lane_pad))))
    same_group = jnp.eye(len(POOL_WINDOWS), dtype=F32)[None, :, None, :, None]
    pool_bd = (pool_w[:, :, :, None, :] * same_group).reshape(depth, BRANCH_W, BRANCH_W).astype(BF16)

    def weights(wcat, sbias):
        return [row(norm1_g), pool_bd, row(pool_scale), rep8(conf_dw), rep8(conf_dw_b), row(conf_ln_g),
                row(conf_ln_b), rep8(sconv_w), row(sgu_ln_g), row(sgu_ln_b), wcat, sbias, row(norm2_g), w_rt, b_rt]

    weights_p, weights_s = weights(wcat_p, bias_p), weights(wcat_s, bias_s)
    final_row = final_g.reshape(1, -1)

    h_p, h_s = x_prompt.reshape(n_p, D_MODEL), x_sample.reshape(n_s, D_MODEL)
    off_p, off_s = 0, 0
    states_out = []
    for l in range(depth):
        h1, route, xn2, pool_p, conv_p, sconv_p, v_p = _mixer_call(
            cfg_p._replace(layer=l), bp, n, h_p, off_p, None, weights_p, big_weights, None)
        h1, route, xn2, pool_s, conv_s, sconv_s, v_s = _mixer_call(
            cfg_s._replace(layer=l), bs, n, h_s, off_s, (state_pool, state_conv, state_sconv),
            weights_s, big_weights, (h1, route, xn2))
        states_out.append((pool_p, conv_p, sconv_p, v_p, pool_s, conv_s, sconv_s, v_s))

        outs = _moe(h1, route, xn2, l, w_gate, w_up, w_down, final_row, l == depth - 1, n_p)
        h_p = h_s = outs[0]
        off_p, off_s = 0, n_p // MIX_TILE_S

    y_prompt = outs[0].reshape(bp, seq, D_MODEL)
    y_sample = outs[1].reshape(bs, dseq, D_MODEL)
    st = [jnp.stack([s[k] for s in states_out]) for k in range(8)]
    return (y_prompt, y_sample, st[0], st[1], st[2], st[3], st[4], st[5], st[6], st[7])
```

```python
import functools
from typing import NamedTuple

import jax
import jax.numpy as jnp
from jax import lax
from jax.experimental import pallas as pl
from jax.experimental.pallas import tpu as pltpu

F32 = jnp.float32
BF16 = jnp.bfloat16

D_MODEL = 1024
BRANCH_W = 256
N_BRANCH = 4
POOL_WINDOWS = (2, 4, 8, 16)
POOL_GW = 64
POOL_HIST = 15
CONF_WIDTH = 31
CONF_HIST = 30
SCONV_WIDTH = 3
SCONV_HIST = 2
CHUNK = 128
PAST_LEN = 16384
SGU_HEADS = 4
SGU_HW = 64
N_GROUPS = 4
EXPERTS_PER_GROUP = 8
N_EXPERTS = 32
D_EXPERT = 256
RMS_EPS = 1e-6
LN_EPS = 1e-5
IN_COLS = 6144
GATE_COL0 = 2048

COL_TILE = 256
LANES = 128
SUBLANES = 8
POOL_PAD = 32
CONF_PAD = 32
SCONV_PAD = 8
ROW_BLOCK = 32
W_IN_ROWS = 64
WINDOW_ITEMS = 4
ROUTE_GROUP_LANE0 = 32
NEG_BIG = -3.0e38

MIX_TILE = 512
MIX_TILE_S = 256
SORT_TILE = 512
SEG_PAD = 16
PERM_BLOCK = 256
N_PERM_BLOCKS = -(-(2 * SORT_TILE + N_EXPERTS * (SEG_PAD - 1)) // PERM_BLOCK)
LOCAL_ROWS = N_PERM_BLOCKS * PERM_BLOCK
EXP_TILE = 512
CHUNKS_PER_STEP = EXP_TILE // SEG_PAD
EXP_SLOTS = 3
SPARE_BLOCKS = -(-EXP_SLOTS * EXP_TILE // LOCAL_ROWS)
VMEM_LIMIT = 56 * 1024 * 1024


class MixCfg(NamedTuple):
    layer: int
    nb: int
    t: int
    has_state: bool
    start_pos: int
    row_off: int
    v_rows: int
    seq_len: int


def _rms(x, g):
    return x * lax.rsqrt(jnp.mean(x * x, axis=-1, keepdims=True) + RMS_EPS) * g


def _ln(x, g, b):
    mu = jnp.mean(x, axis=-1, keepdims=True)
    xc = x - mu
    return xc * lax.rsqrt(jnp.mean(xc * xc, axis=-1, keepdims=True) + LN_EPS) * g + b


def _sigmoid(x):
    return 0.5 * jnp.tanh(0.5 * x) + 0.5


def _gelu_tanh(x):
    return 0.5 * x * (1.0 + jnp.tanh(0.7978845608028654 * (x + 0.044715 * (x * x * x))))


def _row_blocks(nb, t):
    if t >= ROW_BLOCK:
        return [(slice(b, b + 1), t0, ROW_BLOCK) for b in range(nb) for t0 in range(0, t, ROW_BLOCK)]
    bb = ROW_BLOCK // t
    return [(slice(b0, b0 + bb), 0, t) for b0 in range(0, nb, bb)]


def _load_big_weights(layer, w_in_hbm, w_br_hbm, w_out_hbm, w_in, w_br, w_out, stage, stage_b, sem):
    n_in, n_out = IN_COLS // COL_TILE, D_MODEL // COL_TILE
    jobs = ([("in", r) for r in range(0, D_MODEL, W_IN_ROWS)] + [("out", r) for r in range(0, D_MODEL, BRANCH_W)]
            + [("br", i) for i in range(N_BRANCH)])

    def copy(job, slot):
        kind, j = job
        if kind == "in":
            return pltpu.make_async_copy(w_in_hbm.at[layer, pl.ds(j, W_IN_ROWS), :], stage.at[slot], sem.at[slot])
        src = w_br_hbm.at[layer, j] if kind == "br" else w_out_hbm.at[layer, pl.ds(j, BRANCH_W), :]
        return pltpu.make_async_copy(src, stage_b.at[slot], sem.at[slot])

    def col(buf, slot, c):
        return buf[slot, :, c * COL_TILE:(c + 1) * COL_TILE]

    copy(jobs[0], 0).start()
    for n, job in enumerate(jobs):
        slot = n % 2
        if n + 1 < len(jobs):
            copy(jobs[n + 1], 1 - slot).start()
        copy(job, slot).wait()
        kind, j = job
        if kind == "in":
            for c in range(n_in):
                tile = col(stage, slot, c)
                w_in[c, j:j + W_IN_ROWS, :] = (tile * 0.5 if c * COL_TILE >= GATE_COL0 else tile).astype(BF16)
        elif kind == "out":
            for c in range(n_out):
                w_out[c, j:j + BRANCH_W, :] = col(stage_b, slot, c).astype(BF16)
        else:
            for c in range(n_out):
                w_br[j * n_out + c] = (col(stage_b, slot, c) * 0.5).astype(BF16)


def _mixer_kernel(cfg, *refs):
    nb, t, tm = cfg.nb, cfg.t, cfg.nb * cfg.t
    refs = list(refs)
    x_ref = refs.pop(0)
    if cfg.has_state:
        pool_st, conv_st, sconv_st = (refs.pop(0).at[0] for _ in range(3))
    (n1g, pool_w, pool_sc, cdw, cdb, clg, clb, scw, slg, slb, wcat, sbias, n2g, w_rt, b_rt) = (
        r.at[0] for r in refs[:15])
    w_in_hbm, w_br_hbm, w_out_hbm = refs[15:18]
    refs = refs[18:]
    if cfg.has_state:
        refs = refs[3:]
    h1_o, route_o, xn2_o, pool_o, conv_o, sconv_o, v_o = refs[:7]
    (pool_ext, sum_a, sum_b, conv_ext, conv_sh, sc_ext, buf_a, buf_b, buf_c, gate_buf, xb_buf,
     w_in, w_br, w_out, stage, stage_b, w_sem) = refs[7:]

    first_step = pl.program_id(0) == 0
    if not cfg.has_state:
        first_step = first_step & (pl.program_id(1) == 0)
    pl.when(first_step)(functools.partial(
        _load_big_weights, cfg.layer, w_in_hbm, w_br_hbm, w_out_hbm, w_in, w_br, w_out, stage, stage_b, w_sem))

    if cfg.has_state:
        seq_pos0 = cfg.start_pos
        pool_ext[:, 0:POOL_PAD - SUBLANES, :] = jnp.zeros((nb, POOL_PAD - SUBLANES, BRANCH_W), F32)
        pool_ext[:, POOL_PAD - POOL_HIST:POOL_PAD, :] = pool_st[...]
        conv_ext[:, CONF_PAD - CONF_HIST:CONF_PAD, :] = conv_st[...]
        sc_ext[:, SCONV_PAD - SCONV_HIST:SCONV_PAD, :] = sconv_st[...]
    else:
        c = pl.program_id(1)
        seq_pos0 = cfg.start_pos + c * t

        @pl.when(c == 0)
        def _():
            pool_ext[:, 0:POOL_PAD, :] = jnp.zeros((nb, POOL_PAD, BRANCH_W), F32)
            conv_ext[:, 0:CONF_PAD, :] = jnp.zeros((nb, CONF_PAD, BRANCH_W), F32)
            sc_ext[:, 0:SCONV_PAD, :] = jnp.zeros((nb, SCONV_PAD, BRANCH_W), F32)

    x = x_ref[...]
    xb_buf[...] = _rms(x, n1g[...]).astype(BF16)

    def proj(lo, hi):
        assert lo % COL_TILE == 0 and hi == lo + COL_TILE
        return jnp.dot(xb_buf[...], w_in[lo // COL_TILE], preferred_element_type=F32)

    a_pool = proj(0, 256)
    pool_ext[:, POOL_PAD:, :] = a_pool.reshape(nb, t, BRANCH_W)
    glu = proj(256, 512) * _sigmoid(proj(512, 768))
    conv_ext[:, CONF_PAD:, :] = glu.reshape(nb, t, BRANCH_W)
    z = proj(1024, 1280) * proj(1280, 1536)
    sc_ext[:, SCONV_PAD:, :] = z.reshape(nb, t, BRANCH_W)

    pl_len = POOL_PAD + t
    sum_b[:, 8:pl_len, :] = pool_ext[:, 8:pl_len, :] + pool_ext[:, 7:pl_len - 1, :]
    sum_a[:, 16:pl_len, :] = sum_b[:, 16:pl_len, :] + sum_b[:, 14:pl_len - 2, :]
    sum_b[:, 24:pl_len, :] = sum_a[:, 24:pl_len, :] + sum_a[:, 20:pl_len - 4, :]
    for sh in range(SUBLANES):
        n_rows = t + SUBLANES * ((CONF_WIDTH - 1 - sh) // SUBLANES)
        first = CONF_PAD - CONF_HIST + sh
        conv_sh[sh, :, 0:n_rows, :] = conv_ext[:, first:first + n_rows, :]

    def window_block(bs, t0, tb):
        bb = bs.stop - bs.start
        shp = (bb, tb, BRANCH_W)
        lane = lax.broadcasted_iota(jnp.int32, shp, 2)
        pos = seq_pos0 + t0 + lax.broadcasted_iota(jnp.int32, shp, 1)
        r0 = POOL_PAD + t0
        cur = pool_ext[bs, r0:r0 + tb, :]
        s2 = cur + pool_ext[bs, r0 - 1:r0 - 1 + tb, :]
        s4 = sum_a[bs, r0:r0 + tb, :]
        s8 = sum_b[bs, r0:r0 + tb, :]
        s16 = s8 + sum_b[bs, r0 - SUBLANES:r0 - SUBLANES + tb, :]
        g1, g2, g3 = (lane < g * POOL_GW for g in (1, 2, 3))
        win_sum = jnp.where(g1, s2, jnp.where(g2, s4, jnp.where(g3, s8, s16)))
        win = jnp.where(g1, POOL_WINDOWS[0], jnp.where(g2, POOL_WINDOWS[1],
                                                         jnp.where(g3, POOL_WINDOWS[2], POOL_WINDOWS[3])))
        cnt = jnp.minimum(pos + 1, win).astype(F32)
        buf_a[bs, t0:t0 + tb, :] = win_sum / cnt - cur
        tiles = (ROW_BLOCK // SUBLANES, SUBLANES, BRANCH_W)
        acc = jnp.broadcast_to(cdb[...][None], tiles)
        for k in range(CONF_WIDTH):
            q, sh = divmod(k, SUBLANES)
            r = t0 + SUBLANES * q
            acc = acc + conv_sh[sh, bs, r:r + tb, :].reshape(tiles) * cdw[k][None]
        buf_b[bs, t0:t0 + tb, :] = acc.reshape(shp)
        base = SCONV_PAD - SCONV_HIST + t0
        acc = sc_ext[bs, base:base + tb, :].reshape(tiles) * scw[0][None]
        for k in range(1, SCONV_WIDTH):
            acc = acc + sc_ext[bs, base + k:base + k + tb, :].reshape(tiles) * scw[k][None]
        buf_c[bs, t0:t0 + tb, :] = acc.reshape(shp)

    def gate_chunk(k):
        g0 = k * COL_TILE
        gate_buf[:, g0:g0 + COL_TILE] = jnp.tanh(proj(GATE_COL0 + g0, GATE_COL0 + g0 + COL_TILE)) + 1.0

    br = {}

    def item_windows(blks):
        def run():
            for bs, t0, tb in blks:
                window_block(bs, t0, tb)
        return run

    def item_state():
        pool_o[...] = pool_ext[:, POOL_PAD + t - POOL_HIST:POOL_PAD + t, :]
        conv_o[...] = conv_ext[:, CONF_PAD + t - CONF_HIST:CONF_PAD + t, :]
        sconv_o[...] = sc_ext[:, SCONV_PAD + t - SCONV_HIST:SCONV_PAD + t, :]
        if not cfg.has_state:
            pool_ext[:, 0:POOL_PAD, :] = pool_ext[:, t:t + POOL_PAD, :]
            conv_ext[:, 0:CONF_PAD, :] = conv_ext[:, t:t + CONF_PAD, :]
            sc_ext[:, 0:SCONV_PAD, :] = sc_ext[:, t:t + SCONV_PAD, :]

    def item_a():
        pooled = buf_a[...].reshape(tm, BRANCH_W).astype(BF16)
        br["a"] = (jnp.dot(pooled, pool_w[...], preferred_element_type=F32) * pool_sc[...]).astype(BF16)

    def item_b():
        cb = _ln(buf_b[...].reshape(tm, BRANCH_W), clg[...], clb[...])
        br["b"] = (cb * _sigmoid(cb)).astype(BF16)

    def item_c():
        br["c"] = (proj(768, 1024) * buf_c[...].reshape(tm, BRANCH_W)).astype(BF16)

    def item_u():
        br["u"] = _gelu_tanh(proj(1536, 1792))

    def item_v():
        v = _ln(_gelu_tanh(proj(1792, 2048)), slg[...], slb[...])
        if cfg.has_state:
            v_o[...] = v.reshape(nb, t, BRANCH_W)
        else:
            v_o[...] = v[tm - cfg.v_rows:, :].reshape(1, cfg.v_rows, BRANCH_W)
        br["v"] = v

    def item_d():
        head = lax.broadcasted_iota(jnp.int32, (CHUNK, BRANCH_W), 1) // SGU_HW
        bias = sbias[...]
        mixed = []
        for j in range(tm // CHUNK):
            vj = br["v"][j * CHUNK:(j + 1) * CHUNK, :]
            stacked = jnp.concatenate([jnp.where(head == h, vj, 0.0) for h in range(SGU_HEADS)], axis=0)
            mixed.append(jnp.dot(wcat[...], stacked.astype(BF16), preferred_element_type=F32) + bias)
        br["d"] = (br["u"] * jnp.concatenate(mixed, axis=0)).astype(BF16)

    blocks = _row_blocks(nb, t)
    n_gate = N_BRANCH * D_MODEL // COL_TILE
    per_item = -(-len(blocks) // WINDOW_ITEMS)
    items =[item_windows(blocks[i:i + per_item]) for i in range(0, len(blocks), per_item)]
    items += [item_state, item_c, item_b, item_u, item_v, item_a, item_d]
    assert len(items) <= n_gate
    for k in range(n_gate):
        if k < len(items):
            items[k]()
        gate_chunk(k)

    n_ct = D_MODEL // COL_TILE
    brs = [br["a"], br["b"], br["c"], br["d"]]
    for c in range(n_ct):
        part = None
        for i in range(N_BRANCH):
            g0 = i * D_MODEL + c * COL_TILE
            term = gate_buf[:, g0:g0 + COL_TILE] * jnp.dot(
                brs[i], w_br[i * n_ct + c], preferred_element_type=F32)
            part = term if part is None else part + term
        xb_buf[:, c * COL_TILE:(c + 1) * COL_TILE] = part.astype(BF16)
    for c in range(n_ct):
        cs = slice(c * COL_TILE, (c + 1) * COL_TILE)
        h1_o[:, cs] = x_ref[:, cs] + jnp.dot(xb_buf[...], w_out[c], preferred_element_type=F32)

    xn2 = _rms(h1_o[...], n2g[...])
    x_hi = xn2.astype(BF16)
    xn2_o[...] = x_hi
    x_lo = (xn2 - x_hi.astype(F32)).astype(BF16)
    hi_both = jnp.dot(x_hi, w_rt[...], preferred_element_type=F32)
    lo_hi = jnp.dot(x_lo, w_rt[:, 0:LANES], preferred_element_type=F32)
    logits = (hi_both[:, 0:LANES] + (lo_hi + hi_both[:, LANES:2 * LANES])) + b_rt[...]
    lane = lax.broadcasted_iota(jnp.int32, (tm, LANES), 1)
    is_g = (lane >= ROUTE_GROUP_LANE0) & (lane < ROUTE_GROUP_LANE0 + N_GROUPS)
    glog = jnp.where(is_g, logits, NEG_BIG)
    gmax = jnp.max(glog, axis=-1, keepdims=True)
    lane_f = lane.astype(F32)
    no_lane = float(4 * LANES)
    gsel = jnp.min(jnp.where(glog == gmax, lane_f, no_lane), axis=-1, keepdims=True) - ROUTE_GROUP_LANE0
    pg = 1.0 / jnp.sum(jnp.where(is_g, jnp.exp(glog - gmax), 0.0), axis=-1, keepdims=True)
    in_grp = (lane < N_EXPERTS) & ((lane // EXPERTS_PER_GROUP) == gsel.astype(jnp.int32))
    el = jnp.where(in_grp, logits, NEG_BIG)
    m1 = jnp.max(el, axis=-1, keepdims=True)
    i1 = jnp.min(jnp.where(in_grp & (el == m1), lane_f, no_lane), axis=-1, keepdims=True)
    rest = in_grp & (lane_f != i1)
    el2 = jnp.where(rest, logits, NEG_BIG)
    m2 = jnp.max(el2, axis=-1, keepdims=True)
    i2 = jnp.min(jnp.where(rest & (el2 == m2), lane_f, no_lane), axis=-1, keepdims=True)
    e21 = jnp.exp(m2 - m1)
    w1 = pg / (1.0 + e21)
    w2 = pg * e21 / (1.0 + e21)
    route_o[...] = jnp.where(lane == 0, i1, jnp.where(lane == 1, i2,
                             jnp.where(lane == 2, w1, jnp.where(lane == 3, w2, 0.0))))


def _layer_spec(shape, layer):
    nd = len(shape)
    return pl.BlockSpec((1,) + shape[1:], lambda *_: (layer,) + (0,) * (nd - 1), pipeline_mode=pl.Buffered(1))


def _mixer_call(cfg, n_seq, n_rows_total, x2d, x_tile_off, states, weights, big_weights, aliased):
    nb, t = cfg.nb, cfg.t
    tm = nb * t
    if cfg.has_state:
        grid = (n_seq // nb,)
        tok = lambda i: (x_tile_off + i, 0)
        out_tok = lambda i: (cfg.row_off + i, 0)
        seq3 = lambda i: (i, 0, 0)
    else:
        grid = (n_seq, cfg.seq_len // t)
        nt = grid[1]
        tok = lambda b, c: (x_tile_off + b * nt + c, 0)
        out_tok = lambda b, c: (cfg.row_off + b * nt + c, 0)
        seq3 = lambda b, c: (b, 0, 0)

    in_specs = [pl.BlockSpec((tm, D_MODEL), tok)]
    args = [x2d]
    if cfg.has_state:
        for s in states:
            in_specs.append(pl.BlockSpec((1, nb) + s.shape[2:], lambda i: (cfg.layer, i, 0, 0)))
            args.append(s)
    for w in weights:
        in_specs.append(_layer_spec(w.shape, cfg.layer))
        args.append(w)
    for w in big_weights:
        in_specs.append(pl.BlockSpec(memory_space=pl.ANY))
        args.append(w)
    io_alias = {}
    if cfg.has_state:
        for k, a in enumerate(aliased):
            in_specs.append(pl.BlockSpec(memory_space=pl.ANY))
            io_alias[len(args)] = k
            args.append(a)

    out_shape = [
        jax.ShapeDtypeStruct((n_rows_total, D_MODEL), F32),
        jax.ShapeDtypeStruct((n_rows_total, LANES), F32),
        jax.ShapeDtypeStruct((n_rows_total, D_MODEL), BF16),
        jax.ShapeDtypeStruct((n_seq, POOL_HIST, BRANCH_W), F32),
        jax.ShapeDtypeStruct((n_seq, CONF_HIST, BRANCH_W), F32),
        jax.ShapeDtypeStruct((n_seq, SCONV_HIST, BRANCH_W), F32),
        jax.ShapeDtypeStruct((n_seq, cfg.v_rows, BRANCH_W), F32),
    ]
    out_specs = [
        pl.BlockSpec((tm, D_MODEL), out_tok),
        pl.BlockSpec((tm, LANES), out_tok),
        pl.BlockSpec((tm, D_MODEL), out_tok),
        pl.BlockSpec((nb, POOL_HIST, BRANCH_W), seq3),
        pl.BlockSpec((nb, CONF_HIST, BRANCH_W), seq3),
        pl.BlockSpec((nb, SCONV_HIST, BRANCH_W), seq3),
        pl.BlockSpec((nb, cfg.v_rows, BRANCH_W), seq3),
    ]
    scratch = [
        pltpu.VMEM((nb, POOL_PAD + t, BRANCH_W), F32),
        pltpu.VMEM((nb, POOL_PAD + t, BRANCH_W), F32),
        pltpu.VMEM((nb, POOL_PAD + t, BRANCH_W), F32),
        pltpu.VMEM((nb, CONF_PAD + t, BRANCH_W), F32),
        pltpu.VMEM((SUBLANES, nb, t + CONF_PAD - SUBLANES, BRANCH_W), F32),
        pltpu.VMEM((nb, SCONV_PAD + t, BRANCH_W), F32),
        pltpu.VMEM((nb, t, BRANCH_W), F32),
        pltpu.VMEM((nb, t, BRANCH_W), F32),
        pltpu.VMEM((nb, t, BRANCH_W), F32),
        pltpu.VMEM((tm, N_BRANCH * D_MODEL), F32),
        pltpu.VMEM((tm, D_MODEL), BF16),
        pltpu.VMEM((IN_COLS // COL_TILE, D_MODEL, COL_TILE), BF16),
        pltpu.VMEM((N_BRANCH * D_MODEL // COL_TILE, BRANCH_W, COL_TILE), BF16),
        pltpu.VMEM((D_MODEL // COL_TILE, D_MODEL, COL_TILE), BF16),
        pltpu.VMEM((2, W_IN_ROWS, IN_COLS), F32),
        pltpu.VMEM((2, BRANCH_W, D_MODEL), F32),
        pltpu.SemaphoreType.DMA((2,)),
    ]
    return pl.pallas_call(
        functools.partial(_mixer_kernel, cfg),
        grid=grid,
        in_specs=in_specs,
        out_specs=out_specs,
        out_shape=out_shape,
        scratch_shapes=scratch,
        input_output_aliases=io_alias,
        compiler_params=pltpu.CompilerParams(
            dimension_semantics=("arbitrary",) * len(grid), vmem_limit_bytes=VMEM_LIMIT),
        name="mixer_sample" if cfg.has_state else "mixer_prompt",
    )(*args)


def _sort_kernel(xn_ref, route_ref, xl_o, pos_o, nch_o):
    t = SORT_TILE
    xn = xn_ref[...]
    r = route_ref[...]
    lane_i = lax.broadcasted_iota(jnp.int32, (t, LANES), 1)
    lane = lane_i.astype(F32)
    hit1 = lane == r[:, 0:1]
    hit2 = lane == r[:, 1:2]
    onehot = jnp.where(hit1 | hit2, 1.0, 0.0)
    row = lax.broadcasted_iota(jnp.int32, (t, t), 0)
    col = lax.broadcasted_iota(jnp.int32, (t, t), 1)
    strict_lower = jnp.where(col < row, 1.0, 0.0).astype(BF16)
    before = jnp.dot(strict_lower, onehot.astype(BF16), preferred_element_type=F32)
    cnt = jnp.sum(onehot, axis=0, keepdims=True)
    nch = jnp.floor((cnt + (SEG_PAD - 1)) * (1.0 / SEG_PAD))
    er = lax.broadcasted_iota(jnp.int32, (LANES, LANES), 0)
    ec = lax.broadcasted_iota(jnp.int32, (LANES, LANES), 1)
    strict_upper = jnp.where(er < ec, 1.0, 0.0).astype(BF16)
    nch8 = jnp.broadcast_to(nch, (SUBLANES, LANES)).astype(BF16)
    seg0 = jnp.dot(nch8, strict_upper, preferred_element_type=F32)[0:1, :] * SEG_PAD
    where = seg0 + before
    pos0 = jnp.sum(jnp.where(hit1, where, 0.0), axis=-1, keepdims=True)
    pos1 = jnp.sum(jnp.where(hit2, where, 0.0), axis=-1, keepdims=True)
    pos = jnp.where(lane_i == 0, pos0, jnp.where(lane_i == 1, pos1, -1.0))
    pos_o[...] = pos
    nch_o[...] = jnp.broadcast_to(nch, (1, SUBLANES, LANES))
    pos_t = jnp.transpose(pos)
    p0 = pos_t[0:1, :]
    p1 = pos_t[1:2, :]
    used_rows = jnp.sum(nch) * SEG_PAD

    def perm_block(blk):
        dst = (lax.broadcasted_iota(jnp.int32, (PERM_BLOCK, t), 0) + blk * PERM_BLOCK).astype(F32)
        perm = jnp.where((dst == p0) | (dst == p1), 1.0, 0.0).astype(BF16)
        xl_o[0, blk * PERM_BLOCK:(blk + 1) * PERM_BLOCK, :] = jnp.dot(
            perm, xn, preferred_element_type=F32).astype(BF16)

    for blk in range(N_PERM_BLOCKS - 1):
        perm_block(blk)
    pl.when(used_rows > (N_PERM_BLOCKS - 1) * PERM_BLOCK)(functools.partial(perm_block, N_PERM_BLOCKS - 1))


def _sort_call(xn2, route):
    n = xn2.shape[0]
    nt = n // SORT_TILE
    return pl.pallas_call(
        _sort_kernel,
        grid=(nt,),
        in_specs=[pl.BlockSpec((SORT_TILE, D_MODEL), lambda i: (i, 0)),
                  pl.BlockSpec((SORT_TILE, LANES), lambda i: (i, 0))],
        out_specs=[pl.BlockSpec((1, LOCAL_ROWS, D_MODEL), lambda i: (i, 0, 0)),
                   pl.BlockSpec((SORT_TILE, LANES), lambda i: (i, 0)),
                   pl.BlockSpec((1, SUBLANES, LANES), lambda i: (i, 0, 0))],
        out_shape=[jax.ShapeDtypeStruct((nt + SPARE_BLOCKS, LOCAL_ROWS, D_MODEL), BF16),
                   jax.ShapeDtypeStruct((n, LANES), F32),
                   jax.ShapeDtypeStruct((nt, SUBLANES, LANES), F32)],
        compiler_params=pltpu.CompilerParams(dimension_semantics=("arbitrary",), vmem_limit_bytes=VMEM_LIMIT),
        name="moe_sort",
    )(xn2, route)


def _chunk_copy(src, src_row, dst, dst_row, sem):
    return pltpu.make_async_copy(src.at[pl.ds(src_row, SEG_PAD), :], dst.at[pl.ds(dst_row, SEG_PAD), :], sem)


def _expert_kernel(layer, trash_row0, se_ref, nv_ref, src_ref, dst_ref, wfirst_ref, wpar_ref, wnext_ref,
                   xl_in, wg_hbm, wu_hbm, wd_hbm, xl_io, *scratch):
    del xl_in
    ns = EXP_SLOTS
    xbuf, ybuf = scratch[:ns], scratch[ns:2 * ns]
    sem_in, sem_out, wg, wu, wd, w_sem = scratch[2 * ns:]
    s = pl.program_id(0)
    nv = nv_ref[0]

    def weight_copies(expert, par):
        return [pltpu.make_async_copy(hbm.at[layer, expert], buf.at[par], w_sem.at[par])
                for hbm, buf in ((wg_hbm, wg), (wu_hbm, wu), (wd_hbm, wd))]

    def gather(step, sl):
        return [_chunk_copy(xl_io, pl.multiple_of(src_ref[step * CHUNKS_PER_STEP + j], SEG_PAD),
                            xbuf[sl], j * SEG_PAD, sem_in.at[sl]) for j in range(CHUNKS_PER_STEP)]

    def write_back(step, sl):
        return [_chunk_copy(ybuf[sl], j * SEG_PAD, xl_io,
                            pl.multiple_of(dst_ref[step * CHUNKS_PER_STEP + j], SEG_PAD), sem_out.at[sl])
                for j in range(CHUNKS_PER_STEP)]

    def spare_write(sl):
        return [_chunk_copy(ybuf[sl], j * SEG_PAD, xl_io, trash_row0 + sl * EXP_TILE + j * SEG_PAD,
                            sem_out.at[sl]) for j in range(CHUNKS_PER_STEP)]

    def start_chunks(copies):
        for j, c in enumerate(copies):
            c.start(priority=j % 2)

    @pl.when(s == 0)
    def _():
        for c in weight_copies(se_ref[0], 0):
            c.start()
        for ahead in range(ns - 1):
            start_chunks(gather(ahead, ahead))
        for sl in range(ns):
            ybuf[sl][...] = jnp.zeros_like(ybuf[sl])
            start_chunks(spare_write(sl))

    def live_step(sl):
        par = wpar_ref[s]

        @pl.when(wfirst_ref[s] == 1)
        def _():
            for c in weight_copies(se_ref[s], par):
                c.wait()

            @pl.when(wnext_ref[s] >= 0)
            def _():
                for c in weight_copies(wnext_ref[s], 1 - par):
                    c.start()

        start_chunks(gather(s + ns - 1, (sl + ns - 1) % ns))
        for c in gather(s, sl):
            c.wait()
        xb = xbuf[sl][...]
        a = jnp.dot(xb, wg[par].astype(BF16), preferred_element_type=F32)
        b = jnp.dot(xb, wu[par].astype(BF16), preferred_element_type=F32)
        hh = (a * _sigmoid(a) * b).astype(BF16)
        y = jnp.dot(hh, wd[par].astype(BF16), preferred_element_type=F32).astype(BF16)
        for c in write_back(s, sl):
            c.wait()
        ybuf[sl][...] = y
        start_chunks(write_back(s, sl))

    def drain_step(sl):
        for ahead in range(ns - 1):
            for c in gather(s + ahead, (sl + ahead) % ns):
                c.wait()
        for every in range(ns):
            for c in spare_write(every):
                c.wait()

    for sl in range(ns):
        pl.when((s < nv) & (s % ns == sl))(functools.partial(live_step, sl))
        pl.when((s == nv) & (s % ns == sl))(functools.partial(drain_step, sl))


def _expert_call(xl, trash_row0, tables, layer, w_gate, w_up, w_down):
    n_steps = tables[0].shape[0] - (EXP_SLOTS - 1)
    grid_spec = pltpu.PrefetchScalarGridSpec(
        num_scalar_prefetch=len(tables),
        grid=(n_steps,),
        in_specs=[pl.BlockSpec(memory_space=pl.ANY) for _ in range(4)],
        out_specs=pl.BlockSpec(memory_space=pl.ANY),
        scratch_shapes=[pltpu.VMEM((EXP_TILE, D_MODEL), BF16) for _ in range(2 * EXP_SLOTS)] + [
                        pltpu.SemaphoreType.DMA((EXP_SLOTS,)), pltpu.SemaphoreType.DMA((EXP_SLOTS,)),
                        pltpu.VMEM((2, D_MODEL, D_EXPERT), F32), pltpu.VMEM((2, D_MODEL, D_EXPERT), F32),
                        pltpu.VMEM((2, D_EXPERT, D_MODEL), F32), pltpu.SemaphoreType.DMA((2,))],
    )
    return pl.pallas_call(
        functools.partial(_expert_kernel, layer, trash_row0),
        grid_spec=grid_spec,
        out_shape=jax.ShapeDtypeStruct(xl.shape, BF16),
        input_output_aliases={len(tables): 0},
        compiler_params=pltpu.CompilerParams(dimension_semantics=("arbitrary",), has_side_effects=True),
        name="moe_experts",
    )(*tables, xl, w_gate, w_up, w_down)


def _combine_kernel(final_norm, n_first, h1_ref, route_ref, pos_ref, fg, yl_ref, *outs):
    t = SORT_TILE
    r = route_ref[...]
    p = pos_ref[...]
    w1, w2 = r[:, 2:3], r[:, 3:4]
    pos0, pos1 = p[:, 0:1], p[:, 1:2]
    *outs, acc_ref = outs

    def gathered(blk):
        src = (lax.broadcasted_iota(jnp.int32, (t, PERM_BLOCK), 1) + blk * PERM_BLOCK).astype(F32)
        pw = jnp.where(src == pos0, w1, 0.0) + jnp.where(src == pos1, w2, 0.0)
        return jnp.dot(pw.astype(BF16), yl_ref[0, blk * PERM_BLOCK:(blk + 1) * PERM_BLOCK, :],
                       preferred_element_type=F32)

    acc = h1_ref[...]
    for blk in range(N_PERM_BLOCKS - 1):
        acc = acc + gathered(blk)
    acc_ref[...] = acc

    @pl.when(jnp.max(p) >= (N_PERM_BLOCKS - 1) * PERM_BLOCK)
    def _():
        acc_ref[...] += gathered(N_PERM_BLOCKS - 1)

    if not final_norm:
        outs[0][...] = acc_ref[...]
        return
    y = _rms(acc_ref[...], fg[...])
    i = pl.program_id(0)

    @pl.when(i < n_first)
    def _():
        outs[0][...] = y

    @pl.when(i >= n_first)
    def _():
        outs[1][...] = y


def _combine_call(h1, route, pos, yl, final_g, final_norm, n_first_rows):
    n = h1.shape[0]
    nt = n // SORT_TILE
    n_first = n_first_rows // SORT_TILE
    tok = lambda i: (i, 0)
    if final_norm:
        out_shape = [jax.ShapeDtypeStruct((n_first_rows, D_MODEL), F32),
                     jax.ShapeDtypeStruct((n - n_first_rows, D_MODEL), F32)]
        out_specs = [pl.BlockSpec((SORT_TILE, D_MODEL), lambda i: (jnp.minimum(i, n_first - 1), 0)),
                     pl.BlockSpec((SORT_TILE, D_MODEL), lambda i: (jnp.maximum(i - n_first, 0), 0))]
    else:
        out_shape = [jax.ShapeDtypeStruct((n, D_MODEL), F32)]
        out_specs = [pl.BlockSpec((SORT_TILE, D_MODEL), tok)]
    return pl.pallas_call(
        functools.partial(_combine_kernel, final_norm, n_first),
        grid=(nt,),
        in_specs=[pl.BlockSpec((SORT_TILE, D_MODEL), tok),
                  pl.BlockSpec((SORT_TILE, LANES), tok),
                  pl.BlockSpec((SORT_TILE, LANES), tok),
                  pl.BlockSpec((1, D_MODEL), lambda i: (0, 0)),
                  pl.BlockSpec((1, LOCAL_ROWS, D_MODEL), lambda i: (i, 0, 0))],
        out_specs=out_specs,
        out_shape=out_shape,
        scratch_shapes=[pltpu.VMEM((SORT_TILE, D_MODEL), F32)],
        compiler_params=pltpu.CompilerParams(dimension_semantics=("arbitrary",), vmem_limit_bytes=VMEM_LIMIT),
        name="moe_combine",
    )(h1, route, pos, final_g, yl)


def _expert_tables(nch, n_steps, trash_row0):
    nt = nch.shape[0]
    cps = CHUNKS_PER_STEP
    seg_row0 = (jnp.cumsum(nch, axis=1) - nch) * SEG_PAD
    first = jnp.cumsum(nch, axis=0) - nch
    tot = jnp.sum(nch, axis=0)
    steps = (tot + cps - 1) // cps
    step_end = jnp.cumsum(steps)
    n_valid = step_end[-1:]
    s_ids = jnp.arange(n_steps, dtype=jnp.int32)
    step_expert = jnp.minimum(jnp.sum((step_end[None, :] <= s_ids[:, None]).astype(jnp.int32), axis=1),
                              N_EXPERTS - 1)
    sel = (step_expert[:, None] == jnp.arange(N_EXPERTS, dtype=jnp.int32)[None, :]).astype(jnp.int32)
    step0 = sel @ (step_end - steps)
    tot_s = sel @ tot
    first_s = sel @ first.T
    row0_s = sel @ seg_row0.T
    k = (s_ids - step0)[:, None] * cps + jnp.arange(cps, dtype=jnp.int32)[None, :]
    ok = (k < tot_s[:, None]) & (s_ids < n_valid[0])[:, None]
    tile = jnp.sum((first_s[:, None, :] <= k[:, :, None]).astype(jnp.int32), axis=2) - 1
    tsel = (tile[:, :, None] == jnp.arange(nt, dtype=jnp.int32)[None, None, :]).astype(jnp.int32)
    first_k = jnp.sum(tsel * first_s[:, None, :], axis=2)
    row0_k = jnp.sum(tsel * row0_s[:, None, :], axis=2)
    row = tile * LOCAL_ROWS + row0_k + (k - first_k) * SEG_PAD
    live = (s_ids < n_valid[0])[:, None]
    src = jnp.where(live, jnp.where(ok, row, row[:, 0:1]), 0)
    spare = trash_row0 + (s_ids % EXP_SLOTS)[:, None] * EXP_TILE + jnp.arange(cps, dtype=jnp.int32)[None, :] * SEG_PAD
    dst = jnp.where(ok, row, spare)
    e_ids = jnp.arange(N_EXPERTS, dtype=jnp.int32)
    has = steps > 0
    run_of_expert = jnp.cumsum(has.astype(jnp.int32)) - 1
    later = has[None, :] & (e_ids[None, :] > e_ids[:, None])
    next_live = jnp.min(jnp.where(later, e_ids[None, :], N_EXPERTS), axis=1)
    next_live = jnp.where(next_live == N_EXPERTS, -1, next_live)
    w_first = ((s_ids == step0) & live[:, 0]).astype(jnp.int32)
    w_par = (sel @ run_of_expert) % 2
    w_next = sel @ next_live
    i32 = lambda a: a.reshape(-1).astype(jnp.int32)
    return (step_expert, i32(n_valid), i32(src), i32(dst), w_first, i32(w_par), i32(w_next))


def _moe(h1, route, xn2, layer, w_gate, w_up, w_down, final_g, final_norm, n_first_rows):
    n = h1.shape[0]
    nt = n // SORT_TILE
    xl, pos, nch = _sort_call(xn2, route)
    nch = nch[:, 0, :N_EXPERTS].astype(jnp.int32)
    max_chunks = (2 * n) // SEG_PAD + nt * N_EXPERTS
    n_steps = max_chunks // CHUNKS_PER_STEP + N_EXPERTS + 1
    trash_row0 = nt * LOCAL_ROWS
    tables = _expert_tables(nch, n_steps + EXP_SLOTS - 1, trash_row0)
    yl = _expert_call(xl.reshape((nt + SPARE_BLOCKS) * LOCAL_ROWS, D_MODEL), trash_row0, tables,
                      layer, w_gate, w_up, w_down)
    return _combine_call(h1, route, pos, yl.reshape(nt + SPARE_BLOCKS, LOCAL_ROWS, D_MODEL), final_g, final_norm,
                         n_first_rows)


def kernel(x_prompt, x_sample, state_pool, state_conv, state_sconv, norm1_g, w_in, pool_w, pool_scale, conf_dw, conf_dw_b, conf_ln_g, conf_ln_b, sconv_w, sgu_ln_g, sgu_ln_b, sgu_ws, sgu_b, w_branch, w_out, norm2_g, router_g, router_g_b, router_e, router_e_b, w_gate, w_up, w_down, final_g):
    depth = w_in.shape[0]
    bp, seq, _ = x_prompt.shape
    bs, dseq, _ = x_sample.shape
    n_p, n_s = bp * seq, bs * dseq
    n = n_p + n_s
    assert seq % MIX_TILE == 0 and n_s % MIX_TILE_S == 0 and n_p % MIX_TILE_S == 0 and dseq == SUBLANES
    assert MIX_TILE % CHUNK == 0 and MIX_TILE_S % CHUNK == 0

    cfg_p = MixCfg(layer=0, nb=1, t=MIX_TILE, has_state=False, start_pos=0, row_off=0, v_rows=CHUNK, seq_len=seq)
    cfg_s = MixCfg(layer=0, nb=MIX_TILE_S // dseq, t=dseq, has_state=True, start_pos=PAST_LEN,
                   row_off=n_p // MIX_TILE_S, v_rows=dseq, seq_len=dseq)
    big_weights = (w_in, w_branch, w_out)

    row = lambda a: a[:, None, :]
    rep8 = lambda a: jnp.broadcast_to(a[..., None, :], a.shape[:-1] + (SUBLANES, a.shape[-1]))
    tril = jnp.tril(jnp.ones((CHUNK, CHUNK), F32))
    eye_blk = jnp.kron(jnp.eye(CHUNK // dseq, dtype=F32), jnp.ones((dseq, dseq), F32))
    cat = lambda w: jnp.concatenate([w[:, h] for h in range(SGU_HEADS)], axis=2).astype(BF16)
    wcat_p = cat(sgu_ws * tril)
    wcat_s = cat(jnp.tile(sgu_ws[:, :, :dseq, :dseq], (1, 1, CHUNK // dseq, CHUNK // dseq)) * (tril * eye_blk))
    bias_p = jnp.repeat(jnp.swapaxes(sgu_b, 1, 2), SGU_HW, axis=2)
    bias_s = jnp.tile(jnp.repeat(jnp.swapaxes(sgu_b[:, :, :dseq], 1, 2), SGU_HW, axis=2), (1, CHUNK // dseq, 1))
    lane_pad = LANES - N_EXPERTS - N_GROUPS
    w_rt32 = jnp.pad(jnp.concatenate([router_e, router_g], axis=2), ((0, 0), (0, 0), (0, lane_pad)))
    w_rt_hi = w_rt32.astype(BF16)
    w_rt = jnp.concatenate([w_rt_hi, (w_rt32 - w_rt_hi.astype(F32)).astype(BF16)], axis=2)
    b_rt = row(jnp.pad(jnp.concatenate([router_e_b, router_g_b], axis=1), ((0, 0), (0, lane_pad))))
    same_group = jnp.eye(len(POOL_WINDOWS), dtype=F32)[None, :, None, :, None]
    pool_bd = (pool_w[:, :, :, None, :] * same_group).reshape(depth, BRANCH_W, BRANCH_W).astype(BF16)

    def weights(wcat, sbias):
        return [row(norm1_g), pool_bd, row(pool_scale), rep8(conf_dw), rep8(conf_dw_b), row(conf_ln_g),
                row(conf_ln_b), rep8(sconv_w), row(sgu_ln_g), row(sgu_ln_b), wcat, sbias, row(norm2_g), w_rt, b_rt]

    weights_p, weights_s = weights(wcat_p, bias_p), weights(wcat_s, bias_s)
    final_row = final_g.reshape(1, -1)

    h_p, h_s = x_prompt.reshape(n_p, D_MODEL), x_sample.reshape(n_s, D_MODEL)
    off_p, off_s = 0, 0
    states_out = []
    for l in range(depth):
        h1, route, xn2, pool_p, conv_p, sconv_p, v_p = _mixer_call(
            cfg_p._replace(layer=l), bp, n, h_p, off_p, None, weights_p, big_weights, None)
        h1, route, xn2, pool_s, conv_s, sconv_s, v_s = _mixer_call(
            cfg_s._replace(layer=l), bs, n, h_s, off_s, (state_pool, state_conv, state_sconv),
            weights_s, big_weights, (h1, route, xn2))
        states_out.append((pool_p, conv_p, sconv_p, v_p, pool_s, conv_s, sconv_s, v_s))

        outs = _moe(h1, route, xn2, l, w_gate, w_up, w_down, final_row, l == depth - 1, n_p)
        h_p = h_s = outs[0]
        off_p, off_s = 0, n_p // MIX_TILE_S

    y_prompt = outs[0].reshape(bp, seq, D_MODEL)
    y_sample = outs[1].reshape(bs, dseq, D_MODEL)
    st = [jnp.stack([s[k] for s in states_out]) for k in range(8)]
    return (y_prompt, y_sample, st[0], st[1], st[2], st[3], st[4], st[5], st[6], st[7])
```

```python
import functools
from typing import NamedTuple

import jax
import jax.numpy as jnp
from jax import lax
from jax.experimental import pallas as pl
from jax.experimental.pallas import tpu as pltpu

F32 = jnp.float32
BF16 = jnp.bfloat16

D_MODEL = 1024
BRANCH_W = 256
N_BRANCH = 4
POOL_WINDOWS = (2, 4, 8, 16)
POOL_GW = 64
POOL_HIST = 15
CONF_WIDTH = 31
CONF_HIST = 30
SCONV_WIDTH = 3
SCONV_HIST = 2
CHUNK = 128
PAST_LEN = 16384
SGU_HEADS = 4
SGU_HW = 64
N_GROUPS = 4
EXPERTS_PER_GROUP = 8
N_EXPERTS = 32
D_EXPERT = 256
RMS_EPS = 1e-6
LN_EPS = 1e-5
IN_COLS = 6144
GATE_COL0 = 2048

COL_TILE = 256
LANES = 128
SUBLANES = 8
POOL_PAD = 32
CONF_PAD = 32
SCONV_PAD = 8
ROW_BLOCK = 32
W_IN_ROWS = 64
WINDOW_ITEMS = 4
ROUTE_GROUP_LANE0 = 32
NEG_BIG = -3.0e38

MIX_TILE = 512
MIX_TILE_S = 256
SORT_TILE = 512
SEG_PAD = 16
PERM_BLOCK = 256
N_PERM_BLOCKS = -(-(2 * SORT_TILE + N_EXPERTS * (SEG_PAD - 1)) // PERM_BLOCK)
LOCAL_ROWS = N_PERM_BLOCKS * PERM_BLOCK
EXP_TILE = 512
CHUNKS_PER_STEP = EXP_TILE // SEG_PAD
EXP_SLOTS = 3
SPARE_BLOCKS = -(-EXP_SLOTS * EXP_TILE // LOCAL_ROWS)
VMEM_LIMIT = 56 * 1024 * 1024


class MixCfg(NamedTuple):
    layer: int
    nb: int
    t: int
    has_state: bool
    start_pos: int
    row_off: int
    v_rows: int
    seq_len: int


def _rms(x, g):
    return x * lax.rsqrt(jnp.mean(x * x, axis=-1, keepdims=True) + RMS_EPS) * g


def _ln(x, g, b):
    mu = jnp.mean(x, axis=-1, keepdims=True)
    xc = x - mu
    return xc * lax.rsqrt(jnp.mean(xc * xc, axis=-1, keepdims=True) + LN_EPS) * g + b


def _sigmoid(x):
    return 0.5 * jnp.tanh(0.5 * x) + 0.5


def _gelu_tanh(x):
    return 0.5 * x * (1.0 + jnp.tanh(0.7978845608028654 * (x + 0.044715 * (x * x * x))))


def _row_blocks(nb, t):
    if t >= ROW_BLOCK:
        return [(slice(b, b + 1), t0, ROW_BLOCK) for b in range(nb) for t0 in range(0, t, ROW_BLOCK)]
    bb = ROW_BLOCK // t
    return [(slice(b0, b0 + bb), 0, t) for b0 in range(0, nb, bb)]


def _load_big_weights(layer, w_in_hbm, w_br_hbm, w_out_hbm, w_in, w_br, w_out, stage, stage_b, sem):
    n_in, n_out = IN_COLS // COL_TILE, D_MODEL // COL_TILE
    jobs = ([("in", r) for r in range(0, D_MODEL, W_IN_ROWS)] + [("out", r) for r in range(0, D_MODEL, BRANCH_W)]
            + [("br", i) for i in range(N_BRANCH)])

    def copy(job, slot):
        kind, j = job
        if kind == "in":
            return pltpu.make_async_copy(w_in_hbm.at[layer, pl.ds(j, W_IN_ROWS), :], stage.at[slot], sem.at[slot])
        src = w_br_hbm.at[layer, j] if kind == "br" else w_out_hbm.at[layer, pl.ds(j, BRANCH_W), :]
        return pltpu.make_async_copy(src, stage_b.at[slot], sem.at[slot])

    def col(buf, slot, c):
        return buf[slot, :, c * COL_TILE:(c + 1) * COL_TILE]

    copy(jobs[0], 0).start()
    for n, job in enumerate(jobs):
        slot = n % 2
        if n + 1 < len(jobs):
            copy(jobs[n + 1], 1 - slot).start()
        copy(job, slot).wait()
        kind, j = job
        if kind == "in":
            for c in range(n_in):
                tile = col(stage, slot, c)
                w_in[c, j:j + W_IN_ROWS, :] = (tile * 0.5 if c * COL_TILE >= GATE_COL0 else tile).astype(BF16)
        elif kind == "out":
            for c in range(n_out):
                w_out[c, j:j + BRANCH_W, :] = col(stage_b, slot, c).astype(BF16)
        else:
            for c in range(n_out):
                w_br[j * n_out + c] = (col(stage_b, slot, c) * 0.5).astype(BF16)


def _mixer_kernel(cfg, *refs):
    nb, t, tm = cfg.nb, cfg.t, cfg.nb * cfg.t
    refs = list(refs)
    x_ref = refs.pop(0)
    if cfg.has_state:
        pool_st, conv_st, sconv_st = (refs.pop(0).at[0] for _ in range(3))
    (n1g, pool_w, pool_sc, cdw, cdb, clg, clb, scw, slg, slb, wcat, sbias, n2g, w_rt, b_rt) = (
        r.at[0] for r in refs[:15])
    w_in_hbm, w_br_hbm, w_out_hbm = refs[15:18]
    refs = refs[18:]
    if cfg.has_state:
        refs = refs[3:]
    h1_o, route_o, xn2_o, pool_o, conv_o, sconv_o, v_o = refs[:7]
    (pool_ext, sum_a, sum_b, conv_ext, conv_sh, sc_ext, buf_a, buf_b, buf_c, gate_buf, xb_buf,
     w_in, w_br, w_out, stage, stage_b, w_sem) = refs[7:]

    first_step = pl.program_id(0) == 0
    if not cfg.has_state:
        first_step = first_step & (pl.program_id(1) == 0)
    pl.when(first_step)(functools.partial(
        _load_big_weights, cfg.layer, w_in_hbm, w_br_hbm, w_out_hbm, w_in, w_br, w_out, stage, stage_b, w_sem))

    if cfg.has_state:
        seq_pos0 = cfg.start_pos
        pool_ext[:, 0:POOL_PAD - SUBLANES, :] = jnp.zeros((nb, POOL_PAD - SUBLANES, BRANCH_W), F32)
        pool_ext[:, POOL_PAD - POOL_HIST:POOL_PAD, :] = pool_st[...]
        conv_ext[:, CONF_PAD - CONF_HIST:CONF_PAD, :] = conv_st[...]
        sc_ext[:, SCONV_PAD - SCONV_HIST:SCONV_PAD, :] = sconv_st[...]
    else:
        c = pl.program_id(1)
        seq_pos0 = cfg.start_pos + c * t

        @pl.when(c == 0)
        def _():
            pool_ext[:, 0:POOL_PAD, :] = jnp.zeros((nb, POOL_PAD, BRANCH_W), F32)
            conv_ext[:, 0:CONF_PAD, :] = jnp.zeros((nb, CONF_PAD, BRANCH_W), F32)
            sc_ext[:, 0:SCONV_PAD, :] = jnp.zeros((nb, SCONV_PAD, BRANCH_W), F32)

    x = x_ref[...]
    xb_buf[...] = _rms(x, n1g[...]).astype(BF16)

    def proj(lo, hi):
        assert lo % COL_TILE == 0 and hi == lo + COL_TILE
        return jnp.dot(xb_buf[...], w_in[lo // COL_TILE], preferred_element_type=F32)

    a_pool = proj(0, 256)
    pool_ext[:, POOL_PAD:, :] = a_pool.reshape(nb, t, BRANCH_W)
    glu = proj(256, 512) * _sigmoid(proj(512, 768))
    conv_ext[:, CONF_PAD:, :] = glu.reshape(nb, t, BRANCH_W)
    z = proj(1024, 1280) * proj(1280, 1536)
    sc_ext[:, SCONV_PAD:, :] = z.reshape(nb, t, BRANCH_W)

    pl_len = POOL_PAD + t
    sum_b[:, 8:pl_len, :] = pool_ext[:, 8:pl_len, :] + pool_ext[:, 7:pl_len - 1, :]
    sum_a[:, 16:pl_len, :] = sum_b[:, 16:pl_len, :] + sum_b[:, 14:pl_len - 2, :]
    sum_b[:, 24:pl_len, :] = sum_a[:, 24:pl_len, :] + sum_a[:, 20:pl_len - 4, :]
    for sh in range(SUBLANES):
        n_rows = t + SUBLANES * ((CONF_WIDTH - 1 - sh) // SUBLANES)
        first = CONF_PAD - CONF_HIST + sh
        conv_sh[sh, :, 0:n_rows, :] = conv_ext[:, first:first + n_rows, :]

    def window_block(bs, t0, tb):
        bb = bs.stop - bs.start
        shp = (bb, tb, BRANCH_W)
        lane = lax.broadcasted_iota(jnp.int32, shp, 2)
        pos = seq_pos0 + t0 + lax.broadcasted_iota(jnp.int32, shp, 1)
        r0 = POOL_PAD + t0
        cur = pool_ext[bs, r0:r0 + tb, :]
        s2 = cur + pool_ext[bs, r0 - 1:r0 - 1 + tb, :]
        s4 = sum_a[bs, r0:r0 + tb, :]
        s8 = sum_b[bs, r0:r0 + tb, :]
        s16 = s8 + sum_b[bs, r0 - SUBLANES:r0 - SUBLANES + tb, :]
        g1, g2, g3 = (lane < g * POOL_GW for g in (1, 2, 3))
        win_sum = jnp.where(g1, s2, jnp.where(g2, s4, jnp.where(g3, s8, s16)))
        win = jnp.where(g1, POOL_WINDOWS[0], jnp.where(g2, POOL_WINDOWS[1],
                                                         jnp.where(g3, POOL_WINDOWS[2], POOL_WINDOWS[3])))
        cnt = jnp.minimum(pos + 1, win).astype(F32)
        buf_a[bs, t0:t0 + tb, :] = win_sum / cnt - cur
        tiles = (ROW_BLOCK // SUBLANES, SUBLANES, BRANCH_W)
        acc = jnp.broadcast_to(cdb[...][None], tiles)
        for k in range(CONF_WIDTH):
            q, sh = divmod(k, SUBLANES)
            r = t0 + SUBLANES * q
            acc = acc + conv_sh[sh, bs, r:r + tb, :].reshape(tiles) * cdw[k][None]
        buf_b[bs, t0:t0 + tb, :] = acc.reshape(shp)
        base = SCONV_PAD - SCONV_HIST + t0
        acc = sc_ext[bs, base:base + tb, :].reshape(tiles) * scw[0][None]
        for k in range(1, SCONV_WIDTH):
            acc = acc + sc_ext[bs, base + k:base + k + tb, :].reshape(tiles) * scw[k][None]
        buf_c[bs, t0:t0 + tb, :] = acc.reshape(shp)

    def gate_chunk(k):
        g0 = k * COL_TILE
        gate_buf[:, g0:g0 + COL_TILE] = jnp.tanh(proj(GATE_COL0 + g0, GATE_COL0 + g0 + COL_TILE)) + 1.0

    br = {}

    def item_windows(blks):
        def run():
            for bs, t0, tb in blks:
                window_block(bs, t0, tb)
        return run

    def item_state():
        pool_o[...] = pool_ext[:, POOL_PAD + t - POOL_HIST:POOL_PAD + t, :]
        conv_o[...] = conv_ext[:, CONF_PAD + t - CONF_HIST:CONF_PAD + t, :]
        sconv_o[...] = sc_ext[:, SCONV_PAD + t - SCONV_HIST:SCONV_PAD + t, :]
        if not cfg.has_state:
            pool_ext[:, 0:POOL_PAD, :] = pool_ext[:, t:t + POOL_PAD, :]
            conv_ext[:, 0:CONF_PAD, :] = conv_ext[:, t:t + CONF_PAD, :]
            sc_ext[:, 0:SCONV_PAD, :] = sc_ext[:, t:t + SCONV_PAD, :]

    def item_a():
        pooled = buf_a[...].reshape(tm, BRANCH_W).astype(BF16)
        br["a"] = (jnp.dot(pooled, pool_w[...], preferred_element_type=F32) * pool_sc[...]).astype(BF16)

    def item_b():
        cb = _ln(buf_b[...].reshape(tm, BRANCH_W), clg[...], clb[...])
        br["b"] = (cb * _sigmoid(cb)).astype(BF16)

    def item_c():
        br["c"] = (proj(768, 1024) * buf_c[...].reshape(tm, BRANCH_W)).astype(BF16)

    def item_u():
        br["u"] = _gelu_tanh(proj(1536, 1792))

    def item_v():
        v = _ln(_gelu_tanh(proj(1792, 2048)), slg[...], slb[...])
        if cfg.has_state:
            v_o[...] = v.reshape(nb, t, BRANCH_W)
        else:
            v_o[...] = v[tm - cfg.v_rows:, :].reshape(1, cfg.v_rows, BRANCH_W)
        br["v"] = v

    def item_d():
        head = lax.broadcasted_iota(jnp.int32, (CHUNK, BRANCH_W), 1) // SGU_HW
        bias = sbias[...]
        mixed = []
        for j in range(tm // CHUNK):
            vj = br["v"][j * CHUNK:(j + 1) * CHUNK, :]
            stacked = jnp.concatenate([jnp.where(head == h, vj, 0.0) for h in range(SGU_HEADS)], axis=0)
            mixed.append(jnp.dot(wcat[...], stacked.astype(BF16), preferred_element_type=F32) + bias)
        br["d"] = (br["u"] * jnp.concatenate(mixed, axis=0)).astype(BF16)

    blocks = _row_blocks(nb, t)
    n_gate = N_BRANCH * D_MODEL // COL_TILE
    per_item = -(-len(blocks) // WINDOW_ITEMS)
    items =[item_windows(blocks[i:i + per_item]) for i in range(0, len(blocks), per_item)]
    items += [item_state, item_c, item_b, item_u, item_v, item_a, item_d]
    assert len(items) <= n_gate
    for k in range(n_gate):
        if k < len(items):
            items[k]()
        gate_chunk(k)

    n_ct = D_MODEL // COL_TILE
    brs = [br["a"], br["b"], br["c"], br["d"]]
    for c in range(n_ct):
        part = None
        for i in range(N_BRANCH):
            g0 = i * D_MODEL + c * COL_TILE
            term = gate_buf[:, g0:g0 + COL_TILE] * jnp.dot(
                brs[i], w_br[i * n_ct + c], preferred_element_type=F32)
            part = term if part is None else part + term
        xb_buf[:, c * COL_TILE:(c + 1) * COL_TILE] = part.astype(BF16)
    for c in range(n_ct):
        cs = slice(c * COL_TILE, (c + 1) * COL_TILE)
        h1_o[:, cs] = x_ref[:, cs] + jnp.dot(xb_buf[...], w_out[c], preferred_element_type=F32)

    xn2 = _rms(h1_o[...], n2g[...])
    x_hi = xn2.astype(BF16)
    xn2_o[...] = x_hi
    x_lo = (xn2 - x_hi.astype(F32)).astype(BF16)
    hi_both = jnp.dot(x_hi, w_rt[...], preferred_element_type=F32)
    lo_hi = jnp.dot(x_lo, w_rt[:, 0:LANES], preferred_element_type=F32)
    logits = (hi_both[:, 0:LANES] + (lo_hi + hi_both[:, LANES:2 * LANES])) + b_rt[...]
    lane = lax.broadcasted_iota(jnp.int32, (tm, LANES), 1)
    is_g = (lane >= ROUTE_GROUP_LANE0) & (lane < ROUTE_GROUP_LANE0 + N_GROUPS)
    glog = jnp.where(is_g, logits, NEG_BIG)
    gmax = jnp.max(glog, axis=-1, keepdims=True)
    lane_f = lane.astype(F32)
    no_lane = float(4 * LANES)
    gsel = jnp.min(jnp.where(glog == gmax, lane_f, no_lane), axis=-1, keepdims=True) - ROUTE_GROUP_LANE0
    pg = 1.0 / jnp.sum(jnp.where(is_g, jnp.exp(glog - gmax), 0.0), axis=-1, keepdims=True)
    in_grp = (lane < N_EXPERTS) & ((lane // EXPERTS_PER_GROUP) == gsel.astype(jnp.int32))
    el = jnp.where(in_grp, logits, NEG_BIG)
    m1 = jnp.max(el, axis=-1, keepdims=True)
    i1 = jnp.min(jnp.where(in_grp & (el == m1), lane_f, no_lane), axis=-1, keepdims=True)
    rest = in_grp & (lane_f != i1)
    el2 = jnp.where(rest, logits, NEG_BIG)
    m2 = jnp.max(el2, axis=-1, keepdims=True)
    i2 = jnp.min(jnp.where(rest & (el2 == m2), lane_f, no_lane), axis=-1, keepdims=True)
    e21 = jnp.exp(m2 - m1)
    w1 = pg / (1.0 + e21)
    w2 = pg * e21 / (1.0 + e21)
    route_o[...] = jnp.where(lane == 0, i1, jnp.where(lane == 1, i2,
                             jnp.where(lane == 2, w1, jnp.where(lane == 3, w2, 0.0))))


def _layer_spec(shape, layer):
    nd = len(shape)
    return pl.BlockSpec((1,) + shape[1:], lambda *_: (layer,) + (0,) * (nd - 1), pipeline_mode=pl.Buffered(1))


def _mixer_call(cfg, n_seq, n_rows_total, x2d, x_tile_off, states, weights, big_weights, aliased):
    nb, t = cfg.nb, cfg.t
    tm = nb * t
    if cfg.has_state:
        grid = (n_seq // nb,)
        tok = lambda i: (x_tile_off + i, 0)
        out_tok = lambda i: (cfg.row_off + i, 0)
        seq3 = lambda i: (i, 0, 0)
    else:
        grid = (n_seq, cfg.seq_len // t)
        nt = grid[1]
        tok = lambda b, c: (x_tile_off + b * nt + c, 0)
        out_tok = lambda b, c: (cfg.row_off + b * nt + c, 0)
        seq3 = lambda b, c: (b, 0, 0)

    in_specs = [pl.BlockSpec((tm, D_MODEL), tok)]
    args = [x2d]
    if cfg.has_state:
        for s in states:
            in_specs.append(pl.BlockSpec((1, nb) + s.shape[2:], lambda i: (cfg.layer, i, 0, 0)))
            args.append(s)
    for w in weights:
        in_specs.append(_layer_spec(w.shape, cfg.layer))
        args.append(w)
    for w in big_weights:
        in_specs.append(pl.BlockSpec(memory_space=pl.ANY))
        args.append(w)
    io_alias = {}
    if cfg.has_state:
        for k, a in enumerate(aliased):
            in_specs.append(pl.BlockSpec(memory_space=pl.ANY))
            io_alias[len(args)] = k
            args.append(a)

    out_shape = [
        jax.ShapeDtypeStruct((n_rows_total, D_MODEL), F32),
        jax.ShapeDtypeStruct((n_rows_total, LANES), F32),
        jax.ShapeDtypeStruct((n_rows_total, D_MODEL), BF16),
        jax.ShapeDtypeStruct((n_seq, POOL_HIST, BRANCH_W), F32),
        jax.ShapeDtypeStruct((n_seq, CONF_HIST, BRANCH_W), F32),
        jax.ShapeDtypeStruct((n_seq, SCONV_HIST, BRANCH_W), F32),
        jax.ShapeDtypeStruct((n_seq, cfg.v_rows, BRANCH_W), F32),
    ]
    out_specs = [
        pl.BlockSpec((tm, D_MODEL), out_tok),
        pl.BlockSpec((tm, LANES), out_tok),
        pl.BlockSpec((tm, D_MODEL), out_tok),
        pl.BlockSpec((nb, POOL_HIST, BRANCH_W), seq3),
        pl.BlockSpec((nb, CONF_HIST, BRANCH_W), seq3),
        pl.BlockSpec((nb, SCONV_HIST, BRANCH_W), seq3),
        pl.BlockSpec((nb, cfg.v_rows, BRANCH_W), seq3),
    ]
    scratch = [
        pltpu.VMEM((nb, POOL_PAD + t, BRANCH_W), F32),
        pltpu.VMEM((nb, POOL_PAD + t, BRANCH_W), F32),
        pltpu.VMEM((nb, POOL_PAD + t, BRANCH_W), F32),
        pltpu.VMEM((nb, CONF_PAD + t, BRANCH_W), F32),
        pltpu.VMEM((SUBLANES, nb, t + CONF_PAD - SUBLANES, BRANCH_W), F32),
        pltpu.VMEM((nb, SCONV_PAD + t, BRANCH_W), F32),
        pltpu.VMEM((nb, t, BRANCH_W), F32),
        pltpu.VMEM((nb, t, BRANCH_W), F32),
        pltpu.VMEM((nb, t, BRANCH_W), F32),
        pltpu.VMEM((tm, N_BRANCH * D_MODEL), F32),
        pltpu.VMEM((tm, D_MODEL), BF16),
        pltpu.VMEM((IN_COLS // COL_TILE, D_MODEL, COL_TILE), BF16),
        pltpu.VMEM((N_BRANCH * D_MODEL // COL_TILE, BRANCH_W, COL_TILE), BF16),
        pltpu.VMEM((D_MODEL // COL_TILE, D_MODEL, COL_TILE), BF16),
        pltpu.VMEM((2, W_IN_ROWS, IN_COLS), F32),
        pltpu.VMEM((2, BRANCH_W, D_MODEL), F32),
        pltpu.SemaphoreType.DMA((2,)),
    ]
    return pl.pallas_call(
        functools.partial(_mixer_kernel, cfg),
        grid=grid,
        in_specs=in_specs,
        out_specs=out_specs,
        out_shape=out_shape,
        scratch_shapes=scratch,
        input_output_aliases=io_alias,
        compiler_params=pltpu.CompilerParams(
            dimension_semantics=("arbitrary",) * len(grid), vmem_limit_bytes=VMEM_LIMIT),
        name="mixer_sample" if cfg.has_state else "mixer_prompt",
    )(*args)


def _sort_kernel(xn_ref, route_ref, xl_o, pos_o, nch_o):
    t = SORT_TILE
    xn = xn_ref[...]
    r = route_ref[...]
    lane_i = lax.broadcasted_iota(jnp.int32, (t, LANES), 1)
    lane = lane_i.astype(F32)
    hit1 = lane == r[:, 0:1]
    hit2 = lane == r[:, 1:2]
    onehot = jnp.where(hit1 | hit2, 1.0, 0.0)
    row = lax.broadcasted_iota(jnp.int32, (t, t), 0)
    col = lax.broadcasted_iota(jnp.int32, (t, t), 1)
    strict_lower = jnp.where(col < row, 1.0, 0.0).astype(BF16)
    before = jnp.dot(strict_lower, onehot.astype(BF16), preferred_element_type=F32)
    cnt = jnp.sum(onehot, axis=0, keepdims=True)
    nch = jnp.floor((cnt + (SEG_PAD - 1)) * (1.0 / SEG_PAD))
    er = lax.broadcasted_iota(jnp.int32, (LANES, LANES), 0)
    ec = lax.broadcasted_iota(jnp.int32, (LANES, LANES), 1)
    strict_upper = jnp.where(er < ec, 1.0, 0.0).astype(BF16)
    nch8 = jnp.broadcast_to(nch, (SUBLANES, LANES)).astype(BF16)
    seg0 = jnp.dot(nch8, strict_upper, preferred_element_type=F32)[0:1, :] * SEG_PAD
    where = seg0 + before
    pos0 = jnp.sum(jnp.where(hit1, where, 0.0), axis=-1, keepdims=True)
    pos1 = jnp.sum(jnp.where(hit2, where, 0.0), axis=-1, keepdims=True)
    pos = jnp.where(lane_i == 0, pos0, jnp.where(lane_i == 1, pos1, -1.0))
    pos_o[...] = pos
    nch_o[...] = jnp.broadcast_to(nch, (1, SUBLANES, LANES))
    pos_t = jnp.transpose(pos)
    p0 = pos_t[0:1, :]
    p1 = pos_t[1:2, :]
    used_rows = jnp.sum(nch) * SEG_PAD

    def perm_block(blk):
        dst = (lax.broadcasted_iota(jnp.int32, (PERM_BLOCK, t), 0) + blk * PERM_BLOCK).astype(F32)
        perm = jnp.where((dst == p0) | (dst == p1), 1.0, 0.0).astype(BF16)
        xl_o[0, blk * PERM_BLOCK:(blk + 1) * PERM_BLOCK, :] = jnp.dot(
            perm, xn, preferred_element_type=F32).astype(BF16)

    for blk in range(N_PERM_BLOCKS - 1):
        perm_block(blk)
    pl.when(used_rows > (N_PERM_BLOCKS - 1) * PERM_BLOCK)(functools.partial(perm_block, N_PERM_BLOCKS - 1))


def _sort_call(xn2, route):
    n = xn2.shape[0]
    nt = n // SORT_TILE
    return pl.pallas_call(
        _sort_kernel,
        grid=(nt,),
        in_specs=[pl.BlockSpec((SORT_TILE, D_MODEL), lambda i: (i, 0)),
                  pl.BlockSpec((SORT_TILE, LANES), lambda i: (i, 0))],
        out_specs=[pl.BlockSpec((1, LOCAL_ROWS, D_MODEL), lambda i: (i, 0, 0)),
                   pl.BlockSpec((SORT_TILE, LANES), lambda i: (i, 0)),
                   pl.BlockSpec((1, SUBLANES, LANES), lambda i: (i, 0, 0))],
        out_shape=[jax.ShapeDtypeStruct((nt + SPARE_BLOCKS, LOCAL_ROWS, D_MODEL), BF16),
                   jax.ShapeDtypeStruct((n, LANES), F32),
                   jax.ShapeDtypeStruct((nt, SUBLANES, LANES), F32)],
        compiler_params=pltpu.CompilerParams(dimension_semantics=("arbitrary",), vmem_limit_bytes=VMEM_LIMIT),
        name="moe_sort",
    )(xn2, route)


def _chunk_copy(src, src_row, dst, dst_row, sem):
    return pltpu.make_async_copy(src.at[pl.ds(src_row, SEG_PAD), :], dst.at[pl.ds(dst_row, SEG_PAD), :], sem)


def _expert_kernel(layer, trash_row0, se_ref, nv_ref, src_ref, dst_ref, wfirst_ref, wpar_ref, wnext_ref,
                   xl_in, wg_hbm, wu_hbm, wd_hbm, xl_io, *scratch):
    del xl_in
    ns = EXP_SLOTS
    xbuf, ybuf = scratch[:ns], scratch[ns:2 * ns]
    sem_in, sem_out, wg, wu, wd, w_sem = scratch[2 * ns:]
    s = pl.program_id(0)
    nv = nv_ref[0]

    def weight_copies(expert, par):
        return [pltpu.make_async_copy(hbm.at[layer, expert], buf.at[par], w_sem.at[par])
                for hbm, buf in ((wg_hbm, wg), (wu_hbm, wu), (wd_hbm, wd))]

    def gather(step, sl):
        return [_chunk_copy(xl_io, pl.multiple_of(src_ref[step * CHUNKS_PER_STEP + j], SEG_PAD),
                            xbuf[sl], j * SEG_PAD, sem_in.at[sl]) for j in range(CHUNKS_PER_STEP)]

    def write_back(step, sl):
        return [_chunk_copy(ybuf[sl], j * SEG_PAD, xl_io,
                            pl.multiple_of(dst_ref[step * CHUNKS_PER_STEP + j], SEG_PAD), sem_out.at[sl])
                for j in range(CHUNKS_PER_STEP)]

    def spare_write(sl):
        return [_chunk_copy(ybuf[sl], j * SEG_PAD, xl_io, trash_row0 + sl * EXP_TILE + j * SEG_PAD,
                            sem_out.at[sl]) for j in range(CHUNKS_PER_STEP)]

    def start_chunks(copies):
        for c in copies:
            c.start(priority=0)

    @pl.when(s == 0)
    def _():
        for c in weight_copies(se_ref[0], 0):
            c.start(priority=1)
        for ahead in range(ns - 1):
            start_chunks(gather(ahead, ahead))
        for sl in range(ns):
            ybuf[sl][...] = jnp.zeros_like(ybuf[sl])
            start_chunks(spare_write(sl))

    def live_step(sl):
        par = wpar_ref[s]

        @pl.when(wfirst_ref[s] == 1)
        def _():
            for c in weight_copies(se_ref[s], par):
                c.wait()

            @pl.when(wnext_ref[s] >= 0)
            def _():
                for c in weight_copies(wnext_ref[s], 1 - par):
                    c.start(priority=1)

        start_chunks(gather(s + ns - 1, (sl + ns - 1) % ns))
        for c in gather(s, sl):
            c.wait()
        xb = xbuf[sl][...]
        a = jnp.dot(xb, wg[par].astype(BF16), preferred_element_type=F32)
        b = jnp.dot(xb, wu[par].astype(BF16), preferred_element_type=F32)
        hh = (a * _sigmoid(a) * b).astype(BF16)
        y = jnp.dot(hh, wd[par].astype(BF16), preferred_element_type=F32).astype(BF16)
        for c in write_back(s, sl):
            c.wait()
        ybuf[sl][...] = y
        start_chunks(write_back(s, sl))

    def drain_step(sl):
        for ahead in range(ns - 1):
            for c in gather(s + ahead, (sl + ahead) % ns):
                c.wait()
        for every in range(ns):
            for c in spare_write(every):
                c.wait()

    for sl in range(ns):
        pl.when((s < nv) & (s % ns == sl))(functools.partial(live_step, sl))
        pl.when((s == nv) & (s % ns == sl))(functools.partial(drain_step, sl))


def _expert_call(xl, trash_row0, tables, layer, w_gate, w_up, w_down):
    n_steps = tables[0].shape[0] - (EXP_SLOTS - 1)
    grid_spec = pltpu.PrefetchScalarGridSpec(
        num_scalar_prefetch=len(tables),
        grid=(n_steps,),
        in_specs=[pl.BlockSpec(memory_space=pl.ANY) for _ in range(4)],
        out_specs=pl.BlockSpec(memory_space=pl.ANY),
        scratch_shapes=[pltpu.VMEM((EXP_TILE, D_MODEL), BF16) for _ in range(2 * EXP_SLOTS)] + [
                        pltpu.SemaphoreType.DMA((EXP_SLOTS,)), pltpu.SemaphoreType.DMA((EXP_SLOTS,)),
                        pltpu.VMEM((2, D_MODEL, D_EXPERT), F32), pltpu.VMEM((2, D_MODEL, D_EXPERT), F32),
                        pltpu.VMEM((2, D_EXPERT, D_MODEL), F32), pltpu.SemaphoreType.DMA((2,))],
    )
    return pl.pallas_call(
        functools.partial(_expert_kernel, layer, trash_row0),
        grid_spec=grid_spec,
        out_shape=jax.ShapeDtypeStruct(xl.shape, BF16),
        input_output_aliases={len(tables): 0},
        compiler_params=pltpu.CompilerParams(dimension_semantics=("arbitrary",), has_side_effects=True),
        name="moe_experts",
    )(*tables, xl, w_gate, w_up, w_down)


def _combine_kernel(final_norm, n_first, h1_ref, route_ref, pos_ref, fg, yl_ref, *outs):
    t = SORT_TILE
    r = route_ref[...]
    p = pos_ref[...]
    w1, w2 = r[:, 2:3], r[:, 3:4]
    pos0, pos1 = p[:, 0:1], p[:, 1:2]
    *outs, acc_ref = outs

    def gathered(blk):
        src = (lax.broadcasted_iota(jnp.int32, (t, PERM_BLOCK), 1) + blk * PERM_BLOCK).astype(F32)
        pw = jnp.where(src == pos0, w1, 0.0) + jnp.where(src == pos1, w2, 0.0)
        return jnp.dot(pw.astype(BF16), yl_ref[0, blk * PERM_BLOCK:(blk + 1) * PERM_BLOCK, :],
                       preferred_element_type=F32)

    acc = h1_ref[...]
    for blk in range(N_PERM_BLOCKS - 1):
        acc = acc + gathered(blk)
    acc_ref[...] = acc

    @pl.when(jnp.max(p) >= (N_PERM_BLOCKS - 1) * PERM_BLOCK)
    def _():
        acc_ref[...] += gathered(N_PERM_BLOCKS - 1)

    if not final_norm:
        outs[0][...] = acc_ref[...]
        return
    y = _rms(acc_ref[...], fg[...])
    i = pl.program_id(0)

    @pl.when(i < n_first)
    def _():
        outs[0][...] = y

    @pl.when(i >= n_first)
    def _():
        outs[1][...] = y


def _combine_call(h1, route, pos, yl, final_g, final_norm, n_first_rows):
    n = h1.shape[0]
    nt = n // SORT_TILE
    n_first = n_first_rows // SORT_TILE
    tok = lambda i: (i, 0)
    if final_norm:
        out_shape = [jax.ShapeDtypeStruct((n_first_rows, D_MODEL), F32),
                     jax.ShapeDtypeStruct((n - n_first_rows, D_MODEL), F32)]
        out_specs = [pl.BlockSpec((SORT_TILE, D_MODEL), lambda i: (jnp.minimum(i, n_first - 1), 0)),
                     pl.BlockSpec((SORT_TILE, D_MODEL), lambda i: (jnp.maximum(i - n_first, 0), 0))]
    else:
        out_shape = [jax.ShapeDtypeStruct((n, D_MODEL), F32)]
        out_specs = [pl.BlockSpec((SORT_TILE, D_MODEL), tok)]
    return pl.pallas_call(
        functools.partial(_combine_kernel, final_norm, n_first),
        grid=(nt,),
        in_specs=[pl.BlockSpec((SORT_TILE, D_MODEL), tok),
                  pl.BlockSpec((SORT_TILE, LANES), tok),
                  pl.BlockSpec((SORT_TILE, LANES), tok),
                  pl.BlockSpec((1, D_MODEL), lambda i: (0, 0)),
                  pl.BlockSpec((1, LOCAL_ROWS, D_MODEL), lambda i: (i, 0, 0))],
        out_specs=out_specs,
        out_shape=out_shape,
        scratch_shapes=[pltpu.VMEM((SORT_TILE, D_MODEL), F32)],
        compiler_params=pltpu.CompilerParams(dimension_semantics=("arbitrary",), vmem_limit_bytes=VMEM_LIMIT),
        name="moe_combine",
    )(h1, route, pos, final_g, yl)


def _expert_tables(nch, n_steps, trash_row0):
    nt = nch.shape[0]
    cps = CHUNKS_PER_STEP
    seg_row0 = (jnp.cumsum(nch, axis=1) - nch) * SEG_PAD
    first = jnp.cumsum(nch, axis=0) - nch
    tot = jnp.sum(nch, axis=0)
    steps = (tot + cps - 1) // cps
    step_end = jnp.cumsum(steps)
    n_valid = step_end[-1:]
    s_ids = jnp.arange(n_steps, dtype=jnp.int32)
    step_expert = jnp.minimum(jnp.sum((step_end[None, :] <= s_ids[:, None]).astype(jnp.int32), axis=1),
                              N_EXPERTS - 1)
    sel = (step_expert[:, None] == jnp.arange(N_EXPERTS, dtype=jnp.int32)[None, :]).astype(jnp.int32)
    step0 = sel @ (step_end - steps)
    tot_s = sel @ tot
    first_s = sel @ first.T
    row0_s = sel @ seg_row0.T
    k = (s_ids - step0)[:, None] * cps + jnp.arange(cps, dtype=jnp.int32)[None, :]
    ok = (k < tot_s[:, None]) & (s_ids < n_valid[0])[:, None]
    tile = jnp.sum((first_s[:, None, :] <= k[:, :, None]).astype(jnp.int32), axis=2) - 1
    tsel = (tile[:, :, None] == jnp.arange(nt, dtype=jnp.int32)[None, None, :]).astype(jnp.int32)
    first_k = jnp.sum(tsel * first_s[:, None, :], axis=2)
    row0_k = jnp.sum(tsel * row0_s[:, None, :], axis=2)
    row = tile * LOCAL_ROWS + row0_k + (k - first_k) * SEG_PAD
    live = (s_ids < n_valid[0])[:, None]
    src = jnp.where(live, jnp.where(ok, row, row[:, 0:1]), 0)
    spare = trash_row0 + (s_ids % EXP_SLOTS)[:, None] * EXP_TILE + jnp.arange(cps, dtype=jnp.int32)[None, :] * SEG_PAD
    dst = jnp.where(ok, row, spare)
    e_ids = jnp.arange(N_EXPERTS, dtype=jnp.int32)
    has = steps > 0
    run_of_expert = jnp.cumsum(has.astype(jnp.int32)) - 1
    later = has[None, :] & (e_ids[None, :] > e_ids[:, None])
    next_live = jnp.min(jnp.where(later, e_ids[None, :], N_EXPERTS), axis=1)
    next_live = jnp.where(next_live == N_EXPERTS, -1, next_live)
    w_first = ((s_ids == step0) & live[:, 0]).astype(jnp.int32)
    w_par = (sel @ run_of_expert) % 2
    w_next = sel @ next_live
    i32 = lambda a: a.reshape(-1).astype(jnp.int32)
    return (step_expert, i32(n_valid), i32(src), i32(dst), w_first, i32(w_par), i32(w_next))


def _moe(h1, route, xn2, layer, w_gate, w_up, w_down, final_g, final_norm, n_first_rows):
    n = h1.shape[0]
    nt = n // SORT_TILE
    xl, pos, nch = _sort_call(xn2, route)
    nch = nch[:, 0, :N_EXPERTS].astype(jnp.int32)
    max_chunks = (2 * n) // SEG_PAD + nt * N_EXPERTS
    n_steps = max_chunks // CHUNKS_PER_STEP + N_EXPERTS + 1
    trash_row0 = nt * LOCAL_ROWS
    tables = _expert_tables(nch, n_steps + EXP_SLOTS - 1, trash_row0)
    yl = _expert_call(xl.reshape((nt + SPARE_BLOCKS) * LOCAL_ROWS, D_MODEL), trash_row0, tables,
                      layer, w_gate, w_up, w_down)
    return _combine_call(h1, route, pos, yl.reshape(nt + SPARE_BLOCKS, LOCAL_ROWS, D_MODEL), final_g, final_norm,
                         n_first_rows)


def kernel(x_prompt, x_sample, state_pool, state_conv, state_sconv, norm1_g, w_in, pool_w, pool_scale, conf_dw, conf_dw_b, conf_ln_g, conf_ln_b, sconv_w, sgu_ln_g, sgu_ln_b, sgu_ws, sgu_b, w_branch, w_out, norm2_g, router_g, router_g_b, router_e, router_e_b, w_gate, w_up, w_down, final_g):
    depth = w_in.shape[0]
    bp, seq, _ = x_prompt.shape
    bs, dseq, _ = x_sample.shape
    n_p, n_s = bp * seq, bs * dseq
    n = n_p + n_s
    assert seq % MIX_TILE == 0 and n_s % MIX_TILE_S == 0 and n_p % MIX_TILE_S == 0 and dseq == SUBLANES
    assert MIX_TILE % CHUNK == 0 and MIX_TILE_S % CHUNK == 0

    cfg_p = MixCfg(layer=0, nb=1, t=MIX_TILE, has_state=False, start_pos=0, row_off=0, v_rows=CHUNK, seq_len=seq)
    cfg_s = MixCfg(layer=0, nb=MIX_TILE_S // dseq, t=dseq, has_state=True, start_pos=PAST_LEN,
                   row_off=n_p // MIX_TILE_S, v_rows=dseq, seq_len=dseq)
    big_weights = (w_in, w_branch, w_out)

    row = lambda a: a[:, None, :]
    rep8 = lambda a: jnp.broadcast_to(a[..., None, :], a.shape[:-1] + (SUBLANES, a.shape[-1]))
    tril = jnp.tril(jnp.ones((CHUNK, CHUNK), F32))
    eye_blk = jnp.kron(jnp.eye(CHUNK // dseq, dtype=F32), jnp.ones((dseq, dseq), F32))
    cat = lambda w: jnp.concatenate([w[:, h] for h in range(SGU_HEADS)], axis=2).astype(BF16)
    wcat_p = cat(sgu_ws * tril)
    wcat_s = cat(jnp.tile(sgu_ws[:, :, :dseq, :dseq], (1, 1, CHUNK // dseq, CHUNK // dseq)) * (tril * eye_blk))
    bias_p = jnp.repeat(jnp.swapaxes(sgu_b, 1, 2), SGU_HW, axis=2)
    bias_s = jnp.tile(jnp.repeat(jnp.swapaxes(sgu_b[:, :, :dseq], 1, 2), SGU_HW, axis=2), (1, CHUNK // dseq, 1))
    lane_pad = LANES - N_EXPERTS - N_GROUPS
    w_rt32 = jnp.pad(jnp.concatenate([router_e, router_g], axis=2), ((0, 0), (0, 0), (0, lane_pad)))
    w_rt_hi = w_rt32.astype(BF16)
    w_rt = jnp.concatenate([w_rt_hi, (w_rt32 - w_rt_hi.astype(F32)).astype(BF16)], axis=2)
    b_rt = row(jnp.pad(jnp.concatenate([router_e_b, router_g_b], axis=1), ((0, 0), (0, lane_pad))))
    same_group = jnp.eye(len(POOL_WINDOWS), dtype=F32)[None, :, None, :, None]
    pool_bd = (pool_w[:, :, :, None, :] * same_group).reshape(depth, BRANCH_W, BRANCH_W).astype(BF16)

    def weights(wcat, sbias):
        return [row(norm1_g), pool_bd, row(pool_scale), rep8(conf_dw), rep8(conf_dw_b), row(conf_ln_g),
                row(conf_ln_b), rep8(sconv_w), row(sgu_ln_g), row(sgu_ln_b), wcat, sbias, row(norm2_g), w_rt, b_rt]

    weights_p, weights_s = weights(wcat_p, bias_p), weights(wcat_s, bias_s)
    final_row = final_g.reshape(1, -1)

    h_p, h_s = x_prompt.reshape(n_p, D_MODEL), x_sample.reshape(n_s, D_MODEL)
    off_p, off_s = 0, 0
    states_out = []
    for l in range(depth):
        h1, route, xn2, pool_p, conv_p, sconv_p, v_p = _mixer_call(
            cfg_p._replace(layer=l), bp, n, h_p, off_p, None, weights_p, big_weights, None)
        h1, route, xn2, pool_s, conv_s, sconv_s, v_s = _mixer_call(
            cfg_s._replace(layer=l), bs, n, h_s, off_s, (state_pool, state_conv, state_sconv),
            weights_s, big_weights, (h1, route, xn2))
        states_out.append((pool_p, conv_p, sconv_p, v_p, pool_s, conv_s, sconv_s, v_s))

        outs = _moe(h1, route, xn2, l, w_gate, w_up, w_down, final_row, l == depth - 1, n_p)
        h_p = h_s = outs[0]
        off_p, off_s = 0, n_p // MIX_TILE_S

    y_prompt = outs[0].reshape(bp, seq, D_MODEL)
    y_sample = outs[1].reshape(bs, dseq, D_MODEL)
    st = [jnp.stack([s[k] for s in states_out]) for k in range(8)]
    return (y_prompt, y_sample, st[0], st[1], st[2], st[3], st[4], st[5], st[6], st[7])
```
